```python
import math
import jax, jax.numpy as jnp
from jax import lax
import numpy as np

D_MODEL = 2048
BATCH = 32
SEQ = 256
DEPTH = 2
DEC_BATCH = 4
DEC_SEQ = 2048
PAST_LEN = 512

GRID_W = 64
N_EVEN = (DEPTH + 1) // 2
N_ODD = DEPTH // 2
N_MOD = 6
EPS = 1e-6
MLSTM_HEADS = 4
MLSTM_DH = D_MODEL // (2 * MLSTM_HEADS)
MLSTM_W = MLSTM_HEADS * MLSTM_DH
MLSTM_CHUNK = 128
N_GATES = 4 * MLSTM_HEADS
POOL_GROUPS = 4
POOL_WINDOWS = (2, 4, 8, 16)
POOL_W = D_MODEL // 2
POOL_GW = POOL_W // POOL_GROUPS
EVEN_IN = 4 * MLSTM_W + POOL_W + N_GATES
EVEN_OUT = MLSTM_W + POOL_W
DA_HEADS = 8
DA_DH = D_MODEL // (4 * DA_HEADS)
DA_DV = 2 * DA_DH
DA_W = DA_HEADS * DA_DV
Q_BLOCK = 128
ROPE_BASE = 10000.0
GM_GROUPS = 8
GM_CHUNK = 128
GM_W = D_MODEL // 2
GM_GW = GM_W // GM_GROUPS
ODD_IN = 3 * DA_W + 2 * GM_W
ODD_OUT = DA_W + GM_W
PEER_HEADS = 8
PEER_NKEYS = 128
PEER_EXPERTS = PEER_NKEYS * PEER_NKEYS
PEER_DQ = 256
PEER_HALF = PEER_DQ // 2
PEER_TOPK = 16
PEER_TOKEN_BLOCK = 128

kernel_name = 'hybrid_mlstm_pool_diffattn_gmlp_peer_step'


def rms(x):
    xf = x.astype(jnp.float32)
    return xf * lax.rsqrt(jnp.mean(xf * xf, axis=-1, keepdims=True) + EPS)


def rms_norm(x, gain):
    return (rms(x) * gain).astype(x.dtype)


def adaln(cond, w, b):
    return (jax.nn.silu(cond) @ w + b).reshape(cond.shape[0], N_MOD, D_MODEL)


def modulate(x, gain, mod, i):
    return rms_norm(x, gain) * (1.0 + mod[:, None, i + 1]) + mod[:, None, i]


def mlstm_chunkwise(q, k, v, ig, flog, C0, n0, m0):
    B, H, T, _ = q.shape
    nc = T // MLSTM_CHUNK

    def chunks(a):
        return jnp.moveaxis(a.reshape(B, H, nc, MLSTM_CHUNK, *a.shape[3:]), 2, 0)

    lower = jnp.tril(jnp.ones((MLSTM_CHUNK, MLSTM_CHUNK), dtype=bool))

    def step(carry, xs):
        C, n, m = carry
        qc, kc, vc, ic, fc = xs
        b = jnp.cumsum(fc, axis=-1)
        dmat = jnp.where(lower, b[..., :, None] - b[..., None, :] + ic[..., None, :], -jnp.inf)
        inter = b + m[..., None]
        m_t = jnp.maximum(inter, jnp.max(dmat, axis=-1))
        w = jnp.exp(dmat - m_t[..., None])
        a = jnp.exp(inter - m_t)
        s = jnp.einsum('bhtd,bhsd->bhts', qc, kc) * w
        num = a[..., None] * jnp.einsum('bhtd,bhdv->bhtv', qc, C) + jnp.einsum('bhts,bhsv->bhtv', s, vc)
        den = a * jnp.einsum('bhtd,bhd->bht', qc, n) + jnp.sum(s, axis=-1)
        h = num / jnp.maximum(jnp.abs(den), jnp.exp(-m_t))[..., None]
        b_end = b[..., -1]
        g = b_end[..., None] - b + ic
        m_new = jnp.maximum(b_end + m, jnp.max(g, axis=-1))
        decay = jnp.exp(b_end + m - m_new)
        ws = jnp.exp(g - m_new[..., None])
        C_new = decay[..., None, None] * C + jnp.einsum('bhsd,bhsv->bhdv', kc * ws[..., None], vc)
        n_new = decay[..., None] * n + jnp.einsum('bhs,bhsd->bhd', ws, kc)
        return (C_new, n_new, m_new), h

    carry0 = (C0.astype(jnp.float32), n0.astype(jnp.float32), m0.astype(jnp.float32))
    (C, n, m), hs = lax.scan(step, carry0, (chunks(q), chunks(k), chunks(v), chunks(ig), chunks(flog)))
    h = jnp.moveaxis(hs, 0, 2).reshape(B, H, T, -1)
    return h, C, n, m


def mlstm_mixer(q, k, v, o, gates, gain, C0, n0, m0):
    B, T, _ = q.shape

    def heads(a):
        return a.reshape(B, T, MLSTM_HEADS, MLSTM_DH).transpose(0, 2, 1, 3)

    qh, kh, vh = heads(q), heads(k) * (MLSTM_DH ** -0.5), heads(v)
    g = gates.astype(jnp.float32).reshape(B, T, 4, MLSTM_HEADS).transpose(2, 0, 3, 1)
    ig, flog = g[:2], jax.nn.log_sigmoid(g[2:])
    h_f, C_f, n_f, m_f = mlstm_chunkwise(qh, kh, vh, ig[0], flog[0], C0[:, 0], n0[:, 0], m0[:, 0])
    rev = lambda a: jnp.flip(a, axis=2)
    h_b, C_b, n_b, m_b = mlstm_chunkwise(rev(qh), rev(kh), rev(vh), rev(ig[1]), rev(flog[1]),
                                         C0[:, 1], n0[:, 1], m0[:, 1])
    h = rms_norm(h_f + rev(h_b), gain.reshape(MLSTM_HEADS, 1, MLSTM_DH))
    h = h.transpose(0, 2, 1, 3).reshape(B, T, MLSTM_W) * jax.nn.sigmoid(o)
    return h, jnp.stack([C_f, C_b], 1), jnp.stack([n_f, n_b], 1), jnp.stack([m_f, m_b], 1)


def pool_mixer(x, w_grp, scale):
    B, T, _ = x.shape
    xf = x.astype(jnp.float32).reshape(B, T, POOL_GROUPS, POOL_GW)
    S = jnp.concatenate([jnp.zeros((B, 1, POOL_GROUPS, POOL_GW), jnp.float32), jnp.cumsum(xf, axis=1)], axis=1)
    t = jnp.arange(T)
    outs = []
    for g, w in enumerate(POOL_WINDOWS):
        lo = jnp.clip(t - w // 2, 0, T)
        hi = jnp.clip(t - w // 2 + w, 0, T)
        cnt = (hi - lo).astype(jnp.float32)
        Sg = S[:, :, g]
        outs.append((Sg[:, hi] - Sg[:, lo]) / cnt[None, :, None] - xf[:, :, g])
    p = jnp.stack(outs, axis=2)
    y = jnp.einsum('btgc,gcd->btgd', p, w_grp).reshape(B, T, POOL_W)
    return y * scale


def even_mixer(xn, w_in, b_gate, gain, pool_w, pool_scale, w_out, C0, n0, m0):
    z = xn @ w_in
    q, k, v, o, p, gates = jnp.split(
        z, [MLSTM_W, 2 * MLSTM_W, 3 * MLSTM_W, 4 * MLSTM_W, 4 * MLSTM_W + POOL_W], axis=-1)
    h_m, C, n, m = mlstm_mixer(q, k, v, o, gates + b_gate, gain, C0, n0, m0)
    h_p = pool_mixer(p, pool_w, pool_scale)
    return jnp.concatenate([h_m, h_p], axis=-1) @ w_out, C, n, m


def axial_rope_angles(n_tokens):
    rows = n_tokens // GRID_W
    row = jnp.repeat(jnp.arange(rows), GRID_W).astype(jnp.float32)
    col = (jnp.arange(rows * GRID_W) % GRID_W).astype(jnp.float32)
    n_freq = DA_DH // 4
    inv = ROPE_BASE ** (-jnp.arange(n_freq, dtype=jnp.float32) / n_freq)
    return row[:, None] * inv, col[:, None] * inv


def rotate_pairs(x, ang):
    x1, x2 = jnp.split(x, 2, axis=-1)
    cos, sin = jnp.cos(ang), jnp.sin(ang)
    return jnp.concatenate([x1 * cos - x2 * sin, x1 * sin + x2 * cos], axis=-1)


def apply_axial_rope(x, ang_row, ang_col):
    ar = ang_row[None, :, None, None, :]
    ac = ang_col[None, :, None, None, :]
    xr, xc = jnp.split(x, 2, axis=-1)
    return jnp.concatenate([rotate_pairs(xr, ar), rotate_pairs(xc, ac)], axis=-1).astype(x.dtype)


def odd_project(xn, w_in, qk_gain, rope):
    B, T, _ = xn.shape
    z = xn @ w_in
    q, k, v, gu, gv = jnp.split(z, [DA_W, 2 * DA_W, 3 * DA_W, 3 * DA_W + GM_W], axis=-1)
    q = rms_norm(q.reshape(B, T, DA_HEADS, 2, DA_DH), qk_gain[0])
    k = rms_norm(k.reshape(B, T, DA_HEADS, 2, DA_DH), qk_gain[1])
    if rope is not None:
        q = apply_axial_rope(q, *rope)
        k = apply_axial_rope(k, *rope)
    q = q.transpose(0, 2, 1, 3, 4)
    k = k.transpose(0, 2, 1, 3, 4)
    v = v.reshape(B, T, DA_HEADS, DA_DV).transpose(0, 2, 1, 3)
    return q, k, v, jax.nn.gelu(gu), jax.nn.gelu(gv)


def diff_lambda(lp, lam_init):
    lp = lp.astype(jnp.float32)
    return jnp.exp(jnp.sum(lp[0] * lp[1])) - jnp.exp(jnp.sum(lp[2] * lp[3])) + lam_init


def diff_attention(q, k, v, lam, lam_init, subln):
    B, H, Tq = q.shape[:3]
    nb = Tq // Q_BLOCK
    qb = jnp.moveaxis(q.reshape(B, H, nb, Q_BLOCK, 2, DA_DH), 2, 0)

    def one_block(qq):
        s = jnp.einsum('bhqmd,bhkmd->bhmqk', qq, k).astype(jnp.float32) * (DA_DH ** -0.5)
        p = jax.nn.softmax(s, axis=-1)
        a = p[:, :, 0] - lam * p[:, :, 1]
        return jnp.einsum('bhqk,bhkv->bhqv', a, v)

    o = jnp.moveaxis(lax.map(one_block, qb), 0, 2).reshape(B, H, Tq, DA_DV)
    o = rms_norm(o, subln) * (1.0 - lam_init)
    return o.transpose(0, 2, 1, 3).reshape(B, Tq, DA_W)


def chunk_mlp(u, v, ws, b):
    B, T, _ = u.shape
    nc = T // GM_CHUNK
    vg = rms(v.reshape(B, nc, GM_CHUNK, GM_GROUPS, GM_GW))
    mixed = jnp.einsum('gts,bnsgc->bntgc', ws, vg) + b.T[None, None, :, :, None]
    return u * mixed.reshape(B, T, GM_W)


def odd_finish(attn, gu, gv, gm_ws, gm_b, w_out):
    return jnp.concatenate([attn, chunk_mlp(gu, gv, gm_ws, gm_b)], axis=-1) @ w_out


def peer(x, wq, subkeys, U, V):
    B, T, Dm = x.shape
    N = B * T
    xf = x.reshape(N, Dm)
    q = (xf @ wq).reshape(N, PEER_HEADS, 2, PEER_HALF)
    s = jnp.einsum('nhpd,hpkd->nhpk', q, subkeys).astype(jnp.float32)
    sv, si = lax.top_k(s, PEER_TOPK)
    cand = (sv[:, :, 0, :, None] + sv[:, :, 1, None, :]).reshape(N, PEER_HEADS, PEER_TOPK * PEER_TOPK)
    cidx = (si[:, :, 0, :, None] * PEER_NKEYS + si[:, :, 1, None, :]).reshape(N, PEER_HEADS, PEER_TOPK * PEER_TOPK)
    top, pos = lax.top_k(cand, PEER_TOPK)
    nb = N // PEER_TOKEN_BLOCK
    idx = jnp.take_along_axis(cidx, pos, axis=-1).reshape(nb, PEER_TOKEN_BLOCK, PEER_HEADS * PEER_TOPK)
    gate = jax.nn.softmax(top, axis=-1).reshape(nb, PEER_TOKEN_BLOCK, PEER_HEADS * PEER_TOPK)
    xb = xf.reshape(nb, PEER_TOKEN_BLOCK, Dm)

    def one_block(args):
        xx, ii, gg = args
        act = jax.nn.gelu(jnp.einsum('tkd,td->tk', jnp.take(U, ii, axis=0), xx).astype(jnp.float32))
        return jnp.einsum('tk,tkd->td', gg * act, jnp.take(V, ii, axis=0))

    return lax.map(one_block, (xb, idx, gate)).reshape(B, T, Dm)


def setup_inputs(seed: int = 0) -> dict:
    key = jax.random.key(seed)
    ks = iter(jax.random.split(key, 48))
    nrm = lambda shape, scale=1.0: scale * jax.random.normal(next(ks), shape, jnp.float32)
    gain = lambda shape: 1.0 + nrm(shape, 0.02)
    f_bias = jnp.linspace(3.0, 6.0, 2 * MLSTM_HEADS, dtype=jnp.float32)
    b_gate_even = jnp.concatenate(
        [nrm((N_EVEN, 2 * MLSTM_HEADS), 0.1), f_bias + nrm((N_EVEN, 2 * MLSTM_HEADS), 0.1)], axis=-1)
    x_prompt = nrm((BATCH, SEQ, D_MODEL))
    x_sample = nrm((DEC_BATCH, DEC_SEQ, D_MODEL))
    state_mlstm_C = nrm((DEC_BATCH, N_EVEN, 2, MLSTM_HEADS, MLSTM_DH, MLSTM_DH), 0.1)
    state_mlstm_n = nrm((DEC_BATCH, N_EVEN, 2, MLSTM_HEADS, MLSTM_DH), 0.1)
    state_mlstm_m = jax.random.uniform(next(ks), (DEC_BATCH, N_EVEN, 2, MLSTM_HEADS), jnp.float32, 0.0, 4.0)
    cache_da_k = nrm((DEC_BATCH, N_ODD, DA_HEADS, PAST_LEN, 2 * DA_DH))
    cache_da_v = nrm((DEC_BATCH, N_ODD, DA_HEADS, PAST_LEN, DA_DV))
    return {
        'x_prompt': x_prompt,
        'x_sample': x_sample,
        'state_mlstm_C': state_mlstm_C,
        'state_mlstm_n': state_mlstm_n,
        'state_mlstm_m': state_mlstm_m,
        'cache_da_k': cache_da_k,
        'cache_da_v': cache_da_v,
        'c': nrm((DEC_BATCH, D_MODEL)),
        'c_ctx': nrm((D_MODEL,)),
        'norm_mix': gain((DEPTH, D_MODEL)),
        'norm_ffn': gain((DEPTH, D_MODEL)),
        'w_mod': nrm((DEPTH, D_MODEL, N_MOD * D_MODEL), D_MODEL ** -0.5),
        'b_mod': nrm((DEPTH, N_MOD * D_MODEL), 0.02),
        'w_in_even': nrm((N_EVEN, D_MODEL, EVEN_IN), D_MODEL ** -0.5),
        'b_gate_even': b_gate_even,
        'mlstm_gain': gain((N_EVEN, MLSTM_W)),
        'pool_w': nrm((N_EVEN, POOL_GROUPS, POOL_GW, POOL_GW), POOL_GW ** -0.5),
        'pool_scale': gain((N_EVEN, POOL_W)),
        'w_out_even': nrm((N_EVEN, EVEN_OUT, D_MODEL), EVEN_OUT ** -0.5),
        'w_in_odd': nrm((N_ODD, D_MODEL, ODD_IN), D_MODEL ** -0.5),
        'qk_gain': gain((N_ODD, 2, DA_DH)),
        'da_lambda': nrm((N_ODD, 4, DA_DH), 0.1),
        'da_subln': gain((N_ODD, DA_DV)),
        'gm_ws': nrm((N_ODD, GM_GROUPS, GM_CHUNK, GM_CHUNK), GM_CHUNK ** -0.5),
        'gm_b': gain((N_ODD, GM_GROUPS, GM_CHUNK)),
        'w_out_odd': nrm((N_ODD, ODD_OUT, D_MODEL), ODD_OUT ** -0.5),
        'peer_wq': nrm((DEPTH, D_MODEL, PEER_HEADS * PEER_DQ), D_MODEL ** -0.5),
        'peer_subkeys': nrm((DEPTH, PEER_HEADS, 2, PEER_NKEYS, PEER_HALF), PEER_HALF ** -0.5),
        'peer_u': nrm((DEPTH, PEER_EXPERTS, D_MODEL), D_MODEL ** -0.5),
        'peer_v': nrm((DEPTH, PEER_EXPERTS, D_MODEL), (PEER_HEADS * PEER_TOPK) ** -0.5),
    }


def reference(x_prompt, x_sample, state_mlstm_C, state_mlstm_n, state_mlstm_m, cache_da_k, cache_da_v,
              c, c_ctx, norm_mix, norm_ffn, w_mod, b_mod, w_in_even, b_gate_even, mlstm_gain, pool_w,
              pool_scale, w_out_even, w_in_odd, qk_gain, da_lambda, da_subln, gm_ws, gm_b, w_out_odd,
              peer_wq, peer_subkeys, peer_u, peer_v):
    B, S = x_prompt.shape[:2]
    Bd, T = x_sample.shape[:2]
    P = cache_da_k.shape[3]
    rope = axial_rope_angles(T)
    hc, hl = x_prompt, x_sample
    new_C, new_n, new_m, new_k, new_v = [], [], [], [], []
    for l in range(DEPTH):
        j = l // 2
        mod_c = adaln(c_ctx[None], w_mod[l], b_mod[l])
        mod_l = adaln(c, w_mod[l], b_mod[l])
        xc = modulate(hc, norm_mix[l], mod_c, 0)
        xl = modulate(hl, norm_mix[l], mod_l, 0)
        if l % 2 == 0:
            zC = jnp.zeros((B, 2, MLSTM_HEADS, MLSTM_DH, MLSTM_DH), jnp.float32)
            zn = jnp.zeros((B, 2, MLSTM_HEADS, MLSTM_DH), jnp.float32)
            zm = jnp.zeros((B, 2, MLSTM_HEADS), jnp.float32)
            oc, C_c, n_c, m_c = even_mixer(xc, w_in_even[j], b_gate_even[j], mlstm_gain[j], pool_w[j],
                                           pool_scale[j], w_out_even[j], zC, zn, zm)
            ol, _, _, _ = even_mixer(xl, w_in_even[j], b_gate_even[j], mlstm_gain[j], pool_w[j],
                                     pool_scale[j], w_out_even[j], state_mlstm_C[:, j],
                                     state_mlstm_n[:, j], state_mlstm_m[:, j])
            new_C.append(C_c)
            new_n.append(n_c)
            new_m.append(m_c)
        else:
            lam_init = 0.8 - 0.6 * math.exp(-0.3 * l)
            lam = diff_lambda(da_lambda[j], lam_init)
            qc, kc, vc, uc, gvc = odd_project(xc, w_in_odd[j], qk_gain[j], None)
            ac = diff_attention(qc, kc, vc, lam, lam_init, da_subln[j])
            oc = odd_finish(ac, uc, gvc, gm_ws[j], gm_b[j], w_out_odd[j])
            new_k.append(kc.reshape(B, DA_HEADS, S, 2 * DA_DH))
            new_v.append(vc)
            ql, kl, vl, ul, gvl = odd_project(xl, w_in_odd[j], qk_gain[j], rope)
            k_all = jnp.concatenate([kl, cache_da_k[:, j].reshape(Bd, DA_HEADS, P, 2, DA_DH)], axis=2)
            v_all = jnp.concatenate([vl, cache_da_v[:, j]], axis=2)
            al = diff_attention(ql, k_all, v_all, lam, lam_init, da_subln[j])
            ol = odd_finish(al, ul, gvl, gm_ws[j], gm_b[j], w_out_odd[j])
        hc = hc + mod_c[:, None, 2] * oc
        hl = hl + mod_l[:, None, 2] * ol
        xc = modulate(hc, norm_ffn[l], mod_c, 3)
        xl = modulate(hl, norm_ffn[l], mod_l, 3)
        hc = hc + mod_c[:, None, 5] * peer(xc, peer_wq[l], peer_subkeys[l], peer_u[l], peer_v[l])
        hl = hl + mod_l[:, None, 5] * peer(xl, peer_wq[l], peer_subkeys[l], peer_u[l], peer_v[l])
    return (hc, hl, jnp.stack(new_C, axis=1), jnp.stack(new_n, axis=1), jnp.stack(new_m, axis=1),
            jnp.stack(new_k, axis=1), jnp.stack(new_v, axis=1))
```

```python
import functools
import math

import jax
import jax.numpy as jnp
from jax import lax
from jax.experimental import pallas as pl
from jax.experimental.pallas import tpu as pltpu

F32 = jnp.float32
BF16 = jnp.bfloat16

N_MOD = 6
EPS = 1e-6
TOKEN_TILE = 256
LANES = 128
SUBLANES = 8
VMEM_LIMIT_BYTES = 56 * 1024 * 1024

MLSTM_HEADS = 4
MLSTM_DH = 256
MLSTM_CHUNK = 128
POOL_WINDOWS = (2, 4, 8, 16)
POOL_HALO = 8
DA_HEADS = 8
DA_DH = 64
GRID_W = 64
ROPE_BASE = 10000.0
GM_GROUPS = 8
GM_CHUNK = 128
PEER_HEADS = 8
PEER_NKEYS = 128
PEER_TOPK = 16
NEG_INF = float("-inf")
POS_INF = float("inf")


def _params(*sem):
    return pltpu.CompilerParams(dimension_semantics=sem, vmem_limit_bytes=VMEM_LIMIT_BYTES)


def _dot(a, b):
    return jnp.dot(a, b, preferred_element_type=F32)


def _dot_nt(a, b):
    return lax.dot_general(a, b, (((1,), (1,)), ((), ())), preferred_element_type=F32)


def _dot_tn(a, b):
    return lax.dot_general(a, b, (((0,), (0,)), ((), ())), preferred_element_type=F32)


def _split2(x):
    hi = x.astype(BF16)
    lo = (x - hi.astype(F32)).astype(BF16)
    return hi, lo


def _split3(x):
    hi = x.astype(BF16)
    r = x - hi.astype(F32)
    mid = r.astype(BF16)
    lo = (r - mid.astype(F32)).astype(BF16)
    return hi, mid, lo


def _dot3(a, b, dot=_dot):
    ah, al = _split2(a)
    bh, bl = _split2(b)
    return dot(ah, bh) + (dot(ah, bl) + dot(al, bh))


def _gelu_tanh(x):
    return 0.5 * x * (1.0 + jnp.tanh(math.sqrt(2.0 / math.pi) * (x + 0.044715 * (x * x * x))))


def _log_sigmoid(x):
    return -(jnp.maximum(-x, 0.0) + jnp.log1p(jnp.exp(-jnp.abs(x))))


def _mod_row(i, n_ctx_tiles, tiles_per_latent):
    return jnp.where(i < n_ctx_tiles, 0, 1 + (i - n_ctx_tiles) // tiles_per_latent)


def _adaln_kernel(cond_ref, w_ref, b_ref, o_ref):
    c = cond_ref[...]
    s = c * jax.nn.sigmoid(c)
    o_ref[...] = _dot3(s, w_ref[...]) + b_ref[...]


def _adaln(cond8, w_mod, b_mod):
    depth, d, dout = w_mod.shape
    tn = 1024
    return pl.pallas_call(
        _adaln_kernel,
        out_shape=jax.ShapeDtypeStruct((depth, 8, dout), F32),
        grid=(depth, dout // tn),
        in_specs=[
            pl.BlockSpec((8, d), lambda l, j: (0, 0)),
            pl.BlockSpec((None, d, tn), lambda l, j: (l, 0, j)),
            pl.BlockSpec((None, 1, tn), lambda l, j: (l, 0, j)),
        ],
        out_specs=pl.BlockSpec((None, 8, tn), lambda l, j: (l, 0, j)),
        compiler_params=_params("arbitrary", "arbitrary"),
        name="adaln",
    )(cond8, w_mod, b_mod.reshape(depth, 1, dout))


def _inproj_kernel(*refs, with_gates, emit_xn):
    h_ref, gain_ref, shift_ref, scale_ref, w_ref = refs[:5]
    rest = list(refs[5:])
    if with_gates:
        wg_ref, wgt_ref, bg_ref, bgt_ref = rest[:4]
        rest = rest[4:]
    z_ref = rest.pop(0)
    if with_gates:
        g_ref, gt_ref = rest[:2]
        rest = rest[2:]
    if emit_xn:
        xn_out_ref = rest.pop(0)
    xn_scr = rest.pop(0)

    @pl.when(pl.program_id(1) == 0)
    def _():
        x = h_ref[...]
        xn = x * lax.rsqrt(jnp.mean(x * x, axis=-1, keepdims=True) + EPS) * gain_ref[...]
        xn = xn * (1.0 + scale_ref[...]) + shift_ref[...]
        xb = xn.astype(BF16)
        xn_scr[...] = xb
        if emit_xn:
            xn_out_ref[...] = xb
        if with_gates:
            g_ref[...] = _dot3(xn, wg_ref[...]) + bg_ref[...]
            gt_ref[...] = _dot3(wgt_ref[...], xn, dot=_dot_nt) + bgt_ref[...]

    z_ref[...] = _dot(xn_scr[...], w_ref[...])


def _inproj(h, gain, mod3, layer, mod_base, w_bf16, n_ctx_tiles, tiles_per_latent, w_gates=None, b_gates=None,
            emit_xn=False, tn=1024, name="inproj"):
    n, d = h.shape
    dout = w_bf16.shape[1]
    tm = TOKEN_TILE
    with_gates = w_gates is not None

    def mod_idx(off):
        return lambda i, j: ((layer * 8 + _mod_row(i, n_ctx_tiles, tiles_per_latent)) * N_MOD + mod_base + off, 0, 0)

    in_specs = [
        pl.BlockSpec((tm, d), lambda i, j: (i, 0)),
        pl.BlockSpec((1, d), lambda i, j: (0, 0)),
        pl.BlockSpec((None, 1, d), mod_idx(0)),
        pl.BlockSpec((None, 1, d), mod_idx(1)),
        pl.BlockSpec((d, tn), lambda i, j: (0, j)),
    ]
    args = [h, gain.reshape(1, d), mod3, mod3, w_bf16]
    out_shape = [jax.ShapeDtypeStruct((n, dout), F32)]
    out_specs = [pl.BlockSpec((tm, tn), lambda i, j: (i, j))]
    if with_gates:
        ng = w_gates.shape[1]
        in_specs += [pl.BlockSpec((d, ng), lambda i, j: (0, 0)), pl.BlockSpec((ng, d), lambda i, j: (0, 0)),
                     pl.BlockSpec((1, ng), lambda i, j: (0, 0)), pl.BlockSpec((ng, 1), lambda i, j: (0, 0))]
        args += [w_gates, w_gates.T, b_gates.reshape(1, ng), b_gates.reshape(ng, 1)]
        out_shape += [jax.ShapeDtypeStruct((n, ng), F32), jax.ShapeDtypeStruct((ng, n), F32)]
        out_specs += [pl.BlockSpec((tm, ng), lambda i, j: (i, 0)), pl.BlockSpec((ng, tm), lambda i, j: (0, i))]
    if emit_xn:
        out_shape.append(jax.ShapeDtypeStruct((n, d), BF16))
        out_specs.append(pl.BlockSpec((tm, d), lambda i, j: (i, 0)))
    return pl.pallas_call(
        functools.partial(_inproj_kernel, with_gates=with_gates, emit_xn=emit_xn),
        out_shape=out_shape,
        grid=(n // tm, dout // tn),
        in_specs=in_specs,
        out_specs=out_specs,
        scratch_shapes=[pltpu.VMEM((tm, d), BF16)],
        compiler_params=_params("arbitrary", "arbitrary"),
        name=name,
    )(*args)


def _mlstm_kernel(*refs, has_init, emit_state, nchunks):
    q_ref, k_ref, v_ref, g_ref, gt_ref = refs[:5]
    rest = list(refs[5:])
    if has_init:
        c0_ref, n0_ref, m0_ref = rest[:3]
        rest = rest[3:]
    hs_ref = rest.pop(0)
    if emit_state:
        cout_ref, nout_ref, mout_ref = rest[:3]
        rest = rest[3:]
    c_scr, n_scr, m_scr = rest
    nh, dh, L = MLSTM_HEADS, MLSTM_DH, MLSTM_CHUNK
    d = pl.program_id(0)
    s = pl.program_id(2)

    @pl.when(s == 0)
    def _():
        if has_init:
            c_scr[...] = c0_ref[...]
            n_scr[...] = n0_ref[...]
            m_scr[...] = m0_ref[...]
        else:
            c_scr[...] = jnp.zeros_like(c_scr)
            n_scr[...] = jnp.zeros_like(n_scr)
            m_scr[...] = jnp.zeros_like(m_scr)

    row = lax.broadcasted_iota(jnp.int32, (L, L), 0)
    col = lax.broadcasted_iota(jnp.int32, (L, L), 1)
    sgn = jnp.where(d == 0, 1, -1)
    mask = (row - col) * sgn >= 0
    maskb = jnp.where(mask, 1.0, 0.0).astype(BF16)

    g = g_ref[...]
    gt = gt_ref[...]
    fwd = d == 0
    i_col = jnp.where(fwd, g[:, 0:nh], g[:, nh:2 * nh])
    f_col = _log_sigmoid(jnp.where(fwd, g[:, 2 * nh:3 * nh], g[:, 3 * nh:4 * nh]))
    i_row = jnp.where(fwd, gt[0:nh], gt[nh:2 * nh])
    f_row = _log_sigmoid(jnp.where(fwd, gt[2 * nh:3 * nh], gt[3 * nh:4 * nh]))
    fc = _split3(f_col)
    b_col = _dot(maskb, fc[0]) + (_dot(maskb, fc[1]) + _dot(maskb, fc[2]))
    fr = _split3(f_row)
    b_row = _dot_nt(fr[0], maskb) + (_dot_nt(fr[1], maskb) + _dot_nt(fr[2], maskb))
    btot_col = jnp.sum(f_col, axis=0, keepdims=True)
    m_all = m_scr[...]

    m_new_parts = []
    for h in range(nh):
        sl = slice(h * dh, (h + 1) * dh)
        qh = q_ref[:, sl].astype(BF16)
        kf = k_ref[:, sl] * (dh ** -0.5)
        kh = kf.astype(BF16)
        vh = v_ref[:, sl].astype(BF16)
        b_c = b_col[:, h:h + 1]
        b_r = b_row[h:h + 1, :]
        i_c = i_col[:, h:h + 1]
        i_r = i_row[h:h + 1, :]
        m = m_all[:, h:h + 1]
        btot = btot_col[:, h:h + 1]

        dm = jnp.where(mask, b_c - b_r + i_r, NEG_INF)
        inter = b_c + m
        m_t = jnp.maximum(inter, jnp.max(dm, axis=1, keepdims=True))
        w = jnp.exp(dm - m_t)
        a = jnp.exp(inter - m_t)
        sc = _dot_nt(qh, kh) * w
        cb = c_scr[h].astype(BF16)
        num = a * _dot(qh, cb) + _dot(sc.astype(BF16), vh)
        nb = n_scr[h:h + 1, :].astype(BF16).astype(F32)
        qn = jnp.sum(qh.astype(F32) * nb, axis=1, keepdims=True)
        den = a * qn + jnp.sum(sc, axis=1, keepdims=True)
        hs_ref[:, sl] = num / jnp.maximum(jnp.abs(den), jnp.exp(-m_t))

        g_c = btot - b_c + i_c
        g_r = btot - b_r + i_r
        m_new = jnp.maximum(btot + m, jnp.max(g_c, axis=0, keepdims=True))
        decay = jnp.exp(btot + m - m_new)
        ws_c = jnp.exp(g_c - m_new)
        ws_r = jnp.exp(g_r - m_new)
        kw = (kf * ws_c).astype(BF16)
        c_scr[h] = decay * c_scr[h] + _dot_tn(kw, vh)
        n_scr[h:h + 1, :] = decay * n_scr[h:h + 1, :] + _dot(ws_r.astype(BF16), kh)
        m_new_parts.append(m_new)
    m_scr[...] = jnp.concatenate(m_new_parts, axis=1)

    if emit_state:
        @pl.when(s == nchunks - 1)
        def _():
            cout_ref[...] = c_scr[...]
            nout_ref[...] = n_scr[...]
            mout_ref[...] = m_scr[...]


def _mlstm(z, gates, gates_t, tok_off, nseq, seqlen, j, init=None, emit_state=False, n_even=1):
    nh, dh, L = MLSTM_HEADS, MLSTM_DH, MLSTM_CHUNK
    w = nh * dh
    nchunks = seqlen // L
    off = tok_off // L

    def chunk(d, b, s):
        return off + b * nchunks + jnp.where(d == 0, s, nchunks - 1 - s)

    in_specs = [
        pl.BlockSpec((L, w), lambda d, b, s: (chunk(d, b, s), 0)),
        pl.BlockSpec((L, w), lambda d, b, s: (chunk(d, b, s), 1)),
        pl.BlockSpec((L, w), lambda d, b, s: (chunk(d, b, s), 2)),
        pl.BlockSpec((L, 4 * nh), lambda d, b, s: (chunk(d, b, s), 0)),
        pl.BlockSpec((4 * nh, L), lambda d, b, s: (0, chunk(d, b, s))),
    ]
    args = [z, z, z, gates, gates_t]
    has_init = init is not None
    if has_init:
        c0, n0, m0 = init
        in_specs += [
            pl.BlockSpec((None, None, None, nh, dh, dh), lambda d, b, s: (b, j, d, 0, 0, 0)),
            pl.BlockSpec((None, None, None, nh, dh), lambda d, b, s: (b, j, d, 0, 0)),
            pl.BlockSpec((None, None, None, 1, nh), lambda d, b, s: (b, j, d, 0, 0)),
        ]
        args += [c0, n0, m0.reshape(m0.shape[:3] + (1, nh))]
    out_shape = [jax.ShapeDtypeStruct((2, nseq * seqlen, w), F32)]
    out_specs = [pl.BlockSpec((None, L, w), lambda d, b, s: (d, chunk(d, b, s) - off, 0))]
    if emit_state:
        out_shape += [
            jax.ShapeDtypeStruct((nseq, n_even, 2, nh, dh, dh), F32),
            jax.ShapeDtypeStruct((nseq, n_even, 2, nh, dh), F32),
            jax.ShapeDtypeStruct((nseq, n_even, 2, 1, nh), F32),
        ]
        out_specs += [
            pl.BlockSpec((None, None, None, nh, dh, dh), lambda d, b, s: (b, j, d, 0, 0, 0)),
            pl.BlockSpec((None, None, None, nh, dh), lambda d, b, s: (b, j, d, 0, 0)),
            pl.BlockSpec((None, None, None, 1, nh), lambda d, b, s: (b, j, d, 0, 0)),
        ]
    return pl.pallas_call(
        functools.partial(_mlstm_kernel, has_init=has_init, emit_state=emit_state, nchunks=nchunks),
        out_shape=out_shape,
        grid=(2, nseq, nchunks),
        in_specs=in_specs,
        out_specs=out_specs,
        scratch_shapes=[pltpu.VMEM((nh, dh, dh), F32), pltpu.VMEM((nh, dh), F32), pltpu.VMEM((1, nh), F32)],
        compiler_params=_params("arbitrary", "arbitrary", "arbitrary"),
        name="mlstm_ctx" if emit_state else "mlstm_lat",
    )(*args)


def _even_out_kernel(h_ref, gate_ref, hsc_ref, hsl_ref, o_ref, p_ref, pprev_ref, pnext_ref, gain_ref, pw_ref,
                     ps_ref, w_ref, out_ref, cat_scr, *, n_ctx_tiles, tiles_per_latent, ctx_len, lat_len):
    i = pl.program_id(0)
    tm = TOKEN_TILE
    nh, dh = MLSTM_HEADS, MLSTM_DH
    wm = nh * dh

    @pl.when(pl.program_id(1) == 0)
    def _():
        is_ctx = i < n_ctx_tiles
        hs = jnp.where(is_ctx, hsc_ref[0] + hsc_ref[1], hsl_ref[0] + hsl_ref[1])
        for h in range(nh):
            sl = slice(h * dh, (h + 1) * dh)
            x = hs[:, sl]
            y = x * lax.rsqrt(jnp.mean(x * x, axis=-1, keepdims=True) + EPS) * gain_ref[:, sl]
            cat_scr[:, sl] = (y * jax.nn.sigmoid(o_ref[:, sl])).astype(BF16)

        tiles_ctx = ctx_len // tm
        pos = jnp.where(is_ctx, i % tiles_ctx, (i - n_ctx_tiles) % tiles_per_latent)
        ntile = jnp.where(is_ctx, tiles_ctx, tiles_per_latent)
        seqlen = jnp.where(is_ctx, ctx_len, lat_len)
        x = p_ref[...]
        prev = jnp.where(pos > 0, pprev_ref[...], 0.0)
        nxt = jnp.where(pos < ntile - 1, pnext_ref[...], 0.0)
        pad = jnp.zeros((LANES - 2 * POOL_HALO, x.shape[1]), F32)
        xcat = jnp.concatenate([prev, x, nxt, pad], axis=0)
        xh, xl = _split2(xcat)
        t = lax.broadcasted_iota(jnp.int32, (tm, tm + LANES), 0)
        sidx = lax.broadcasted_iota(jnp.int32, (tm, tm + LANES), 1) - POOL_HALO
        tpos = pos * tm + lax.broadcasted_iota(jnp.int32, (tm, 1), 0)
        gw = wm // len(POOL_WINDOWS)
        for gi, win in enumerate(POOL_WINDOWS):
            sl = slice(gi * gw, (gi + 1) * gw)
            band = jnp.where((sidx >= t - win // 2) & (sidx < t - win // 2 + win), 1.0, 0.0).astype(BF16)
            lo = jnp.maximum(tpos - win // 2, 0)
            hi = jnp.minimum(tpos - win // 2 + win, seqlen)
            cnt = (hi - lo).astype(F32)
            p = (_dot(band, xh[:, sl]) + _dot(band, xl[:, sl])) / cnt - x[:, sl]
            y = _dot(p.astype(BF16), pw_ref[gi]) * ps_ref[:, sl]
            cat_scr[:, wm + gi * gw:wm + (gi + 1) * gw] = y.astype(BF16)

    out_ref[...] = h_ref[...] + gate_ref[...] * _dot(cat_scr[...], w_ref[...])


def _even_out(h, mod3, layer, hs_ctx, hs_lat, z, gain, pool_w_bf16, pool_scale, w_out_bf16, n_ctx_tiles,
              tiles_per_latent, ctx_len, lat_len, tn=1024):
    n, d = h.shape
    tm = TOKEN_TILE
    wm = MLSTM_HEADS * MLSTM_DH
    nlt = n // tm - n_ctx_tiles
    rows8 = n // POOL_HALO
    per = tm // POOL_HALO

    def gate_idx(i, j):
        return ((layer * 8 + _mod_row(i, n_ctx_tiles, tiles_per_latent)) * N_MOD + 2, 0, j)

    return pl.pallas_call(
        functools.partial(_even_out_kernel, n_ctx_tiles=n_ctx_tiles, tiles_per_latent=tiles_per_latent,
                          ctx_len=ctx_len, lat_len=lat_len),
        out_shape=jax.ShapeDtypeStruct((n, d), F32),
        grid=(n // tm, d // tn),
        in_specs=[
            pl.BlockSpec((tm, tn), lambda i, j: (i, j)),
            pl.BlockSpec((None, 1, tn), gate_idx),
            pl.BlockSpec((2, tm, wm), lambda i, j: (0, jnp.minimum(i, n_ctx_tiles - 1), 0)),
            pl.BlockSpec((2, tm, wm), lambda i, j: (0, jnp.clip(i - n_ctx_tiles, 0, nlt - 1), 0)),
            pl.BlockSpec((tm, wm), lambda i, j: (i, 3)),
            pl.BlockSpec((tm, wm), lambda i, j: (i, 4)),
            pl.BlockSpec((POOL_HALO, wm), lambda i, j: (jnp.maximum(i * per - 1, 0), 4)),
            pl.BlockSpec((POOL_HALO, wm), lambda i, j: (jnp.minimum((i + 1) * per, rows8 - 1), 4)),
            pl.BlockSpec((1, wm), lambda i, j: (0, 0)),
            pl.BlockSpec(pool_w_bf16.shape, lambda i, j: (0, 0, 0)),
            pl.BlockSpec((1, wm), lambda i, j: (0, 0)),
            pl.BlockSpec((d, tn), lambda i, j: (0, j)),
        ],
        out_specs=pl.BlockSpec((tm, tn), lambda i, j: (i, j)),
        scratch_shapes=[pltpu.VMEM((tm, d), BF16)],
        compiler_params=_params("arbitrary", "arbitrary"),
        name="even_out",
    )(h, mod3, hs_ctx, hs_lat, z, z, z, z, gain.reshape(1, wm), pool_w_bf16, pool_scale.reshape(1, wm), w_out_bf16)


def _qk_kernel(*refs, rope, emit_cache):
    q_ref, k_ref, v_ref, gain_ref = refs[:4]
    rest = list(refs[4:])
    if rope:
        cos_ref, sin_ref = rest[:2]
        rest = rest[2:]
    qn_ref, kn_ref, vb_ref = rest[:3]
    rest = rest[3:]
    lane = lax.broadcasted_iota(jnp.int32, q_ref.shape, 1)
    first = lane < DA_DH

    def norm(x, gain):
        sq = x * x
        s1 = jnp.sum(jnp.where(first, sq, 0.0), axis=-1, keepdims=True)
        s2 = jnp.sum(jnp.where(first, 0.0, sq), axis=-1, keepdims=True)
        ms = jnp.where(first, s1, s2) * (1.0 / DA_DH)
        y = x * lax.rsqrt(ms + EPS) * gain
        if rope:
            quarter = DA_DH // 4
            partner = jnp.where((lane & quarter) == 0, pltpu.roll(y, LANES - quarter, 1), pltpu.roll(y, quarter, 1))
            y = y * cos_ref[...] + partner * sin_ref[...]
        return y

    qn = norm(q_ref[...], gain_ref[0:1, :])
    kn = norm(k_ref[...], gain_ref[1:2, :])
    qn_ref[...] = (qn * (DA_DH ** -0.5)).astype(BF16)
    kn_ref[...] = kn.astype(BF16)
    v = v_ref[...]
    vb_ref[...] = v.astype(BF16)
    if emit_cache:
        newk_ref, newv_ref = rest
        newk_ref[...] = kn
        newv_ref[...] = v


def _qk(z, qk_gain2, tok_off, ntok, j, rope_tables=None, cache_shape=None, seqlen=None):
    tm = TOKEN_TILE
    hd = 2 * DA_DH
    w = DA_HEADS * hd
    off = tok_off // tm
    nblk_w = w // hd
    rope = rope_tables is not None
    emit_cache = cache_shape is not None
    in_specs = [
        pl.BlockSpec((tm, hd), lambda i, h: (off + i, h)),
        pl.BlockSpec((tm, hd), lambda i, h: (off + i, nblk_w + h)),
        pl.BlockSpec((tm, hd), lambda i, h: (off + i, 2 * nblk_w + h)),
        pl.BlockSpec((2, hd), lambda i, h: (0, 0)),
    ]
    args = [z, z, z, qk_gain2]
    if rope:
        tps = seqlen // tm
        in_specs += [pl.BlockSpec((tm, hd), lambda i, h: (i % tps, 0))] * 2
        args += list(rope_tables)
    out_shape = [jax.ShapeDtypeStruct((ntok, w), BF16)] * 3
    out_specs = [pl.BlockSpec((tm, hd), lambda i, h: (i, h))] * 3
    if emit_cache:
        out_shape += [jax.ShapeDtypeStruct(cache_shape, F32)] * 2
        out_specs += [pl.BlockSpec((None, None, None, tm, hd), lambda i, h: (i, j, h, 0, 0))] * 2
    return pl.pallas_call(
        functools.partial(_qk_kernel, rope=rope, emit_cache=emit_cache),
        out_shape=out_shape,
        grid=(ntok // tm, DA_HEADS),
        in_specs=in_specs,
        out_specs=out_specs,
        compiler_params=_params("arbitrary", "arbitrary"),
        name="qk_ctx" if emit_cache else "qk_lat",
    )(*args)


def _rope_tables(t):
    rows = t // GRID_W
    row = jnp.repeat(jnp.arange(rows), GRID_W).astype(F32)
    col = (jnp.arange(rows * GRID_W) % GRID_W).astype(F32)
    n_freq = DA_DH // 4
    inv = ROPE_BASE ** (-jnp.arange(n_freq, dtype=F32) / n_freq)
    ar, ac = row[:, None] * inv, col[:, None] * inv
    cos = jnp.concatenate([jnp.cos(ar), jnp.cos(ar), jnp.cos(ac), jnp.cos(ac)], axis=-1)
    sin = jnp.concatenate([-jnp.sin(ar), jnp.sin(ar), -jnp.sin(ac), jnp.sin(ac)], axis=-1)
    return jnp.tile(cos, (1, 2)), jnp.tile(sin, (1, 2))


def _attn_kernel(*refs, n_pieces, lam_init):
    q_ref, lp_ref, subln_ref = refs[:3]
    kv = refs[3:3 + 2 * n_pieces]
    o_ref = refs[3 + 2 * n_pieces]
    lp = lp_ref[...]
    lam = (jnp.exp(jnp.sum(lp[0:1] * lp[1:2], axis=-1, keepdims=True))
           - jnp.exp(jnp.sum(lp[2:3] * lp[3:4], axis=-1, keepdims=True)) + lam_init)
    q = q_ref[...]
    lane = lax.broadcasted_iota(jnp.int32, q.shape, 1)
    zero = jnp.zeros_like(q)
    qa = jnp.where(lane < DA_DH, q, zero)
    qb = jnp.where(lane < DA_DH, zero, q)
    ks = [kv[2 * p][...].astype(BF16) for p in range(n_pieces)]
    vs = [kv[2 * p + 1][...].astype(BF16) for p in range(n_pieces)]
    sa = [_dot_nt(qa, k) for k in ks]
    sb = [_dot_nt(qb, k) for k in ks]

    def softmax_parts(parts):
        m = functools.reduce(jnp.maximum, [jnp.max(x, axis=-1, keepdims=True) for x in parts])
        es = [jnp.exp(x - m) for x in parts]
        tot = functools.reduce(lambda a, b: a + b, [jnp.sum(e, axis=-1, keepdims=True) for e in es])
        return [e / tot for e in es]

    pa = softmax_parts(sa)
    pb = softmax_parts(sb)
    o = None
    for p in range(n_pieces):
        contrib = _dot((pa[p] - lam * pb[p]).astype(BF16), vs[p])
        o = contrib if o is None else o + contrib
    o = o * lax.rsqrt(jnp.mean(o * o, axis=-1, keepdims=True) + EPS) * subln_ref[...]
    o_ref[...] = (o * (1.0 - lam_init)).astype(BF16)


def _attn(qn, kn, vb, lam_params, subln, lam_init, nseq, seqlen, tq, j, caches=None):
    hd = 2 * DA_DH
    nq = seqlen // tq
    in_specs = [
        pl.BlockSpec((tq, hd), lambda b, h, qi: (b * nq + qi, h)),
        pl.BlockSpec((None, 4, DA_DH), lambda b, h, qi: (j, 0, 0)),
        pl.BlockSpec((None, 1, hd), lambda b, h, qi: (j, 0, 0)),
        pl.BlockSpec((seqlen, hd), lambda b, h, qi: (b, h)),
        pl.BlockSpec((seqlen, hd), lambda b, h, qi: (b, h)),
    ]
    args = [qn, lam_params, subln.reshape(subln.shape[0], 1, hd), kn, vb]
    n_pieces = 1
    if caches is not None:
        ck, cv = caches
        past = ck.shape[3]
        in_specs += [pl.BlockSpec((None, None, None, past, hd), lambda b, h, qi: (b, j, h, 0, 0))] * 2
        args += [ck, cv]
        n_pieces = 2
    return pl.pallas_call(
        functools.partial(_attn_kernel, n_pieces=n_pieces, lam_init=lam_init),
        out_shape=jax.ShapeDtypeStruct(qn.shape, BF16),
        grid=(nseq, DA_HEADS, nq),
        in_specs=in_specs,
        out_specs=pl.BlockSpec((tq, hd), lambda b, h, qi: (b * nq + qi, h)),
        compiler_params=_params("arbitrary", "arbitrary", "arbitrary"),
        name="attn_lat" if caches is not None else "attn_ctx",
    )(*args)


def _odd_out_kernel(h_ref, gate_ref, ac_ref, al_ref, gu_ref, gv_ref, ws_ref, bt_ref, w_ref, out_ref, cat_scr, *,
                    n_ctx_tiles):
    i = pl.program_id(0)
    tm = TOKEN_TILE
    wa = DA_HEADS * 2 * DA_DH
    gw = LANES

    @pl.when(pl.program_id(1) == 0)
    def _():
        cat_scr[:, 0:wa] = jnp.where(i < n_ctx_tiles, ac_ref[...], al_ref[...])
        for c in range(tm // GM_CHUNK):
            rows = slice(c * GM_CHUNK, (c + 1) * GM_CHUNK)
            for g in range(GM_GROUPS):
                cols = slice(g * gw, (g + 1) * gw)
                u = _gelu_tanh(gu_ref[rows, cols])
                v = _gelu_tanh(gv_ref[rows, cols])
                vn = v * lax.rsqrt(jnp.mean(v * v, axis=-1, keepdims=True) + EPS)
                mixed = _dot(ws_ref[g], vn.astype(BF16)) + bt_ref[:, g:g + 1]
                cat_scr[rows, wa + g * gw:wa + (g + 1) * gw] = (u * mixed).astype(BF16)

    out_ref[...] = h_ref[...] + gate_ref[...] * _dot(cat_scr[...], w_ref[...])


def _odd_out(h, mod3, layer, a_ctx, a_lat, z, gm_ws_bf16, gm_b_t, w_out_bf16, n_ctx_tiles, tiles_per_latent, tn=1024):
    n, d = h.shape
    tm = TOKEN_TILE
    wa = DA_HEADS * 2 * DA_DH
    wg = GM_GROUPS * LANES
    nlt = n // tm - n_ctx_tiles

    def gate_idx(i, j):
        return ((layer * 8 + _mod_row(i, n_ctx_tiles, tiles_per_latent)) * N_MOD + 2, 0, j)

    return pl.pallas_call(
        functools.partial(_odd_out_kernel, n_ctx_tiles=n_ctx_tiles),
        out_shape=jax.ShapeDtypeStruct((n, d), F32),
        grid=(n // tm, d // tn),
        in_specs=[
            pl.BlockSpec((tm, tn), lambda i, j: (i, j)),
            pl.BlockSpec((None, 1, tn), gate_idx),
            pl.BlockSpec((tm, wa), lambda i, j: (jnp.minimum(i, n_ctx_tiles - 1), 0)),
            pl.BlockSpec((tm, wa), lambda i, j: (jnp.clip(i - n_ctx_tiles, 0, nlt - 1), 0)),
            pl.BlockSpec((tm, wg), lambda i, j: (i, 3)),
            pl.BlockSpec((tm, wg), lambda i, j: (i, 4)),
            pl.BlockSpec(gm_ws_bf16.shape, lambda i, j: (0, 0, 0)),
            pl.BlockSpec(gm_b_t.shape, lambda i, j: (0, 0)),
            pl.BlockSpec((d, tn), lambda i, j: (0, j)),
        ],
        out_specs=pl.BlockSpec((tm, tn), lambda i, j: (i, j)),
        scratch_shapes=[pltpu.VMEM((tm, d), BF16)],
        compiler_params=_params("arbitrary", "arbitrary"),
        name="odd_out",
    )(h, mod3, a_ctx, a_lat, z, z, gm_ws_bf16, gm_b_t, w_out_bf16)


def _top_values(x, k):
    vals = []
    cur = x
    for r in range(k):
        m = jnp.max(cur, axis=0, keepdims=True)
        vals.append(m)
        if r + 1 < k:
            cur = jnp.where(cur == m, NEG_INF, cur)
    return jnp.concatenate(vals, axis=0)


def _peer_route_kernel(q_ref, sk_ref, thr_ref, c_ref, e2_ref):
    half = PEER_NKEYS
    k = PEER_TOPK
    for h in range(PEER_HEADS):
        q1 = q_ref[:, 2 * h * half:(2 * h + 1) * half].astype(BF16)
        q2 = q_ref[:, (2 * h + 1) * half:(2 * h + 2) * half].astype(BF16)
        s1 = _dot_nt(sk_ref[2 * h], q1)
        s2 = _dot_nt(sk_ref[2 * h + 1], q2)
        a = _top_values(s1, k)
        b = _top_values(s2, k)
        cand = jnp.concatenate([a[r:r + 1] + b for r in range(k)], axis=0)
        top = _top_values(cand, k)
        tau = top[k - 1:k]
        z = jnp.sum(jnp.exp(top - top[0:1]), axis=0, keepdims=True)
        thr = jnp.full(s1.shape, POS_INF, F32)
        for r in range(k):
            thr = jnp.where(s1 + b[r:r + 1] >= tau, b[r:r + 1], thr)
        thr_ref[h] = jnp.exp(thr - b[0:1])
        c_ref[h] = jnp.exp(s1 - a[0:1]) / z
        e2_ref[h] = jnp.exp(s2 - b[0:1])


def _peer_route(qp, subkeys_bf16, tm=512):
    n = qp.shape[0]
    shape = jax.ShapeDtypeStruct((PEER_HEADS, PEER_NKEYS, n), F32)
    spec = pl.BlockSpec((PEER_HEADS, PEER_NKEYS, tm), lambda i: (0, 0, i))
    return pl.pallas_call(
        _peer_route_kernel,
        out_shape=[shape] * 3,
        grid=(n // tm,),
        in_specs=[
            pl.BlockSpec((tm, qp.shape[1]), lambda i: (i, 0)),
            pl.BlockSpec(subkeys_bf16.shape, lambda i: (0, 0, 0)),
        ],
        out_specs=[spec] * 3,
        compiler_params=_params("arbitrary"),
        name="peer_route",
    )(qp, subkeys_bf16)


def _peer_dense_kernel(xn_ref, h_ref, gate_ref, u_ref, vt_ref, thr_ref, c_ref, e2_ref, out_ref, acc_scr, *, te):
    e = pl.program_id(1)
    nkeys = PEER_NKEYS

    @pl.when(e == 0)
    def _():
        acc_scr[...] = jnp.zeros_like(acc_scr)

    act = _gelu_tanh(_dot_nt(u_ref[...], xn_ref[...]))
    parts = []
    for ii in range(te // nkeys):
        r = e * (te // nkeys) + ii
        w = None
        for h in range(PEER_HEADS):
            thr = thr_ref[h, pl.ds(r, 1), :]
            cc = c_ref[h, pl.ds(r, 1), :]
            e2 = e2_ref[h]
            wh = jnp.where(e2 >= thr, e2 * cc, 0.0)
            w = wh if w is None else w + wh
        parts.append((w * act[ii * nkeys:(ii + 1) * nkeys]).astype(BF16))
    wa = jnp.concatenate(parts, axis=0)
    acc_scr[...] += _dot(vt_ref[...], wa)

    @pl.when(e == pl.num_programs(1) - 1)
    def _():
        out_ref[...] = h_ref[...] + gate_ref[...] * acc_scr[...].T


def _peer_dense(xn, h, mod3, layer, u_bf16, vt_bf16, thr, c, e2, n_ctx_tiles, tiles_per_latent, tm=512, te=512):
    n, d = h.shape
    ne = u_bf16.shape[0]
    per = tm // TOKEN_TILE

    def gate_idx(i, e):
        return ((layer * 8 + _mod_row(i * per, n_ctx_tiles, tiles_per_latent)) * N_MOD + 5, 0, 0)

    rspec = pl.BlockSpec((PEER_HEADS, PEER_NKEYS, tm), lambda i, e: (0, 0, i))
    return pl.pallas_call(
        functools.partial(_peer_dense_kernel, te=te),
        out_shape=jax.ShapeDtypeStruct((n, d), F32),
        grid=(n // tm, ne // te),
        in_specs=[
            pl.BlockSpec((tm, d), lambda i, e: (i, 0)),
            pl.BlockSpec((tm, d), lambda i, e: (i, 0)),
            pl.BlockSpec((None, 1, d), gate_idx),
            pl.BlockSpec((te, d), lambda i, e: (e, 0)),
            pl.BlockSpec((d, te), lambda i, e: (0, e)),
            rspec, rspec, rspec,
        ],
        out_specs=pl.BlockSpec((tm, d), lambda i, e: (i, 0)),
        scratch_shapes=[pltpu.VMEM((d, tm), F32)],
        compiler_params=_params("arbitrary", "arbitrary"),
        name="peer_dense",
    )(xn, h, mod3, u_bf16, vt_bf16, thr, c, e2)


def kernel(x_prompt, x_sample, state_mlstm_C, state_mlstm_n, state_mlstm_m, cache_da_k, cache_da_v, c, c_ctx, norm_mix, norm_ffn, w_mod, b_mod, w_in_even, b_gate_even, mlstm_gain, pool_w, pool_scale, w_out_even, w_in_odd, qk_gain, da_lambda, da_subln, gm_ws, gm_b, w_out_odd, peer_wq, peer_subkeys, peer_u, peer_v):
    nb, s_len, d = x_prompt.shape
    nbd, t_len, _ = x_sample.shape
    depth = w_mod.shape[0]
    tm = TOKEN_TILE
    assert s_len == tm and t_len % tm == 0 and nbd <= 7 and t_len % GRID_W == 0
    n_ctx = nb * s_len
    n_lat = nbd * t_len
    n_ctx_tiles = n_ctx // tm
    tiles_per_latent = t_len // tm
    n_even = (depth + 1) // 2
    n_odd = depth // 2

    h = jnp.concatenate([x_prompt.reshape(n_ctx, d), x_sample.reshape(n_lat, d)], axis=0)
    cond8 = jnp.concatenate([c_ctx[None], c, jnp.zeros((7 - nbd, d), F32)], axis=0)
    mod3 = _adaln(cond8, w_mod, b_mod).reshape(depth * 8 * N_MOD, 1, d)
    rope = _rope_tables(t_len)

    new_c, new_n, new_m, new_k, new_v = [], [], [], [], []
    wm = MLSTM_HEADS * MLSTM_DH
    for l in range(depth):
        j = l // 2
        if l % 2 == 0:
            w_in = w_in_even[j]
            n_main = 5 * wm
            z, gates, gates_t = _inproj(h, norm_mix[l], mod3, l, 0, w_in[:, :n_main].astype(BF16), n_ctx_tiles,
                                        tiles_per_latent, w_gates=w_in[:, n_main:], b_gates=b_gate_even[j],
                                        name="inproj_even")
            hs_ctx, cc, cn, cm = _mlstm(z, gates, gates_t, 0, nb, s_len, 0, emit_state=True)
            (hs_lat,) = _mlstm(z, gates, gates_t, n_ctx, nbd, t_len, j,
                               init=(state_mlstm_C, state_mlstm_n, state_mlstm_m))
            new_c.append(cc[:, 0])
            new_n.append(cn[:, 0])
            new_m.append(cm[:, 0, :, 0, :])
            h = _even_out(h, mod3, l, hs_ctx, hs_lat, z, mlstm_gain[j], pool_w[j].astype(BF16), pool_scale[j],
                          w_out_even[j].astype(BF16), n_ctx_tiles, tiles_per_latent, s_len, t_len)
        else:
            lam_init = 0.8 - 0.6 * math.exp(-0.3 * l)
            (z,) = _inproj(h, norm_mix[l], mod3, l, 0, w_in_odd[j].astype(BF16), n_ctx_tiles, tiles_per_latent,
                           name="inproj_odd")
            gain2 = jnp.tile(qk_gain[j], (1, 2))
            cache_shape = (nb, 1, DA_HEADS, s_len, 2 * DA_DH)
            qn_c, kn_c, vb_c, nk, nv = _qk(z, gain2, 0, n_ctx, 0, cache_shape=cache_shape)
            qn_l, kn_l, vb_l = _qk(z, gain2, n_ctx, n_lat, 0, rope_tables=rope, seqlen=t_len)
            new_k.append(nk[:, 0])
            new_v.append(nv[:, 0])
            a_ctx = _attn(qn_c, kn_c, vb_c, da_lambda, da_subln, lam_init, nb, s_len, s_len, j)
            a_lat = _attn(qn_l, kn_l, vb_l, da_lambda, da_subln, lam_init, nbd, t_len, tm, j,
                          caches=(cache_da_k, cache_da_v))
            h = _odd_out(h, mod3, l, a_ctx, a_lat, z, gm_ws[j].astype(BF16), gm_b[j].T, w_out_odd[j].astype(BF16),
                         n_ctx_tiles, tiles_per_latent)
        qp, xn = _inproj(h, norm_ffn[l], mod3, l, 3, peer_wq[l].astype(BF16), n_ctx_tiles, tiles_per_latent,
                         emit_xn=True, name="inproj_peer")
        sk = peer_subkeys[l].reshape(2 * PEER_HEADS, PEER_NKEYS, PEER_NKEYS).astype(BF16)
        thr, cw, e2 = _peer_route(qp, sk)
        h = _peer_dense(xn, h, mod3, l, peer_u[l].astype(BF16), peer_v[l].T.astype(BF16), thr, cw, e2,
                        n_ctx_tiles, tiles_per_latent)

    return (h[:n_ctx].reshape(nb, s_len, d), h[n_ctx:].reshape(nbd, t_len, d),
            jnp.stack(new_c, axis=1), jnp.stack(new_n, axis=1), jnp.stack(new_m, axis=1),
            jnp.stack(new_k, axis=1), jnp.stack(new_v, axis=1))
```

```python
import functools
import math

import jax
import jax.numpy as jnp
from jax import lax
from jax.experimental import pallas as pl
from jax.experimental.pallas import tpu as pltpu

F32 = jnp.float32
BF16 = jnp.bfloat16

N_MOD = 6
EPS = 1e-6
TOKEN_TILE = 256
LANES = 128
SUBLANES = 8
VMEM_LIMIT_BYTES = 56 * 1024 * 1024

MLSTM_HEADS = 4
MLSTM_DH = 256
MLSTM_CHUNK = 128
POOL_WINDOWS = (2, 4, 8, 16)
POOL_HALO = 8
DA_HEADS = 8
DA_DH = 64
GRID_W = 64
ROPE_BASE = 10000.0
GM_GROUPS = 8
GM_CHUNK = 128
PEER_HEADS = 8
PEER_NKEYS = 128
PEER_TOPK = 16
NEG_INF = float("-inf")
POS_INF = float("inf")


def _params(*sem):
    return pltpu.CompilerParams(dimension_semantics=sem, vmem_limit_bytes=VMEM_LIMIT_BYTES)


def _dot(a, b):
    return jnp.dot(a, b, preferred_element_type=F32)


def _dot_nt(a, b):
    return lax.dot_general(a, b, (((1,), (1,)), ((), ())), preferred_element_type=F32)


def _dot_tn(a, b):
    return lax.dot_general(a, b, (((0,), (0,)), ((), ())), preferred_element_type=F32)


def _split2(x):
    hi = x.astype(BF16)
    lo = (x - hi.astype(F32)).astype(BF16)
    return hi, lo


def _split3(x):
    hi = x.astype(BF16)
    r = x - hi.astype(F32)
    mid = r.astype(BF16)
    lo = (r - mid.astype(F32)).astype(BF16)
    return hi, mid, lo


def _dot3(a, b, dot=_dot):
    ah, al = _split2(a)
    bh, bl = _split2(b)
    return dot(ah, bh) + (dot(ah, bl) + dot(al, bh))


def _gelu_tanh(x):
    return 0.5 * x * (1.0 + jnp.tanh(math.sqrt(2.0 / math.pi) * (x + 0.044715 * (x * x * x))))


def _log_sigmoid(x):
    return -(jnp.maximum(-x, 0.0) + jnp.log1p(jnp.exp(-jnp.abs(x))))


def _mod_row(i, n_ctx_tiles, tiles_per_latent):
    return jnp.where(i < n_ctx_tiles, 0, 1 + (i - n_ctx_tiles) // tiles_per_latent)


def _adaln_kernel(cond_ref, w_ref, b_ref, o_ref):
    c = cond_ref[...]
    s = c * jax.nn.sigmoid(c)
    o_ref[...] = _dot3(s, w_ref[...]) + b_ref[...]


def _adaln(cond8, w_mod, b_mod):
    depth, d, dout = w_mod.shape
    tn = 1024
    return pl.pallas_call(
        _adaln_kernel,
        out_shape=jax.ShapeDtypeStruct((depth, 8, dout), F32),
        grid=(depth, dout // tn),
        in_specs=[
            pl.BlockSpec((8, d), lambda l, j: (0, 0)),
            pl.BlockSpec((None, d, tn), lambda l, j: (l, 0, j)),
            pl.BlockSpec((None, 1, tn), lambda l, j: (l, 0, j)),
        ],
        out_specs=pl.BlockSpec((None, 8, tn), lambda l, j: (l, 0, j)),
        compiler_params=_params("arbitrary", "arbitrary"),
        name="adaln",
    )(cond8, w_mod, b_mod.reshape(depth, 1, dout))


def _inproj_kernel(*refs, with_gates, emit_xn):
    h_ref, gain_ref, shift_ref, scale_ref, w_ref = refs[:5]
    rest = list(refs[5:])
    if with_gates:
        wg_ref, wgt_ref, bg_ref, bgt_ref = rest[:4]
        rest = rest[4:]
    z_ref = rest.pop(0)
    if with_gates:
        g_ref, gt_ref = rest[:2]
        rest = rest[2:]
    if emit_xn:
        xn_out_ref = rest.pop(0)
    xn_scr = rest.pop(0)

    @pl.when(pl.program_id(1) == 0)
    def _():
        x = h_ref[...]
        xn = x * lax.rsqrt(jnp.mean(x * x, axis=-1, keepdims=True) + EPS) * gain_ref[...]
        xn = xn * (1.0 + scale_ref[...]) + shift_ref[...]
        xb = xn.astype(BF16)
        xn_scr[...] = xb
        if emit_xn:
            xn_out_ref[...] = pltpu.bitcast(xn.T.astype(BF16), F32)
        if with_gates:
            g_ref[...] = _dot3(xn, wg_ref[...]) + bg_ref[...]
            gt_ref[...] = _dot3(wgt_ref[...], xn, dot=_dot_nt) + bgt_ref[...]

    z_ref[...] = _dot(xn_scr[...], w_ref[...])


def _inproj(h, gain, mod3, layer, mod_base, w_bf16, n_ctx_tiles, tiles_per_latent, w_gates=None, b_gates=None,
            emit_xn=False, tn=1024, name="inproj"):
    n, d = h.shape
    dout = w_bf16.shape[1]
    tm = TOKEN_TILE * math.gcd(4, n_ctx_tiles, tiles_per_latent)
    per = tm // TOKEN_TILE
    with_gates = w_gates is not None

    def mod_idx(off):
        return lambda i, j: ((layer * 8 + _mod_row(i * per, n_ctx_tiles, tiles_per_latent)) * N_MOD + mod_base + off,
                             0, 0)

    in_specs = [
        pl.BlockSpec((tm, d), lambda i, j: (i, 0)),
        pl.BlockSpec((1, d), lambda i, j: (0, 0)),
        pl.BlockSpec((None, 1, d), mod_idx(0)),
        pl.BlockSpec((None, 1, d), mod_idx(1)),
        pl.BlockSpec((d, tn), lambda i, j: (0, j)),
    ]
    args = [h, gain.reshape(1, d), mod3, mod3, w_bf16]
    out_shape = [jax.ShapeDtypeStruct((n, dout), F32)]
    out_specs = [pl.BlockSpec((tm, tn), lambda i, j: (i, j))]
    if with_gates:
        ng = w_gates.shape[1]
        in_specs += [pl.BlockSpec((d, ng), lambda i, j: (0, 0)), pl.BlockSpec((ng, d), lambda i, j: (0, 0)),
                     pl.BlockSpec((1, ng), lambda i, j: (0, 0)), pl.BlockSpec((ng, 1), lambda i, j: (0, 0))]
        args += [w_gates, w_gates.T, b_gates.reshape(1, ng), b_gates.reshape(ng, 1)]
        out_shape += [jax.ShapeDtypeStruct((n, ng), F32), jax.ShapeDtypeStruct((ng, n), F32)]
        out_specs += [pl.BlockSpec((tm, ng), lambda i, j: (i, 0)), pl.BlockSpec((ng, tm), lambda i, j: (0, i))]
    if emit_xn:
        out_shape.append(jax.ShapeDtypeStruct((d // 2, n), F32))
        out_specs.append(pl.BlockSpec((d // 2, tm), lambda i, j: (0, i)))
    return pl.pallas_call(
        functools.partial(_inproj_kernel, with_gates=with_gates, emit_xn=emit_xn),
        out_shape=out_shape,
        grid=(n // tm, dout // tn),
        in_specs=in_specs,
        out_specs=out_specs,
        scratch_shapes=[pltpu.VMEM((tm, d), BF16)],
        compiler_params=_params("arbitrary", "arbitrary"),
        name=name,
    )(*args)


def _mlstm_kernel(*refs, has_init, emit_state, nchunks):
    q_ref, k_ref, v_ref, g_ref, gt_ref = refs[:5]
    rest = list(refs[5:])
    if has_init:
        c0_ref, n0_ref, m0_ref = rest[:3]
        rest = rest[3:]
    hs_ref = rest.pop(0)
    if emit_state:
        cout_ref, nout_ref, mout_ref = rest[:3]
        rest = rest[3:]
    c_scr, n_scr, m_scr = rest
    nh, dh, L = MLSTM_HEADS, MLSTM_DH, MLSTM_CHUNK
    d = pl.program_id(0)
    s = pl.program_id(2)

    @pl.when(s == 0)
    def _():
        if has_init:
            c_scr[...] = c0_ref[...]
            n_scr[...] = n0_ref[...]
            m_scr[...] = m0_ref[...]
        else:
            c_scr[...] = jnp.zeros_like(c_scr)
            n_scr[...] = jnp.zeros_like(n_scr)
            m_scr[...] = jnp.zeros_like(m_scr)

    row = lax.broadcasted_iota(jnp.int32, (L, L), 0)
    col = lax.broadcasted_iota(jnp.int32, (L, L), 1)
    sgn = jnp.where(d == 0, 1, -1)
    mask = (row - col) * sgn >= 0
    maskb = jnp.where(mask, 1.0, 0.0).astype(BF16)

    g = g_ref[...]
    gt = gt_ref[...]
    fwd = d == 0
    i_col = jnp.where(fwd, g[:, 0:nh], g[:, nh:2 * nh])
    f_col = _log_sigmoid(jnp.where(fwd, g[:, 2 * nh:3 * nh], g[:, 3 * nh:4 * nh]))
    i_row = jnp.where(fwd, gt[0:nh], gt[nh:2 * nh])
    f_row = _log_sigmoid(jnp.where(fwd, gt[2 * nh:3 * nh], gt[3 * nh:4 * nh]))
    fc = _split3(f_col)
    b_col = _dot(maskb, fc[0]) + (_dot(maskb, fc[1]) + _dot(maskb, fc[2]))
    fr = _split3(f_row)
    b_row = _dot_nt(fr[0], maskb) + (_dot_nt(fr[1], maskb) + _dot_nt(fr[2], maskb))
    btot_col = jnp.sum(f_col, axis=0, keepdims=True)
    m_all = m_scr[...]

    m_new_parts = []
    for h in range(nh):
        sl = slice(h * dh, (h + 1) * dh)
        qh = q_ref[:, sl].astype(BF16)
        kf = k_ref[:, sl] * (dh ** -0.5)
        kh = kf.astype(BF16)
        vh = v_ref[:, sl].astype(BF16)
        b_c = b_col[:, h:h + 1]
        b_r = b_row[h:h + 1, :]
        i_c = i_col[:, h:h + 1]
        i_r = i_row[h:h + 1, :]
        m = m_all[:, h:h + 1]
        btot = btot_col[:, h:h + 1]

        dm = jnp.where(mask, b_c - b_r + i_r, NEG_INF)
        inter = b_c + m
        m_t = jnp.maximum(inter, jnp.max(dm, axis=1, keepdims=True))
        w = jnp.exp(dm - m_t)
        a = jnp.exp(inter - m_t)
        sc = _dot_nt(qh, kh) * w
        cb = c_scr[h].astype(BF16)
        num = a * _dot(qh, cb) + _dot(sc.astype(BF16), vh)
        nb = n_scr[h:h + 1, :].astype(BF16).astype(F32)
        qn = jnp.sum(qh.astype(F32) * nb, axis=1, keepdims=True)
        den = a * qn + jnp.sum(sc, axis=1, keepdims=True)
        hs_ref[:, sl] = num / jnp.maximum(jnp.abs(den), jnp.exp(-m_t))

        g_c = btot - b_c + i_c
        g_r = btot - b_r + i_r
        m_new = jnp.maximum(btot + m, jnp.max(g_c, axis=0, keepdims=True))
        decay = jnp.exp(btot + m - m_new)
        ws_c = jnp.exp(g_c - m_new)
        ws_r = jnp.exp(g_r - m_new)
        kw = (kf * ws_c).astype(BF16)
        c_scr[h] = decay * c_scr[h] + _dot_tn(kw, vh)
        n_scr[h:h + 1, :] = decay * n_scr[h:h + 1, :] + _dot(ws_r.astype(BF16), kh)
        m_new_parts.append(m_new)
    m_scr[...] = jnp.concatenate(m_new_parts, axis=1)

    if emit_state:
        @pl.when(s == nchunks - 1)
        def _():
            cout_ref[...] = c_scr[...]
            nout_ref[...] = n_scr[...]
            mout_ref[...] = m_scr[...]


def _mlstm(z, gates, gates_t, tok_off, nseq, seqlen, j, init=None, emit_state=False, n_even=1):
    nh, dh, L = MLSTM_HEADS, MLSTM_DH, MLSTM_CHUNK
    w = nh * dh
    nchunks = seqlen // L
    off = tok_off // L

    def chunk(d, b, s):
        return off + b * nchunks + jnp.where(d == 0, s, nchunks - 1 - s)

    in_specs = [
        pl.BlockSpec((L, w), lambda d, b, s: (chunk(d, b, s), 0)),
        pl.BlockSpec((L, w), lambda d, b, s: (chunk(d, b, s), 1)),
        pl.BlockSpec((L, w), lambda d, b, s: (chunk(d, b, s), 2)),
        pl.BlockSpec((L, 4 * nh), lambda d, b, s: (chunk(d, b, s), 0)),
        pl.BlockSpec((4 * nh, L), lambda d, b, s: (0, chunk(d, b, s))),
    ]
    args = [z, z, z, gates, gates_t]
    has_init = init is not None
    if has_init:
        c0, n0, m0 = init
        in_specs += [
            pl.BlockSpec((None, None, None, nh, dh, dh), lambda d, b, s: (b, j, d, 0, 0, 0)),
            pl.BlockSpec((None, None, None, nh, dh), lambda d, b, s: (b, j, d, 0, 0)),
            pl.BlockSpec((None, None, None, 1, nh), lambda d, b, s: (b, j, d, 0, 0)),
        ]
        args += [c0, n0, m0.reshape(m0.shape[:3] + (1, nh))]
    out_shape = [jax.ShapeDtypeStruct((2, nseq * seqlen, w), F32)]
    out_specs = [pl.BlockSpec((None, L, w), lambda d, b, s: (d, chunk(d, b, s) - off, 0))]
    if emit_state:
        out_shape += [
            jax.ShapeDtypeStruct((nseq, n_even, 2, nh, dh, dh), F32),
            jax.ShapeDtypeStruct((nseq, n_even, 2, nh, dh), F32),
            jax.ShapeDtypeStruct((nseq, n_even, 2, 1, nh), F32),
        ]
        out_specs += [
            pl.BlockSpec((None, None, None, nh, dh, dh), lambda d, b, s: (b, j, d, 0, 0, 0)),
            pl.BlockSpec((None, None, None, nh, dh), lambda d, b, s: (b, j, d, 0, 0)),
            pl.BlockSpec((None, None, None, 1, nh), lambda d, b, s: (b, j, d, 0, 0)),
        ]
    return pl.pallas_call(
        functools.partial(_mlstm_kernel, has_init=has_init, emit_state=emit_state, nchunks=nchunks),
        out_shape=out_shape,
        grid=(2, nseq, nchunks),
        in_specs=in_specs,
        out_specs=out_specs,
        scratch_shapes=[pltpu.VMEM((nh, dh, dh), F32), pltpu.VMEM((nh, dh), F32), pltpu.VMEM((1, nh), F32)],
        compiler_params=_params("arbitrary", "arbitrary", "arbitrary"),
        name="mlstm_ctx" if emit_state else "mlstm_lat",
    )(*args)


def _even_out_kernel(h_ref, gate_ref, hsc_ref, hsl_ref, o_ref, p_ref, pprev_ref, pnext_ref, gain_ref, pw_ref,
                     ps_ref, w_ref, out_ref, cat_scr, *, n_ctx_tiles, tiles_per_latent, ctx_len, lat_len):
    i = pl.program_id(0)
    tm = TOKEN_TILE
    nh, dh = MLSTM_HEADS, MLSTM_DH
    wm = nh * dh

    @pl.when(pl.program_id(1) == 0)
    def _():
        is_ctx = i < n_ctx_tiles
        hs = jnp.where(is_ctx, hsc_ref[0] + hsc_ref[1], hsl_ref[0] + hsl_ref[1])
        for h in range(nh):
            sl = slice(h * dh, (h + 1) * dh)
            x = hs[:, sl]
            y = x * lax.rsqrt(jnp.mean(x * x, axis=-1, keepdims=True) + EPS) * gain_ref[:, sl]
            cat_scr[:, sl] = (y * jax.nn.sigmoid(o_ref[:, sl])).astype(BF16)

        tiles_ctx = ctx_len // tm
        pos = jnp.where(is_ctx, i % tiles_ctx, (i - n_ctx_tiles) % tiles_per_latent)
        ntile = jnp.where(is_ctx, tiles_ctx, tiles_per_latent)
        seqlen = jnp.where(is_ctx, ctx_len, lat_len)
        x = p_ref[...]
        prev = jnp.where(pos > 0, pprev_ref[...], 0.0)
        nxt = jnp.where(pos < ntile - 1, pnext_ref[...], 0.0)
        pad = jnp.zeros((LANES - 2 * POOL_HALO, x.shape[1]), F32)
        xcat = jnp.concatenate([prev, x, nxt, pad], axis=0)
        xh, xl = _split2(xcat)
        t = lax.broadcasted_iota(jnp.int32, (tm, tm + LANES), 0)
        sidx = lax.broadcasted_iota(jnp.int32, (tm, tm + LANES), 1) - POOL_HALO
        tpos = pos * tm + lax.broadcasted_iota(jnp.int32, (tm, 1), 0)
        gw = wm // len(POOL_WINDOWS)
        for gi, win in enumerate(POOL_WINDOWS):
            sl = slice(gi * gw, (gi + 1) * gw)
            band = jnp.where((sidx >= t - win // 2) & (sidx < t - win // 2 + win), 1.0, 0.0).astype(BF16)
            lo = jnp.maximum(tpos - win // 2, 0)
            hi = jnp.minimum(tpos - win // 2 + win, seqlen)
            cnt = (hi - lo).astype(F32)
            p = (_dot(band, xh[:, sl]) + _dot(band, xl[:, sl])) / cnt - x[:, sl]
            y = _dot(p.astype(BF16), pw_ref[gi]) * ps_ref[:, sl]
            cat_scr[:, wm + gi * gw:wm + (gi + 1) * gw] = y.astype(BF16)

    out_ref[...] = h_ref[...] + gate_ref[...] * _dot(cat_scr[...], w_ref[...])


def _even_out(h, mod3, layer, hs_ctx, hs_lat, z, gain, pool_w_bf16, pool_scale, w_out_bf16, n_ctx_tiles,
              tiles_per_latent, ctx_len, lat_len, tn=2048):
    n, d = h.shape
    tm = TOKEN_TILE
    wm = MLSTM_HEADS * MLSTM_DH
    nlt = n // tm - n_ctx_tiles
    rows8 = n // POOL_HALO
    per = tm // POOL_HALO

    def gate_idx(i, j):
        return ((layer * 8 + _mod_row(i, n_ctx_tiles, tiles_per_latent)) * N_MOD + 2, 0, j)

    return pl.pallas_call(
        functools.partial(_even_out_kernel, n_ctx_tiles=n_ctx_tiles, tiles_per_latent=tiles_per_latent,
                          ctx_len=ctx_len, lat_len=lat_len),
        out_shape=jax.ShapeDtypeStruct((n, d), F32),
        grid=(n // tm, d // tn),
        in_specs=[
            pl.BlockSpec((tm, tn), lambda i, j: (i, j)),
            pl.BlockSpec((None, 1, tn), gate_idx),
            pl.BlockSpec((2, tm, wm), lambda i, j: (0, jnp.minimum(i, n_ctx_tiles - 1), 0)),
            pl.BlockSpec((2, tm, wm), lambda i, j: (0, jnp.clip(i - n_ctx_tiles, 0, nlt - 1), 0)),
            pl.BlockSpec((tm, wm), lambda i, j: (i, 3)),
            pl.BlockSpec((tm, wm), lambda i, j: (i, 4)),
            pl.BlockSpec((POOL_HALO, wm), lambda i, j: (jnp.maximum(i * per - 1, 0), 4)),
            pl.BlockSpec((POOL_HALO, wm), lambda i, j: (jnp.minimum((i + 1) * per, rows8 - 1), 4)),
            pl.BlockSpec((1, wm), lambda i, j: (0, 0)),
            pl.BlockSpec(pool_w_bf16.shape, lambda i, j: (0, 0, 0)),
            pl.BlockSpec((1, wm), lambda i, j: (0, 0)),
            pl.BlockSpec((d, tn), lambda i, j: (0, j)),
        ],
        out_specs=pl.BlockSpec((tm, tn), lambda i, j: (i, j)),
        scratch_shapes=[pltpu.VMEM((tm, d), BF16)],
        compiler_params=_params("arbitrary", "arbitrary"),
        name="even_out",
    )(h, mod3, hs_ctx, hs_lat, z, z, z, z, gain.reshape(1, wm), pool_w_bf16, pool_scale.reshape(1, wm), w_out_bf16)


def _qk_kernel(*refs, rope, emit_cache):
    q_ref, k_ref, v_ref, gain_ref = refs[:4]
    rest = list(refs[4:])
    if rope:
        cos_ref, sin_ref = rest[:2]
        rest = rest[2:]
    qn_ref, kn_ref, vb_ref = rest[:3]
    rest = rest[3:]
    lane = lax.broadcasted_iota(jnp.int32, q_ref.shape, 1)
    first = lane < DA_DH

    def norm(x, gain):
        sq = x * x
        s1 = jnp.sum(jnp.where(first, sq, 0.0), axis=-1, keepdims=True)
        s2 = jnp.sum(jnp.where(first, 0.0, sq), axis=-1, keepdims=True)
        ms = jnp.where(first, s1, s2) * (1.0 / DA_DH)
        y = x * lax.rsqrt(ms + EPS) * gain
        if rope:
            quarter = DA_DH // 4
            partner = jnp.where((lane & quarter) == 0, pltpu.roll(y, LANES - quarter, 1), pltpu.roll(y, quarter, 1))
            y = y * cos_ref[...] + partner * sin_ref[...]
        return y

    qn = norm(q_ref[...], gain_ref[0:1, :])
    kn = norm(k_ref[...], gain_ref[1:2, :])
    qn_ref[...] = (qn * (DA_DH ** -0.5)).astype(BF16)
    kn_ref[...] = kn.astype(BF16)
    v = v_ref[...]
    vb_ref[...] = v.astype(BF16)
    if emit_cache:
        newk_ref, newv_ref = rest
        newk_ref[...] = kn
        newv_ref[...] = v


def _qk(z, qk_gain2, tok_off, ntok, j, rope_tables=None, cache_shape=None, seqlen=None):
    tm = TOKEN_TILE
    hd = 2 * DA_DH
    w = DA_HEADS * hd
    off = tok_off // tm
    nblk_w = w // hd
    rope = rope_tables is not None
    emit_cache = cache_shape is not None
    in_specs = [
        pl.BlockSpec((tm, hd), lambda i, h: (off + i, h)),
        pl.BlockSpec((tm, hd), lambda i, h: (off + i, nblk_w + h)),
        pl.BlockSpec((tm, hd), lambda i, h: (off + i, 2 * nblk_w + h)),
        pl.BlockSpec((2, hd), lambda i, h: (0, 0)),
    ]
    args = [z, z, z, qk_gain2]
    if rope:
        tps = seqlen // tm
        in_specs += [pl.BlockSpec((tm, hd), lambda i, h: (i % tps, 0))] * 2
        args += list(rope_tables)
    out_shape = [jax.ShapeDtypeStruct((ntok, w), BF16)] * 3
    out_specs = [pl.BlockSpec((tm, hd), lambda i, h: (i, h))] * 3
    if emit_cache:
        out_shape += [jax.ShapeDtypeStruct(cache_shape, F32)] * 2
        out_specs += [pl.BlockSpec((None, None, None, tm, hd), lambda i, h: (i, j, h, 0, 0))] * 2
    return pl.pallas_call(
        functools.partial(_qk_kernel, rope=rope, emit_cache=emit_cache),
        out_shape=out_shape,
        grid=(ntok // tm, DA_HEADS),
        in_specs=in_specs,
        out_specs=out_specs,
        compiler_params=_params("arbitrary", "arbitrary"),
        name="qk_ctx" if emit_cache else "qk_lat",
    )(*args)


def _rope_tables(t):
    rows = t // GRID_W
    row = jnp.repeat(jnp.arange(rows), GRID_W).astype(F32)
    col = (jnp.arange(rows * GRID_W) % GRID_W).astype(F32)
    n_freq = DA_DH // 4
    inv = ROPE_BASE ** (-jnp.arange(n_freq, dtype=F32) / n_freq)
    ar, ac = row[:, None] * inv, col[:, None] * inv
    cos = jnp.concatenate([jnp.cos(ar), jnp.cos(ar), jnp.cos(ac), jnp.cos(ac)], axis=-1)
    sin = jnp.concatenate([-jnp.sin(ar), jnp.sin(ar), -jnp.sin(ac), jnp.sin(ac)], axis=-1)
    return jnp.tile(cos, (1, 2)), jnp.tile(sin, (1, 2))


def _attn_kernel(*refs, n_pieces, lam_init):
    q_ref, lp_ref, subln_ref = refs[:3]
    kv = refs[3:3 + 2 * n_pieces]
    o_ref = refs[3 + 2 * n_pieces]
    lp = lp_ref[...]
    lam = (jnp.exp(jnp.sum(lp[0:1] * lp[1:2], axis=-1, keepdims=True))
           - jnp.exp(jnp.sum(lp[2:3] * lp[3:4], axis=-1, keepdims=True)) + lam_init)
    q = q_ref[...]
    lane = lax.broadcasted_iota(jnp.int32, q.shape, 1)
    zero = jnp.zeros_like(q)
    qa = jnp.where(lane < DA_DH, q, zero)
    qb = jnp.where(lane < DA_DH, zero, q)
    ks = [kv[2 * p][...].astype(BF16) for p in range(n_pieces)]
    vs = [kv[2 * p + 1][...].astype(BF16) for p in range(n_pieces)]
    sa = [_dot_nt(qa, k) for k in ks]
    sb = [_dot_nt(qb, k) for k in ks]

    def softmax_parts(parts):
        m = functools.reduce(jnp.maximum, [jnp.max(x, axis=-1, keepdims=True) for x in parts])
        es = [jnp.exp(x - m) for x in parts]
        tot = functools.reduce(lambda a, b: a + b, [jnp.sum(e, axis=-1, keepdims=True) for e in es])
        return [e / tot for e in es]

    pa = softmax_parts(sa)
    pb = softmax_parts(sb)
    o = None
    for p in range(n_pieces):
        contrib = _dot((pa[p] - lam * pb[p]).astype(BF16), vs[p])
        o = contrib if o is None else o + contrib
    o = o * lax.rsqrt(jnp.mean(o * o, axis=-1, keepdims=True) + EPS) * subln_ref[...]
    o_ref[...] = (o * (1.0 - lam_init)).astype(BF16)


def _attn(qn, kn, vb, lam_params, subln, lam_init, nseq, seqlen, tq, j, caches=None):
    hd = 2 * DA_DH
    nq = seqlen // tq
    in_specs = [
        pl.BlockSpec((tq, hd), lambda b, h, qi: (b * nq + qi, h)),
        pl.BlockSpec((None, 4, DA_DH), lambda b, h, qi: (j, 0, 0)),
        pl.BlockSpec((None, 1, hd), lambda b, h, qi: (j, 0, 0)),
        pl.BlockSpec((seqlen, hd), lambda b, h, qi: (b, h)),
        pl.BlockSpec((seqlen, hd), lambda b, h, qi: (b, h)),
    ]
    args = [qn, lam_params, subln.reshape(subln.shape[0], 1, hd), kn, vb]
    n_pieces = 1
    if caches is not None:
        ck, cv = caches
        past = ck.shape[3]
        in_specs += [pl.BlockSpec((None, None, None, past, hd), lambda b, h, qi: (b, j, h, 0, 0))] * 2
        args += [ck, cv]
        n_pieces = 2
    return pl.pallas_call(
        functools.partial(_attn_kernel, n_pieces=n_pieces, lam_init=lam_init),
        out_shape=jax.ShapeDtypeStruct(qn.shape, BF16),
        grid=(nseq, DA_HEADS, nq),
        in_specs=in_specs,
        out_specs=pl.BlockSpec((tq, hd), lambda b, h, qi: (b * nq + qi, h)),
        compiler_params=_params("arbitrary", "arbitrary", "arbitrary"),
        name="attn_lat" if caches is not None else "attn_ctx",
    )(*args)


def _odd_out_kernel(h_ref, gate_ref, ac_ref, al_ref, gu_ref, gv_ref, ws_ref, bt_ref, w_ref, out_ref, cat_scr, *,
                    n_ctx_tiles):
    i = pl.program_id(0)
    tm = TOKEN_TILE
    wa = DA_HEADS * 2 * DA_DH
    gw = LANES

    @pl.when(pl.program_id(1) == 0)
    def _():
        cat_scr[:, 0:wa] = jnp.where(i < n_ctx_tiles, ac_ref[...], al_ref[...])
        for c in range(tm // GM_CHUNK):
            rows = slice(c * GM_CHUNK, (c + 1) * GM_CHUNK)
            for g in range(GM_GROUPS):
                cols = slice(g * gw, (g + 1) * gw)
                u = _gelu_tanh(gu_ref[rows, cols])
                v = _gelu_tanh(gv_ref[rows, cols])
                vn = v * lax.rsqrt(jnp.mean(v * v, axis=-1, keepdims=True) + EPS)
                mixed = _dot(ws_ref[g], vn.astype(BF16)) + bt_ref[:, g:g + 1]
                cat_scr[rows, wa + g * gw:wa + (g + 1) * gw] = (u * mixed).astype(BF16)

    out_ref[...] = h_ref[...] + gate_ref[...] * _dot(cat_scr[...], w_ref[...])


def _odd_out(h, mod3, layer, a_ctx, a_lat, z, gm_ws_bf16, gm_b_t, w_out_bf16, n_ctx_tiles, tiles_per_latent, tn=2048):
    n, d = h.shape
    tm = TOKEN_TILE
    wa = DA_HEADS * 2 * DA_DH
    wg = GM_GROUPS * LANES
    nlt = n // tm - n_ctx_tiles

    def gate_idx(i, j):
        return ((layer * 8 + _mod_row(i, n_ctx_tiles, tiles_per_latent)) * N_MOD + 2, 0, j)

    return pl.pallas_call(
        functools.partial(_odd_out_kernel, n_ctx_tiles=n_ctx_tiles),
        out_shape=jax.ShapeDtypeStruct((n, d), F32),
        grid=(n // tm, d // tn),
        in_specs=[
            pl.BlockSpec((tm, tn), lambda i, j: (i, j)),
            pl.BlockSpec((None, 1, tn), gate_idx),
            pl.BlockSpec((tm, wa), lambda i, j: (jnp.minimum(i, n_ctx_tiles - 1), 0)),
            pl.BlockSpec((tm, wa), lambda i, j: (jnp.clip(i - n_ctx_tiles, 0, nlt - 1), 0)),
            pl.BlockSpec((tm, wg), lambda i, j: (i, 3)),
            pl.BlockSpec((tm, wg), lambda i, j: (i, 4)),
            pl.BlockSpec(gm_ws_bf16.shape, lambda i, j: (0, 0, 0)),
            pl.BlockSpec(gm_b_t.shape, lambda i, j: (0, 0)),
            pl.BlockSpec((d, tn), lambda i, j: (0, j)),
        ],
        out_specs=pl.BlockSpec((tm, tn), lambda i, j: (i, j)),
        scratch_shapes=[pltpu.VMEM((tm, d), BF16)],
        compiler_params=_params("arbitrary", "arbitrary"),
        name="odd_out",
    )(h, mod3, a_ctx, a_lat, z, z, gm_ws_bf16, gm_b_t, w_out_bf16)


def _top_values(x, k, with_rank=False):
    vals = []
    cur = x
    rank = jnp.full(x.shape, float(k), F32) if with_rank else None
    for r in range(k):
        m = jnp.max(cur, axis=0, keepdims=True)
        vals.append(m)
        hit = cur == m
        if with_rank:
            rank = jnp.where(hit, float(r), rank)
        if r + 1 < k:
            cur = jnp.where(hit, NEG_INF, cur)
    vals = jnp.concatenate(vals, axis=0)
    return (vals, rank) if with_rank else vals


def _dup_bf16(x):
    b = pltpu.bitcast(x.astype(BF16).astype(F32), jnp.uint32)
    return pltpu.bitcast(b | (b >> 16), F32)


def _peer_route_kernel(q_ref, sk_ref, rank_ref, e2_ref, cnt_ref, c_ref):
    half = PEER_NKEYS
    k = PEER_TOPK
    for h in range(PEER_HEADS):
        q1 = q_ref[:, 2 * h * half:(2 * h + 1) * half].astype(BF16)
        q2 = q_ref[:, (2 * h + 1) * half:(2 * h + 2) * half].astype(BF16)
        s1 = _dot_nt(sk_ref[2 * h], q1)
        s2 = _dot_nt(sk_ref[2 * h + 1], q2)
        a = _top_values(s1, k)
        b, rank2 = _top_values(s2, k, with_rank=True)
        rows = [a[r:r + 1] + b[0:k // (r + 1)] for r in range(k)]
        nrows = sum(k // (r + 1) for r in range(k))
        rows.append(jnp.full((-nrows % SUBLANES, s1.shape[1]), NEG_INF, F32))
        top = _top_values(jnp.concatenate(rows, axis=0), k)
        tau = top[k - 1:k]
        z = jnp.sum(jnp.exp(top - top[0:1]), axis=0, keepdims=True)
        cnt = jnp.zeros(s1.shape, F32)
        for r in range(k):
            cnt = jnp.where(s1 + b[r:r + 1] >= tau, float(r + 1), cnt)
        rank_ref[h] = pltpu.bitcast(rank2.astype(BF16), F32)
        e2_ref[h] = pltpu.bitcast(jnp.exp(s2 - b[0:1]).astype(BF16), F32)
        cnt_ref[h] = _dup_bf16(cnt)
        c_ref[h] = _dup_bf16(jnp.exp(s1 - a[0:1]) / z)


def _peer_route(qp, subkeys_bf16, tm=512):
    n = qp.shape[0]
    def out(rows):
        return (jax.ShapeDtypeStruct((PEER_HEADS, rows, n), F32),
                pl.BlockSpec((PEER_HEADS, rows, tm), lambda i: (0, 0, i)))

    outs = [out(PEER_NKEYS // 2)] * 2 + [out(PEER_NKEYS)] * 2
    return pl.pallas_call(
        _peer_route_kernel,
        out_shape=[o[0] for o in outs],
        grid=(n // tm,),
        in_specs=[
            pl.BlockSpec((tm, qp.shape[1]), lambda i: (i, 0)),
            pl.BlockSpec(subkeys_bf16.shape, lambda i: (0, 0, 0)),
        ],
        out_specs=[o[1] for o in outs],
        compiler_params=_params("arbitrary"),
        name="peer_route",
    )(qp, subkeys_bf16)


def _peer_dense_kernel(xnt_ref, h_ref, gate_ref, u_ref, vt_ref, rank_ref, e2_ref, cnt_ref, c_ref, out_ref,
                       act0_scr, act1_scr, wa0_scr, wa1_scr, acc_scr, *, te, n_tiles):
    s = pl.program_id(1)
    nkeys = PEER_NKEYS
    per = te // nkeys
    tm = acc_scr.shape[1]
    half = tm // 2
    assert per % 4 == 0
    pack = 2 * SUBLANES

    @pl.when(s == 0)
    def _():
        acc_scr[...] = jnp.zeros_like(acc_scr)
        act1_scr[...] = jnp.zeros_like(act1_scr)
        wa0_scr[...] = jnp.zeros_like(wa0_scr)

    live = jnp.logical_and(s >= 1, s <= n_tiles)
    t2 = jnp.clip(s - 1, 0, n_tiles - 1)

    def step(act_w, act_r, wa_w, wa_r):
        def stage1(c):
            cols = slice(c * half, (c + 1) * half)
            act_w[:, cols] = _dot(pltpu.bitcast(u_ref[...], BF16),
                                  pltpu.bitcast(xnt_ref[:, cols], BF16))

        def stage3(c):
            cols = slice(c * half, (c + 1) * half)
            acc_scr[:, cols] += _dot(pltpu.bitcast(vt_ref[...], BF16), wa_r[:, cols])

        mxu_work = [functools.partial(stage1, 0), functools.partial(stage3, 0),
                    functools.partial(stage1, 1), functools.partial(stage3, 1)]
        stride = per // len(mxu_work)
        for ii in range(per):
            if ii % stride == 0:
                mxu_work[ii // stride]()
            r = t2 * per + ii
            cnt_rows = [jnp.where(live, cnt_ref[h, pl.ds(r, 1), :], 0.0) for h in range(PEER_HEADS)]
            c_rows = [c_ref[h, pl.ds(r, 1), :] for h in range(PEER_HEADS)]
            for lt in range(tm // LANES):
                lanes = slice(lt * LANES, (lt + 1) * LANES)
                cnts = [pltpu.bitcast(jnp.broadcast_to(x[:, lanes], (SUBLANES, LANES)), BF16) for x in cnt_rows]
                ccs = [pltpu.bitcast(jnp.broadcast_to(x[:, lanes], (SUBLANES, LANES)), BF16) for x in c_rows]
                for sb in range(nkeys // pack):
                    words = slice(sb * SUBLANES, (sb + 1) * SUBLANES)
                    rows = slice(ii * nkeys + sb * pack, ii * nkeys + (sb + 1) * pack)
                    w = None
                    for h in range(PEER_HEADS):
                        e2 = pltpu.bitcast(e2_ref[h, words, lanes], BF16)
                        rank = pltpu.bitcast(rank_ref[h, words, lanes], BF16)
                        wh = jnp.where(rank < cnts[h], e2 * ccs[h], jnp.zeros_like(e2))
                        w = wh if w is None else w + wh
                    wa_w[rows, lanes] = w * _gelu_tanh(act_r[rows, lanes]).astype(BF16)

    @pl.when(s % 2 == 0)
    def _():
        step(act0_scr, act1_scr, wa1_scr, wa0_scr)

    @pl.when(s % 2 == 1)
    def _():
        step(act1_scr, act0_scr, wa0_scr, wa1_scr)

    @pl.when(s == n_tiles + 1)
    def _():
        out_ref[...] = h_ref[...] + gate_ref[...] * acc_scr[...].T


def _pack_rows(x):
    r, c = x.shape
    return lax.bitcast_convert_type(jnp.swapaxes(x.astype(BF16).reshape(r // 2, 2, c), 1, 2), F32)


def _peer_dense(xnt, h, mod3, layer, u_packed, vt_packed, rank2, e2, cnt, c, n_ctx_tiles, tiles_per_latent, tm=512,
                te=512):
    n, d = h.shape
    n_tiles = 2 * u_packed.shape[0] // te
    per = tm // TOKEN_TILE

    def gate_idx(i, s):
        return ((layer * 8 + _mod_row(i * per, n_ctx_tiles, tiles_per_latent)) * N_MOD + 5, 0, 0)

    rspec = pl.BlockSpec((PEER_HEADS, PEER_NKEYS, tm), lambda i, s: (0, 0, i))
    pspec = pl.BlockSpec((PEER_HEADS, PEER_NKEYS // 2, tm), lambda i, s: (0, 0, i))
    return pl.pallas_call(
        functools.partial(_peer_dense_kernel, te=te, n_tiles=n_tiles),
        out_shape=jax.ShapeDtypeStruct((n, d), F32),
        grid=(n // tm, n_tiles + 2),
        in_specs=[
            pl.BlockSpec((d // 2, tm), lambda i, s: (0, i)),
            pl.BlockSpec((tm, d), lambda i, s: (i, 0)),
            pl.BlockSpec((None, 1, d), gate_idx),
            pl.BlockSpec((te // 2, d), lambda i, s: (jnp.minimum(s, n_tiles - 1), 0)),
            pl.BlockSpec((d // 2, te), lambda i, s: (0, jnp.clip(s - 2, 0, n_tiles - 1))),
            pspec, pspec, rspec, rspec,
        ],
        out_specs=pl.BlockSpec((tm, d), lambda i, s: (i, 0)),
        scratch_shapes=[pltpu.VMEM((te, tm), F32), pltpu.VMEM((te, tm), F32), pltpu.VMEM((te, tm), BF16),
                        pltpu.VMEM((te, tm), BF16), pltpu.VMEM((d, tm), F32)],
        compiler_params=_params("arbitrary", "arbitrary"),
        name="peer_dense",
    )(xnt, h, mod3, u_packed, vt_packed, rank2, e2, cnt, c)


def kernel(x_prompt, x_sample, state_mlstm_C, state_mlstm_n, state_mlstm_m, cache_da_k, cache_da_v, c, c_ctx, norm_mix, norm_ffn, w_mod, b_mod, w_in_even, b_gate_even, mlstm_gain, pool_w, pool_scale, w_out_even, w_in_odd, qk_gain, da_lambda, da_subln, gm_ws, gm_b, w_out_odd, peer_wq, peer_subkeys, peer_u, peer_v):
    nb, s_len, d = x_prompt.shape
    nbd, t_len, _ = x_sample.shape
    depth = w_mod.shape[0]
    tm = TOKEN_TILE
    assert s_len == tm and t_len % tm == 0 and nbd <= 7 and t_len % GRID_W == 0
    n_ctx = nb * s_len
    n_lat = nbd * t_len
    n_ctx_tiles = n_ctx // tm
    tiles_per_latent = t_len // tm
    n_even = (depth + 1) // 2
    n_odd = depth // 2

    h = jnp.concatenate([x_prompt.reshape(n_ctx, d), x_sample.reshape(n_lat, d)], axis=0)
    cond8 = jnp.concatenate([c_ctx[None], c, jnp.zeros((7 - nbd, d), F32)], axis=0)
    mod3 = _adaln(cond8, w_mod, b_mod).reshape(depth * 8 * N_MOD, 1, d)
    rope = _rope_tables(t_len)

    new_c, new_n, new_m, new_k, new_v = [], [], [], [], []
    wm = MLSTM_HEADS * MLSTM_DH
    for l in range(depth):
        j = l // 2
        if l % 2 == 0:
            w_in = w_in_even[j]
            n_main = 5 * wm
            z, gates, gates_t = _inproj(h, norm_mix[l], mod3, l, 0, w_in[:, :n_main].astype(BF16), n_ctx_tiles,
                                        tiles_per_latent, w_gates=w_in[:, n_main:], b_gates=b_gate_even[j],
                                        name="inproj_even")
            hs_ctx, cc, cn, cm = _mlstm(z, gates, gates_t, 0, nb, s_len, 0, emit_state=True)
            (hs_lat,) = _mlstm(z, gates, gates_t, n_ctx, nbd, t_len, j,
                               init=(state_mlstm_C, state_mlstm_n, state_mlstm_m))
            new_c.append(cc[:, 0])
            new_n.append(cn[:, 0])
            new_m.append(cm[:, 0, :, 0, :])
            h = _even_out(h, mod3, l, hs_ctx, hs_lat, z, mlstm_gain[j], pool_w[j].astype(BF16), pool_scale[j],
                          w_out_even[j].astype(BF16), n_ctx_tiles, tiles_per_latent, s_len, t_len)
        else:
            lam_init = 0.8 - 0.6 * math.exp(-0.3 * l)
            (z,) = _inproj(h, norm_mix[l], mod3, l, 0, w_in_odd[j].astype(BF16), n_ctx_tiles, tiles_per_latent,
                           name="inproj_odd")
            gain2 = jnp.tile(qk_gain[j], (1, 2))
            cache_shape = (nb, 1, DA_HEADS, s_len, 2 * DA_DH)
            qn_c, kn_c, vb_c, nk, nv = _qk(z, gain2, 0, n_ctx, 0, cache_shape=cache_shape)
            qn_l, kn_l, vb_l = _qk(z, gain2, n_ctx, n_lat, 0, rope_tables=rope, seqlen=t_len)
            new_k.append(nk[:, 0])
            new_v.append(nv[:, 0])
            a_ctx = _attn(qn_c, kn_c, vb_c, da_lambda, da_subln, lam_init, nb, s_len, s_len, j)
            a_lat = _attn(qn_l, kn_l, vb_l, da_lambda, da_subln, lam_init, nbd, t_len, tm, j,
                          caches=(cache_da_k, cache_da_v))
            h = _odd_out(h, mod3, l, a_ctx, a_lat, z, gm_ws[j].astype(BF16), gm_b[j].T, w_out_odd[j].astype(BF16),
                         n_ctx_tiles, tiles_per_latent)
        qp, xnt = _inproj(h, norm_ffn[l], mod3, l, 3, peer_wq[l].astype(BF16), n_ctx_tiles, tiles_per_latent,
                          emit_xn=True, name="inproj_peer")
        sk = peer_subkeys[l].reshape(2 * PEER_HEADS, PEER_NKEYS, PEER_NKEYS).astype(BF16)
        rank2, e2, cnt, cw = _peer_route(qp, sk)
        h = _peer_dense(xnt, h, mod3, l, _pack_rows(peer_u[l]), _pack_rows(peer_v[l].T), rank2, e2, cnt, cw,
                        n_ctx_tiles, tiles_per_latent)

    return (h[:n_ctx].reshape(nb, s_len, d), h[n_ctx:].reshape(nbd, t_len, d),
            jnp.stack(new_c, axis=1), jnp.stack(new_n, axis=1), jnp.stack(new_m, axis=1),
            jnp.stack(new_k, axis=1), jnp.stack(new_v, axis=1))
```

```python
import functools
import math

import jax
import jax.numpy as jnp
from jax import lax
from jax.experimental import pallas as pl
from jax.experimental.pallas import tpu as pltpu

F32 = jnp.float32
BF16 = jnp.bfloat16

N_MOD = 6
EPS = 1e-6
TOKEN_TILE = 256
LANES = 128
SUBLANES = 8
VMEM_LIMIT_BYTES = 56 * 1024 * 1024

MLSTM_HEADS = 4
MLSTM_DH = 256
MLSTM_CHUNK = 128
POOL_WINDOWS = (2, 4, 8, 16)
POOL_HALO = 8
DA_HEADS = 8
DA_DH = 64
GRID_W = 64
ROPE_BASE = 10000.0
GM_GROUPS = 8
GM_CHUNK = 128
PEER_HEADS = 8
PEER_NKEYS = 128
PEER_TOPK = 16
NEG_INF = float("-inf")
POS_INF = float("inf")


def _params(*sem):
    return pltpu.CompilerParams(dimension_semantics=sem, vmem_limit_bytes=VMEM_LIMIT_BYTES)


def _dot(a, b):
    return jnp.dot(a, b, preferred_element_type=F32)


def _dot_nt(a, b):
    return lax.dot_general(a, b, (((1,), (1,)), ((), ())), preferred_element_type=F32)


def _dot_tn(a, b):
    return lax.dot_general(a, b, (((0,), (0,)), ((), ())), preferred_element_type=F32)


def _split2(x):
    hi = x.astype(BF16)
    lo = (x - hi.astype(F32)).astype(BF16)
    return hi, lo


def _split3(x):
    hi = x.astype(BF16)
    r = x - hi.astype(F32)
    mid = r.astype(BF16)
    lo = (r - mid.astype(F32)).astype(BF16)
    return hi, mid, lo


def _dot3(a, b, dot=_dot):
    ah, al = _split2(a)
    bh, bl = _split2(b)
    return dot(ah, bh) + (dot(ah, bl) + dot(al, bh))


def _gelu_tanh(x):
    return 0.5 * x * (1.0 + jnp.tanh(math.sqrt(2.0 / math.pi) * (x + 0.044715 * (x * x * x))))


def _log_sigmoid(x):
    return -(jnp.maximum(-x, 0.0) + jnp.log1p(jnp.exp(-jnp.abs(x))))


def _mod_row(i, n_ctx_tiles, tiles_per_latent):
    return jnp.where(i < n_ctx_tiles, 0, 1 + (i - n_ctx_tiles) // tiles_per_latent)


def _pack_kernel(x_ref, o_ref, *, transpose):
    x = x_ref[...]
    if transpose:
        x = x.T
    o_ref[...] = pltpu.bitcast(x.astype(BF16), F32)


def _pack_rows(x, ncols=None, transpose=False):
    r, c = x.shape
    c = c if ncols is None else ncols
    tr, tc = 512, 1024
    assert r % tr == 0 and c % tc == 0
    if transpose:
        out_shape, out_spec = (c // 2, r), pl.BlockSpec((tc // 2, tr), lambda i, j: (j, i))
    else:
        out_shape, out_spec = (r // 2, c), pl.BlockSpec((tr // 2, tc), lambda i, j: (i, j))
    return pl.pallas_call(
        functools.partial(_pack_kernel, transpose=transpose),
        out_shape=jax.ShapeDtypeStruct(out_shape, F32),
        grid=(r // tr, c // tc),
        in_specs=[pl.BlockSpec((tr, tc), lambda i, j: (i, j))],
        out_specs=out_spec,
        compiler_params=_params("arbitrary", "arbitrary"),
        name="pack_t" if transpose else "pack",
    )(x)


def _adaln_kernel(cond_ref, w_ref, b_ref, o_ref):
    c = cond_ref[...]
    s = c * jax.nn.sigmoid(c)
    o_ref[...] = _dot3(s, w_ref[...]) + b_ref[...]


def _adaln(cond8, w_mod, b_mod):
    depth, d, dout = w_mod.shape
    tn = 1024
    return pl.pallas_call(
        _adaln_kernel,
        out_shape=jax.ShapeDtypeStruct((depth, 8, dout), F32),
        grid=(depth, dout // tn),
        in_specs=[
            pl.BlockSpec((8, d), lambda l, j: (0, 0)),
            pl.BlockSpec((None, d, tn), lambda l, j: (l, 0, j)),
            pl.BlockSpec((None, 1, tn), lambda l, j: (l, 0, j)),
        ],
        out_specs=pl.BlockSpec((None, 8, tn), lambda l, j: (l, 0, j)),
        compiler_params=_params("arbitrary", "arbitrary"),
        name="adaln",
    )(cond8, w_mod, b_mod.reshape(depth, 1, dout))


def _stream(h, tm, cols, col_of=lambda j: 0):
    if not isinstance(h, tuple):
        return [h], [pl.BlockSpec((tm, cols), lambda i, j: (i, col_of(j)))], h.shape[0], None
    hc, hl = h
    nct = hc.shape[0] // tm
    specs = [pl.BlockSpec((tm, cols), lambda i, j: (jnp.minimum(i, nct - 1), col_of(j))),
             pl.BlockSpec((tm, cols), lambda i, j: (jnp.maximum(i - nct, 0), col_of(j)))]
    return [hc, hl], specs, hc.shape[0] + hl.shape[0], nct


def _load_stream(h_refs, split_tiles):
    if split_tiles is None:
        return h_refs[0][...]
    return jnp.where(pl.program_id(0) < split_tiles, h_refs[0][...], h_refs[1][...])


def _inproj_kernel(*refs, with_gates, emit_xn, split_tiles):
    nh = 1 if split_tiles is None else 2
    h_refs = refs[:nh]
    gain_ref, shift_ref, scale_ref, w_ref = refs[nh:nh + 4]
    rest = list(refs[nh + 4:])
    if with_gates:
        wg_ref, wgt_ref, bg_ref, bgt_ref = rest[:4]
        rest = rest[4:]
    z_ref = rest.pop(0)
    if with_gates:
        g_ref, gt_ref = rest[:2]
        rest = rest[2:]
    if emit_xn:
        xn_out_ref = rest.pop(0)
    xn_scr = rest.pop(0)

    @pl.when(pl.program_id(1) == 0)
    def _():
        x = _load_stream(h_refs, split_tiles)
        xn = x * lax.rsqrt(jnp.mean(x * x, axis=-1, keepdims=True) + EPS) * gain_ref[...]
        xn = xn * (1.0 + scale_ref[...]) + shift_ref[...]
        xb = xn.astype(BF16)
        xn_scr[...] = xb
        if emit_xn:
            xn_out_ref[...] = pltpu.bitcast(xn.T.astype(BF16), F32)
        if with_gates:
            g_ref[...] = _dot3(xn, wg_ref[...]) + bg_ref[...]
            gt_ref[...] = _dot3(wgt_ref[...], xn, dot=_dot_nt) + bgt_ref[...]

    z_ref[...] = _dot(xn_scr[...], pltpu.bitcast(w_ref[...], BF16))


def _inproj(h, gain, mod3, layer, mod_base, w_packed, n_ctx_tiles, tiles_per_latent, w_gates=None, b_gates=None,
            emit_xn=False, tn=1024, name="inproj"):
    d, dout = 2 * w_packed.shape[0], w_packed.shape[1]
    tm = TOKEN_TILE * math.gcd(2 if isinstance(h, tuple) else 4, n_ctx_tiles, tiles_per_latent)
    per = tm // TOKEN_TILE
    with_gates = w_gates is not None
    h_arrays, h_specs, n, split_tiles = _stream(h, tm, d)

    def mod_idx(off):
        return lambda i, j: ((layer * 8 + _mod_row(i * per, n_ctx_tiles, tiles_per_latent)) * N_MOD + mod_base + off,
                             0, 0)

    in_specs = h_specs + [
        pl.BlockSpec((1, d), lambda i, j: (0, 0)),
        pl.BlockSpec((None, 1, d), mod_idx(0)),
        pl.BlockSpec((None, 1, d), mod_idx(1)),
        pl.BlockSpec((d // 2, tn), lambda i, j: (0, j)),
    ]
    args = h_arrays + [gain.reshape(1, d), mod3, mod3, w_packed]
    out_shape = [jax.ShapeDtypeStruct((n, dout), F32)]
    out_specs = [pl.BlockSpec((tm, tn), lambda i, j: (i, j))]
    if with_gates:
        ng = w_gates.shape[1]
        in_specs += [pl.BlockSpec((d, ng), lambda i, j: (0, 0)), pl.BlockSpec((ng, d), lambda i, j: (0, 0)),
                     pl.BlockSpec((1, ng), lambda i, j: (0, 0)), pl.BlockSpec((ng, 1), lambda i, j: (0, 0))]
        args += [w_gates, w_gates.T, b_gates.reshape(1, ng), b_gates.reshape(ng, 1)]
        out_shape += [jax.ShapeDtypeStruct((n, ng), F32), jax.ShapeDtypeStruct((ng, n), F32)]
        out_specs += [pl.BlockSpec((tm, ng), lambda i, j: (i, 0)), pl.BlockSpec((ng, tm), lambda i, j: (0, i))]
    if emit_xn:
        out_shape.append(jax.ShapeDtypeStruct((d // 2, n), F32))
        out_specs.append(pl.BlockSpec((d // 2, tm), lambda i, j: (0, i)))
    return pl.pallas_call(
        functools.partial(_inproj_kernel, with_gates=with_gates, emit_xn=emit_xn, split_tiles=split_tiles),
        out_shape=out_shape,
        grid=(n // tm, dout // tn),
        in_specs=in_specs,
        out_specs=out_specs,
        scratch_shapes=[pltpu.VMEM((tm, d), BF16)],
        compiler_params=_params("arbitrary", "arbitrary"),
        name=name,
    )(*args)


def _mlstm_kernel(*refs, has_init, emit_state, nchunks):
    q_ref, k_ref, v_ref, g_ref, gt_ref = refs[:5]
    rest = list(refs[5:])
    if has_init:
        c0_ref, n0_ref, m0_ref = rest[:3]
        rest = rest[3:]
    hs_ref = rest.pop(0)
    if emit_state:
        cout_ref, nout_ref, mout_ref = rest[:3]
        rest = rest[3:]
    c_scr, n_scr, m_scr = rest
    nh, dh, L = MLSTM_HEADS, MLSTM_DH, MLSTM_CHUNK
    d = pl.program_id(0)
    s = pl.program_id(2)

    @pl.when(s == 0)
    def _():
        if has_init:
            c_scr[...] = c0_ref[...]
            n_scr[...] = n0_ref[...]
            m_scr[...] = m0_ref[...]
        else:
            c_scr[...] = jnp.zeros_like(c_scr)
            n_scr[...] = jnp.zeros_like(n_scr)
            m_scr[...] = jnp.zeros_like(m_scr)

    row = lax.broadcasted_iota(jnp.int32, (L, L), 0)
    col = lax.broadcasted_iota(jnp.int32, (L, L), 1)
    sgn = jnp.where(d == 0, 1, -1)
    mask = (row - col) * sgn >= 0
    maskb = jnp.where(mask, 1.0, 0.0).astype(BF16)

    g = g_ref[...]
    gt = gt_ref[...]
    fwd = d == 0
    i_col = jnp.where(fwd, g[:, 0:nh], g[:, nh:2 * nh])
    f_col = _log_sigmoid(jnp.where(fwd, g[:, 2 * nh:3 * nh], g[:, 3 * nh:4 * nh]))
    i_row = jnp.where(fwd, gt[0:nh], gt[nh:2 * nh])
    f_row = _log_sigmoid(jnp.where(fwd, gt[2 * nh:3 * nh], gt[3 * nh:4 * nh]))
    fc = _split3(f_col)
    b_col = _dot(maskb, fc[0]) + (_dot(maskb, fc[1]) + _dot(maskb, fc[2]))
    fr = _split3(f_row)
    b_row = _dot_nt(fr[0], maskb) + (_dot_nt(fr[1], maskb) + _dot_nt(fr[2], maskb))
    btot_col = jnp.sum(f_col, axis=0, keepdims=True)
    m_all = m_scr[...]

    m_new_parts = []
    for h in range(nh):
        sl = slice(h * dh, (h + 1) * dh)
        qh = q_ref[:, sl].astype(BF16)
        kf = k_ref[:, sl] * (dh ** -0.5)
        kh = kf.astype(BF16)
        vh = v_ref[:, sl].astype(BF16)
        b_c = b_col[:, h:h + 1]
        b_r = b_row[h:h + 1, :]
        i_c = i_col[:, h:h + 1]
        i_r = i_row[h:h + 1, :]
        m = m_all[:, h:h + 1]
        btot = btot_col[:, h:h + 1]

        dm = jnp.where(mask, b_c - b_r + i_r, NEG_INF)
        inter = b_c + m
        m_t = jnp.maximum(inter, jnp.max(dm, axis=1, keepdims=True))
        w = jnp.exp(dm - m_t)
        a = jnp.exp(inter - m_t)
        sc = _dot_nt(qh, kh) * w
        cb = c_scr[h].astype(BF16)
        num = a * _dot(qh, cb) + _dot(sc.astype(BF16), vh)
        nb = n_scr[h:h + 1, :].astype(BF16).astype(F32)
        qn = jnp.sum(qh.astype(F32) * nb, axis=1, keepdims=True)
        den = a * qn + jnp.sum(sc, axis=1, keepdims=True)
        hs_ref[:, sl] = num / jnp.maximum(jnp.abs(den), jnp.exp(-m_t))

        g_c = btot - b_c + i_c
        g_r = btot - b_r + i_r
        m_new = jnp.maximum(btot + m, jnp.max(g_c, axis=0, keepdims=True))
        decay = jnp.exp(btot + m - m_new)
        ws_c = jnp.exp(g_c - m_new)
        ws_r = jnp.exp(g_r - m_new)
        kw = (kf * ws_c).astype(BF16)
        c_scr[h] = decay * c_scr[h] + _dot_tn(kw, vh)
        n_scr[h:h + 1, :] = decay * n_scr[h:h + 1, :] + _dot(ws_r.astype(BF16), kh)
        m_new_parts.append(m_new)
    m_scr[...] = jnp.concatenate(m_new_parts, axis=1)

    if emit_state:
        @pl.when(s == nchunks - 1)
        def _():
            cout_ref[...] = c_scr[...]
            nout_ref[...] = n_scr[...]
            mout_ref[...] = m_scr[...]


def _mlstm(z, gates, gates_t, tok_off, nseq, seqlen, j, init=None, emit_state=False, n_even=1):
    nh, dh, L = MLSTM_HEADS, MLSTM_DH, MLSTM_CHUNK
    w = nh * dh
    nchunks = seqlen // L
    off = tok_off // L

    def chunk(d, b, s):
        return off + b * nchunks + jnp.where(d == 0, s, nchunks - 1 - s)

    in_specs = [
        pl.BlockSpec((L, w), lambda d, b, s: (chunk(d, b, s), 0)),
        pl.BlockSpec((L, w), lambda d, b, s: (chunk(d, b, s), 1)),
        pl.BlockSpec((L, w), lambda d, b, s: (chunk(d, b, s), 2)),
        pl.BlockSpec((L, 4 * nh), lambda d, b, s: (chunk(d, b, s), 0)),
        pl.BlockSpec((4 * nh, L), lambda d, b, s: (0, chunk(d, b, s))),
    ]
    args = [z, z, z, gates, gates_t]
    has_init = init is not None
    if has_init:
        c0, n0, m0 = init
        in_specs += [
            pl.BlockSpec((None, None, None, nh, dh, dh), lambda d, b, s: (b, j, d, 0, 0, 0)),
            pl.BlockSpec((None, None, None, nh, dh), lambda d, b, s: (b, j, d, 0, 0)),
            pl.BlockSpec((None, None, None, 1, nh), lambda d, b, s: (b, j, d, 0, 0)),
        ]
        args += [c0, n0, m0.reshape(m0.shape[:3] + (1, nh))]
    out_shape = [jax.ShapeDtypeStruct((2, nseq * seqlen, w), F32)]
    out_specs = [pl.BlockSpec((None, L, w), lambda d, b, s: (d, chunk(d, b, s) - off, 0))]
    if emit_state:
        out_shape += [
            jax.ShapeDtypeStruct((nseq, n_even, 2, nh, dh, dh), F32),
            jax.ShapeDtypeStruct((nseq, n_even, 2, nh, dh), F32),
            jax.ShapeDtypeStruct((nseq, n_even, 2, 1, nh), F32),
        ]
        out_specs += [
            pl.BlockSpec((None, None, None, nh, dh, dh), lambda d, b, s: (b, j, d, 0, 0, 0)),
            pl.BlockSpec((None, None, None, nh, dh), lambda d, b, s: (b, j, d, 0, 0)),
            pl.BlockSpec((None, None, None, 1, nh), lambda d, b, s: (b, j, d, 0, 0)),
        ]
    return pl.pallas_call(
        functools.partial(_mlstm_kernel, has_init=has_init, emit_state=emit_state, nchunks=nchunks),
        out_shape=out_shape,
        grid=(2, nseq, nchunks),
        in_specs=in_specs,
        out_specs=out_specs,
        scratch_shapes=[pltpu.VMEM((nh, dh, dh), F32), pltpu.VMEM((nh, dh), F32), pltpu.VMEM((1, nh), F32)],
        compiler_params=_params("arbitrary", "arbitrary", "arbitrary"),
        name="mlstm_ctx" if emit_state else "mlstm_lat",
    )(*args)


def _even_out_kernel(*refs, n_ctx_tiles, tiles_per_latent, ctx_len, lat_len, split_tiles):
    nhr = 1 if split_tiles is None else 2
    h_refs = refs[:nhr]
    (gate_ref, hsc_ref, hsl_ref, o_ref, p_ref, pprev_ref, pnext_ref, gain_ref, pw_ref, ps_ref, w_ref, out_ref,
     cat_scr) = refs[nhr:]
    i = pl.program_id(0)
    tm = TOKEN_TILE
    nh, dh = MLSTM_HEADS, MLSTM_DH
    wm = nh * dh

    @pl.when(pl.program_id(1) == 0)
    def _():
        is_ctx = i < n_ctx_tiles
        hs = jnp.where(is_ctx, hsc_ref[0] + hsc_ref[1], hsl_ref[0] + hsl_ref[1])
        for h in range(nh):
            sl = slice(h * dh, (h + 1) * dh)
            x = hs[:, sl]
            y = x * lax.rsqrt(jnp.mean(x * x, axis=-1, keepdims=True) + EPS) * gain_ref[:, sl]
            cat_scr[:, sl] = (y * jax.nn.sigmoid(o_ref[:, sl])).astype(BF16)

        tiles_ctx = ctx_len // tm
        pos = jnp.where(is_ctx, i % tiles_ctx, (i - n_ctx_tiles) % tiles_per_latent)
        ntile = jnp.where(is_ctx, tiles_ctx, tiles_per_latent)
        seqlen = jnp.where(is_ctx, ctx_len, lat_len)
        x = p_ref[...]
        prev = jnp.where(pos > 0, pprev_ref[...], 0.0)
        nxt = jnp.where(pos < ntile - 1, pnext_ref[...], 0.0)
        pad = jnp.zeros((LANES - 2 * POOL_HALO, x.shape[1]), F32)
        xcat = jnp.concatenate([prev, x, nxt, pad], axis=0)
        xh, xl = _split2(xcat)
        t = lax.broadcasted_iota(jnp.int32, (tm, tm + LANES), 0)
        sidx = lax.broadcasted_iota(jnp.int32, (tm, tm + LANES), 1) - POOL_HALO
        tpos = pos * tm + lax.broadcasted_iota(jnp.int32, (tm, 1), 0)
        gw = wm // len(POOL_WINDOWS)
        for gi, win in enumerate(POOL_WINDOWS):
            sl = slice(gi * gw, (gi + 1) * gw)
            band = jnp.where((sidx >= t - win // 2) & (sidx < t - win // 2 + win), 1.0, 0.0).astype(BF16)
            lo = jnp.maximum(tpos - win // 2, 0)
            hi = jnp.minimum(tpos - win // 2 + win, seqlen)
            cnt = (hi - lo).astype(F32)
            p = (_dot(band, xh[:, sl]) + _dot(band, xl[:, sl])) / cnt - x[:, sl]
            y = _dot(p.astype(BF16), pw_ref[gi]) * ps_ref[:, sl]
            cat_scr[:, wm + gi * gw:wm + (gi + 1) * gw] = y.astype(BF16)

    out_ref[...] = (_load_stream(h_refs, split_tiles)
                    + gate_ref[...] * _dot(cat_scr[...], pltpu.bitcast(w_ref[...], BF16)))


def _even_out(h, mod3, layer, hs_ctx, hs_lat, z, gain, pool_w_bf16, pool_scale, w_out_bf16, n_ctx_tiles,
              tiles_per_latent, ctx_len, lat_len, tn=2048):
    tm = TOKEN_TILE
    d = 2 * w_out_bf16.shape[0]
    h_arrays, h_specs, n, split_tiles = _stream(h, tm, tn, col_of=lambda j: j)
    wm = MLSTM_HEADS * MLSTM_DH
    nlt = n // tm - n_ctx_tiles
    rows8 = n // POOL_HALO
    per = tm // POOL_HALO

    def gate_idx(i, j):
        return ((layer * 8 + _mod_row(i, n_ctx_tiles, tiles_per_latent)) * N_MOD + 2, 0, j)

    return pl.pallas_call(
        functools.partial(_even_out_kernel, n_ctx_tiles=n_ctx_tiles, tiles_per_latent=tiles_per_latent,
                          ctx_len=ctx_len, lat_len=lat_len, split_tiles=split_tiles),
        out_shape=jax.ShapeDtypeStruct((n, d), F32),
        grid=(n // tm, d // tn),
        in_specs=h_specs + [
            pl.BlockSpec((None, 1, tn), gate_idx),
            pl.BlockSpec((2, tm, wm), lambda i, j: (0, jnp.minimum(i, n_ctx_tiles - 1), 0)),
            pl.BlockSpec((2, tm, wm), lambda i, j: (0, jnp.clip(i - n_ctx_tiles, 0, nlt - 1), 0)),
            pl.BlockSpec((tm, wm), lambda i, j: (i, 3)),
            pl.BlockSpec((tm, wm), lambda i, j: (i, 4)),
            pl.BlockSpec((POOL_HALO, wm), lambda i, j: (jnp.maximum(i * per - 1, 0), 4)),
            pl.BlockSpec((POOL_HALO, wm), lambda i, j: (jnp.minimum((i + 1) * per, rows8 - 1), 4)),
            pl.BlockSpec((1, wm), lambda i, j: (0, 0)),
            pl.BlockSpec(pool_w_bf16.shape, lambda i, j: (0, 0, 0)),
            pl.BlockSpec((1, wm), lambda i, j: (0, 0)),
            pl.BlockSpec((d // 2, tn), lambda i, j: (0, j)),
        ],
        out_specs=pl.BlockSpec((tm, tn), lambda i, j: (i, j)),
        scratch_shapes=[pltpu.VMEM((tm, d), BF16)],
        compiler_params=_params("arbitrary", "arbitrary"),
        name="even_out",
    )(*h_arrays, mod3, hs_ctx, hs_lat, z, z, z, z, gain.reshape(1, wm), pool_w_bf16, pool_scale.reshape(1, wm),
      w_out_bf16)


def _qk_kernel(*refs, rope, emit_cache):
    q_ref, k_ref, v_ref, gain_ref = refs[:4]
    rest = list(refs[4:])
    if rope:
        cos_ref, sin_ref = rest[:2]
        rest = rest[2:]
    qn_ref, kn_ref, vb_ref = rest[:3]
    rest = rest[3:]
    lane = lax.broadcasted_iota(jnp.int32, q_ref.shape, 1)
    first = lane < DA_DH

    def norm(x, gain):
        sq = x * x
        s1 = jnp.sum(jnp.where(first, sq, 0.0), axis=-1, keepdims=True)
        s2 = jnp.sum(jnp.where(first, 0.0, sq), axis=-1, keepdims=True)
        ms = jnp.where(first, s1, s2) * (1.0 / DA_DH)
        y = x * lax.rsqrt(ms + EPS) * gain
        if rope:
            quarter = DA_DH // 4
            partner = jnp.where((lane & quarter) == 0, pltpu.roll(y, LANES - quarter, 1), pltpu.roll(y, quarter, 1))
            y = y * cos_ref[...] + partner * sin_ref[...]
        return y

    qn = norm(q_ref[...], gain_ref[0:1, :])
    kn = norm(k_ref[...], gain_ref[1:2, :])
    qn_ref[...] = (qn * (DA_DH ** -0.5)).astype(BF16)
    kn_ref[...] = kn.astype(BF16)
    v = v_ref[...]
    vb_ref[...] = v.astype(BF16)
    if emit_cache:
        newk_ref, newv_ref = rest
        newk_ref[...] = kn
        newv_ref[...] = v


def _qk(z, qk_gain2, tok_off, ntok, j, rope_tables=None, cache_shape=None, seqlen=None):
    tm = TOKEN_TILE
    hd = 2 * DA_DH
    w = DA_HEADS * hd
    off = tok_off // tm
    nblk_w = w // hd
    rope = rope_tables is not None
    emit_cache = cache_shape is not None
    in_specs = [
        pl.BlockSpec((tm, hd), lambda i, h: (off + i, h)),
        pl.BlockSpec((tm, hd), lambda i, h: (off + i, nblk_w + h)),
        pl.BlockSpec((tm, hd), lambda i, h: (off + i, 2 * nblk_w + h)),
        pl.BlockSpec((2, hd), lambda i, h: (0, 0)),
    ]
    args = [z, z, z, qk_gain2]
    if rope:
        tps = seqlen // tm
        in_specs += [pl.BlockSpec((tm, hd), lambda i, h: (i % tps, 0))] * 2
        args += list(rope_tables)
    out_shape = [jax.ShapeDtypeStruct((ntok, w), BF16)] * 3
    out_specs = [pl.BlockSpec((tm, hd), lambda i, h: (i, h))] * 3
    if emit_cache:
        out_shape += [jax.ShapeDtypeStruct(cache_shape, F32)] * 2
        out_specs += [pl.BlockSpec((None, None, None, tm, hd), lambda i, h: (i, j, h, 0, 0))] * 2
    return pl.pallas_call(
        functools.partial(_qk_kernel, rope=rope, emit_cache=emit_cache),
        out_shape=out_shape,
        grid=(ntok // tm, DA_HEADS),
        in_specs=in_specs,
        out_specs=out_specs,
        compiler_params=_params("arbitrary", "arbitrary"),
        name="qk_ctx" if emit_cache else "qk_lat",
    )(*args)


def _rope_tables(t):
    rows = t // GRID_W
    row = jnp.repeat(jnp.arange(rows), GRID_W).astype(F32)
    col = (jnp.arange(rows * GRID_W) % GRID_W).astype(F32)
    n_freq = DA_DH // 4
    inv = ROPE_BASE ** (-jnp.arange(n_freq, dtype=F32) / n_freq)
    ar, ac = row[:, None] * inv, col[:, None] * inv
    cos = jnp.concatenate([jnp.cos(ar), jnp.cos(ar), jnp.cos(ac), jnp.cos(ac)], axis=-1)
    sin = jnp.concatenate([-jnp.sin(ar), jnp.sin(ar), -jnp.sin(ac), jnp.sin(ac)], axis=-1)
    return jnp.tile(cos, (1, 2)), jnp.tile(sin, (1, 2))


def _attn_kernel(*refs, n_pieces, lam_init):
    q_ref, lp_ref, subln_ref = refs[:3]
    kv = refs[3:3 + 2 * n_pieces]
    o_ref = refs[3 + 2 * n_pieces]
    lp = lp_ref[...]
    lam = (jnp.exp(jnp.sum(lp[0:1] * lp[1:2], axis=-1, keepdims=True))
           - jnp.exp(jnp.sum(lp[2:3] * lp[3:4], axis=-1, keepdims=True)) + lam_init)
    q = q_ref[...]
    lane = lax.broadcasted_iota(jnp.int32, q.shape, 1)
    zero = jnp.zeros_like(q)
    qa = jnp.where(lane < DA_DH, q, zero)
    qb = jnp.where(lane < DA_DH, zero, q)
    ks = [kv[2 * p][...].astype(BF16) for p in range(n_pieces)]
    vs = [kv[2 * p + 1][...].astype(BF16) for p in range(n_pieces)]
    sa = [_dot_nt(qa, k) for k in ks]
    sb = [_dot_nt(qb, k) for k in ks]

    def softmax_parts(parts):
        m = functools.reduce(jnp.maximum, [jnp.max(x, axis=-1, keepdims=True) for x in parts])
        es = [jnp.exp(x - m) for x in parts]
        tot = functools.reduce(lambda a, b: a + b, [jnp.sum(e, axis=-1, keepdims=True) for e in es])
        return [e / tot for e in es]

    pa = softmax_parts(sa)
    pb = softmax_parts(sb)
    o = None
    for p in range(n_pieces):
        contrib = _dot((pa[p] - lam * pb[p]).astype(BF16), vs[p])
        o = contrib if o is None else o + contrib
    o = o * lax.rsqrt(jnp.mean(o * o, axis=-1, keepdims=True) + EPS) * subln_ref[...]
    o_ref[...] = (o * (1.0 - lam_init)).astype(BF16)


def _attn(qn, kn, vb, lam_params, subln, lam_init, nseq, seqlen, tq, j, caches=None):
    hd = 2 * DA_DH
    nq = seqlen // tq
    in_specs = [
        pl.BlockSpec((tq, hd), lambda b, h, qi: (b * nq + qi, h)),
        pl.BlockSpec((None, 4, DA_DH), lambda b, h, qi: (j, 0, 0)),
        pl.BlockSpec((None, 1, hd), lambda b, h, qi: (j, 0, 0)),
        pl.BlockSpec((seqlen, hd), lambda b, h, qi: (b, h)),
        pl.BlockSpec((seqlen, hd), lambda b, h, qi: (b, h)),
    ]
    args = [qn, lam_params, subln.reshape(subln.shape[0], 1, hd), kn, vb]
    n_pieces = 1
    if caches is not None:
        ck, cv = caches
        past = ck.shape[3]
        in_specs += [pl.BlockSpec((None, None, None, past, hd), lambda b, h, qi: (b, j, h, 0, 0))] * 2
        args += [ck, cv]
        n_pieces = 2
    return pl.pallas_call(
        functools.partial(_attn_kernel, n_pieces=n_pieces, lam_init=lam_init),
        out_shape=jax.ShapeDtypeStruct(qn.shape, BF16),
        grid=(nseq, DA_HEADS, nq),
        in_specs=in_specs,
        out_specs=pl.BlockSpec((tq, hd), lambda b, h, qi: (b * nq + qi, h)),
        compiler_params=_params("arbitrary", "arbitrary", "arbitrary"),
        name="attn_lat" if caches is not None else "attn_ctx",
    )(*args)


def _odd_out_kernel(h_ref, gate_ref, ac_ref, al_ref, gu_ref, gv_ref, ws_ref, bt_ref, w_ref, out_ref, cat_scr, *,
                    n_ctx_tiles):
    i = pl.program_id(0)
    tm = TOKEN_TILE
    wa = DA_HEADS * 2 * DA_DH
    gw = LANES

    @pl.when(pl.program_id(1) == 0)
    def _():
        cat_scr[:, 0:wa] = jnp.where(i < n_ctx_tiles, ac_ref[...], al_ref[...])
        for c in range(tm // GM_CHUNK):
            rows = slice(c * GM_CHUNK, (c + 1) * GM_CHUNK)
            for g in range(GM_GROUPS):
                cols = slice(g * gw, (g + 1) * gw)
                u = _gelu_tanh(gu_ref[rows, cols])
                v = _gelu_tanh(gv_ref[rows, cols])
                vn = v * lax.rsqrt(jnp.mean(v * v, axis=-1, keepdims=True) + EPS)
                mixed = _dot(ws_ref[g], vn.astype(BF16)) + bt_ref[:, g:g + 1]
                cat_scr[rows, wa + g * gw:wa + (g + 1) * gw] = (u * mixed).astype(BF16)

    out_ref[...] = h_ref[...] + gate_ref[...] * _dot(cat_scr[...], pltpu.bitcast(w_ref[...], BF16))


def _odd_out(h, mod3, layer, a_ctx, a_lat, z, gm_ws_bf16, gm_b_t, w_out_bf16, n_ctx_tiles, tiles_per_latent, tn=2048):
    n, d = h.shape
    tm = TOKEN_TILE
    wa = DA_HEADS * 2 * DA_DH
    wg = GM_GROUPS * LANES
    nlt = n // tm - n_ctx_tiles

    def gate_idx(i, j):
        return ((layer * 8 + _mod_row(i, n_ctx_tiles, tiles_per_latent)) * N_MOD + 2, 0, j)

    return pl.pallas_call(
        functools.partial(_odd_out_kernel, n_ctx_tiles=n_ctx_tiles),
        out_shape=jax.ShapeDtypeStruct((n, d), F32),
        grid=(n // tm, d // tn),
        in_specs=[
            pl.BlockSpec((tm, tn), lambda i, j: (i, j)),
            pl.BlockSpec((None, 1, tn), gate_idx),
            pl.BlockSpec((tm, wa), lambda i, j: (jnp.minimum(i, n_ctx_tiles - 1), 0)),
            pl.BlockSpec((tm, wa), lambda i, j: (jnp.clip(i - n_ctx_tiles, 0, nlt - 1), 0)),
            pl.BlockSpec((tm, wg), lambda i, j: (i, 3)),
            pl.BlockSpec((tm, wg), lambda i, j: (i, 4)),
            pl.BlockSpec(gm_ws_bf16.shape, lambda i, j: (0, 0, 0)),
            pl.BlockSpec(gm_b_t.shape, lambda i, j: (0, 0)),
            pl.BlockSpec((d // 2, tn), lambda i, j: (0, j)),
        ],
        out_specs=pl.BlockSpec((tm, tn), lambda i, j: (i, j)),
        scratch_shapes=[pltpu.VMEM((tm, d), BF16)],
        compiler_params=_params("arbitrary", "arbitrary"),
        name="odd_out",
    )(h, mod3, a_ctx, a_lat, z, z, gm_ws_bf16, gm_b_t, w_out_bf16)


def _top_values(x, k, with_rank=False):
    vals = []
    cur = x
    rank = jnp.full(x.shape, float(k), F32) if with_rank else None
    for r in range(k):
        m = jnp.max(cur, axis=0, keepdims=True)
        vals.append(m)
        hit = cur == m
        if with_rank:
            rank = jnp.where(hit, float(r), rank)
        if r + 1 < k:
            cur = jnp.where(hit, NEG_INF, cur)
    vals = jnp.concatenate(vals, axis=0)
    return (vals, rank) if with_rank else vals


def _dup_bf16(x):
    b = pltpu.bitcast(x.astype(BF16).astype(F32), jnp.uint32)
    return pltpu.bitcast(b | (b >> 16), F32)


def _peer_route_kernel(q_ref, sk_ref, rank_ref, e2_ref, cnt_ref, c_ref):
    half = PEER_NKEYS
    k = PEER_TOPK
    for h in range(PEER_HEADS):
        q1 = q_ref[:, 2 * h * half:(2 * h + 1) * half].astype(BF16)
        q2 = q_ref[:, (2 * h + 1) * half:(2 * h + 2) * half].astype(BF16)
        s1 = _dot_nt(sk_ref[2 * h], q1)
        s2 = _dot_nt(sk_ref[2 * h + 1], q2)
        a = _top_values(s1, k)
        b, rank2 = _top_values(s2, k, with_rank=True)
        rows = [a[r:r + 1] + b[0:k // (r + 1)] for r in range(k)]
        nrows = sum(k // (r + 1) for r in range(k))
        rows.append(jnp.full((-nrows % SUBLANES, s1.shape[1]), NEG_INF, F32))
        top = _top_values(jnp.concatenate(rows, axis=0), k)
        tau = top[k - 1:k]
        z = jnp.sum(jnp.exp(top - top[0:1]), axis=0, keepdims=True)
        cnt = jnp.zeros(s1.shape, F32)
        for r in range(k):
            cnt = jnp.where(s1 + b[r:r + 1] >= tau, float(r + 1), cnt)
        rank_ref[h] = pltpu.bitcast(rank2.astype(BF16), F32)
        e2_ref[h] = pltpu.bitcast(jnp.exp(s2 - b[0:1]).astype(BF16), F32)
        cnt_ref[h] = _dup_bf16(cnt)
        c_ref[h] = _dup_bf16(jnp.exp(s1 - a[0:1]) / z)


def _peer_route(qp, subkeys_bf16, tm=512):
    n = qp.shape[0]
    def out(rows):
        return (jax.ShapeDtypeStruct((PEER_HEADS, rows, n), F32),
                pl.BlockSpec((PEER_HEADS, rows, tm), lambda i: (0, 0, i)))

    outs = [out(PEER_NKEYS // 2)] * 2 + [out(PEER_NKEYS)] * 2
    return pl.pallas_call(
        _peer_route_kernel,
        out_shape=[o[0] for o in outs],
        grid=(n // tm,),
        in_specs=[
            pl.BlockSpec((tm, qp.shape[1]), lambda i: (i, 0)),
            pl.BlockSpec(subkeys_bf16.shape, lambda i: (0, 0, 0)),
        ],
        out_specs=[o[1] for o in outs],
        compiler_params=_params("arbitrary"),
        name="peer_route",
    )(qp, subkeys_bf16)


def _peer_dense_kernel(xnt_ref, h_ref, gate_ref, u_ref, vt_ref, rank_ref, e2_ref, cnt_ref, c_ref, *rest, te, n_tiles,
                       split_tiles):
    n_out = 1 if split_tiles is None else 2
    out_refs = rest[:n_out]
    act0_scr, act1_scr, wa0_scr, wa1_scr, acc_scr = rest[n_out:]
    s = pl.program_id(1)
    nkeys = PEER_NKEYS
    per = te // nkeys
    tm = acc_scr.shape[1]
    half = tm // 2
    assert per % 4 == 0
    pack = 2 * SUBLANES

    @pl.when(s == 0)
    def _():
        acc_scr[...] = jnp.zeros_like(acc_scr)
        act1_scr[...] = jnp.zeros_like(act1_scr)
        wa0_scr[...] = jnp.zeros_like(wa0_scr)

    live = jnp.logical_and(s >= 1, s <= n_tiles)
    t2 = jnp.clip(s - 1, 0, n_tiles - 1)

    def step(act_w, act_r, wa_w, wa_r):
        def stage1(c):
            cols = slice(c * half, (c + 1) * half)
            act_w[:, cols] = _dot(pltpu.bitcast(u_ref[...], BF16),
                                  pltpu.bitcast(xnt_ref[:, cols], BF16))

        def stage3(c):
            cols = slice(c * half, (c + 1) * half)
            acc_scr[:, cols] += _dot(pltpu.bitcast(vt_ref[...], BF16), wa_r[:, cols])

        mxu_work = [functools.partial(stage1, 0), functools.partial(stage3, 0),
                    functools.partial(stage1, 1), functools.partial(stage3, 1)]
        stride = per // len(mxu_work)
        for ii in range(per):
            if ii % stride == 0:
                mxu_work[ii // stride]()
            r = t2 * per + ii
            cnt_rows = [jnp.where(live, cnt_ref[h, pl.ds(r, 1), :], 0.0) for h in range(PEER_HEADS)]
            c_rows = [c_ref[h, pl.ds(r, 1), :] for h in range(PEER_HEADS)]
            for lt in range(tm // LANES):
                lanes = slice(lt * LANES, (lt + 1) * LANES)
                cnts = [pltpu.bitcast(jnp.broadcast_to(x[:, lanes], (SUBLANES, LANES)), BF16) for x in cnt_rows]
                ccs = [pltpu.bitcast(jnp.broadcast_to(x[:, lanes], (SUBLANES, LANES)), BF16) for x in c_rows]
                for sb in range(nkeys // pack):
                    words = slice(sb * SUBLANES, (sb + 1) * SUBLANES)
                    rows = slice(ii * nkeys + sb * pack, ii * nkeys + (sb + 1) * pack)
                    w = None
                    for h in range(PEER_HEADS):
                        e2 = pltpu.bitcast(e2_ref[h, words, lanes], BF16)
                        rank = pltpu.bitcast(rank_ref[h, words, lanes], BF16)
                        wh = jnp.where(rank < cnts[h], e2 * ccs[h], jnp.zeros_like(e2))
                        w = wh if w is None else w + wh
                    wa_w[rows, lanes] = w * _gelu_tanh(act_r[rows, lanes]).astype(BF16)

    @pl.when(s % 2 == 0)
    def _():
        step(act0_scr, act1_scr, wa1_scr, wa0_scr)

    @pl.when(s % 2 == 1)
    def _():
        step(act1_scr, act0_scr, wa0_scr, wa1_scr)

    @pl.when(s == n_tiles + 1)
    def _():
        res = h_ref[...] + gate_ref[...] * acc_scr[...].T
        if split_tiles is None:
            out_refs[0][...] = res
        else:
            @pl.when(pl.program_id(0) < split_tiles)
            def _():
                out_refs[0][...] = res

            @pl.when(pl.program_id(0) >= split_tiles)
            def _():
                out_refs[1][...] = res


def _peer_dense(xnt, h, mod3, layer, u_packed, vt_packed, rank2, e2, cnt, c, n_ctx_tiles, tiles_per_latent, tm=512,
                te=512, split_out=False):
    n, d = h.shape
    n_tiles = 2 * u_packed.shape[0] // te
    per = tm // TOKEN_TILE
    if split_out:
        split_tiles = n_ctx_tiles // per
        n_ctx = split_tiles * tm
        out_shape = [jax.ShapeDtypeStruct((n_ctx, d), F32), jax.ShapeDtypeStruct((n - n_ctx, d), F32)]
        out_specs = [pl.BlockSpec((tm, d), lambda i, s: (jnp.minimum(i, split_tiles - 1), 0)),
                     pl.BlockSpec((tm, d), lambda i, s: (jnp.maximum(i - split_tiles, 0), 0))]
    else:
        split_tiles = None
        out_shape = jax.ShapeDtypeStruct((n, d), F32)
        out_specs = pl.BlockSpec((tm, d), lambda i, s: (i, 0))

    def gate_idx(i, s):
        return ((layer * 8 + _mod_row(i * per, n_ctx_tiles, tiles_per_latent)) * N_MOD + 5, 0, 0)

    rspec = pl.BlockSpec((PEER_HEADS, PEER_NKEYS, tm), lambda i, s: (0, 0, i))
    pspec = pl.BlockSpec((PEER_HEADS, PEER_NKEYS // 2, tm), lambda i, s: (0, 0, i))
    return pl.pallas_call(
        functools.partial(_peer_dense_kernel, te=te, n_tiles=n_tiles, split_tiles=split_tiles),
        out_shape=out_shape,
        grid=(n // tm, n_tiles + 2),
        in_specs=[
            pl.BlockSpec((d // 2, tm), lambda i, s: (0, i)),
            pl.BlockSpec((tm, d), lambda i, s: (i, 0), pipeline_mode=pl.Buffered(1)),
            pl.BlockSpec((None, 1, d), gate_idx),
            pl.BlockSpec((te // 2, d), lambda i, s: (jnp.minimum(s, n_tiles - 1), 0)),
            pl.BlockSpec((d // 2, te), lambda i, s: (0, jnp.clip(s - 2, 0, n_tiles - 1))),
            pspec, pspec, rspec, rspec,
        ],
        out_specs=out_specs,
        scratch_shapes=[pltpu.VMEM((te, tm), F32), pltpu.VMEM((te, tm), F32), pltpu.VMEM((te, tm), BF16),
                        pltpu.VMEM((te, tm), BF16), pltpu.VMEM((d, tm), F32)],
        compiler_params=_params("arbitrary", "arbitrary"),
        name="peer_dense",
    )(xnt, h, mod3, u_packed, vt_packed, rank2, e2, cnt, c)


def kernel(x_prompt, x_sample, state_mlstm_C, state_mlstm_n, state_mlstm_m, cache_da_k, cache_da_v, c, c_ctx, norm_mix, norm_ffn, w_mod, b_mod, w_in_even, b_gate_even, mlstm_gain, pool_w, pool_scale, w_out_even, w_in_odd, qk_gain, da_lambda, da_subln, gm_ws, gm_b, w_out_odd, peer_wq, peer_subkeys, peer_u, peer_v):
    nb, s_len, d = x_prompt.shape
    nbd, t_len, _ = x_sample.shape
    depth = w_mod.shape[0]
    tm = TOKEN_TILE
    assert s_len == tm and t_len % tm == 0 and nbd <= 7 and t_len % GRID_W == 0
    n_ctx = nb * s_len
    n_lat = nbd * t_len
    n_ctx_tiles = n_ctx // tm
    tiles_per_latent = t_len // tm
    n_even = (depth + 1) // 2
    n_odd = depth // 2

    h = (x_prompt.reshape(n_ctx, d), x_sample.reshape(n_lat, d))
    cond8 = jnp.concatenate([c_ctx[None], c, jnp.zeros((7 - nbd, d), F32)], axis=0)
    mod3 = _adaln(cond8, w_mod, b_mod).reshape(depth * 8 * N_MOD, 1, d)
    rope = _rope_tables(t_len)

    new_c, new_n, new_m, new_k, new_v = [], [], [], [], []
    wm = MLSTM_HEADS * MLSTM_DH
    for l in range(depth):
        j = l // 2
        if l % 2 == 0:
            w_in = w_in_even[j]
            n_main = 5 * wm
            z, gates, gates_t = _inproj(h, norm_mix[l], mod3, l, 0, _pack_rows(w_in, ncols=n_main), n_ctx_tiles,
                                        tiles_per_latent, w_gates=w_in[:, n_main:], b_gates=b_gate_even[j],
                                        name="inproj_even")
            hs_ctx, cc, cn, cm = _mlstm(z, gates, gates_t, 0, nb, s_len, 0, emit_state=True)
            (hs_lat,) = _mlstm(z, gates, gates_t, n_ctx, nbd, t_len, j,
                               init=(state_mlstm_C, state_mlstm_n, state_mlstm_m))
            new_c.append(cc[:, 0])
            new_n.append(cn[:, 0])
            new_m.append(cm[:, 0, :, 0, :])
            h = _even_out(h, mod3, l, hs_ctx, hs_lat, z, mlstm_gain[j], pool_w[j].astype(BF16), pool_scale[j],
                          _pack_rows(w_out_even[j]), n_ctx_tiles, tiles_per_latent, s_len, t_len)
        else:
            lam_init = 0.8 - 0.6 * math.exp(-0.3 * l)
            (z,) = _inproj(h, norm_mix[l], mod3, l, 0, _pack_rows(w_in_odd[j]), n_ctx_tiles, tiles_per_latent,
                           name="inproj_odd")
            gain2 = jnp.tile(qk_gain[j], (1, 2))
            cache_shape = (nb, 1, DA_HEADS, s_len, 2 * DA_DH)
            qn_c, kn_c, vb_c, nk, nv = _qk(z, gain2, 0, n_ctx, 0, cache_shape=cache_shape)
            qn_l, kn_l, vb_l = _qk(z, gain2, n_ctx, n_lat, 0, rope_tables=rope, seqlen=t_len)
            new_k.append(nk[:, 0])
            new_v.append(nv[:, 0])
            a_ctx = _attn(qn_c, kn_c, vb_c, da_lambda, da_subln, lam_init, nb, s_len, s_len, j)
            a_lat = _attn(qn_l, kn_l, vb_l, da_lambda, da_subln, lam_init, nbd, t_len, tm, j,
                          caches=(cache_da_k, cache_da_v))
            h = _odd_out(h, mod3, l, a_ctx, a_lat, z, gm_ws[j].astype(BF16), gm_b[j].T, _pack_rows(w_out_odd[j]),
                         n_ctx_tiles, tiles_per_latent)
        qp, xnt = _inproj(h, norm_ffn[l], mod3, l, 3, _pack_rows(peer_wq[l]), n_ctx_tiles, tiles_per_latent,
                          emit_xn=True, name="inproj_peer")
        sk = peer_subkeys[l].reshape(2 * PEER_HEADS, PEER_NKEYS, PEER_NKEYS).astype(BF16)
        rank2, e2, cnt, cw = _peer_route(qp, sk)
        h = _peer_dense(xnt, h, mod3, l, _pack_rows(peer_u[l]), _pack_rows(peer_v[l], transpose=True), rank2,
                        e2, cnt, cw,
                        n_ctx_tiles, tiles_per_latent, split_out=(l == depth - 1))

    return (h[0].reshape(nb, s_len, d), h[1].reshape(nbd, t_len, d),
            jnp.stack(new_c, axis=1), jnp.stack(new_n, axis=1), jnp.stack(new_m, axis=1),
            jnp.stack(new_k, axis=1), jnp.stack(new_v, axis=1))
```

```python
import functools
import math

import jax
import jax.numpy as jnp
from jax import lax
from jax.experimental import pallas as pl
from jax.experimental.pallas import tpu as pltpu

F32 = jnp.float32
BF16 = jnp.bfloat16

N_MOD = 6
EPS = 1e-6
TOKEN_TILE = 256
LANES = 128
SUBLANES = 8
VMEM_LIMIT_BYTES = 56 * 1024 * 1024

MLSTM_HEADS = 4
MLSTM_DH = 256
MLSTM_CHUNK = 128
POOL_WINDOWS = (2, 4, 8, 16)
POOL_HALO = 8
DA_HEADS = 8
DA_DH = 64
GRID_W = 64
ROPE_BASE = 10000.0
GM_GROUPS = 8
GM_CHUNK = 128
PEER_HEADS = 8
PEER_NKEYS = 128
PEER_TOPK = 16
NEG_INF = float("-inf")
POS_INF = float("inf")


def _params(*sem):
    return pltpu.CompilerParams(dimension_semantics=sem, vmem_limit_bytes=VMEM_LIMIT_BYTES)


def _dot(a, b):
    return jnp.dot(a, b, preferred_element_type=F32)


def _dot_nt(a, b):
    return lax.dot_general(a, b, (((1,), (1,)), ((), ())), preferred_element_type=F32)


def _dot_tn(a, b):
    return lax.dot_general(a, b, (((0,), (0,)), ((), ())), preferred_element_type=F32)


def _split2(x):
    hi = x.astype(BF16)
    lo = (x - hi.astype(F32)).astype(BF16)
    return hi, lo


def _split3(x):
    hi = x.astype(BF16)
    r = x - hi.astype(F32)
    mid = r.astype(BF16)
    lo = (r - mid.astype(F32)).astype(BF16)
    return hi, mid, lo


def _dot3(a, b, dot=_dot):
    ah, al = _split2(a)
    bh, bl = _split2(b)
    return dot(ah, bh) + (dot(ah, bl) + dot(al, bh))


def _gelu_tanh(x):
    c = math.sqrt(2.0 / math.pi)
    half = 0.5 * x
    return half + half * jnp.tanh(x * (c + (0.044715 * c) * (x * x)))


def _log_sigmoid(x):
    return -(jnp.maximum(-x, 0.0) + jnp.log1p(jnp.exp(-jnp.abs(x))))


def _mod_row(i, n_ctx_tiles, tiles_per_latent):
    return jnp.where(i < n_ctx_tiles, 0, 1 + (i - n_ctx_tiles) // tiles_per_latent)


def _pack_kernel(x_ref, o_ref, *, transpose):
    x = x_ref[...]
    if transpose:
        x = x.T
    o_ref[...] = pltpu.bitcast(x.astype(BF16), F32)


def _pack_rows(x, ncols=None, transpose=False):
    r, c = x.shape
    c = c if ncols is None else ncols
    tr, tc = 512, 1024
    assert r % tr == 0 and c % tc == 0
    if transpose:
        out_shape, out_spec = (c // 2, r), pl.BlockSpec((tc // 2, tr), lambda i, j: (j, i))
    else:
        out_shape, out_spec = (r // 2, c), pl.BlockSpec((tr // 2, tc), lambda i, j: (i, j))
    return pl.pallas_call(
        functools.partial(_pack_kernel, transpose=transpose),
        out_shape=jax.ShapeDtypeStruct(out_shape, F32),
        grid=(r // tr, c // tc),
        in_specs=[pl.BlockSpec((tr, tc), lambda i, j: (i, j))],
        out_specs=out_spec,
        compiler_params=_params("arbitrary", "arbitrary"),
        name="pack_t" if transpose else "pack",
    )(x)


def _adaln_kernel(cond_ref, w_ref, b_ref, o_ref):
    c = cond_ref[...]
    s = c * jax.nn.sigmoid(c)
    o_ref[...] = _dot3(s, w_ref[...]) + b_ref[...]


def _adaln(cond8, w_mod, b_mod):
    depth, d, dout = w_mod.shape
    tn = 1024
    return pl.pallas_call(
        _adaln_kernel,
        out_shape=jax.ShapeDtypeStruct((depth, 8, dout), F32),
        grid=(depth, dout // tn),
        in_specs=[
            pl.BlockSpec((8, d), lambda l, j: (0, 0)),
            pl.BlockSpec((None, d, tn), lambda l, j: (l, 0, j)),
            pl.BlockSpec((None, 1, tn), lambda l, j: (l, 0, j)),
        ],
        out_specs=pl.BlockSpec((None, 8, tn), lambda l, j: (l, 0, j)),
        compiler_params=_params("arbitrary", "arbitrary"),
        name="adaln",
    )(cond8, w_mod, b_mod.reshape(depth, 1, dout))


def _stream(h, tm, cols, col_of=lambda j: 0, single_buffer_split=False):
    if not isinstance(h, tuple):
        return [h], [pl.BlockSpec((tm, cols), lambda i, j: (i, col_of(j)))], h.shape[0], None
    hc, hl = h
    nct = hc.shape[0] // tm
    kw = dict(pipeline_mode=pl.Buffered(1)) if single_buffer_split else {}
    specs = [pl.BlockSpec((tm, cols), lambda i, j: (jnp.minimum(i, nct - 1), col_of(j)), **kw),
             pl.BlockSpec((tm, cols), lambda i, j: (jnp.maximum(i - nct, 0), col_of(j)), **kw)]
    return [hc, hl], specs, hc.shape[0] + hl.shape[0], nct


def _load_stream(h_refs, split_tiles):
    if split_tiles is None:
        return h_refs[0][...]
    return jnp.where(pl.program_id(0) < split_tiles, h_refs[0][...], h_refs[1][...])


def _inproj_kernel(*refs, with_gates, emit_xn, split_tiles):
    nh = 1 if split_tiles is None else 2
    h_refs = refs[:nh]
    gain_ref, shift_ref, scale_ref, w_ref = refs[nh:nh + 4]
    rest = list(refs[nh + 4:])
    if with_gates:
        wg2_ref, bg_ref = rest[:2]
        rest = rest[2:]
    z_ref = rest.pop(0)
    if with_gates:
        g_ref, gt_ref = rest[:2]
        rest = rest[2:]
    if emit_xn:
        xn_out_ref = rest.pop(0)
    xn_scr = rest.pop(0)

    @pl.when(pl.program_id(1) == 0)
    def _():
        x = _load_stream(h_refs, split_tiles)
        xn = x * lax.rsqrt(jnp.mean(x * x, axis=-1, keepdims=True) + EPS) * gain_ref[...]
        xn = xn * (1.0 + scale_ref[...]) + shift_ref[...]
        xb = xn.astype(BF16)
        xn_scr[...] = xb
        if emit_xn:
            xn_out_ref[...] = pltpu.bitcast(xn.T.astype(BF16), F32)
        if with_gates:
            ng = bg_ref.shape[1]
            w2 = wg2_ref[...]
            w2_hi = w2.astype(BF16)
            lane = lax.broadcasted_iota(jnp.int32, w2.shape, 1)
            w_cat = jnp.where(lane < ng, w2_hi, (w2 - w2_hi.astype(F32)).astype(BF16))
            xl = (xn - xb.astype(F32)).astype(BF16)
            p = _dot(xb, w_cat)
            g = p[:, :ng] + (p[:, ng:] + _dot(xl, w2_hi)[:, :ng]) + bg_ref[...]
            g_ref[...] = g
            eye = (lax.broadcasted_iota(jnp.int32, (ng, ng), 0)
                   == lax.broadcasted_iota(jnp.int32, (ng, ng), 1)).astype(F32).astype(BF16)
            g3 = _split3(g)
            gt_ref[...] = _dot_nt(eye, g3[0]) + (_dot_nt(eye, g3[1]) + _dot_nt(eye, g3[2]))

    z_ref[...] = _dot(xn_scr[...], pltpu.bitcast(w_ref[...], BF16))


def _inproj(h, gain, mod3, layer, mod_base, w_packed, n_ctx_tiles, tiles_per_latent, w_gates=None, b_gates=None,
            emit_xn=False, tn=1024, name="inproj"):
    d, dout = 2 * w_packed.shape[0], w_packed.shape[1]
    tm = TOKEN_TILE * math.gcd(4, n_ctx_tiles, tiles_per_latent)
    per = tm // TOKEN_TILE
    with_gates = w_gates is not None
    h_arrays, h_specs, n, split_tiles = _stream(h, tm, d, single_buffer_split=True)

    def mod_idx(off):
        return lambda i, j: ((layer * 8 + _mod_row(i * per, n_ctx_tiles, tiles_per_latent)) * N_MOD + mod_base + off,
                             0, 0)

    in_specs = h_specs + [
        pl.BlockSpec((1, d), lambda i, j: (0, 0)),
        pl.BlockSpec((None, 1, d), mod_idx(0)),
        pl.BlockSpec((None, 1, d), mod_idx(1)),
        pl.BlockSpec((d // 2, tn), lambda i, j: (0, j)),
    ]
    args = h_arrays + [gain.reshape(1, d), mod3, mod3, w_packed]
    out_shape = [jax.ShapeDtypeStruct((n, dout), F32)]
    out_specs = [pl.BlockSpec((tm, tn), lambda i, j: (i, j))]
    if with_gates:
        ng = w_gates.shape[1]
        in_specs += [pl.BlockSpec((d, 2 * ng), lambda i, j: (0, 0)), pl.BlockSpec((1, ng), lambda i, j: (0, 0))]
        args += [jnp.concatenate([w_gates, w_gates], axis=1), b_gates.reshape(1, ng)]
        out_shape += [jax.ShapeDtypeStruct((n, ng), F32), jax.ShapeDtypeStruct((ng, n), F32)]
        out_specs += [pl.BlockSpec((tm, ng), lambda i, j: (i, 0)), pl.BlockSpec((ng, tm), lambda i, j: (0, i))]
    if emit_xn:
        out_shape.append(jax.ShapeDtypeStruct((d // 2, n), F32))
        out_specs.append(pl.BlockSpec((d // 2, tm), lambda i, j: (0, i)))
    return pl.pallas_call(
        functools.partial(_inproj_kernel, with_gates=with_gates, emit_xn=emit_xn, split_tiles=split_tiles),
        out_shape=out_shape,
        grid=(n // tm, dout // tn),
        in_specs=in_specs,
        out_specs=out_specs,
        scratch_shapes=[pltpu.VMEM((tm, d), BF16)],
        compiler_params=_params("arbitrary", "arbitrary"),
        name=name,
    )(*args)


def _mlstm_kernel(*refs, has_init, emit_state, nchunks):
    q_ref, k_ref, v_ref, g_ref, gt_ref = refs[:5]
    rest = list(refs[5:])
    if has_init:
        c0_ref, n0_ref, m0_ref = rest[:3]
        rest = rest[3:]
    hs_ref = rest.pop(0)
    if emit_state:
        cout_ref, nout_ref, mout_ref = rest[:3]
        rest = rest[3:]
    c_scr, n_scr, m_scr = rest
    nh, dh, L = MLSTM_HEADS, MLSTM_DH, MLSTM_CHUNK
    d = pl.program_id(0)
    s = pl.program_id(2)

    @pl.when(s == 0)
    def _():
        if has_init:
            c_scr[...] = c0_ref[...]
            n_scr[...] = n0_ref[...]
            m_scr[...] = m0_ref[...]
        else:
            c_scr[...] = jnp.zeros_like(c_scr)
            n_scr[...] = jnp.zeros_like(n_scr)
            m_scr[...] = jnp.zeros_like(m_scr)

    row = lax.broadcasted_iota(jnp.int32, (L, L), 0)
    col = lax.broadcasted_iota(jnp.int32, (L, L), 1)
    sgn = jnp.where(d == 0, 1, -1)
    mask = (row - col) * sgn >= 0
    maskb = jnp.where(mask, 1.0, 0.0).astype(BF16)

    g = g_ref[...]
    gt = gt_ref[...]
    fwd = d == 0
    i_col = jnp.where(fwd, g[:, 0:nh], g[:, nh:2 * nh])
    f_col = _log_sigmoid(jnp.where(fwd, g[:, 2 * nh:3 * nh], g[:, 3 * nh:4 * nh]))
    i_row = jnp.where(fwd, gt[0:nh], gt[nh:2 * nh])
    f_row = _log_sigmoid(jnp.where(fwd, gt[2 * nh:3 * nh], gt[3 * nh:4 * nh]))
    fc = _split3(f_col)
    b_col = _dot(maskb, fc[0]) + (_dot(maskb, fc[1]) + _dot(maskb, fc[2]))
    fr = _split3(f_row)
    b_row = _dot_nt(fr[0], maskb) + (_dot_nt(fr[1], maskb) + _dot_nt(fr[2], maskb))
    btot_col = jnp.sum(f_col, axis=0, keepdims=True)
    m_all = m_scr[...]

    m_new_parts = []
    for h in range(nh):
        sl = slice(h * dh, (h + 1) * dh)
        qh = q_ref[:, sl].astype(BF16)
        kf = k_ref[:, sl] * (dh ** -0.5)
        kh = kf.astype(BF16)
        vh = v_ref[:, sl].astype(BF16)
        b_c = b_col[:, h:h + 1]
        b_r = b_row[h:h + 1, :]
        i_c = i_col[:, h:h + 1]
        i_r = i_row[h:h + 1, :]
        m = m_all[:, h:h + 1]
        btot = btot_col[:, h:h + 1]

        dm = jnp.where(mask, b_c - b_r + i_r, NEG_INF)
        inter = b_c + m
        m_t = jnp.maximum(inter, jnp.max(dm, axis=1, keepdims=True))
        w = jnp.exp(dm - m_t)
        a = jnp.exp(inter - m_t)
        sc = _dot_nt(qh, kh) * w
        cb = c_scr[h].astype(BF16)
        num = a * _dot(qh, cb) + _dot(sc.astype(BF16), vh)
        nb = n_scr[h:h + 1, :].astype(BF16).astype(F32)
        qn = jnp.sum(qh.astype(F32) * nb, axis=1, keepdims=True)
        den = a * qn + jnp.sum(sc, axis=1, keepdims=True)
        hs_ref[:, sl] = num / jnp.maximum(jnp.abs(den), jnp.exp(-m_t))

        g_c = btot - b_c + i_c
        g_r = btot - b_r + i_r
        m_new = jnp.maximum(btot + m, jnp.max(g_c, axis=0, keepdims=True))
        decay = jnp.exp(btot + m - m_new)
        ws_c = jnp.exp(g_c - m_new)
        ws_r = jnp.exp(g_r - m_new)
        kw = (kf * ws_c).astype(BF16)
        c_scr[h] = decay * c_scr[h] + _dot_tn(kw, vh)
        n_scr[h:h + 1, :] = decay * n_scr[h:h + 1, :] + _dot(ws_r.astype(BF16), kh)
        m_new_parts.append(m_new)
    m_scr[...] = jnp.concatenate(m_new_parts, axis=1)

    if emit_state:
        @pl.when(s == nchunks - 1)
        def _():
            cout_ref[...] = c_scr[...]
            nout_ref[...] = n_scr[...]
            mout_ref[...] = m_scr[...]


def _mlstm(z, gates, gates_t, tok_off, nseq, seqlen, j, init=None, emit_state=False, n_even=1):
    nh, dh, L = MLSTM_HEADS, MLSTM_DH, MLSTM_CHUNK
    w = nh * dh
    nchunks = seqlen // L
    off = tok_off // L

    def chunk(d, b, s):
        return off + b * nchunks + jnp.where(d == 0, s, nchunks - 1 - s)

    in_specs = [
        pl.BlockSpec((L, w), lambda d, b, s: (chunk(d, b, s), 0)),
        pl.BlockSpec((L, w), lambda d, b, s: (chunk(d, b, s), 1)),
        pl.BlockSpec((L, w), lambda d, b, s: (chunk(d, b, s), 2)),
        pl.BlockSpec((L, 4 * nh), lambda d, b, s: (chunk(d, b, s), 0)),
        pl.BlockSpec((4 * nh, L), lambda d, b, s: (0, chunk(d, b, s))),
    ]
    args = [z, z, z, gates, gates_t]
    has_init = init is not None
    if has_init:
        c0, n0, m0 = init
        in_specs += [
            pl.BlockSpec((None, None, None, nh, dh, dh), lambda d, b, s: (b, j, d, 0, 0, 0)),
            pl.BlockSpec((None, None, None, nh, dh), lambda d, b, s: (b, j, d, 0, 0)),
            pl.BlockSpec((None, None, None, 1, nh), lambda d, b, s: (b, j, d, 0, 0)),
        ]
        args += [c0, n0, m0.reshape(m0.shape[:3] + (1, nh))]
    out_shape = [jax.ShapeDtypeStruct((2, nseq * seqlen, w), F32)]
    out_specs = [pl.BlockSpec((None, L, w), lambda d, b, s: (d, chunk(d, b, s) - off, 0))]
    if emit_state:
        out_shape += [
            jax.ShapeDtypeStruct((nseq, n_even, 2, nh, dh, dh), F32),
            jax.ShapeDtypeStruct((nseq, n_even, 2, nh, dh), F32),
            jax.ShapeDtypeStruct((nseq, n_even, 2, 1, nh), F32),
        ]
        out_specs += [
            pl.BlockSpec((None, None, None, nh, dh, dh), lambda d, b, s: (b, j, d, 0, 0, 0)),
            pl.BlockSpec((None, None, None, nh, dh), lambda d, b, s: (b, j, d, 0, 0)),
            pl.BlockSpec((None, None, None, 1, nh), lambda d, b, s: (b, j, d, 0, 0)),
        ]
    return pl.pallas_call(
        functools.partial(_mlstm_kernel, has_init=has_init, emit_state=emit_state, nchunks=nchunks),
        out_shape=out_shape,
        grid=(2, nseq, nchunks),
        in_specs=in_specs,
        out_specs=out_specs,
        scratch_shapes=[pltpu.VMEM((nh, dh, dh), F32), pltpu.VMEM((nh, dh), F32), pltpu.VMEM((1, nh), F32)],
        compiler_params=_params("arbitrary", "arbitrary", "arbitrary"),
        name="mlstm_ctx" if emit_state else "mlstm_lat",
    )(*args)


def _even_out_kernel(*refs, n_ctx_tiles, tiles_per_latent, ctx_len, lat_len, split_tiles):
    nhr = 1 if split_tiles is None else 2
    h_refs = refs[:nhr]
    (gate_ref, hsc_ref, hsl_ref, o_ref, p_ref, pprev_ref, pnext_ref, gain_ref, pw_ref, ps_ref, w_ref, out_ref,
     cat_scr) = refs[nhr:]
    i = pl.program_id(0)
    tm = TOKEN_TILE
    nh, dh = MLSTM_HEADS, MLSTM_DH
    wm = nh * dh

    @pl.when(pl.program_id(1) == 0)
    def _():
        is_ctx = i < n_ctx_tiles
        hs = jnp.where(is_ctx, hsc_ref[0] + hsc_ref[1], hsl_ref[0] + hsl_ref[1])
        for h in range(nh):
            sl = slice(h * dh, (h + 1) * dh)
            x = hs[:, sl]
            y = x * lax.rsqrt(jnp.mean(x * x, axis=-1, keepdims=True) + EPS) * gain_ref[:, sl]
            cat_scr[:, sl] = (y * jax.nn.sigmoid(o_ref[:, sl])).astype(BF16)

        tiles_ctx = ctx_len // tm
        pos = jnp.where(is_ctx, i % tiles_ctx, (i - n_ctx_tiles) % tiles_per_latent)
        ntile = jnp.where(is_ctx, tiles_ctx, tiles_per_latent)
        seqlen = jnp.where(is_ctx, ctx_len, lat_len)
        x = p_ref[...]
        prev = jnp.where(pos > 0, pprev_ref[...], 0.0)
        nxt = jnp.where(pos < ntile - 1, pnext_ref[...], 0.0)
        pad = jnp.zeros((LANES - 2 * POOL_HALO, x.shape[1]), F32)
        xcat = jnp.concatenate([prev, x, nxt, pad], axis=0)
        xh, xl = _split2(xcat)
        t = lax.broadcasted_iota(jnp.int32, (tm, tm + LANES), 0)
        sidx = lax.broadcasted_iota(jnp.int32, (tm, tm + LANES), 1) - POOL_HALO
        tpos = pos * tm + lax.broadcasted_iota(jnp.int32, (tm, 1), 0)
        gw = wm // len(POOL_WINDOWS)
        for gi, win in enumerate(POOL_WINDOWS):
            sl = slice(gi * gw, (gi + 1) * gw)
            band = jnp.where((sidx >= t - win // 2) & (sidx < t - win // 2 + win), 1.0, 0.0).astype(BF16)
            lo = jnp.maximum(tpos - win // 2, 0)
            hi = jnp.minimum(tpos - win // 2 + win, seqlen)
            cnt = (hi - lo).astype(F32)
            p = (_dot(band, xh[:, sl]) + _dot(band, xl[:, sl])) / cnt - x[:, sl]
            y = _dot(p.astype(BF16), pw_ref[gi]) * ps_ref[:, sl]
            cat_scr[:, wm + gi * gw:wm + (gi + 1) * gw] = y.astype(BF16)

    out_ref[...] = (_load_stream(h_refs, split_tiles)
                    + gate_ref[...] * _dot(cat_scr[...], pltpu.bitcast(w_ref[...], BF16)))


def _even_out(h, mod3, layer, hs_ctx, hs_lat, z, gain, pool_w_bf16, pool_scale, w_out_bf16, n_ctx_tiles,
              tiles_per_latent, ctx_len, lat_len, tn=2048):
    tm = TOKEN_TILE
    d = 2 * w_out_bf16.shape[0]
    h_arrays, h_specs, n, split_tiles = _stream(h, tm, tn, col_of=lambda j: j)
    wm = MLSTM_HEADS * MLSTM_DH
    nlt = n // tm - n_ctx_tiles
    rows8 = n // POOL_HALO
    per = tm // POOL_HALO

    def gate_idx(i, j):
        return ((layer * 8 + _mod_row(i, n_ctx_tiles, tiles_per_latent)) * N_MOD + 2, 0, j)

    return pl.pallas_call(
        functools.partial(_even_out_kernel, n_ctx_tiles=n_ctx_tiles, tiles_per_latent=tiles_per_latent,
                          ctx_len=ctx_len, lat_len=lat_len, split_tiles=split_tiles),
        out_shape=jax.ShapeDtypeStruct((n, d), F32),
        grid=(n // tm, d // tn),
        in_specs=h_specs + [
            pl.BlockSpec((None, 1, tn), gate_idx),
            pl.BlockSpec((2, tm, wm), lambda i, j: (0, jnp.minimum(i, n_ctx_tiles - 1), 0)),
            pl.BlockSpec((2, tm, wm), lambda i, j: (0, jnp.clip(i - n_ctx_tiles, 0, nlt - 1), 0)),
            pl.BlockSpec((tm, wm), lambda i, j: (i, 3)),
            pl.BlockSpec((tm, wm), lambda i, j: (i, 4)),
            pl.BlockSpec((POOL_HALO, wm), lambda i, j: (jnp.maximum(i * per - 1, 0), 4)),
            pl.BlockSpec((POOL_HALO, wm), lambda i, j: (jnp.minimum((i + 1) * per, rows8 - 1), 4)),
            pl.BlockSpec((1, wm), lambda i, j: (0, 0)),
            pl.BlockSpec(pool_w_bf16.shape, lambda i, j: (0, 0, 0)),
            pl.BlockSpec((1, wm), lambda i, j: (0, 0)),
            pl.BlockSpec((d // 2, tn), lambda i, j: (0, j)),
        ],
        out_specs=pl.BlockSpec((tm, tn), lambda i, j: (i, j)),
        scratch_shapes=[pltpu.VMEM((tm, d), BF16)],
        compiler_params=_params("arbitrary", "arbitrary"),
        name="even_out",
    )(*h_arrays, mod3, hs_ctx, hs_lat, z, z, z, z, gain.reshape(1, wm), pool_w_bf16, pool_scale.reshape(1, wm),
      w_out_bf16)


def _qk_kernel(*refs, rope, emit_cache):
    q_ref, k_ref, v_ref, gain_ref = refs[:4]
    rest = list(refs[4:])
    if rope:
        cos_ref, sin_ref = rest[:2]
        rest = rest[2:]
    qn_ref, kn_ref, vb_ref = rest[:3]
    rest = rest[3:]
    lane = lax.broadcasted_iota(jnp.int32, q_ref.shape, 1)
    first = lane < DA_DH

    def norm(x, gain):
        sq = x * x
        s1 = jnp.sum(jnp.where(first, sq, 0.0), axis=-1, keepdims=True)
        s2 = jnp.sum(jnp.where(first, 0.0, sq), axis=-1, keepdims=True)
        ms = jnp.where(first, s1, s2) * (1.0 / DA_DH)
        y = x * lax.rsqrt(ms + EPS) * gain
        if rope:
            quarter = DA_DH // 4
            partner = jnp.where((lane & quarter) == 0, pltpu.roll(y, LANES - quarter, 1), pltpu.roll(y, quarter, 1))
            y = y * cos_ref[...] + partner * sin_ref[...]
        return y

    qn = norm(q_ref[...], gain_ref[0:1, :])
    kn = norm(k_ref[...], gain_ref[1:2, :])
    qn_ref[...] = (qn * (DA_DH ** -0.5)).astype(BF16)
    kn_ref[...] = kn.astype(BF16)
    v = v_ref[...]
    vb_ref[...] = v.astype(BF16)
    if emit_cache:
        newk_ref, newv_ref = rest
        newk_ref[...] = kn
        newv_ref[...] = v


def _qk(z, qk_gain2, tok_off, ntok, j, rope_tables=None, cache_shape=None, seqlen=None):
    tm = TOKEN_TILE
    hd = 2 * DA_DH
    w = DA_HEADS * hd
    off = tok_off // tm
    nblk_w = w // hd
    rope = rope_tables is not None
    emit_cache = cache_shape is not None
    in_specs = [
        pl.BlockSpec((tm, hd), lambda i, h: (off + i, h)),
        pl.BlockSpec((tm, hd), lambda i, h: (off + i, nblk_w + h)),
        pl.BlockSpec((tm, hd), lambda i, h: (off + i, 2 * nblk_w + h)),
        pl.BlockSpec((2, hd), lambda i, h: (0, 0)),
    ]
    args = [z, z, z, qk_gain2]
    if rope:
        tps = seqlen // tm
        in_specs += [pl.BlockSpec((tm, hd), lambda i, h: (i % tps, 0))] * 2
        args += list(rope_tables)
    out_shape = [jax.ShapeDtypeStruct((ntok, w), BF16)] * 3
    out_specs = [pl.BlockSpec((tm, hd), lambda i, h: (i, h))] * 3
    if emit_cache:
        out_shape += [jax.ShapeDtypeStruct(cache_shape, F32)] * 2
        out_specs += [pl.BlockSpec((None, None, None, tm, hd), lambda i, h: (i, j, h, 0, 0))] * 2
    return pl.pallas_call(
        functools.partial(_qk_kernel, rope=rope, emit_cache=emit_cache),
        out_shape=out_shape,
        grid=(ntok // tm, DA_HEADS),
        in_specs=in_specs,
        out_specs=out_specs,
        compiler_params=_params("arbitrary", "arbitrary"),
        name="qk_ctx" if emit_cache else "qk_lat",
    )(*args)


def _rope_tables(t):
    rows = t // GRID_W
    row = jnp.repeat(jnp.arange(rows), GRID_W).astype(F32)
    col = (jnp.arange(rows * GRID_W) % GRID_W).astype(F32)
    n_freq = DA_DH // 4
    inv = ROPE_BASE ** (-jnp.arange(n_freq, dtype=F32) / n_freq)
    ar, ac = row[:, None] * inv, col[:, None] * inv
    cos = jnp.concatenate([jnp.cos(ar), jnp.cos(ar), jnp.cos(ac), jnp.cos(ac)], axis=-1)
    sin = jnp.concatenate([-jnp.sin(ar), jnp.sin(ar), -jnp.sin(ac), jnp.sin(ac)], axis=-1)
    return jnp.tile(cos, (1, 2)), jnp.tile(sin, (1, 2))


def _attn_kernel(*refs, n_pieces, lam_init):
    q_ref, lp_ref, subln_ref = refs[:3]
    kv = refs[3:3 + 2 * n_pieces]
    o_ref = refs[3 + 2 * n_pieces]
    lp = lp_ref[...]
    lam = (jnp.exp(jnp.sum(lp[0:1] * lp[1:2], axis=-1, keepdims=True))
           - jnp.exp(jnp.sum(lp[2:3] * lp[3:4], axis=-1, keepdims=True)) + lam_init)
    q = q_ref[...]
    lane = lax.broadcasted_iota(jnp.int32, q.shape, 1)
    zero = jnp.zeros_like(q)
    qa = jnp.where(lane < DA_DH, q, zero)
    qb = jnp.where(lane < DA_DH, zero, q)
    ks = [kv[2 * p][...].astype(BF16) for p in range(n_pieces)]
    vs = [kv[2 * p + 1][...].astype(BF16) for p in range(n_pieces)]
    sa = [_dot_nt(qa, k) for k in ks]
    sb = [_dot_nt(qb, k) for k in ks]

    def attend(parts):
        m = functools.reduce(jnp.maximum, [jnp.max(x, axis=-1, keepdims=True) for x in parts])
        es = [jnp.exp(x - m) for x in parts]
        tot = functools.reduce(lambda a, b: a + b, [jnp.sum(e, axis=-1, keepdims=True) for e in es])
        pv = functools.reduce(lambda a, b: a + b, [_dot(e.astype(BF16), v) for e, v in zip(es, vs)])
        return pv / tot

    o = attend(sa) - lam * attend(sb)
    o = o * lax.rsqrt(jnp.mean(o * o, axis=-1, keepdims=True) + EPS) * subln_ref[...]
    o_ref[...] = (o * (1.0 - lam_init)).astype(BF16)


def _attn(qn, kn, vb, lam_params, subln, lam_init, nseq, seqlen, tq, j, caches=None):
    hd = 2 * DA_DH
    nq = seqlen // tq
    in_specs = [
        pl.BlockSpec((tq, hd), lambda b, h, qi: (b * nq + qi, h)),
        pl.BlockSpec((None, 4, DA_DH), lambda b, h, qi: (j, 0, 0)),
        pl.BlockSpec((None, 1, hd), lambda b, h, qi: (j, 0, 0)),
        pl.BlockSpec((seqlen, hd), lambda b, h, qi: (b, h)),
        pl.BlockSpec((seqlen, hd), lambda b, h, qi: (b, h)),
    ]
    args = [qn, lam_params, subln.reshape(subln.shape[0], 1, hd), kn, vb]
    n_pieces = 1
    if caches is not None:
        ck, cv = caches
        past = ck.shape[3]
        in_specs += [pl.BlockSpec((None, None, None, past, hd), lambda b, h, qi: (b, j, h, 0, 0))] * 2
        args += [ck, cv]
        n_pieces = 2
    return pl.pallas_call(
        functools.partial(_attn_kernel, n_pieces=n_pieces, lam_init=lam_init),
        out_shape=jax.ShapeDtypeStruct(qn.shape, BF16),
        grid=(nseq, DA_HEADS, nq),
        in_specs=in_specs,
        out_specs=pl.BlockSpec((tq, hd), lambda b, h, qi: (b * nq + qi, h)),
        compiler_params=_params("arbitrary", "arbitrary", "arbitrary"),
        name="attn_lat" if caches is not None else "attn_ctx",
    )(*args)


def _odd_out_kernel(h_ref, gate_ref, ac_ref, al_ref, gu_ref, gv_ref, ws_ref, bt_ref, w_ref, out_ref, cat_scr, *,
                    n_ctx_tiles):
    i = pl.program_id(0)
    tm = TOKEN_TILE
    wa = DA_HEADS * 2 * DA_DH
    gw = LANES

    @pl.when(pl.program_id(1) == 0)
    def _():
        cat_scr[:, 0:wa] = jnp.where(i < n_ctx_tiles, ac_ref[...], al_ref[...])
        for c in range(tm // GM_CHUNK):
            rows = slice(c * GM_CHUNK, (c + 1) * GM_CHUNK)
            for g in range(GM_GROUPS):
                cols = slice(g * gw, (g + 1) * gw)
                u = _gelu_tanh(gu_ref[rows, cols])
                v = _gelu_tanh(gv_ref[rows, cols])
                vn = v * lax.rsqrt(jnp.mean(v * v, axis=-1, keepdims=True) + EPS)
                mixed = _dot(ws_ref[g], vn.astype(BF16)) + bt_ref[:, g:g + 1]
                cat_scr[rows, wa + g * gw:wa + (g + 1) * gw] = (u * mixed).astype(BF16)

    out_ref[...] = h_ref[...] + gate_ref[...] * _dot(cat_scr[...], pltpu.bitcast(w_ref[...], BF16))


def _odd_out(h, mod3, layer, a_ctx, a_lat, z, gm_ws_bf16, gm_b_t, w_out_bf16, n_ctx_tiles, tiles_per_latent, tn=2048):
    n, d = h.shape
    tm = TOKEN_TILE
    wa = DA_HEADS * 2 * DA_DH
    wg = GM_GROUPS * LANES
    nlt = n // tm - n_ctx_tiles

    def gate_idx(i, j):
        return ((layer * 8 + _mod_row(i, n_ctx_tiles, tiles_per_latent)) * N_MOD + 2, 0, j)

    return pl.pallas_call(
        functools.partial(_odd_out_kernel, n_ctx_tiles=n_ctx_tiles),
        out_shape=jax.ShapeDtypeStruct((n, d), F32),
        grid=(n // tm, d // tn),
        in_specs=[
            pl.BlockSpec((tm, tn), lambda i, j: (i, j)),
            pl.BlockSpec((None, 1, tn), gate_idx),
            pl.BlockSpec((tm, wa), lambda i, j: (jnp.minimum(i, n_ctx_tiles - 1), 0)),
            pl.BlockSpec((tm, wa), lambda i, j: (jnp.clip(i - n_ctx_tiles, 0, nlt - 1), 0)),
            pl.BlockSpec((tm, wg), lambda i, j: (i, 3)),
            pl.BlockSpec((tm, wg), lambda i, j: (i, 4)),
            pl.BlockSpec(gm_ws_bf16.shape, lambda i, j: (0, 0, 0)),
            pl.BlockSpec(gm_b_t.shape, lambda i, j: (0, 0)),
            pl.BlockSpec((d // 2, tn), lambda i, j: (0, j)),
        ],
        out_specs=pl.BlockSpec((tm, tn), lambda i, j: (i, j)),
        scratch_shapes=[pltpu.VMEM((tm, d), BF16)],
        compiler_params=_params("arbitrary", "arbitrary"),
        name="odd_out",
    )(h, mod3, a_ctx, a_lat, z, z, gm_ws_bf16, gm_b_t, w_out_bf16)


def _top_values(x, k, with_rank=False):
    vals = []
    cur = x
    rank = jnp.full(x.shape, float(k), F32) if with_rank else None
    for r in range(k):
        m = jnp.max(cur, axis=0, keepdims=True)
        vals.append(m)
        hit = cur == m
        if with_rank:
            rank = jnp.where(hit, float(r), rank)
        if r + 1 < k:
            cur = jnp.where(hit, NEG_INF, cur)
    vals = jnp.concatenate(vals, axis=0)
    return (vals, rank) if with_rank else vals


def _dup_bf16(x):
    b = pltpu.bitcast(x.astype(BF16).astype(F32), jnp.uint32)
    return pltpu.bitcast(b | (b >> 16), F32)


def _peer_route_kernel(q_ref, sk_ref, rank_ref, e2_ref, cnt_ref, c_ref):
    half = PEER_NKEYS
    k = PEER_TOPK
    for h in range(PEER_HEADS):
        q1 = q_ref[:, 2 * h * half:(2 * h + 1) * half].astype(BF16)
        q2 = q_ref[:, (2 * h + 1) * half:(2 * h + 2) * half].astype(BF16)
        s1 = _dot_nt(sk_ref[2 * h], q1)
        s2 = _dot_nt(sk_ref[2 * h + 1], q2)
        a = _top_values(s1, k)
        b, rank2 = _top_values(s2, k, with_rank=True)
        rows = [a[r:r + 1] + b[0:k // (r + 1)] for r in range(k)]
        nrows = sum(k // (r + 1) for r in range(k))
        rows.append(jnp.full((-nrows % SUBLANES, s1.shape[1]), NEG_INF, F32))
        top = _top_values(jnp.concatenate(rows, axis=0), k)
        tau = top[k - 1:k]
        z = jnp.sum(jnp.exp(top - top[0:1]), axis=0, keepdims=True)
        cnt = jnp.zeros(s1.shape, F32)
        for r in range(k):
            cnt = jnp.where(s1 + b[r:r + 1] >= tau, float(r + 1), cnt)
        rank_ref[h] = pltpu.bitcast(rank2.astype(BF16), F32)
        e2_ref[h] = pltpu.bitcast(jnp.exp(s2 - b[0:1]).astype(BF16), F32)
        cnt_ref[h] = _dup_bf16(cnt)
        c_ref[h] = _dup_bf16(jnp.exp(s1 - a[0:1]) / z)


def _peer_route(qp, subkeys_bf16, tm=512):
    n = qp.shape[0]
    def out(rows):
        return (jax.ShapeDtypeStruct((PEER_HEADS, rows, n), F32),
                pl.BlockSpec((PEER_HEADS, rows, tm), lambda i: (0, 0, i)))

    outs = [out(PEER_NKEYS // 2)] * 2 + [out(PEER_NKEYS)] * 2
    return pl.pallas_call(
        _peer_route_kernel,
        out_shape=[o[0] for o in outs],
        grid=(n // tm,),
        in_specs=[
            pl.BlockSpec((tm, qp.shape[1]), lambda i: (i, 0)),
            pl.BlockSpec(subkeys_bf16.shape, lambda i: (0, 0, 0)),
        ],
        out_specs=[o[1] for o in outs],
        compiler_params=_params("arbitrary"),
        name="peer_route",
    )(qp, subkeys_bf16)


def _peer_dense_kernel(xnt_ref, h_ref, gate_ref, u_ref, vt_ref, rank_ref, e2_ref, cnt_ref, c_ref, *rest, te, n_tiles,
                       split_tiles):
    n_out = 1 if split_tiles is None else 2
    out_refs = rest[:n_out]
    act0_scr, act1_scr, wa0_scr, wa1_scr, acc_scr = rest[n_out:]
    s = pl.program_id(0)
    n_pairs = pl.num_programs(0) - 2
    nkeys = PEER_NKEYS
    per = te // nkeys
    tm = acc_scr.shape[1]
    half = tm // 2
    assert per % 4 == 0
    pack = 2 * SUBLANES

    @pl.when(s == 0)
    def _():
        act1_scr[...] = jnp.zeros_like(act1_scr)
        wa0_scr[...] = jnp.zeros_like(wa0_scr)

    live = jnp.logical_and(s >= 1, s <= n_pairs)
    t2 = jnp.clip(s - 1, 0, n_pairs - 1) % n_tiles
    pair3 = jnp.clip(s - 2, 0, n_pairs - 1)
    t3 = pair3 % n_tiles
    tok3 = pair3 // n_tiles

    @pl.when(t3 == 0)
    def _():
        acc_scr[...] = jnp.zeros_like(acc_scr)

    def step(act_w, act_r, wa_w, wa_r):
        def stage1(c):
            cols = slice(c * half, (c + 1) * half)
            act_w[:, cols] = _dot(pltpu.bitcast(u_ref[...], BF16),
                                  pltpu.bitcast(xnt_ref[:, cols], BF16))

        def stage3(c):
            cols = slice(c * half, (c + 1) * half)
            acc_scr[:, cols] += _dot(pltpu.bitcast(vt_ref[...], BF16), wa_r[:, cols])

        mxu_work = [functools.partial(stage1, 0), functools.partial(stage3, 0),
                    functools.partial(stage1, 1), functools.partial(stage3, 1)]
        stride = per // len(mxu_work)
        for ii in range(per):
            if ii % stride == 0:
                mxu_work[ii // stride]()
            r = t2 * per + ii
            cnt_rows = [jnp.where(live, cnt_ref[h, pl.ds(r, 1), :], 0.0) for h in range(PEER_HEADS)]
            c_rows = [c_ref[h, pl.ds(r, 1), :] for h in range(PEER_HEADS)]
            for lt in range(tm // LANES):
                lanes = slice(lt * LANES, (lt + 1) * LANES)
                cnts = [pltpu.bitcast(jnp.broadcast_to(x[:, lanes], (SUBLANES, LANES)), BF16) for x in cnt_rows]
                ccs = [pltpu.bitcast(jnp.broadcast_to(x[:, lanes], (SUBLANES, LANES)), BF16) for x in c_rows]
                for sb in range(nkeys // pack):
                    words = slice(sb * SUBLANES, (sb + 1) * SUBLANES)
                    rows = slice(ii * nkeys + sb * pack, ii * nkeys + (sb + 1) * pack)
                    w = None
                    for h in range(PEER_HEADS):
                        e2 = pltpu.bitcast(e2_ref[h, words, lanes], BF16)
                        rank = pltpu.bitcast(rank_ref[h, words, lanes], BF16)
                        wh = jnp.where(rank < cnts[h], e2 * ccs[h], jnp.zeros_like(e2))
                        w = wh if w is None else w + wh
                    wa_w[rows, lanes] = w * _gelu_tanh(act_r[rows, lanes]).astype(BF16)

    @pl.when(s % 2 == 0)
    def _():
        step(act0_scr, act1_scr, wa1_scr, wa0_scr)

    @pl.when(s % 2 == 1)
    def _():
        step(act1_scr, act0_scr, wa0_scr, wa1_scr)

    @pl.when(jnp.logical_and(t3 == n_tiles - 1, s >= 2))
    def _():
        res = h_ref[...] + gate_ref[...] * acc_scr[...].T
        if split_tiles is None:
            out_refs[0][...] = res
        else:
            @pl.when(tok3 < split_tiles)
            def _():
                out_refs[0][...] = res

            @pl.when(tok3 >= split_tiles)
            def _():
                out_refs[1][...] = res


def _peer_dense(xnt, h, mod3, layer, u_packed, vt_packed, rank2, e2, cnt, c, n_ctx_tiles, tiles_per_latent, tm=512,
                te=512, split_out=False):
    n, d = h.shape
    n_tiles = 2 * u_packed.shape[0] // te
    per = tm // TOKEN_TILE
    n_pairs = (n // tm) * n_tiles

    def tok(lag):
        return lambda t: jnp.clip(t - lag, 0, n_pairs - 1) // n_tiles

    def exp(lag):
        return lambda t: jnp.clip(t - lag, 0, n_pairs - 1) % n_tiles

    tok1, tok2, tok3, exp1, exp3 = tok(0), tok(1), tok(2), exp(0), exp(2)
    if split_out:
        split_tiles = n_ctx_tiles // per
        n_ctx = split_tiles * tm
        out_shape = [jax.ShapeDtypeStruct((n_ctx, d), F32), jax.ShapeDtypeStruct((n - n_ctx, d), F32)]
        out_specs = [pl.BlockSpec((tm, d), lambda t: (jnp.minimum(tok3(t), split_tiles - 1), 0)),
                     pl.BlockSpec((tm, d), lambda t: (jnp.maximum(tok3(t) - split_tiles, 0), 0))]
    else:
        split_tiles = None
        out_shape = jax.ShapeDtypeStruct((n, d), F32)
        out_specs = pl.BlockSpec((tm, d), lambda t: (tok3(t), 0))

    def gate_idx(t):
        return ((layer * 8 + _mod_row(tok3(t) * per, n_ctx_tiles, tiles_per_latent)) * N_MOD + 5, 0, 0)

    rspec = pl.BlockSpec((PEER_HEADS, PEER_NKEYS, tm), lambda t: (0, 0, tok2(t)))
    pspec = pl.BlockSpec((PEER_HEADS, PEER_NKEYS // 2, tm), lambda t: (0, 0, tok2(t)))
    return pl.pallas_call(
        functools.partial(_peer_dense_kernel, te=te, n_tiles=n_tiles, split_tiles=split_tiles),
        out_shape=out_shape,
        grid=(n_pairs + 2,),
        in_specs=[
            pl.BlockSpec((d // 2, tm), lambda t: (0, tok1(t))),
            pl.BlockSpec((tm, d), lambda t: (tok3(t), 0), pipeline_mode=pl.Buffered(1)),
            pl.BlockSpec((None, 1, d), gate_idx),
            pl.BlockSpec((te // 2, d), lambda t: (exp1(t), 0)),
            pl.BlockSpec((d // 2, te), lambda t: (0, exp3(t))),
            pspec, pspec, rspec, rspec,
        ],
        out_specs=out_specs,
        scratch_shapes=[pltpu.VMEM((te, tm), F32), pltpu.VMEM((te, tm), F32), pltpu.VMEM((te, tm), BF16),
                        pltpu.VMEM((te, tm), BF16), pltpu.VMEM((d, tm), F32)],
        compiler_params=_params("arbitrary"),
        name="peer_dense",
    )(xnt, h, mod3, u_packed, vt_packed, rank2, e2, cnt, c)


def kernel(x_prompt, x_sample, state_mlstm_C, state_mlstm_n, state_mlstm_m, cache_da_k, cache_da_v, c, c_ctx, norm_mix, norm_ffn, w_mod, b_mod, w_in_even, b_gate_even, mlstm_gain, pool_w, pool_scale, w_out_even, w_in_odd, qk_gain, da_lambda, da_subln, gm_ws, gm_b, w_out_odd, peer_wq, peer_subkeys, peer_u, peer_v):
    nb, s_len, d = x_prompt.shape
    nbd, t_len, _ = x_sample.shape
    depth = w_mod.shape[0]
    tm = TOKEN_TILE
    assert s_len == tm and t_len % tm == 0 and nbd <= 7 and t_len % GRID_W == 0
    n_ctx = nb * s_len
    n_lat = nbd * t_len
    n_ctx_tiles = n_ctx // tm
    tiles_per_latent = t_len // tm
    n_even = (depth + 1) // 2
    n_odd = depth // 2

    h = (x_prompt.reshape(n_ctx, d), x_sample.reshape(n_lat, d))
    cond8 = jnp.concatenate([c_ctx[None], c, jnp.zeros((7 - nbd, d), F32)], axis=0)
    mod3 = _adaln(cond8, w_mod, b_mod).reshape(depth * 8 * N_MOD, 1, d)
    rope = _rope_tables(t_len)

    new_c, new_n, new_m, new_k, new_v = [], [], [], [], []
    wm = MLSTM_HEADS * MLSTM_DH
    for l in range(depth):
        j = l // 2
        if l % 2 == 0:
            w_in = w_in_even[j]
            n_main = 5 * wm
            z, gates, gates_t = _inproj(h, norm_mix[l], mod3, l, 0, _pack_rows(w_in, ncols=n_main), n_ctx_tiles,
                                        tiles_per_latent, w_gates=w_in[:, n_main:], b_gates=b_gate_even[j],
                                        name="inproj_even")
            hs_ctx, cc, cn, cm = _mlstm(z, gates, gates_t, 0, nb, s_len, 0, emit_state=True)
            (hs_lat,) = _mlstm(z, gates, gates_t, n_ctx, nbd, t_len, j,
                               init=(state_mlstm_C, state_mlstm_n, state_mlstm_m))
            new_c.append(cc)
            new_n.append(cn)
            new_m.append(cm[:, :, :, 0, :])
            h = _even_out(h, mod3, l, hs_ctx, hs_lat, z, mlstm_gain[j], pool_w[j].astype(BF16), pool_scale[j],
                          _pack_rows(w_out_even[j]), n_ctx_tiles, tiles_per_latent, s_len, t_len)
        else:
            lam_init = 0.8 - 0.6 * math.exp(-0.3 * l)
            (z,) = _inproj(h, norm_mix[l], mod3, l, 0, _pack_rows(w_in_odd[j]), n_ctx_tiles, tiles_per_latent,
                           name="inproj_odd")
            gain2 = jnp.tile(qk_gain[j], (1, 2))
            cache_shape = (nb, 1, DA_HEADS, s_len, 2 * DA_DH)
            qn_c, kn_c, vb_c, nk, nv = _qk(z, gain2, 0, n_ctx, 0, cache_shape=cache_shape)
            qn_l, kn_l, vb_l = _qk(z, gain2, n_ctx, n_lat, 0, rope_tables=rope, seqlen=t_len)
            new_k.append(nk)
            new_v.append(nv)
            a_ctx = _attn(qn_c, kn_c, vb_c, da_lambda, da_subln, lam_init, nb, s_len, s_len, j)
            a_lat = _attn(qn_l, kn_l, vb_l, da_lambda, da_subln, lam_init, nbd, t_len, tm, j,
                          caches=(cache_da_k, cache_da_v))
            h = _odd_out(h, mod3, l, a_ctx, a_lat, z, gm_ws[j].astype(BF16), gm_b[j].T, _pack_rows(w_out_odd[j]),
                         n_ctx_tiles, tiles_per_latent)
        qp, xnt = _inproj(h, norm_ffn[l], mod3, l, 3, _pack_rows(peer_wq[l]), n_ctx_tiles, tiles_per_latent,
                          emit_xn=True, name="inproj_peer")
        sk = peer_subkeys[l].reshape(2 * PEER_HEADS, PEER_NKEYS, PEER_NKEYS).astype(BF16)
        rank2, e2, cnt, cw = _peer_route(qp, sk)
        h = _peer_dense(xnt, h, mod3, l, _pack_rows(peer_u[l]), _pack_rows(peer_v[l], transpose=True), rank2,
                        e2, cnt, cw,
                        n_ctx_tiles, tiles_per_latent, split_out=(l == depth - 1))

    def join(parts):
        return parts[0] if len(parts) == 1 else jnp.concatenate(parts, axis=1)

    return (h[0].reshape(nb, s_len, d), h[1].reshape(nbd, t_len, d), join(new_c), join(new_n), join(new_m),
            join(new_k), join(new_v))
```

```python
import functools
import math

import jax
import jax.numpy as jnp
from jax import lax
from jax.experimental import pallas as pl
from jax.experimental.pallas import tpu as pltpu

F32 = jnp.float32
BF16 = jnp.bfloat16

N_MOD = 6
EPS = 1e-6
TOKEN_TILE = 256
LANES = 128
SUBLANES = 8
VMEM_LIMIT_BYTES = 56 * 1024 * 1024

MLSTM_HEADS = 4
MLSTM_DH = 256
MLSTM_CHUNK = 128
POOL_WINDOWS = (2, 4, 8, 16)
POOL_HALO = 8
DA_HEADS = 8
DA_DH = 64
GRID_W = 64
ROPE_BASE = 10000.0
GM_GROUPS = 8
GM_CHUNK = 128
PEER_HEADS = 8
PEER_NKEYS = 128
PEER_TOPK = 16
NEG_INF = float("-inf")
POS_INF = float("inf")


def _params(*sem):
    return pltpu.CompilerParams(dimension_semantics=sem, vmem_limit_bytes=VMEM_LIMIT_BYTES)


def _dot(a, b):
    return jnp.dot(a, b, preferred_element_type=F32)


def _dot_nt(a, b):
    return lax.dot_general(a, b, (((1,), (1,)), ((), ())), preferred_element_type=F32)


def _dot_tn(a, b):
    return lax.dot_general(a, b, (((0,), (0,)), ((), ())), preferred_element_type=F32)


def _split2(x):
    hi = x.astype(BF16)
    lo = (x - hi.astype(F32)).astype(BF16)
    return hi, lo


def _split3(x):
    hi = x.astype(BF16)
    r = x - hi.astype(F32)
    mid = r.astype(BF16)
    lo = (r - mid.astype(F32)).astype(BF16)
    return hi, mid, lo


def _dot3(a, b, dot=_dot):
    ah, al = _split2(a)
    bh, bl = _split2(b)
    return dot(ah, bh) + (dot(ah, bl) + dot(al, bh))


def _gelu_tanh(x):
    c = math.sqrt(2.0 / math.pi)
    half = 0.5 * x
    return half + half * jnp.tanh(x * (c + (0.044715 * c) * (x * x)))


def _gelu_tanh_lowp(x):
    c = -2.0 * math.sqrt(2.0 / math.pi)
    return x / (1.0 + jnp.exp(x * (c + (0.044715 * c) * (x * x))))


def _log_sigmoid(x):
    return -(jnp.maximum(-x, 0.0) + jnp.log1p(jnp.exp(-jnp.abs(x))))


def _mod_row(i, n_ctx_tiles, tiles_per_latent):
    return jnp.where(i < n_ctx_tiles, 0, 1 + (i - n_ctx_tiles) // tiles_per_latent)


def _pack_kernel(x_ref, o_ref, *, transpose):
    x = x_ref[...]
    if transpose:
        x = x.T
    o_ref[...] = pltpu.bitcast(x.astype(BF16), F32)


def _pack_rows(x, layer, ncols=None, transpose=False):
    _, r, c = x.shape
    c = c if ncols is None else ncols
    tr, tc = 512, 1024
    assert r % tr == 0 and c % tc == 0
    if transpose:
        out_shape, out_spec = (c // 2, r), pl.BlockSpec((tc // 2, tr), lambda i, j: (j, i))
    else:
        out_shape, out_spec = (r // 2, c), pl.BlockSpec((tr // 2, tc), lambda i, j: (i, j))
    return pl.pallas_call(
        functools.partial(_pack_kernel, transpose=transpose),
        out_shape=jax.ShapeDtypeStruct(out_shape, F32),
        grid=(r // tr, c // tc),
        in_specs=[pl.BlockSpec((None, tr, tc), lambda i, j: (layer, i, j))],
        out_specs=out_spec,
        compiler_params=_params("arbitrary", "arbitrary"),
        name="pack_t" if transpose else "pack",
    )(x)


def _adaln_kernel(cond_ref, w_ref, b_ref, o_ref):
    c = cond_ref[...]
    s = c * jax.nn.sigmoid(c)
    o_ref[...] = _dot3(s, w_ref[...]) + b_ref[...]


def _adaln(cond8, w_mod, b_mod):
    depth, d, dout = w_mod.shape
    tn = 1024
    return pl.pallas_call(
        _adaln_kernel,
        out_shape=jax.ShapeDtypeStruct((depth, 8, dout), F32),
        grid=(depth, dout // tn),
        in_specs=[
            pl.BlockSpec((8, d), lambda l, j: (0, 0)),
            pl.BlockSpec((None, d, tn), lambda l, j: (l, 0, j)),
            pl.BlockSpec((None, 1, tn), lambda l, j: (l, 0, j)),
        ],
        out_specs=pl.BlockSpec((None, 8, tn), lambda l, j: (l, 0, j)),
        compiler_params=_params("arbitrary", "arbitrary"),
        name="adaln",
    )(cond8, w_mod, b_mod.reshape(depth, 1, dout))


def _stream(h, tm, cols, col_of=lambda j: 0, single_buffer_split=False):
    if not isinstance(h, tuple):
        return [h], [pl.BlockSpec((tm, cols), lambda i, j: (i, col_of(j)))], h.shape[0], None
    hc, hl = h
    nct = hc.shape[0] // tm
    kw = dict(pipeline_mode=pl.Buffered(1)) if single_buffer_split else {}
    specs = [pl.BlockSpec((tm, cols), lambda i, j: (jnp.minimum(i, nct - 1), col_of(j)), **kw),
             pl.BlockSpec((tm, cols), lambda i, j: (jnp.maximum(i - nct, 0), col_of(j)), **kw)]
    return [hc, hl], specs, hc.shape[0] + hl.shape[0], nct


def _load_stream(h_refs, split_tiles):
    if split_tiles is None:
        return h_refs[0][...]
    return jnp.where(pl.program_id(0) < split_tiles, h_refs[0][...], h_refs[1][...])


def _inproj_kernel(*refs, with_gates, emit_xn, split_tiles):
    nh = 1 if split_tiles is None else 2
    h_refs = refs[:nh]
    gain_ref, shift_ref, scale_ref, w_ref = refs[nh:nh + 4]
    rest = list(refs[nh + 4:])
    if with_gates:
        wg2_ref, bg_ref = rest[:2]
        rest = rest[2:]
    z_ref = rest.pop(0)
    if with_gates:
        g_ref, gt_ref = rest[:2]
        rest = rest[2:]
    if emit_xn:
        xn_out_ref = rest.pop(0)
    xn_scr = rest.pop(0)

    @pl.when(pl.program_id(1) == 0)
    def _():
        x = _load_stream(h_refs, split_tiles)
        xn = x * lax.rsqrt(jnp.mean(x * x, axis=-1, keepdims=True) + EPS) * gain_ref[...]
        xn = xn * (1.0 + scale_ref[...]) + shift_ref[...]
        xb = xn.astype(BF16)
        xn_scr[...] = xb
        if emit_xn:
            xn_out_ref[...] = pltpu.bitcast(xn.T.astype(BF16), F32)
        if with_gates:
            ng = bg_ref.shape[1]
            w2 = wg2_ref[...]
            w2_hi = w2.astype(BF16)
            lane = lax.broadcasted_iota(jnp.int32, w2.shape, 1)
            w_cat = jnp.where(lane < ng, w2_hi, (w2 - w2_hi.astype(F32)).astype(BF16))
            xl = (xn - xb.astype(F32)).astype(BF16)
            p = _dot(xb, w_cat)
            g = p[:, :ng] + (p[:, ng:] + _dot(xl, w2_hi)[:, :ng]) + bg_ref[...]
            g_ref[...] = g
            eye = (lax.broadcasted_iota(jnp.int32, (ng, ng), 0)
                   == lax.broadcasted_iota(jnp.int32, (ng, ng), 1)).astype(F32).astype(BF16)
            g3 = _split3(g)
            gt_ref[...] = _dot_nt(eye, g3[0]) + (_dot_nt(eye, g3[1]) + _dot_nt(eye, g3[2]))

    z_ref[...] = _dot(xn_scr[...], pltpu.bitcast(w_ref[...], BF16))


def _inproj(h, gain, mod3, layer, mod_base, w_packed, n_ctx_tiles, tiles_per_latent, w_gates=None, b_gates=None,
            emit_xn=False, tn=1024, name="inproj"):
    d, dout = 2 * w_packed.shape[0], w_packed.shape[1]
    tm = TOKEN_TILE * math.gcd(4, n_ctx_tiles, tiles_per_latent)
    per = tm // TOKEN_TILE
    with_gates = w_gates is not None
    h_arrays, h_specs, n, split_tiles = _stream(h, tm, d, single_buffer_split=True)

    def mod_idx(off):
        return lambda i, j: ((layer * 8 + _mod_row(i * per, n_ctx_tiles, tiles_per_latent)) * N_MOD + mod_base + off,
                             0, 0)

    in_specs = h_specs + [
        pl.BlockSpec((1, d), lambda i, j: (0, 0)),
        pl.BlockSpec((None, 1, d), mod_idx(0)),
        pl.BlockSpec((None, 1, d), mod_idx(1)),
        pl.BlockSpec((d // 2, tn), lambda i, j: (0, j)),
    ]
    args = h_arrays + [gain.reshape(1, d), mod3, mod3, w_packed]
    out_shape = [jax.ShapeDtypeStruct((n, dout), F32)]
    out_specs = [pl.BlockSpec((tm, tn), lambda i, j: (i, j))]
    if with_gates:
        ng = w_gates.shape[1]
        in_specs += [pl.BlockSpec((d, 2 * ng), lambda i, j: (0, 0)), pl.BlockSpec((1, ng), lambda i, j: (0, 0))]
        args += [jnp.concatenate([w_gates, w_gates], axis=1), b_gates.reshape(1, ng)]
        out_shape += [jax.ShapeDtypeStruct((n, ng), F32), jax.ShapeDtypeStruct((ng, n), F32)]
        out_specs += [pl.BlockSpec((tm, ng), lambda i, j: (i, 0)), pl.BlockSpec((ng, tm), lambda i, j: (0, i))]
    if emit_xn:
        out_shape.append(jax.ShapeDtypeStruct((d // 2, n), F32))
        out_specs.append(pl.BlockSpec((d // 2, tm), lambda i, j: (0, i)))
    return pl.pallas_call(
        functools.partial(_inproj_kernel, with_gates=with_gates, emit_xn=emit_xn, split_tiles=split_tiles),
        out_shape=out_shape,
        grid=(n // tm, dout // tn),
        in_specs=in_specs,
        out_specs=out_specs,
        scratch_shapes=[pltpu.VMEM((tm, d), BF16)],
        compiler_params=_params("arbitrary", "arbitrary"),
        name=name,
    )(*args)


def _mlstm_kernel(*refs, has_init, emit_state, nchunks):
    q_ref, k_ref, v_ref, g_ref, gt_ref = refs[:5]
    rest = list(refs[5:])
    if has_init:
        c0_ref, n0_ref, m0_ref = rest[:3]
        rest = rest[3:]
    hs_ref = rest.pop(0)
    if emit_state:
        cout_ref, nout_ref, mout_ref = rest[:3]
        rest = rest[3:]
    c_scr, n_scr, m_scr = rest
    nh, dh, L = MLSTM_HEADS, MLSTM_DH, MLSTM_CHUNK
    d = pl.program_id(0)
    s = pl.program_id(2)

    @pl.when(s == 0)
    def _():
        if has_init:
            c_scr[...] = c0_ref[...]
            n_scr[...] = n0_ref[...]
            m_scr[...] = m0_ref[...]
        else:
            c_scr[...] = jnp.zeros_like(c_scr)
            n_scr[...] = jnp.zeros_like(n_scr)
            m_scr[...] = jnp.zeros_like(m_scr)

    row = lax.broadcasted_iota(jnp.int32, (L, L), 0)
    col = lax.broadcasted_iota(jnp.int32, (L, L), 1)
    sgn = jnp.where(d == 0, 1, -1)
    mask = (row - col) * sgn >= 0
    maskb = jnp.where(mask, 1.0, 0.0).astype(BF16)

    g = g_ref[...]
    gt = gt_ref[...]
    fwd = d == 0
    i_col = jnp.where(fwd, g[:, 0:nh], g[:, nh:2 * nh])
    f_col = _log_sigmoid(jnp.where(fwd, g[:, 2 * nh:3 * nh], g[:, 3 * nh:4 * nh]))
    i_row = jnp.where(fwd, gt[0:nh], gt[nh:2 * nh])
    f_row = _log_sigmoid(jnp.where(fwd, gt[2 * nh:3 * nh], gt[3 * nh:4 * nh]))
    fc = _split3(f_col)
    b_col = _dot(maskb, fc[0]) + (_dot(maskb, fc[1]) + _dot(maskb, fc[2]))
    fr = _split3(f_row)
    b_row = _dot_nt(fr[0], maskb) + (_dot_nt(fr[1], maskb) + _dot_nt(fr[2], maskb))
    btot_col = jnp.sum(f_col, axis=0, keepdims=True)
    m_all = m_scr[...]

    m_new_parts = []
    for h in range(nh):
        sl = slice(h * dh, (h + 1) * dh)
        qh = q_ref[:, sl].astype(BF16)
        kf = k_ref[:, sl] * (dh ** -0.5)
        kh = kf.astype(BF16)
        vh = v_ref[:, sl].astype(BF16)
        b_c = b_col[:, h:h + 1]
        b_r = b_row[h:h + 1, :]
        i_c = i_col[:, h:h + 1]
        i_r = i_row[h:h + 1, :]
        m = m_all[:, h:h + 1]
        btot = btot_col[:, h:h + 1]

        dm = jnp.where(mask, b_c - b_r + i_r, NEG_INF)
        inter = b_c + m
        m_t = jnp.maximum(inter, jnp.max(dm, axis=1, keepdims=True))
        w = jnp.exp(dm - m_t)
        a = jnp.exp(inter - m_t)
        sc = _dot_nt(qh, kh) * w
        cb = c_scr[h].astype(BF16)
        num = a * _dot(qh, cb) + _dot(sc.astype(BF16), vh)
        nb = n_scr[h:h + 1, :].astype(BF16).astype(F32)
        qn = jnp.sum(qh.astype(F32) * nb, axis=1, keepdims=True)
        den = a * qn + jnp.sum(sc, axis=1, keepdims=True)
        hs_ref[:, sl] = num / jnp.maximum(jnp.abs(den), jnp.exp(-m_t))

        g_c = btot - b_c + i_c
        g_r = btot - b_r + i_r
        m_new = jnp.maximum(btot + m, jnp.max(g_c, axis=0, keepdims=True))
        decay = jnp.exp(btot + m - m_new)
        ws_c = jnp.exp(g_c - m_new)
        ws_r = jnp.exp(g_r - m_new)
        kw = (kf * ws_c).astype(BF16)
        c_scr[h] = decay * c_scr[h] + _dot_tn(kw, vh)
        n_scr[h:h + 1, :] = decay * n_scr[h:h + 1, :] + _dot(ws_r.astype(BF16), kh)
        m_new_parts.append(m_new)
    m_scr[...] = jnp.concatenate(m_new_parts, axis=1)

    if emit_state:
        @pl.when(s == nchunks - 1)
        def _():
            cout_ref[...] = c_scr[...]
            nout_ref[...] = n_scr[...]
            mout_ref[...] = m_scr[...]


def _mlstm(z, gates, gates_t, tok_off, nseq, seqlen, j, init=None, emit_state=False, n_even=1):
    nh, dh, L = MLSTM_HEADS, MLSTM_DH, MLSTM_CHUNK
    w = nh * dh
    nchunks = seqlen // L
    off = tok_off // L

    def chunk(d, b, s):
        return off + b * nchunks + jnp.where(d == 0, s, nchunks - 1 - s)

    in_specs = [
        pl.BlockSpec((L, w), lambda d, b, s: (chunk(d, b, s), 0)),
        pl.BlockSpec((L, w), lambda d, b, s: (chunk(d, b, s), 1)),
        pl.BlockSpec((L, w), lambda d, b, s: (chunk(d, b, s), 2)),
        pl.BlockSpec((L, 4 * nh), lambda d, b, s: (chunk(d, b, s), 0)),
        pl.BlockSpec((4 * nh, L), lambda d, b, s: (0, chunk(d, b, s))),
    ]
    args = [z, z, z, gates, gates_t]
    has_init = init is not None
    if has_init:
        c0, n0, m0 = init
        in_specs += [
            pl.BlockSpec((None, None, None, nh, dh, dh), lambda d, b, s: (b, j, d, 0, 0, 0)),
            pl.BlockSpec((None, None, None, nh, dh), lambda d, b, s: (b, j, d, 0, 0)),
            pl.BlockSpec((None, None, None, 1, nh), lambda d, b, s: (b, j, d, 0, 0)),
        ]
        args += [c0, n0, m0.reshape(m0.shape[:3] + (1, nh))]
    out_shape = [jax.ShapeDtypeStruct((2, nseq * seqlen, w), F32)]
    out_specs = [pl.BlockSpec((None, L, w), lambda d, b, s: (d, chunk(d, b, s) - off, 0))]
    if emit_state:
        out_shape += [
            jax.ShapeDtypeStruct((nseq, n_even, 2, nh, dh, dh), F32),
            jax.ShapeDtypeStruct((nseq, n_even, 2, nh, dh), F32),
            jax.ShapeDtypeStruct((nseq, n_even, 2, 1, nh), F32),
        ]
        out_specs += [
            pl.BlockSpec((None, None, None, nh, dh, dh), lambda d, b, s: (b, j, d, 0, 0, 0)),
            pl.BlockSpec((None, None, None, nh, dh), lambda d, b, s: (b, j, d, 0, 0)),
            pl.BlockSpec((None, None, None, 1, nh), lambda d, b, s: (b, j, d, 0, 0)),
        ]
    return pl.pallas_call(
        functools.partial(_mlstm_kernel, has_init=has_init, emit_state=emit_state, nchunks=nchunks),
        out_shape=out_shape,
        grid=(2, nseq, nchunks),
        in_specs=in_specs,
        out_specs=out_specs,
        scratch_shapes=[pltpu.VMEM((nh, dh, dh), F32), pltpu.VMEM((nh, dh), F32), pltpu.VMEM((1, nh), F32)],
        compiler_params=_params("arbitrary", "arbitrary", "arbitrary"),
        name="mlstm_ctx" if emit_state else "mlstm_lat",
    )(*args)


def _even_out_kernel(*refs, n_ctx_tiles, tiles_per_latent, ctx_len, lat_len, split_tiles):
    nhr = 1 if split_tiles is None else 2
    h_refs = refs[:nhr]
    (gate_ref, hsc_ref, hsl_ref, o_ref, p_ref, pprev_ref, pnext_ref, gain_ref, pw_ref, ps_ref, w_ref, out_ref,
     cat_scr) = refs[nhr:]
    i = pl.program_id(0)
    tm = TOKEN_TILE
    nh, dh = MLSTM_HEADS, MLSTM_DH
    wm = nh * dh

    @pl.when(pl.program_id(1) == 0)
    def _():
        is_ctx = i < n_ctx_tiles
        hs = jnp.where(is_ctx, hsc_ref[0] + hsc_ref[1], hsl_ref[0] + hsl_ref[1])
        for h in range(nh):
            sl = slice(h * dh, (h + 1) * dh)
            x = hs[:, sl]
            y = x * lax.rsqrt(jnp.mean(x * x, axis=-1, keepdims=True) + EPS) * gain_ref[:, sl]
            cat_scr[:, sl] = (y * jax.nn.sigmoid(o_ref[:, sl])).astype(BF16)

        tiles_ctx = ctx_len // tm
        pos = jnp.where(is_ctx, i % tiles_ctx, (i - n_ctx_tiles) % tiles_per_latent)
        ntile = jnp.where(is_ctx, tiles_ctx, tiles_per_latent)
        seqlen = jnp.where(is_ctx, ctx_len, lat_len)
        x = p_ref[...]
        prev = jnp.where(pos > 0, pprev_ref[...], 0.0)
        nxt = jnp.where(pos < ntile - 1, pnext_ref[...], 0.0)
        pad = jnp.zeros((LANES - 2 * POOL_HALO, x.shape[1]), F32)
        xcat = jnp.concatenate([prev, x, nxt, pad], axis=0)
        xh, xl = _split2(xcat)
        t = lax.broadcasted_iota(jnp.int32, (tm, tm + LANES), 0)
        sidx = lax.broadcasted_iota(jnp.int32, (tm, tm + LANES), 1) - POOL_HALO
        tpos = pos * tm + lax.broadcasted_iota(jnp.int32, (tm, 1), 0)
        gw = wm // len(POOL_WINDOWS)
        for gi, win in enumerate(POOL_WINDOWS):
            sl = slice(gi * gw, (gi + 1) * gw)
            band = jnp.where((sidx >= t - win // 2) & (sidx < t - win // 2 + win), 1.0, 0.0).astype(BF16)
            lo = jnp.maximum(tpos - win // 2, 0)
            hi = jnp.minimum(tpos - win // 2 + win, seqlen)
            cnt = (hi - lo).astype(F32)
            p = (_dot(band, xh[:, sl]) + _dot(band, xl[:, sl])) / cnt - x[:, sl]
            y = _dot(p.astype(BF16), pw_ref[gi]) * ps_ref[:, sl]
            cat_scr[:, wm + gi * gw:wm + (gi + 1) * gw] = y.astype(BF16)

    out_ref[...] = (_load_stream(h_refs, split_tiles)
                    + gate_ref[...] * _dot(cat_scr[...], pltpu.bitcast(w_ref[...], BF16)))


def _even_out(h, mod3, layer, hs_ctx, hs_lat, z, gain, pool_w_bf16, pool_scale, w_out_bf16, n_ctx_tiles,
              tiles_per_latent, ctx_len, lat_len, tn=2048):
    tm = TOKEN_TILE
    d = 2 * w_out_bf16.shape[0]
    h_arrays, h_specs, n, split_tiles = _stream(h, tm, tn, col_of=lambda j: j)
    wm = MLSTM_HEADS * MLSTM_DH
    nlt = n // tm - n_ctx_tiles
    rows8 = n // POOL_HALO
    per = tm // POOL_HALO

    def gate_idx(i, j):
        return ((layer * 8 + _mod_row(i, n_ctx_tiles, tiles_per_latent)) * N_MOD + 2, 0, j)

    return pl.pallas_call(
        functools.partial(_even_out_kernel, n_ctx_tiles=n_ctx_tiles, tiles_per_latent=tiles_per_latent,
                          ctx_len=ctx_len, lat_len=lat_len, split_tiles=split_tiles),
        out_shape=jax.ShapeDtypeStruct((n, d), F32),
        grid=(n // tm, d // tn),
        in_specs=h_specs + [
            pl.BlockSpec((None, 1, tn), gate_idx),
            pl.BlockSpec((2, tm, wm), lambda i, j: (0, jnp.minimum(i, n_ctx_tiles - 1), 0)),
            pl.BlockSpec((2, tm, wm), lambda i, j: (0, jnp.clip(i - n_ctx_tiles, 0, nlt - 1), 0)),
            pl.BlockSpec((tm, wm), lambda i, j: (i, 3)),
            pl.BlockSpec((tm, wm), lambda i, j: (i, 4)),
            pl.BlockSpec((POOL_HALO, wm), lambda i, j: (jnp.maximum(i * per - 1, 0), 4)),
            pl.BlockSpec((POOL_HALO, wm), lambda i, j: (jnp.minimum((i + 1) * per, rows8 - 1), 4)),
            pl.BlockSpec((1, wm), lambda i, j: (0, 0)),
            pl.BlockSpec(pool_w_bf16.shape, lambda i, j: (0, 0, 0)),
            pl.BlockSpec((1, wm), lambda i, j: (0, 0)),
            pl.BlockSpec((d // 2, tn), lambda i, j: (0, j)),
        ],
        out_specs=pl.BlockSpec((tm, tn), lambda i, j: (i, j)),
        scratch_shapes=[pltpu.VMEM((tm, d), BF16)],
        compiler_params=_params("arbitrary", "arbitrary"),
        name="even_out",
    )(*h_arrays, mod3, hs_ctx, hs_lat, z, z, z, z, gain.reshape(1, wm), pool_w_bf16, pool_scale.reshape(1, wm),
      w_out_bf16)


def _qk_kernel(*refs, rope, emit_cache):
    q_ref, k_ref, v_ref, gain_ref = refs[:4]
    rest = list(refs[4:])
    if rope:
        cos_ref, sin_ref = rest[:2]
        rest = rest[2:]
    qn_ref, kn_ref, vb_ref = rest[:3]
    rest = rest[3:]
    hd = 2 * DA_DH
    lane = lax.broadcasted_iota(jnp.int32, (q_ref.shape[0], hd), 1)
    first = lane < DA_DH

    def norm(x, gain):
        sq = x * x
        s1 = jnp.sum(jnp.where(first, sq, 0.0), axis=-1, keepdims=True)
        s2 = jnp.sum(jnp.where(first, 0.0, sq), axis=-1, keepdims=True)
        ms = jnp.where(first, s1, s2) * (1.0 / DA_DH)
        y = x * lax.rsqrt(ms + EPS) * gain
        if rope:
            quarter = DA_DH // 4
            partner = jnp.where((lane & quarter) == 0, pltpu.roll(y, LANES - quarter, 1), pltpu.roll(y, quarter, 1))
            y = y * cos_ref[...] + partner * sin_ref[...]
        return y

    vb_ref[...] = v_ref[...].astype(BF16)
    for h in range(DA_HEADS):
        cols = slice(h * hd, (h + 1) * hd)
        qn = norm(q_ref[:, cols], gain_ref[0:1, :])
        kn = norm(k_ref[:, cols], gain_ref[1:2, :])
        qn_ref[:, cols] = (qn * (DA_DH ** -0.5)).astype(BF16)
        kn_ref[:, cols] = kn.astype(BF16)
        if emit_cache:
            newk_ref, newv_ref = rest
            newk_ref[h] = kn
            newv_ref[h] = v_ref[:, cols]


def _qk(z, qk_gain2, tok_off, ntok, j, rope_tables=None, cache_shape=None, seqlen=None):
    tm = TOKEN_TILE
    hd = 2 * DA_DH
    w = DA_HEADS * hd
    off = tok_off // tm
    nblk_w = w // hd
    rope = rope_tables is not None
    emit_cache = cache_shape is not None
    in_specs = [
        pl.BlockSpec((tm, w), lambda i: (off + i, 0)),
        pl.BlockSpec((tm, w), lambda i: (off + i, 1)),
        pl.BlockSpec((tm, w), lambda i: (off + i, 2)),
        pl.BlockSpec((2, hd), lambda i: (0, 0)),
    ]
    args = [z, z, z, qk_gain2]
    if rope:
        tps = seqlen // tm
        in_specs += [pl.BlockSpec((tm, hd), lambda i: (i % tps, 0))] * 2
        args += list(rope_tables)
    out_shape = [jax.ShapeDtypeStruct((ntok, w), BF16)] * 3
    out_specs = [pl.BlockSpec((tm, w), lambda i: (i, 0))] * 3
    if emit_cache:
        out_shape += [jax.ShapeDtypeStruct(cache_shape, F32)] * 2
        out_specs += [pl.BlockSpec((None, None, DA_HEADS, tm, hd), lambda i: (i, j, 0, 0, 0))] * 2
    return pl.pallas_call(
        functools.partial(_qk_kernel, rope=rope, emit_cache=emit_cache),
        out_shape=out_shape,
        grid=(ntok // tm,),
        in_specs=in_specs,
        out_specs=out_specs,
        compiler_params=_params("arbitrary"),
        name="qk_ctx" if emit_cache else "qk_lat",
    )(*args)


def _rope_tables(t):
    rows = t // GRID_W
    row = jnp.repeat(jnp.arange(rows), GRID_W).astype(F32)
    col = (jnp.arange(rows * GRID_W) % GRID_W).astype(F32)
    n_freq = DA_DH // 4
    inv = ROPE_BASE ** (-jnp.arange(n_freq, dtype=F32) / n_freq)
    ar, ac = row[:, None] * inv, col[:, None] * inv
    cos = jnp.concatenate([jnp.cos(ar), jnp.cos(ar), jnp.cos(ac), jnp.cos(ac)], axis=-1)
    sin = jnp.concatenate([-jnp.sin(ar), jnp.sin(ar), -jnp.sin(ac), jnp.sin(ac)], axis=-1)
    return jnp.tile(cos, (1, 2)), jnp.tile(sin, (1, 2))


def _attn_kernel(*refs, n_pieces, lam_init, kc):
    q_ref, lp_ref, subln_ref = refs[:3]
    kv = refs[3:3 + 2 * n_pieces]
    o_ref, sa_scr, sb_scr = refs[3 + 2 * n_pieces:]
    lp = lp_ref[...]
    lam = (jnp.exp(jnp.sum(lp[0:1] * lp[1:2], axis=-1, keepdims=True))
           - jnp.exp(jnp.sum(lp[2:3] * lp[3:4], axis=-1, keepdims=True)) + lam_init)
    q = q_ref[...]
    lane = lax.broadcasted_iota(jnp.int32, q.shape, 1)
    zero = jnp.zeros_like(q)
    qa = jnp.where(lane < DA_DH, q, zero)
    qb = jnp.where(lane < DA_DH, zero, q)
    chunks = [(kv[2 * p], kv[2 * p + 1], c) for p in range(n_pieces) for c in range(kv[2 * p].shape[0] // kc)]

    def lane_fold(x, op):
        return functools.reduce(op, [x[:, t * LANES:(t + 1) * LANES] for t in range(x.shape[1] // LANES)])

    def attend(qx, s_scr):
        m = None
        for n, (k_ref, _, c) in enumerate(chunks):
            s = _dot_nt(qx, k_ref[c * kc:(c + 1) * kc, :].astype(BF16))
            s_scr[:, n * kc:(n + 1) * kc] = s
            part = lane_fold(s, jnp.maximum)
            m = part if m is None else jnp.maximum(m, part)
        m = jnp.max(m, axis=-1, keepdims=True)
        tot, pv = None, None
        for n, (_, v_ref, c) in enumerate(chunks):
            e = jnp.exp(s_scr[:, n * kc:(n + 1) * kc] - m)
            part = lane_fold(e, lambda a, b: a + b)
            contrib = _dot(e.astype(BF16), v_ref[c * kc:(c + 1) * kc, :].astype(BF16))
            tot = part if tot is None else tot + part
            pv = contrib if pv is None else pv + contrib
        return pv / jnp.sum(tot, axis=-1, keepdims=True)

    o = attend(qa, sa_scr) - lam * attend(qb, sb_scr)
    o = o * lax.rsqrt(jnp.mean(o * o, axis=-1, keepdims=True) + EPS) * subln_ref[...]
    o_ref[...] = (o * (1.0 - lam_init)).astype(BF16)


def _attn(qn, kn, vb, lam_params, subln, lam_init, nseq, seqlen, tq, j, caches=None):
    hd = 2 * DA_DH
    nq = seqlen // tq
    in_specs = [
        pl.BlockSpec((tq, hd), lambda b, h, qi: (b * nq + qi, h)),
        pl.BlockSpec((None, 4, DA_DH), lambda b, h, qi: (j, 0, 0)),
        pl.BlockSpec((None, 1, hd), lambda b, h, qi: (j, 0, 0)),
        pl.BlockSpec((seqlen, hd), lambda b, h, qi: (b, h)),
        pl.BlockSpec((seqlen, hd), lambda b, h, qi: (b, h)),
    ]
    args = [qn, lam_params, subln.reshape(subln.shape[0], 1, hd), kn, vb]
    n_pieces = 1
    kc = min(seqlen, 2 * LANES)
    assert seqlen % kc == 0
    if caches is not None:
        ck, cv = caches
        past = ck.shape[3]
        assert past % kc == 0
        in_specs += [pl.BlockSpec((None, None, None, past, hd), lambda b, h, qi: (b, j, h, 0, 0))] * 2
        args += [ck, cv]
        n_pieces = 2
    n_keys = seqlen + (caches[0].shape[3] if caches is not None else 0)
    return pl.pallas_call(
        functools.partial(_attn_kernel, n_pieces=n_pieces, lam_init=lam_init, kc=kc),
        out_shape=jax.ShapeDtypeStruct(qn.shape, BF16),
        grid=(nseq, DA_HEADS, nq),
        in_specs=in_specs,
        out_specs=pl.BlockSpec((tq, hd), lambda b, h, qi: (b * nq + qi, h)),
        scratch_shapes=[pltpu.VMEM((tq, n_keys), F32)] * 2,
        compiler_params=_params("arbitrary", "arbitrary", "arbitrary"),
        name="attn_lat" if caches is not None else "attn_ctx",
    )(*args)


def _odd_out_kernel(h_ref, gate_ref, ac_ref, al_ref, gu_ref, gv_ref, ws_ref, bt_ref, w_ref, out_ref, cat_scr, *,
                    n_ctx_tiles):
    i = pl.program_id(0)
    tm = TOKEN_TILE
    wa = DA_HEADS * 2 * DA_DH
    gw = LANES

    @pl.when(pl.program_id(1) == 0)
    def _():
        cat_scr[:, 0:wa] = jnp.where(i < n_ctx_tiles, ac_ref[...], al_ref[...])
        for c in range(tm // GM_CHUNK):
            rows = slice(c * GM_CHUNK, (c + 1) * GM_CHUNK)
            for g in range(GM_GROUPS):
                cols = slice(g * gw, (g + 1) * gw)
                u = _gelu_tanh(gu_ref[rows, cols])
                v = _gelu_tanh(gv_ref[rows, cols])
                vn = v * lax.rsqrt(jnp.mean(v * v, axis=-1, keepdims=True) + EPS)
                mixed = _dot(ws_ref[g], vn.astype(BF16)) + bt_ref[:, g:g + 1]
                cat_scr[rows, wa + g * gw:wa + (g + 1) * gw] = (u * mixed).astype(BF16)

    out_ref[...] = h_ref[...] + gate_ref[...] * _dot(cat_scr[...], pltpu.bitcast(w_ref[...], BF16))


def _odd_out(h, mod3, layer, a_ctx, a_lat, z, gm_ws_bf16, gm_b_t, w_out_bf16, n_ctx_tiles, tiles_per_latent, tn=2048):
    n, d = h.shape
    tm = TOKEN_TILE
    wa = DA_HEADS * 2 * DA_DH
    wg = GM_GROUPS * LANES
    nlt = n // tm - n_ctx_tiles

    def gate_idx(i, j):
        return ((layer * 8 + _mod_row(i, n_ctx_tiles, tiles_per_latent)) * N_MOD + 2, 0, j)

    return pl.pallas_call(
        functools.partial(_odd_out_kernel, n_ctx_tiles=n_ctx_tiles),
        out_shape=jax.ShapeDtypeStruct((n, d), F32),
        grid=(n // tm, d // tn),
        in_specs=[
            pl.BlockSpec((tm, tn), lambda i, j: (i, j)),
            pl.BlockSpec((None, 1, tn), gate_idx),
            pl.BlockSpec((tm, wa), lambda i, j: (jnp.minimum(i, n_ctx_tiles - 1), 0)),
            pl.BlockSpec((tm, wa), lambda i, j: (jnp.clip(i - n_ctx_tiles, 0, nlt - 1), 0)),
            pl.BlockSpec((tm, wg), lambda i, j: (i, 3)),
            pl.BlockSpec((tm, wg), lambda i, j: (i, 4)),
            pl.BlockSpec(gm_ws_bf16.shape, lambda i, j: (0, 0, 0)),
            pl.BlockSpec(gm_b_t.shape, lambda i, j: (0, 0)),
            pl.BlockSpec((d // 2, tn), lambda i, j: (0, j)),
        ],
        out_specs=pl.BlockSpec((tm, tn), lambda i, j: (i, j)),
        scratch_shapes=[pltpu.VMEM((tm, d), BF16)],
        compiler_params=_params("arbitrary", "arbitrary"),
        name="odd_out",
    )(h, mod3, a_ctx, a_lat, z, z, gm_ws_bf16, gm_b_t, w_out_bf16)


def _top_values(x, k, with_rank=False):
    vals = []
    cur = x
    rank = jnp.full(x.shape, float(k), F32) if with_rank else None
    for r in range(k):
        m = jnp.max(cur, axis=0, keepdims=True)
        vals.append(m)
        hit = cur == m
        if with_rank:
            rank = jnp.where(hit, float(r), rank)
        if r + 1 < k:
            cur = jnp.where(hit, NEG_INF, cur)
    vals = jnp.concatenate(vals, axis=0)
    return (vals, rank) if with_rank else vals


def _dup_bf16(x):
    b = pltpu.bitcast(x.astype(BF16).astype(F32), jnp.uint32)
    return pltpu.bitcast(b | (b >> 16), F32)


def _peer_route_kernel(q_ref, sk_ref, rank_ref, e2_ref, cnt_ref, c_ref):
    half = PEER_NKEYS
    k = PEER_TOPK
    for h in range(PEER_HEADS):
        q1 = q_ref[:, 2 * h * half:(2 * h + 1) * half].astype(BF16)
        q2 = q_ref[:, (2 * h + 1) * half:(2 * h + 2) * half].astype(BF16)
        s1 = _dot_nt(sk_ref[2 * h], q1)
        s2 = _dot_nt(sk_ref[2 * h + 1], q2)
        a = _top_values(s1, k)
        b, rank2 = _top_values(s2, k, with_rank=True)
        rows = [a[r:r + 1] + b[0:k // (r + 1)] for r in range(k)]
        nrows = sum(k // (r + 1) for r in range(k))
        rows.append(jnp.full((-nrows % SUBLANES, s1.shape[1]), NEG_INF, F32))
        top = _top_values(jnp.concatenate(rows, axis=0), k)
        tau = top[k - 1:k]
        z = jnp.sum(jnp.exp(top - top[0:1]), axis=0, keepdims=True)
        cnt = jnp.zeros(s1.shape, F32)
        for r in range(k):
            cnt = jnp.where(s1 + b[r:r + 1] >= tau, float(r + 1), cnt)
        rank_ref[h] = pltpu.bitcast(rank2.astype(BF16), F32)
        e2_ref[h] = pltpu.bitcast(jnp.exp(s2 - b[0:1]).astype(BF16), F32)
        cnt_ref[h] = _dup_bf16(cnt)
        c_ref[h] = _dup_bf16(jnp.exp(s1 - a[0:1]) / z)


def _peer_route(qp, subkeys_bf16, tm=512):
    n = qp.shape[0]
    def out(rows):
        return (jax.ShapeDtypeStruct((PEER_HEADS, rows, n), F32),
                pl.BlockSpec((PEER_HEADS, rows, tm), lambda i: (0, 0, i)))

    outs = [out(PEER_NKEYS // 2)] * 2 + [out(PEER_NKEYS)] * 2
    return pl.pallas_call(
        _peer_route_kernel,
        out_shape=[o[0] for o in outs],
        grid=(n // tm,),
        in_specs=[
            pl.BlockSpec((tm, qp.shape[1]), lambda i: (i, 0)),
            pl.BlockSpec(subkeys_bf16.shape, lambda i: (0, 0, 0)),
        ],
        out_specs=[o[1] for o in outs],
        compiler_params=_params("arbitrary"),
        name="peer_route",
    )(qp, subkeys_bf16)


def _peer_dense_kernel(xnt_ref, h_ref, gate_ref, u_ref, vt_ref, rank_ref, e2_ref, cnt_ref, c_ref, *rest, te, n_tiles,
                       split_tiles):
    n_out = 1 if split_tiles is None else 2
    out_refs = rest[:n_out]
    act0_scr, act1_scr, wa0_scr, wa1_scr, acc_scr = rest[n_out:]
    s = pl.program_id(0)
    n_pairs = pl.num_programs(0) - 2
    nkeys = PEER_NKEYS
    per = te // nkeys
    tm = acc_scr.shape[1]
    half = tm // 2
    assert per % 4 == 0
    pack = 2 * SUBLANES

    @pl.when(s == 0)
    def _():
        act1_scr[...] = jnp.zeros_like(act1_scr)
        wa0_scr[...] = jnp.zeros_like(wa0_scr)

    live = jnp.logical_and(s >= 1, s <= n_pairs)
    t2 = jnp.clip(s - 1, 0, n_pairs - 1) % n_tiles
    pair3 = jnp.clip(s - 2, 0, n_pairs - 1)
    t3 = pair3 % n_tiles
    tok3 = pair3 // n_tiles

    @pl.when(t3 == 0)
    def _():
        acc_scr[...] = jnp.zeros_like(acc_scr)

    def step(act_w, act_r, wa_w, wa_r):
        def stage1(c):
            cols = slice(c * half, (c + 1) * half)
            act_w[:, cols] = _dot(pltpu.bitcast(u_ref[...], BF16),
                                  pltpu.bitcast(xnt_ref[:, cols], BF16))

        def stage3(c):
            cols = slice(c * half, (c + 1) * half)
            acc_scr[:, cols] += _dot(pltpu.bitcast(vt_ref[...], BF16), wa_r[:, cols])

        mxu_work = [functools.partial(stage1, 0), functools.partial(stage3, 0),
                    functools.partial(stage1, 1), functools.partial(stage3, 1)]
        stride = per // len(mxu_work)
        for ii in range(per):
            if ii % stride == 0:
                mxu_work[ii // stride]()
            r = t2 * per + ii
            cnt_rows = [jnp.where(live, cnt_ref[h, pl.ds(r, 1), :], 0.0) for h in range(PEER_HEADS)]
            c_rows = [c_ref[h, pl.ds(r, 1), :] for h in range(PEER_HEADS)]
            for lt in range(tm // LANES):
                lanes = slice(lt * LANES, (lt + 1) * LANES)
                cnts = [pltpu.bitcast(jnp.broadcast_to(x[:, lanes], (SUBLANES, LANES)), BF16) for x in cnt_rows]
                ccs = [pltpu.bitcast(jnp.broadcast_to(x[:, lanes], (SUBLANES, LANES)), BF16) for x in c_rows]
                for sb in range(nkeys // pack):
                    words = slice(sb * SUBLANES, (sb + 1) * SUBLANES)
                    rows = slice(ii * nkeys + sb * pack, ii * nkeys + (sb + 1) * pack)
                    w = None
                    for h in range(PEER_HEADS):
                        e2 = pltpu.bitcast(e2_ref[h, words, lanes], BF16)
                        rank = pltpu.bitcast(rank_ref[h, words, lanes], BF16)
                        wh = jnp.where(rank < cnts[h], e2 * ccs[h], jnp.zeros_like(e2))
                        w = wh if w is None else w + wh
                    wa_w[rows, lanes] = w * _gelu_tanh_lowp(act_r[rows, lanes].astype(BF16))

    @pl.when(s % 2 == 0)
    def _():
        step(act0_scr, act1_scr, wa1_scr, wa0_scr)

    @pl.when(s % 2 == 1)
    def _():
        step(act1_scr, act0_scr, wa0_scr, wa1_scr)

    @pl.when(jnp.logical_and(t3 == n_tiles - 1, s >= 2))
    def _():
        res = h_ref[...] + gate_ref[...] * acc_scr[...].T
        if split_tiles is None:
            out_refs[0][...] = res
        else:
            @pl.when(tok3 < split_tiles)
            def _():
                out_refs[0][...] = res

            @pl.when(tok3 >= split_tiles)
            def _():
                out_refs[1][...] = res


def _peer_dense(xnt, h, mod3, layer, u_packed, vt_packed, rank2, e2, cnt, c, n_ctx_tiles, tiles_per_latent, tm=512,
                te=512, split_out=False):
    n, d = h.shape
    n_tiles = 2 * u_packed.shape[0] // te
    per = tm // TOKEN_TILE
    n_pairs = (n // tm) * n_tiles

    def tok(lag):
        return lambda t: jnp.clip(t - lag, 0, n_pairs - 1) // n_tiles

    def exp(lag):
        return lambda t: jnp.clip(t - lag, 0, n_pairs - 1) % n_tiles

    tok1, tok2, tok3, exp1, exp3 = tok(0), tok(1), tok(2), exp(0), exp(2)
    if split_out:
        split_tiles = n_ctx_tiles // per
        n_ctx = split_tiles * tm
        out_shape = [jax.ShapeDtypeStruct((n_ctx, d), F32), jax.ShapeDtypeStruct((n - n_ctx, d), F32)]
        out_specs = [pl.BlockSpec((tm, d), lambda t: (jnp.minimum(tok3(t), split_tiles - 1), 0)),
                     pl.BlockSpec((tm, d), lambda t: (jnp.maximum(tok3(t) - split_tiles, 0), 0))]
    else:
        split_tiles = None
        out_shape = jax.ShapeDtypeStruct((n, d), F32)
        out_specs = pl.BlockSpec((tm, d), lambda t: (tok3(t), 0))

    def gate_idx(t):
        return ((layer * 8 + _mod_row(tok3(t) * per, n_ctx_tiles, tiles_per_latent)) * N_MOD + 5, 0, 0)

    rspec = pl.BlockSpec((PEER_HEADS, PEER_NKEYS, tm), lambda t: (0, 0, tok2(t)))
    pspec = pl.BlockSpec((PEER_HEADS, PEER_NKEYS // 2, tm), lambda t: (0, 0, tok2(t)))
    return pl.pallas_call(
        functools.partial(_peer_dense_kernel, te=te, n_tiles=n_tiles, split_tiles=split_tiles),
        out_shape=out_shape,
        grid=(n_pairs + 2,),
        in_specs=[
            pl.BlockSpec((d // 2, tm), lambda t: (0, tok1(t))),
            pl.BlockSpec((tm, d), lambda t: (tok3(t), 0), pipeline_mode=pl.Buffered(1)),
            pl.BlockSpec((None, 1, d), gate_idx),
            pl.BlockSpec((te // 2, d), lambda t: (exp1(t), 0)),
            pl.BlockSpec((d // 2, te), lambda t: (0, exp3(t))),
            pspec, pspec, rspec, rspec,
        ],
        out_specs=out_specs,
        scratch_shapes=[pltpu.VMEM((te, tm), F32), pltpu.VMEM((te, tm), F32), pltpu.VMEM((te, tm), BF16),
                        pltpu.VMEM((te, tm), BF16), pltpu.VMEM((d, tm), F32)],
        compiler_params=_params("arbitrary"),
        name="peer_dense",
    )(xnt, h, mod3, u_packed, vt_packed, rank2, e2, cnt, c)


def kernel(x_prompt, x_sample, state_mlstm_C, state_mlstm_n, state_mlstm_m, cache_da_k, cache_da_v, c, c_ctx, norm_mix, norm_ffn, w_mod, b_mod, w_in_even, b_gate_even, mlstm_gain, pool_w, pool_scale, w_out_even, w_in_odd, qk_gain, da_lambda, da_subln, gm_ws, gm_b, w_out_odd, peer_wq, peer_subkeys, peer_u, peer_v):
    nb, s_len, d = x_prompt.shape
    nbd, t_len, _ = x_sample.shape
    depth = w_mod.shape[0]
    tm = TOKEN_TILE
    assert s_len == tm and t_len % tm == 0 and nbd <= 7 and t_len % GRID_W == 0
    n_ctx = nb * s_len
    n_lat = nbd * t_len
    n_ctx_tiles = n_ctx // tm
    tiles_per_latent = t_len // tm
    n_even = (depth + 1) // 2
    n_odd = depth // 2

    h = (x_prompt.reshape(n_ctx, d), x_sample.reshape(n_lat, d))
    cond8 = jnp.concatenate([c_ctx[None], c, jnp.zeros((7 - nbd, d), F32)], axis=0)
    mod3 = _adaln(cond8, w_mod, b_mod).reshape(depth * 8 * N_MOD, 1, d)
    rope = _rope_tables(t_len)

    new_c, new_n, new_m, new_k, new_v = [], [], [], [], []
    wm = MLSTM_HEADS * MLSTM_DH
    for l in range(depth):
        j = l // 2
        if l % 2 == 0:
            n_main = 5 * wm
            z, gates, gates_t = _inproj(h, norm_mix[l], mod3, l, 0, _pack_rows(w_in_even, j, ncols=n_main),
                                        n_ctx_tiles, tiles_per_latent, w_gates=w_in_even[j, :, n_main:],
                                        b_gates=b_gate_even[j],
                                        name="inproj_even")
            hs_ctx, cc, cn, cm = _mlstm(z, gates, gates_t, 0, nb, s_len, 0, emit_state=True)
            (hs_lat,) = _mlstm(z, gates, gates_t, n_ctx, nbd, t_len, j,
                               init=(state_mlstm_C, state_mlstm_n, state_mlstm_m))
            new_c.append(cc)
            new_n.append(cn)
            new_m.append(cm[:, :, :, 0, :])
            h = _even_out(h, mod3, l, hs_ctx, hs_lat, z, mlstm_gain[j], pool_w[j].astype(BF16), pool_scale[j],
                          _pack_rows(w_out_even, j), n_ctx_tiles, tiles_per_latent, s_len, t_len)
        else:
            lam_init = 0.8 - 0.6 * math.exp(-0.3 * l)
            (z,) = _inproj(h, norm_mix[l], mod3, l, 0, _pack_rows(w_in_odd, j), n_ctx_tiles, tiles_per_latent,
                           name="inproj_odd")
            gain2 = jnp.tile(qk_gain[j], (1, 2))
            cache_shape = (nb, 1, DA_HEADS, s_len, 2 * DA_DH)
            qn_c, kn_c, vb_c, nk, nv = _qk(z, gain2, 0, n_ctx, 0, cache_shape=cache_shape)
            qn_l, kn_l, vb_l = _qk(z, gain2, n_ctx, n_lat, 0, rope_tables=rope, seqlen=t_len)
            new_k.append(nk)
            new_v.append(nv)
            a_ctx = _attn(qn_c, kn_c, vb_c, da_lambda, da_subln, lam_init, nb, s_len, s_len, j)
            a_lat = _attn(qn_l, kn_l, vb_l, da_lambda, da_subln, lam_init, nbd, t_len, tm, j,
                          caches=(cache_da_k, cache_da_v))
            h = _odd_out(h, mod3, l, a_ctx, a_lat, z, gm_ws[j].astype(BF16), gm_b[j].T, _pack_rows(w_out_odd, j),
                         n_ctx_tiles, tiles_per_latent)
        qp, xnt = _inproj(h, norm_ffn[l], mod3, l, 3, _pack_rows(peer_wq, l), n_ctx_tiles, tiles_per_latent,
                          emit_xn=True, name="inproj_peer")
        sk = peer_subkeys[l].reshape(2 * PEER_HEADS, PEER_NKEYS, PEER_NKEYS).astype(BF16)
        rank2, e2, cnt, cw = _peer_route(qp, sk)
        h = _peer_dense(xnt, h, mod3, l, _pack_rows(peer_u, l), _pack_rows(peer_v, l, transpose=True), rank2,
                        e2, cnt, cw,
                        n_ctx_tiles, tiles_per_latent, split_out=(l == depth - 1))

    def join(parts):
        return parts[0] if len(parts) == 1 else jnp.concatenate(parts, axis=1)

    return (h[0].reshape(nb, s_len, d), h[1].reshape(nbd, t_len, d), join(new_c), join(new_n), join(new_m),
            join(new_k), join(new_v))
```

```python
import functools
import math

import jax
import jax.numpy as jnp
from jax import lax
from jax.experimental import pallas as pl
from jax.experimental.pallas import tpu as pltpu

F32 = jnp.float32
BF16 = jnp.bfloat16

N_MOD = 6
EPS = 1e-6
TOKEN_TILE = 256
LANES = 128
SUBLANES = 8
VMEM_LIMIT_BYTES = 56 * 1024 * 1024

MLSTM_HEADS = 4
MLSTM_DH = 256
MLSTM_CHUNK = 128
POOL_WINDOWS = (2, 4, 8, 16)
POOL_HALO = 8
DA_HEADS = 8
DA_DH = 64
GRID_W = 64
ROPE_BASE = 10000.0
GM_GROUPS = 8
GM_CHUNK = 128
PEER_HEADS = 8
PEER_NKEYS = 128
PEER_TOPK = 16
NEG_INF = float("-inf")
POS_INF = float("inf")


def _params(*sem):
    return pltpu.CompilerParams(dimension_semantics=sem, vmem_limit_bytes=VMEM_LIMIT_BYTES)


def _dot(a, b):
    return jnp.dot(a, b, preferred_element_type=F32)


def _dot_nt(a, b):
    return lax.dot_general(a, b, (((1,), (1,)), ((), ())), preferred_element_type=F32)


def _dot_tn(a, b):
    return lax.dot_general(a, b, (((0,), (0,)), ((), ())), preferred_element_type=F32)


def _split2(x):
    hi = x.astype(BF16)
    lo = (x - hi.astype(F32)).astype(BF16)
    return hi, lo


def _split3(x):
    hi = x.astype(BF16)
    r = x - hi.astype(F32)
    mid = r.astype(BF16)
    lo = (r - mid.astype(F32)).astype(BF16)
    return hi, mid, lo


def _dot3(a, b, dot=_dot):
    ah, al = _split2(a)
    bh, bl = _split2(b)
    return dot(ah, bh) + (dot(ah, bl) + dot(al, bh))


def _gelu_tanh(x):
    c = math.sqrt(2.0 / math.pi)
    half = 0.5 * x
    return half + half * jnp.tanh(x * (c + (0.044715 * c) * (x * x)))


def _gelu_tanh_lowp(x):
    c = -2.0 * math.sqrt(2.0 / math.pi)
    return x / (1.0 + jnp.exp(x * (c + (0.044715 * c) * (x * x))))


def _log_sigmoid(x):
    return -(jnp.maximum(-x, 0.0) + jnp.log1p(jnp.exp(-jnp.abs(x))))


def _mod_row(i, n_ctx_tiles, tiles_per_latent):
    return jnp.where(i < n_ctx_tiles, 0, 1 + (i - n_ctx_tiles) // tiles_per_latent)


def _pack_kernel(x_ref, o_ref, *, transpose):
    x = x_ref[...]
    if transpose:
        x = x.T
    o_ref[...] = pltpu.bitcast(x.astype(BF16), F32)


def _pack_rows(x, layer, ncols=None, transpose=False):
    _, r, c = x.shape
    c = c if ncols is None else ncols
    tr, tc = 512, 1024
    assert r % tr == 0 and c % tc == 0
    if transpose:
        out_shape, out_spec = (c // 2, r), pl.BlockSpec((tc // 2, tr), lambda i, j: (j, i))
    else:
        out_shape, out_spec = (r // 2, c), pl.BlockSpec((tr // 2, tc), lambda i, j: (i, j))
    return pl.pallas_call(
        functools.partial(_pack_kernel, transpose=transpose),
        out_shape=jax.ShapeDtypeStruct(out_shape, F32),
        grid=(r // tr, c // tc),
        in_specs=[pl.BlockSpec((None, tr, tc), lambda i, j: (layer, i, j))],
        out_specs=out_spec,
        compiler_params=_params("arbitrary", "arbitrary"),
        name="pack_t" if transpose else "pack",
    )(x)


def _adaln_kernel(cond_ref, w_ref, b_ref, o_ref):
    c = cond_ref[...]
    s = c * jax.nn.sigmoid(c)
    o_ref[...] = _dot3(s, w_ref[...]) + b_ref[...]


def _adaln(cond8, w_mod, b_mod):
    depth, d, dout = w_mod.shape
    tn = 1024
    return pl.pallas_call(
        _adaln_kernel,
        out_shape=jax.ShapeDtypeStruct((depth, 8, dout), F32),
        grid=(depth, dout // tn),
        in_specs=[
            pl.BlockSpec((8, d), lambda l, j: (0, 0)),
            pl.BlockSpec((None, d, tn), lambda l, j: (l, 0, j)),
            pl.BlockSpec((None, 1, tn), lambda l, j: (l, 0, j)),
        ],
        out_specs=pl.BlockSpec((None, 8, tn), lambda l, j: (l, 0, j)),
        compiler_params=_params("arbitrary", "arbitrary"),
        name="adaln",
    )(cond8, w_mod, b_mod.reshape(depth, 1, dout))


def _stream(h, tm, cols, col_of=lambda j: 0, single_buffer_split=False):
    if not isinstance(h, tuple):
        return [h], [pl.BlockSpec((tm, cols), lambda i, j: (i, col_of(j)))], h.shape[0], None
    hc, hl = h
    nct = hc.shape[0] // tm
    kw = dict(pipeline_mode=pl.Buffered(1)) if single_buffer_split else {}
    specs = [pl.BlockSpec((tm, cols), lambda i, j: (jnp.minimum(i, nct - 1), col_of(j)), **kw),
             pl.BlockSpec((tm, cols), lambda i, j: (jnp.maximum(i - nct, 0), col_of(j)), **kw)]
    return [hc, hl], specs, hc.shape[0] + hl.shape[0], nct


def _load_stream(h_refs, split_tiles):
    if split_tiles is None:
        return h_refs[0][...]
    return jnp.where(pl.program_id(0) < split_tiles, h_refs[0][...], h_refs[1][...])


def _inproj_kernel(*refs, with_gates, emit_xn, split_tiles):
    nh = 1 if split_tiles is None else 2
    h_refs = refs[:nh]
    gain_ref, shift_ref, scale_ref, w_ref = refs[nh:nh + 4]
    rest = list(refs[nh + 4:])
    if with_gates:
        wg2_ref, bg_ref = rest[:2]
        rest = rest[2:]
    z_ref = rest.pop(0)
    if with_gates:
        g_ref, gt_ref = rest[:2]
        rest = rest[2:]
    if emit_xn:
        xn_out_ref = rest.pop(0)
    xn_scr = rest.pop(0)

    @pl.when(pl.program_id(1) == 0)
    def _():
        x = _load_stream(h_refs, split_tiles)
        xn = x * lax.rsqrt(jnp.mean(x * x, axis=-1, keepdims=True) + EPS) * gain_ref[...]
        xn = xn * (1.0 + scale_ref[...]) + shift_ref[...]
        xb = xn.astype(BF16)
        xn_scr[...] = xb
        if emit_xn:
            xn_out_ref[...] = pltpu.bitcast(xn.T.astype(BF16), F32)
        if with_gates:
            ng = bg_ref.shape[1]
            w2 = wg2_ref[...]
            w2_hi = w2.astype(BF16)
            lane = lax.broadcasted_iota(jnp.int32, w2.shape, 1)
            w_cat = jnp.where(lane < ng, w2_hi, (w2 - w2_hi.astype(F32)).astype(BF16))
            xl = (xn - xb.astype(F32)).astype(BF16)
            p = _dot(xb, w_cat)
            g = p[:, :ng] + (p[:, ng:] + _dot(xl, w2_hi)[:, :ng]) + bg_ref[...]
            g_ref[...] = g
            eye = (lax.broadcasted_iota(jnp.int32, (ng, ng), 0)
                   == lax.broadcasted_iota(jnp.int32, (ng, ng), 1)).astype(F32).astype(BF16)
            g3 = _split3(g)
            gt_ref[...] = _dot_nt(eye, g3[0]) + (_dot_nt(eye, g3[1]) + _dot_nt(eye, g3[2]))

    z_ref[...] = _dot(xn_scr[...], pltpu.bitcast(w_ref[...], BF16))


def _inproj(h, gain, mod3, layer, mod_base, w_packed, n_ctx_tiles, tiles_per_latent, w_gates=None, b_gates=None,
            emit_xn=False, tn=1024, name="inproj"):
    d, dout = 2 * w_packed.shape[0], w_packed.shape[1]
    tm = TOKEN_TILE * math.gcd(4, n_ctx_tiles, tiles_per_latent)
    per = tm // TOKEN_TILE
    with_gates = w_gates is not None
    h_arrays, h_specs, n, split_tiles = _stream(h, tm, d, single_buffer_split=True)

    def mod_idx(off):
        return lambda i, j: ((layer * 8 + _mod_row(i * per, n_ctx_tiles, tiles_per_latent)) * N_MOD + mod_base + off,
                             0, 0)

    in_specs = h_specs + [
        pl.BlockSpec((1, d), lambda i, j: (0, 0)),
        pl.BlockSpec((None, 1, d), mod_idx(0)),
        pl.BlockSpec((None, 1, d), mod_idx(1)),
        pl.BlockSpec((d // 2, tn), lambda i, j: (0, j)),
    ]
    args = h_arrays + [gain.reshape(1, d), mod3, mod3, w_packed]
    out_shape = [jax.ShapeDtypeStruct((n, dout), F32)]
    out_specs = [pl.BlockSpec((tm, tn), lambda i, j: (i, j))]
    if with_gates:
        ng = w_gates.shape[1]
        in_specs += [pl.BlockSpec((d, 2 * ng), lambda i, j: (0, 0)), pl.BlockSpec((1, ng), lambda i, j: (0, 0))]
        args += [jnp.concatenate([w_gates, w_gates], axis=1), b_gates.reshape(1, ng)]
        out_shape += [jax.ShapeDtypeStruct((n, ng), F32), jax.ShapeDtypeStruct((ng, n), F32)]
        out_specs += [pl.BlockSpec((tm, ng), lambda i, j: (i, 0)), pl.BlockSpec((ng, tm), lambda i, j: (0, i))]
    if emit_xn:
        out_shape.append(jax.ShapeDtypeStruct((d // 2, n), F32))
        out_specs.append(pl.BlockSpec((d // 2, tm), lambda i, j: (0, i)))
    return pl.pallas_call(
        functools.partial(_inproj_kernel, with_gates=with_gates, emit_xn=emit_xn, split_tiles=split_tiles),
        out_shape=out_shape,
        grid=(n // tm, dout // tn),
        in_specs=in_specs,
        out_specs=out_specs,
        scratch_shapes=[pltpu.VMEM((tm, d), BF16)],
        compiler_params=_params("arbitrary", "arbitrary"),
        name=name,
    )(*args)


def _mlstm_kernel(*refs, has_init, emit_state, nchunks):
    q_ref, k_ref, v_ref, g_ref, gt_ref = refs[:5]
    rest = list(refs[5:])
    if has_init:
        c0_ref, n0_ref, m0_ref = rest[:3]
        rest = rest[3:]
    hs_ref = rest.pop(0)
    if emit_state:
        cout_ref, nout_ref, mout_ref = rest[:3]
        rest = rest[3:]
    c_scr, n_scr, m_scr = rest
    nh, dh, L = MLSTM_HEADS, MLSTM_DH, MLSTM_CHUNK
    d = pl.program_id(0)
    s = pl.program_id(2)

    @pl.when(s == 0)
    def _():
        if has_init:
            c_scr[...] = c0_ref[...]
            n_scr[...] = n0_ref[...]
            m_scr[...] = m0_ref[...]
        else:
            c_scr[...] = jnp.zeros_like(c_scr)
            n_scr[...] = jnp.zeros_like(n_scr)
            m_scr[...] = jnp.zeros_like(m_scr)

    row = lax.broadcasted_iota(jnp.int32, (L, L), 0)
    col = lax.broadcasted_iota(jnp.int32, (L, L), 1)
    sgn = jnp.where(d == 0, 1, -1)
    mask = (row - col) * sgn >= 0
    maskb = jnp.where(mask, 1.0, 0.0).astype(BF16)

    g = g_ref[...]
    gt = gt_ref[...]
    fwd = d == 0
    i_col = jnp.where(fwd, g[:, 0:nh], g[:, nh:2 * nh])
    f_col = _log_sigmoid(jnp.where(fwd, g[:, 2 * nh:3 * nh], g[:, 3 * nh:4 * nh]))
    i_row = jnp.where(fwd, gt[0:nh], gt[nh:2 * nh])
    f_row = _log_sigmoid(jnp.where(fwd, gt[2 * nh:3 * nh], gt[3 * nh:4 * nh]))
    fc = _split3(f_col)
    b_col = _dot(maskb, fc[0]) + (_dot(maskb, fc[1]) + _dot(maskb, fc[2]))
    fr = _split3(f_row)
    b_row = _dot_nt(fr[0], maskb) + (_dot_nt(fr[1], maskb) + _dot_nt(fr[2], maskb))
    btot_col = jnp.sum(f_col, axis=0, keepdims=True)
    m_all = m_scr[...]

    m_new_parts = []
    for h in range(nh):
        sl = slice(h * dh, (h + 1) * dh)
        qh = q_ref[:, sl].astype(BF16)
        kf = k_ref[:, sl] * (dh ** -0.5)
        kh = kf.astype(BF16)
        vh = v_ref[:, sl].astype(BF16)
        b_c = b_col[:, h:h + 1]
        b_r = b_row[h:h + 1, :]
        i_c = i_col[:, h:h + 1]
        i_r = i_row[h:h + 1, :]
        m = m_all[:, h:h + 1]
        btot = btot_col[:, h:h + 1]

        dm = jnp.where(mask, b_c - b_r + i_r, NEG_INF)
        inter = b_c + m
        m_t = jnp.maximum(inter, jnp.max(dm, axis=1, keepdims=True))
        w = jnp.exp(dm - m_t)
        a = jnp.exp(inter - m_t)
        sc = _dot_nt(qh, kh) * w
        cb = c_scr[h].astype(BF16)
        num = a * _dot(qh, cb) + _dot(sc.astype(BF16), vh)
        nb = n_scr[h:h + 1, :].astype(BF16).astype(F32)
        qn = jnp.sum(qh.astype(F32) * nb, axis=1, keepdims=True)
        den = a * qn + jnp.sum(sc, axis=1, keepdims=True)
        hs_ref[:, sl] = num / jnp.maximum(jnp.abs(den), jnp.exp(-m_t))

        g_c = btot - b_c + i_c
        g_r = btot - b_r + i_r
        m_new = jnp.maximum(btot + m, jnp.max(g_c, axis=0, keepdims=True))
        decay = jnp.exp(btot + m - m_new)
        ws_c = jnp.exp(g_c - m_new)
        ws_r = jnp.exp(g_r - m_new)
        kw = (kf * ws_c).astype(BF16)
        c_scr[h] = decay * c_scr[h] + _dot_tn(kw, vh)
        n_scr[h:h + 1, :] = decay * n_scr[h:h + 1, :] + _dot(ws_r.astype(BF16), kh)
        m_new_parts.append(m_new)
    m_scr[...] = jnp.concatenate(m_new_parts, axis=1)

    if emit_state:
        @pl.when(s == nchunks - 1)
        def _():
            cout_ref[...] = c_scr[...]
            nout_ref[...] = n_scr[...]
            mout_ref[...] = m_scr[...]


def _mlstm(z, gates, gates_t, tok_off, nseq, seqlen, j, init=None, emit_state=False, n_even=1):
    nh, dh, L = MLSTM_HEADS, MLSTM_DH, MLSTM_CHUNK
    w = nh * dh
    nchunks = seqlen // L
    off = tok_off // L

    def chunk(d, b, s):
        return off + b * nchunks + jnp.where(d == 0, s, nchunks - 1 - s)

    in_specs = [
        pl.BlockSpec((L, w), lambda d, b, s: (chunk(d, b, s), 0)),
        pl.BlockSpec((L, w), lambda d, b, s: (chunk(d, b, s), 1)),
        pl.BlockSpec((L, w), lambda d, b, s: (chunk(d, b, s), 2)),
        pl.BlockSpec((L, 4 * nh), lambda d, b, s: (chunk(d, b, s), 0)),
        pl.BlockSpec((4 * nh, L), lambda d, b, s: (0, chunk(d, b, s))),
    ]
    args = [z, z, z, gates, gates_t]
    has_init = init is not None
    if has_init:
        c0, n0, m0 = init
        in_specs += [
            pl.BlockSpec((None, None, None, nh, dh, dh), lambda d, b, s: (b, j, d, 0, 0, 0)),
            pl.BlockSpec((None, None, None, nh, dh), lambda d, b, s: (b, j, d, 0, 0)),
            pl.BlockSpec((None, None, None, 1, nh), lambda d, b, s: (b, j, d, 0, 0)),
        ]
        args += [c0, n0, m0.reshape(m0.shape[:3] + (1, nh))]
    out_shape = [jax.ShapeDtypeStruct((2, nseq * seqlen, w), F32)]
    out_specs = [pl.BlockSpec((None, L, w), lambda d, b, s: (d, chunk(d, b, s) - off, 0))]
    if emit_state:
        out_shape += [
            jax.ShapeDtypeStruct((nseq, n_even, 2, nh, dh, dh), F32),
            jax.ShapeDtypeStruct((nseq, n_even, 2, nh, dh), F32),
            jax.ShapeDtypeStruct((nseq, n_even, 2, 1, nh), F32),
        ]
        out_specs += [
            pl.BlockSpec((None, None, None, nh, dh, dh), lambda d, b, s: (b, j, d, 0, 0, 0)),
            pl.BlockSpec((None, None, None, nh, dh), lambda d, b, s: (b, j, d, 0, 0)),
            pl.BlockSpec((None, None, None, 1, nh), lambda d, b, s: (b, j, d, 0, 0)),
        ]
    return pl.pallas_call(
        functools.partial(_mlstm_kernel, has_init=has_init, emit_state=emit_state, nchunks=nchunks),
        out_shape=out_shape,
        grid=(2, nseq, nchunks),
        in_specs=in_specs,
        out_specs=out_specs,
        scratch_shapes=[pltpu.VMEM((nh, dh, dh), F32), pltpu.VMEM((nh, dh), F32), pltpu.VMEM((1, nh), F32)],
        compiler_params=_params("arbitrary", "arbitrary", "arbitrary"),
        name="mlstm_ctx" if emit_state else "mlstm_lat",
    )(*args)


def _even_out_kernel(*refs, n_ctx_tiles, tiles_per_latent, ctx_len, lat_len, split_tiles):
    nhr = 1 if split_tiles is None else 2
    h_refs = refs[:nhr]
    (gate_ref, hsc_ref, hsl_ref, o_ref, p_ref, pprev_ref, pnext_ref, gain_ref, pw_ref, ps_ref, w_ref, out_ref,
     cat_scr) = refs[nhr:]
    i = pl.program_id(0)
    tm = TOKEN_TILE
    nh, dh = MLSTM_HEADS, MLSTM_DH
    wm = nh * dh

    @pl.when(pl.program_id(1) == 0)
    def _():
        is_ctx = i < n_ctx_tiles
        hs = jnp.where(is_ctx, hsc_ref[0] + hsc_ref[1], hsl_ref[0] + hsl_ref[1])
        for h in range(nh):
            sl = slice(h * dh, (h + 1) * dh)
            x = hs[:, sl]
            y = x * lax.rsqrt(jnp.mean(x * x, axis=-1, keepdims=True) + EPS) * gain_ref[:, sl]
            cat_scr[:, sl] = (y * jax.nn.sigmoid(o_ref[:, sl])).astype(BF16)

        tiles_ctx = ctx_len // tm
        pos = jnp.where(is_ctx, i % tiles_ctx, (i - n_ctx_tiles) % tiles_per_latent)
        ntile = jnp.where(is_ctx, tiles_ctx, tiles_per_latent)
        seqlen = jnp.where(is_ctx, ctx_len, lat_len)
        x = p_ref[...]
        prev = jnp.where(pos > 0, pprev_ref[...], 0.0)
        nxt = jnp.where(pos < ntile - 1, pnext_ref[...], 0.0)
        pad = jnp.zeros((LANES - 2 * POOL_HALO, x.shape[1]), F32)
        xcat = jnp.concatenate([prev, x, nxt, pad], axis=0)
        xh, xl = _split2(xcat)
        t = lax.broadcasted_iota(jnp.int32, (tm, tm + LANES), 0)
        sidx = lax.broadcasted_iota(jnp.int32, (tm, tm + LANES), 1) - POOL_HALO
        tpos = pos * tm + lax.broadcasted_iota(jnp.int32, (tm, 1), 0)
        gw = wm // len(POOL_WINDOWS)
        for gi, win in enumerate(POOL_WINDOWS):
            sl = slice(gi * gw, (gi + 1) * gw)
            band = jnp.where((sidx >= t - win // 2) & (sidx < t - win // 2 + win), 1.0, 0.0).astype(BF16)
            lo = jnp.maximum(tpos - win // 2, 0)
            hi = jnp.minimum(tpos - win // 2 + win, seqlen)
            cnt = (hi - lo).astype(F32)
            p = (_dot(band, xh[:, sl]) + _dot(band, xl[:, sl])) / cnt - x[:, sl]
            y = _dot(p.astype(BF16), pw_ref[gi]) * ps_ref[:, sl]
            cat_scr[:, wm + gi * gw:wm + (gi + 1) * gw] = y.astype(BF16)

    out_ref[...] = (_load_stream(h_refs, split_tiles)
                    + gate_ref[...] * _dot(cat_scr[...], pltpu.bitcast(w_ref[...], BF16)))


def _even_out(h, mod3, layer, hs_ctx, hs_lat, z, gain, pool_w_bf16, pool_scale, w_out_bf16, n_ctx_tiles,
              tiles_per_latent, ctx_len, lat_len, tn=2048):
    tm = TOKEN_TILE
    d = 2 * w_out_bf16.shape[0]
    h_arrays, h_specs, n, split_tiles = _stream(h, tm, tn, col_of=lambda j: j)
    wm = MLSTM_HEADS * MLSTM_DH
    nlt = n // tm - n_ctx_tiles
    rows8 = n // POOL_HALO
    per = tm // POOL_HALO

    def gate_idx(i, j):
        return ((layer * 8 + _mod_row(i, n_ctx_tiles, tiles_per_latent)) * N_MOD + 2, 0, j)

    return pl.pallas_call(
        functools.partial(_even_out_kernel, n_ctx_tiles=n_ctx_tiles, tiles_per_latent=tiles_per_latent,
                          ctx_len=ctx_len, lat_len=lat_len, split_tiles=split_tiles),
        out_shape=jax.ShapeDtypeStruct((n, d), F32),
        grid=(n // tm, d // tn),
        in_specs=h_specs + [
            pl.BlockSpec((None, 1, tn), gate_idx),
            pl.BlockSpec((2, tm, wm), lambda i, j: (0, jnp.minimum(i, n_ctx_tiles - 1), 0)),
            pl.BlockSpec((2, tm, wm), lambda i, j: (0, jnp.clip(i - n_ctx_tiles, 0, nlt - 1), 0)),
            pl.BlockSpec((tm, wm), lambda i, j: (i, 3)),
            pl.BlockSpec((tm, wm), lambda i, j: (i, 4)),
            pl.BlockSpec((POOL_HALO, wm), lambda i, j: (jnp.maximum(i * per - 1, 0), 4)),
            pl.BlockSpec((POOL_HALO, wm), lambda i, j: (jnp.minimum((i + 1) * per, rows8 - 1), 4)),
            pl.BlockSpec((1, wm), lambda i, j: (0, 0)),
            pl.BlockSpec(pool_w_bf16.shape, lambda i, j: (0, 0, 0)),
            pl.BlockSpec((1, wm), lambda i, j: (0, 0)),
            pl.BlockSpec((d // 2, tn), lambda i, j: (0, j)),
        ],
        out_specs=pl.BlockSpec((tm, tn), lambda i, j: (i, j)),
        scratch_shapes=[pltpu.VMEM((tm, d), BF16)],
        compiler_params=_params("arbitrary", "arbitrary"),
        name="even_out",
    )(*h_arrays, mod3, hs_ctx, hs_lat, z, z, z, z, gain.reshape(1, wm), pool_w_bf16, pool_scale.reshape(1, wm),
      w_out_bf16)


def _qk_kernel(*refs, rope, emit_cache):
    q_ref, k_ref, v_ref, gain_ref = refs[:4]
    rest = list(refs[4:])
    if rope:
        cos_ref, sin_ref = rest[:2]
        rest = rest[2:]
    qn_ref, kn_ref, vb_ref = rest[:3]
    rest = rest[3:]
    hd = 2 * DA_DH
    lane = lax.broadcasted_iota(jnp.int32, (q_ref.shape[0], hd), 1)
    first = lane < DA_DH

    def norm(x, gain):
        sq = x * x
        s1 = jnp.sum(jnp.where(first, sq, 0.0), axis=-1, keepdims=True)
        s2 = jnp.sum(jnp.where(first, 0.0, sq), axis=-1, keepdims=True)
        ms = jnp.where(first, s1, s2) * (1.0 / DA_DH)
        y = x * lax.rsqrt(ms + EPS) * gain
        if rope:
            quarter = DA_DH // 4
            partner = jnp.where((lane & quarter) == 0, pltpu.roll(y, LANES - quarter, 1), pltpu.roll(y, quarter, 1))
            y = y * cos_ref[...] + partner * sin_ref[...]
        return y

    vb_ref[...] = v_ref[...].astype(BF16)
    for h in range(DA_HEADS):
        cols = slice(h * hd, (h + 1) * hd)
        qn = norm(q_ref[:, cols], gain_ref[0:1, :])
        kn = norm(k_ref[:, cols], gain_ref[1:2, :])
        qn_ref[:, cols] = (qn * (DA_DH ** -0.5)).astype(BF16)
        kn_ref[:, cols] = kn.astype(BF16)
        if emit_cache:
            newk_ref, newv_ref = rest
            newk_ref[h] = kn
            newv_ref[h] = v_ref[:, cols]


def _qk(z, qk_gain2, tok_off, ntok, j, rope_tables=None, cache_shape=None, seqlen=None):
    tm = TOKEN_TILE
    hd = 2 * DA_DH
    w = DA_HEADS * hd
    off = tok_off // tm
    nblk_w = w // hd
    rope = rope_tables is not None
    emit_cache = cache_shape is not None
    in_specs = [
        pl.BlockSpec((tm, w), lambda i: (off + i, 0)),
        pl.BlockSpec((tm, w), lambda i: (off + i, 1)),
        pl.BlockSpec((tm, w), lambda i: (off + i, 2)),
        pl.BlockSpec((2, hd), lambda i: (0, 0)),
    ]
    args = [z, z, z, qk_gain2]
    if rope:
        tps = seqlen // tm
        in_specs += [pl.BlockSpec((tm, hd), lambda i: (i % tps, 0))] * 2
        args += list(rope_tables)
    out_shape = [jax.ShapeDtypeStruct((ntok, w), BF16)] * 3
    out_specs = [pl.BlockSpec((tm, w), lambda i: (i, 0))] * 3
    if emit_cache:
        out_shape += [jax.ShapeDtypeStruct(cache_shape, F32)] * 2
        out_specs += [pl.BlockSpec((None, None, DA_HEADS, tm, hd), lambda i: (i, j, 0, 0, 0))] * 2
    return pl.pallas_call(
        functools.partial(_qk_kernel, rope=rope, emit_cache=emit_cache),
        out_shape=out_shape,
        grid=(ntok // tm,),
        in_specs=in_specs,
        out_specs=out_specs,
        compiler_params=_params("arbitrary"),
        name="qk_ctx" if emit_cache else "qk_lat",
    )(*args)


def _rope_tables(t):
    rows = t // GRID_W
    row = jnp.repeat(jnp.arange(rows), GRID_W).astype(F32)
    col = (jnp.arange(rows * GRID_W) % GRID_W).astype(F32)
    n_freq = DA_DH // 4
    inv = ROPE_BASE ** (-jnp.arange(n_freq, dtype=F32) / n_freq)
    ar, ac = row[:, None] * inv, col[:, None] * inv
    cos = jnp.concatenate([jnp.cos(ar), jnp.cos(ar), jnp.cos(ac), jnp.cos(ac)], axis=-1)
    sin = jnp.concatenate([-jnp.sin(ar), jnp.sin(ar), -jnp.sin(ac), jnp.sin(ac)], axis=-1)
    return jnp.tile(cos, (1, 2)), jnp.tile(sin, (1, 2))


def _attn_kernel(*refs, n_pieces, lam_init):
    q_ref, lp_ref, subln_ref = refs[:3]
    kv = refs[3:3 + 2 * n_pieces]
    o_ref = refs[3 + 2 * n_pieces]
    lp = lp_ref[...]
    lam = (jnp.exp(jnp.sum(lp[0:1] * lp[1:2], axis=-1, keepdims=True))
           - jnp.exp(jnp.sum(lp[2:3] * lp[3:4], axis=-1, keepdims=True)) + lam_init)
    q = q_ref[...]
    lane = lax.broadcasted_iota(jnp.int32, q.shape, 1)
    zero = jnp.zeros_like(q)
    qa = jnp.where(lane < DA_DH, q, zero)
    qb = jnp.where(lane < DA_DH, zero, q)
    ks = [kv[2 * p][...].astype(BF16) for p in range(n_pieces)]
    vs = [kv[2 * p + 1][...].astype(BF16) for p in range(n_pieces)]

    def attend(qx):
        parts = [_dot_nt(qx, k) for k in ks]
        m = functools.reduce(jnp.maximum, [jnp.max(x, axis=-1, keepdims=True) for x in parts])
        es = [jnp.exp(x - m) for x in parts]
        tot = functools.reduce(lambda a, b: a + b, [jnp.sum(e, axis=-1, keepdims=True) for e in es])
        pv = functools.reduce(lambda a, b: a + b, [_dot(e.astype(BF16), v) for e, v in zip(es, vs)])
        return pv / tot

    o = attend(qa) - lam * attend(qb)
    o = o * lax.rsqrt(jnp.mean(o * o, axis=-1, keepdims=True) + EPS) * subln_ref[...]
    o_ref[...] = (o * (1.0 - lam_init)).astype(BF16)


def _attn(qn, kn, vb, lam_params, subln, lam_init, nseq, seqlen, tq, j, caches=None):
    hd = 2 * DA_DH
    nq = seqlen // tq
    in_specs = [
        pl.BlockSpec((tq, hd), lambda b, h, qi: (b * nq + qi, h)),
        pl.BlockSpec((None, 4, DA_DH), lambda b, h, qi: (j, 0, 0)),
        pl.BlockSpec((None, 1, hd), lambda b, h, qi: (j, 0, 0)),
        pl.BlockSpec((seqlen, hd), lambda b, h, qi: (b, h)),
        pl.BlockSpec((seqlen, hd), lambda b, h, qi: (b, h)),
    ]
    args = [qn, lam_params, subln.reshape(subln.shape[0], 1, hd), kn, vb]
    n_pieces = 1
    if caches is not None:
        ck, cv = caches
        past = ck.shape[3]
        in_specs += [pl.BlockSpec((None, None, None, past, hd), lambda b, h, qi: (b, j, h, 0, 0))] * 2
        args += [ck, cv]
        n_pieces = 2
    return pl.pallas_call(
        functools.partial(_attn_kernel, n_pieces=n_pieces, lam_init=lam_init),
        out_shape=jax.ShapeDtypeStruct(qn.shape, BF16),
        grid=(nseq, DA_HEADS, nq),
        in_specs=in_specs,
        out_specs=pl.BlockSpec((tq, hd), lambda b, h, qi: (b * nq + qi, h)),
        compiler_params=_params("arbitrary", "arbitrary", "arbitrary"),
        name="attn_lat" if caches is not None else "attn_ctx",
    )(*args)


def _odd_out_kernel(h_ref, gate_ref, ac_ref, al_ref, gu_ref, gv_ref, ws_ref, bt_ref, w_ref, out_ref, cat_scr, *,
                    n_ctx_tiles):
    i = pl.program_id(0)
    tm = TOKEN_TILE
    wa = DA_HEADS * 2 * DA_DH
    gw = LANES

    @pl.when(pl.program_id(1) == 0)
    def _():
        cat_scr[:, 0:wa] = jnp.where(i < n_ctx_tiles, ac_ref[...], al_ref[...])
        for c in range(tm // GM_CHUNK):
            rows = slice(c * GM_CHUNK, (c + 1) * GM_CHUNK)
            for g in range(GM_GROUPS):
                cols = slice(g * gw, (g + 1) * gw)
                u = _gelu_tanh(gu_ref[rows, cols])
                v = _gelu_tanh(gv_ref[rows, cols])
                vn = v * lax.rsqrt(jnp.mean(v * v, axis=-1, keepdims=True) + EPS)
                mixed = _dot(ws_ref[g], vn.astype(BF16)) + bt_ref[:, g:g + 1]
                cat_scr[rows, wa + g * gw:wa + (g + 1) * gw] = (u * mixed).astype(BF16)

    out_ref[...] = h_ref[...] + gate_ref[...] * _dot(cat_scr[...], pltpu.bitcast(w_ref[...], BF16))


def _odd_out(h, mod3, layer, a_ctx, a_lat, z, gm_ws_bf16, gm_b_t, w_out_bf16, n_ctx_tiles, tiles_per_latent, tn=2048):
    n, d = h.shape
    tm = TOKEN_TILE
    wa = DA_HEADS * 2 * DA_DH
    wg = GM_GROUPS * LANES
    nlt = n // tm - n_ctx_tiles

    def gate_idx(i, j):
        return ((layer * 8 + _mod_row(i, n_ctx_tiles, tiles_per_latent)) * N_MOD + 2, 0, j)

    return pl.pallas_call(
        functools.partial(_odd_out_kernel, n_ctx_tiles=n_ctx_tiles),
        out_shape=jax.ShapeDtypeStruct((n, d), F32),
        grid=(n // tm, d // tn),
        in_specs=[
            pl.BlockSpec((tm, tn), lambda i, j: (i, j)),
            pl.BlockSpec((None, 1, tn), gate_idx),
            pl.BlockSpec((tm, wa), lambda i, j: (jnp.minimum(i, n_ctx_tiles - 1), 0)),
            pl.BlockSpec((tm, wa), lambda i, j: (jnp.clip(i - n_ctx_tiles, 0, nlt - 1), 0)),
            pl.BlockSpec((tm, wg), lambda i, j: (i, 3)),
            pl.BlockSpec((tm, wg), lambda i, j: (i, 4)),
            pl.BlockSpec(gm_ws_bf16.shape, lambda i, j: (0, 0, 0)),
            pl.BlockSpec(gm_b_t.shape, lambda i, j: (0, 0)),
            pl.BlockSpec((d // 2, tn), lambda i, j: (0, j)),
        ],
        out_specs=pl.BlockSpec((tm, tn), lambda i, j: (i, j)),
        scratch_shapes=[pltpu.VMEM((tm, d), BF16)],
        compiler_params=_params("arbitrary", "arbitrary"),
        name="odd_out",
    )(h, mod3, a_ctx, a_lat, z, z, gm_ws_bf16, gm_b_t, w_out_bf16)


def _top_values(x, k):
    vals = []
    cur = x
    for r in range(k):
        m = jnp.max(cur, axis=0, keepdims=True)
        vals.append(m)
        if r + 1 < k:
            cur = jnp.where(cur == m, NEG_INF, cur)
    return jnp.concatenate(vals, axis=0)


def _oddeven_merge(lo, hi, r):
    step = r * 2
    if step < hi - lo:
        yield from _oddeven_merge(lo, hi, step)
        yield from _oddeven_merge(lo + r, hi, step)
        yield from [(i, i + r) for i in range(lo + r, hi - r, step)]
    else:
        yield (lo, lo + r)


def _oddeven_merge_sort(lo, hi):
    if hi - lo >= 1:
        mid = lo + (hi - lo) // 2
        yield from _oddeven_merge_sort(lo, mid)
        yield from _oddeven_merge_sort(mid + 1, hi)
        yield from _oddeven_merge(lo, hi, 1)


def _top_sorted(x):
    n = len(x)
    assert n == PEER_TOPK

    def exchange(v, i, j):
        v[i], v[j] = jnp.maximum(v[i], v[j]), jnp.minimum(v[i], v[j])

    v = list(x)
    for i, j in _oddeven_merge_sort(0, n - 1):
        exchange(v, i, j)
    shift = SUBLANES // 2
    while shift:
        w = [pltpu.roll(t, SUBLANES - shift, 0) for t in v]
        v = [jnp.maximum(v[i], w[n - 1 - i]) for i in range(n)]
        d = n // 2
        while d:
            for i in range(n):
                if not i & d:
                    exchange(v, i, i + d)
            d //= 2
        shift //= 2
    return jnp.concatenate([t[0:1, :] for t in v], axis=0)


def _dup_bf16(x):
    b = pltpu.bitcast(x.astype(BF16).astype(F32), jnp.uint32)
    return pltpu.bitcast(b | (b >> 16), F32)


def _peer_route_kernel(q_ref, sk_ref, rank_ref, e2_ref, cnt_ref, c_ref, s_scr):
    half = PEER_NKEYS
    k = PEER_TOPK
    n_slabs = half // SUBLANES
    for h in range(PEER_HEADS):
        q1 = q_ref[:, 2 * h * half:(2 * h + 1) * half].astype(BF16)
        q2 = q_ref[:, (2 * h + 1) * half:(2 * h + 2) * half].astype(BF16)
        s_scr[0] = _dot_nt(sk_ref[2 * h], q1)
        s_scr[1] = _dot_nt(sk_ref[2 * h + 1], q2)

        def lane_tile(t, carry):
            lanes = pl.ds(pl.multiple_of(t * LANES, LANES), LANES)
            s1 = s_scr[0, :, lanes]
            s2 = s_scr[1, :, lanes]
            a_all = _top_sorted([s1[SUBLANES * i:SUBLANES * (i + 1)] for i in range(n_slabs)])
            b_all = _top_sorted([s2[SUBLANES * i:SUBLANES * (i + 1)] for i in range(n_slabs)])
            b = [b_all[r:r + 1] for r in range(k)]
            rows = [a_all[r:r + 1] + b_all[0:k // (r + 1)] for r in range(k)]
            nrows = sum(k // (r + 1) for r in range(k))
            rows.append(jnp.full((-nrows % SUBLANES, LANES), NEG_INF, F32))
            top = _top_values(jnp.concatenate(rows, axis=0), k)
            tau = top[k - 1:k]
            z = jnp.sum(jnp.exp(top - top[0:1]), axis=0, keepdims=True)
            cnt = jnp.zeros(s1.shape, F32)
            rank2 = jnp.zeros(s2.shape, F32)
            for r in range(k):
                cnt = jnp.where(s1 + b[r] >= tau, float(r + 1), cnt)
                rank2 = jnp.where(b[r] > s2, float(r + 1), rank2)
            rank_ref[h, :, lanes] = pltpu.bitcast(rank2.astype(BF16), F32)
            e2_ref[h, :, lanes] = pltpu.bitcast(jnp.exp(s2 - b[0]).astype(BF16), F32)
            cnt_ref[h, :, lanes] = _dup_bf16(cnt)
            c_ref[h, :, lanes] = _dup_bf16(jnp.exp(s1 - a_all[0:1]) / z)
            return carry

        lax.fori_loop(0, q_ref.shape[0] // LANES, lane_tile, 0)


def _peer_route(qp, subkeys_bf16, tm=512):
    n = qp.shape[0]
    def out(rows):
        return (jax.ShapeDtypeStruct((PEER_HEADS, rows, n), F32),
                pl.BlockSpec((PEER_HEADS, rows, tm), lambda i: (0, 0, i)))

    outs = [out(PEER_NKEYS // 2)] * 2 + [out(PEER_NKEYS)] * 2
    return pl.pallas_call(
        _peer_route_kernel,
        out_shape=[o[0] for o in outs],
        grid=(n // tm,),
        in_specs=[
            pl.BlockSpec((tm, qp.shape[1]), lambda i: (i, 0)),
            pl.BlockSpec(subkeys_bf16.shape, lambda i: (0, 0, 0)),
        ],
        out_specs=[o[1] for o in outs],
        scratch_shapes=[pltpu.VMEM((2, PEER_NKEYS, tm), F32)],
        compiler_params=_params("arbitrary"),
        name="peer_route",
    )(qp, subkeys_bf16)


def _peer_dense_kernel(xnt_ref, h_ref, gate_ref, u_ref, vt_ref, rank_ref, e2_ref, cnt_ref, c_ref, *rest, te, n_tiles,
                       split_tiles):
    n_out = 1 if split_tiles is None else 2
    out_refs = rest[:n_out]
    act0_scr, act1_scr, wa0_scr, wa1_scr, acc_scr = rest[n_out:]
    s = pl.program_id(0)
    n_pairs = pl.num_programs(0) - 2
    nkeys = PEER_NKEYS
    per = te // nkeys
    tm = acc_scr.shape[1]
    half = tm // 2
    assert per % 4 == 0
    pack = 2 * SUBLANES

    @pl.when(s == 0)
    def _():
        act1_scr[...] = jnp.zeros_like(act1_scr)
        wa0_scr[...] = jnp.zeros_like(wa0_scr)

    live = jnp.logical_and(s >= 1, s <= n_pairs)
    t2 = jnp.clip(s - 1, 0, n_pairs - 1) % n_tiles
    pair3 = jnp.clip(s - 2, 0, n_pairs - 1)
    t3 = pair3 % n_tiles
    tok3 = pair3 // n_tiles

    @pl.when(t3 == 0)
    def _():
        acc_scr[...] = jnp.zeros_like(acc_scr)

    def step(act_w, act_r, wa_w, wa_r):
        def stage1(c):
            cols = slice(c * half, (c + 1) * half)
            act_w[:, cols] = _dot(pltpu.bitcast(u_ref[...], BF16),
                                  pltpu.bitcast(xnt_ref[:, cols], BF16))

        def stage3(c):
            cols = slice(c * half, (c + 1) * half)
            acc_scr[:, cols] += _dot(pltpu.bitcast(vt_ref[...], BF16), wa_r[:, cols])

        mxu_work = [functools.partial(stage1, 0), functools.partial(stage3, 0),
                    functools.partial(stage1, 1), functools.partial(stage3, 1)]
        stride = per // len(mxu_work)
        for ii in range(per):
            if ii % stride == 0:
                mxu_work[ii // stride]()
            r = t2 * per + ii
            cnt_rows = [jnp.where(live, cnt_ref[h, pl.ds(r, 1), :], 0.0) for h in range(PEER_HEADS)]
            c_rows = [c_ref[h, pl.ds(r, 1), :] for h in range(PEER_HEADS)]
            for lt in range(tm // LANES):
                lanes = slice(lt * LANES, (lt + 1) * LANES)
                cnts = [pltpu.bitcast(jnp.broadcast_to(x[:, lanes], (SUBLANES, LANES)), BF16) for x in cnt_rows]
                ccs = [pltpu.bitcast(jnp.broadcast_to(x[:, lanes], (SUBLANES, LANES)), BF16) for x in c_rows]
                for sb in range(nkeys // pack):
                    words = slice(sb * SUBLANES, (sb + 1) * SUBLANES)
                    rows = slice(ii * nkeys + sb * pack, ii * nkeys + (sb + 1) * pack)
                    w = None
                    for h in range(PEER_HEADS):
                        e2 = pltpu.bitcast(e2_ref[h, words, lanes], BF16)
                        rank = pltpu.bitcast(rank_ref[h, words, lanes], BF16)
                        wh = jnp.where(rank < cnts[h], e2 * ccs[h], jnp.zeros_like(e2))
                        w = wh if w is None else w + wh
                    wa_w[rows, lanes] = w * _gelu_tanh_lowp(act_r[rows, lanes].astype(BF16))

    @pl.when(s % 2 == 0)
    def _():
        step(act0_scr, act1_scr, wa1_scr, wa0_scr)

    @pl.when(s % 2 == 1)
    def _():
        step(act1_scr, act0_scr, wa0_scr, wa1_scr)

    @pl.when(jnp.logical_and(t3 == n_tiles - 1, s >= 2))
    def _():
        res = h_ref[...] + gate_ref[...] * acc_scr[...].T
        if split_tiles is None:
            out_refs[0][...] = res
        else:
            @pl.when(tok3 < split_tiles)
            def _():
                out_refs[0][...] = res

            @pl.when(tok3 >= split_tiles)
            def _():
                out_refs[1][...] = res


def _peer_dense(xnt, h, mod3, layer, u_packed, vt_packed, rank2, e2, cnt, c, n_ctx_tiles, tiles_per_latent, tm=512,
                te=512, split_out=False):
    n, d = h.shape
    n_tiles = 2 * u_packed.shape[0] // te
    per = tm // TOKEN_TILE
    n_pairs = (n // tm) * n_tiles

    def tok(lag):
        return lambda t: jnp.clip(t - lag, 0, n_pairs - 1) // n_tiles

    def exp(lag):
        return lambda t: jnp.clip(t - lag, 0, n_pairs - 1) % n_tiles

    tok1, tok2, tok3, exp1, exp3 = tok(0), tok(1), tok(2), exp(0), exp(2)
    if split_out:
        split_tiles = n_ctx_tiles // per
        n_ctx = split_tiles * tm
        out_shape = [jax.ShapeDtypeStruct((n_ctx, d), F32), jax.ShapeDtypeStruct((n - n_ctx, d), F32)]
        out_specs = [pl.BlockSpec((tm, d), lambda t: (jnp.minimum(tok3(t), split_tiles - 1), 0)),
                     pl.BlockSpec((tm, d), lambda t: (jnp.maximum(tok3(t) - split_tiles, 0), 0))]
    else:
        split_tiles = None
        out_shape = jax.ShapeDtypeStruct((n, d), F32)
        out_specs = pl.BlockSpec((tm, d), lambda t: (tok3(t), 0))

    def gate_idx(t):
        return ((layer * 8 + _mod_row(tok3(t) * per, n_ctx_tiles, tiles_per_latent)) * N_MOD + 5, 0, 0)

    rspec = pl.BlockSpec((PEER_HEADS, PEER_NKEYS, tm), lambda t: (0, 0, tok2(t)))
    pspec = pl.BlockSpec((PEER_HEADS, PEER_NKEYS // 2, tm), lambda t: (0, 0, tok2(t)))
    return pl.pallas_call(
        functools.partial(_peer_dense_kernel, te=te, n_tiles=n_tiles, split_tiles=split_tiles),
        out_shape=out_shape,
        grid=(n_pairs + 2,),
        in_specs=[
            pl.BlockSpec((d // 2, tm), lambda t: (0, tok1(t))),
            pl.BlockSpec((tm, d), lambda t: (tok3(t), 0), pipeline_mode=pl.Buffered(1)),
            pl.BlockSpec((None, 1, d), gate_idx),
            pl.BlockSpec((te // 2, d), lambda t: (exp1(t), 0)),
            pl.BlockSpec((d // 2, te), lambda t: (0, exp3(t))),
            pspec, pspec, rspec, rspec,
        ],
        out_specs=out_specs,
        scratch_shapes=[pltpu.VMEM((te, tm), F32), pltpu.VMEM((te, tm), F32), pltpu.VMEM((te, tm), BF16),
                        pltpu.VMEM((te, tm), BF16), pltpu.VMEM((d, tm), F32)],
        compiler_params=_params("arbitrary"),
        name="peer_dense",
    )(xnt, h, mod3, u_packed, vt_packed, rank2, e2, cnt, c)


def kernel(x_prompt, x_sample, state_mlstm_C, state_mlstm_n, state_mlstm_m, cache_da_k, cache_da_v, c, c_ctx, norm_mix, norm_ffn, w_mod, b_mod, w_in_even, b_gate_even, mlstm_gain, pool_w, pool_scale, w_out_even, w_in_odd, qk_gain, da_lambda, da_subln, gm_ws, gm_b, w_out_odd, peer_wq, peer_subkeys, peer_u, peer_v):
    nb, s_len, d = x_prompt.shape
    nbd, t_len, _ = x_sample.shape
    depth = w_mod.shape[0]
    tm = TOKEN_TILE
    assert s_len == tm and t_len % tm == 0 and nbd <= 7 and t_len % GRID_W == 0
    n_ctx = nb * s_len
    n_lat = nbd * t_len
    n_ctx_tiles = n_ctx // tm
    tiles_per_latent = t_len // tm
    n_even = (depth + 1) // 2
    n_odd = depth // 2

    h = (x_prompt.reshape(n_ctx, d), x_sample.reshape(n_lat, d))
    cond8 = jnp.concatenate([c_ctx[None], c, jnp.zeros((7 - nbd, d), F32)], axis=0)
    mod3 = _adaln(cond8, w_mod, b_mod).reshape(depth * 8 * N_MOD, 1, d)
    rope = _rope_tables(t_len)

    new_c, new_n, new_m, new_k, new_v = [], [], [], [], []
    wm = MLSTM_HEADS * MLSTM_DH
    for l in range(depth):
        j = l // 2
        if l % 2 == 0:
            n_main = 5 * wm
            z, gates, gates_t = _inproj(h, norm_mix[l], mod3, l, 0, _pack_rows(w_in_even, j, ncols=n_main),
                                        n_ctx_tiles, tiles_per_latent, w_gates=w_in_even[j, :, n_main:],
                                        b_gates=b_gate_even[j],
                                        name="inproj_even")
            hs_ctx, cc, cn, cm = _mlstm(z, gates, gates_t, 0, nb, s_len, 0, emit_state=True)
            (hs_lat,) = _mlstm(z, gates, gates_t, n_ctx, nbd, t_len, j,
                               init=(state_mlstm_C, state_mlstm_n, state_mlstm_m))
            new_c.append(cc)
            new_n.append(cn)
            new_m.append(cm[:, :, :, 0, :])
            h = _even_out(h, mod3, l, hs_ctx, hs_lat, z, mlstm_gain[j], pool_w[j].astype(BF16), pool_scale[j],
                          _pack_rows(w_out_even, j), n_ctx_tiles, tiles_per_latent, s_len, t_len)
        else:
            lam_init = 0.8 - 0.6 * math.exp(-0.3 * l)
            (z,) = _inproj(h, norm_mix[l], mod3, l, 0, _pack_rows(w_in_odd, j), n_ctx_tiles, tiles_per_latent,
                           name="inproj_odd")
            gain2 = jnp.tile(qk_gain[j], (1, 2))
            cache_shape = (nb, 1, DA_HEADS, s_len, 2 * DA_DH)
            qn_c, kn_c, vb_c, nk, nv = _qk(z, gain2, 0, n_ctx, 0, cache_shape=cache_shape)
            qn_l, kn_l, vb_l = _qk(z, gain2, n_ctx, n_lat, 0, rope_tables=rope, seqlen=t_len)
            new_k.append(nk)
            new_v.append(nv)
            a_ctx = _attn(qn_c, kn_c, vb_c, da_lambda, da_subln, lam_init, nb, s_len, s_len, j)
            a_lat = _attn(qn_l, kn_l, vb_l, da_lambda, da_subln, lam_init, nbd, t_len, tm, j,
                          caches=(cache_da_k, cache_da_v))
            h = _odd_out(h, mod3, l, a_ctx, a_lat, z, gm_ws[j].astype(BF16), gm_b[j].T, _pack_rows(w_out_odd, j),
                         n_ctx_tiles, tiles_per_latent)
        qp, xnt = _inproj(h, norm_ffn[l], mod3, l, 3, _pack_rows(peer_wq, l), n_ctx_tiles, tiles_per_latent,
                          emit_xn=True, name="inproj_peer")
        sk = peer_subkeys[l].reshape(2 * PEER_HEADS, PEER_NKEYS, PEER_NKEYS).astype(BF16)
        rank2, e2, cnt, cw = _peer_route(qp, sk)
        h = _peer_dense(xnt, h, mod3, l, _pack_rows(peer_u, l), _pack_rows(peer_v, l, transpose=True), rank2,
                        e2, cnt, cw,
                        n_ctx_tiles, tiles_per_latent, split_out=(l == depth - 1))

    def join(parts):
        return parts[0] if len(parts) == 1 else jnp.concatenate(parts, axis=1)

    return (h[0].reshape(nb, s_len, d), h[1].reshape(nbd, t_len, d), join(new_c), join(new_n), join(new_m),
            join(new_k), join(new_v))
```

```python
import functools
import math

import jax
import jax.numpy as jnp
from jax import lax
from jax.experimental import pallas as pl
from jax.experimental.pallas import tpu as pltpu

F32 = jnp.float32
BF16 = jnp.bfloat16

N_MOD = 6
EPS = 1e-6
TOKEN_TILE = 256
LANES = 128
SUBLANES = 8
VMEM_LIMIT_BYTES = 56 * 1024 * 1024

MLSTM_HEADS = 4
MLSTM_DH = 256
MLSTM_CHUNK = 128
POOL_WINDOWS = (2, 4, 8, 16)
POOL_HALO = 8
DA_HEADS = 8
DA_DH = 64
GRID_W = 64
ROPE_BASE = 10000.0
GM_GROUPS = 8
GM_CHUNK = 128
PEER_HEADS = 8
PEER_NKEYS = 128
PEER_TOPK = 16
NEG_INF = float("-inf")
POS_INF = float("inf")


def _params(*sem):
    return pltpu.CompilerParams(dimension_semantics=sem, vmem_limit_bytes=VMEM_LIMIT_BYTES)


def _dot(a, b):
    return jnp.dot(a, b, preferred_element_type=F32)


def _dot_nt(a, b):
    return lax.dot_general(a, b, (((1,), (1,)), ((), ())), preferred_element_type=F32)


def _dot_tn(a, b):
    return lax.dot_general(a, b, (((0,), (0,)), ((), ())), preferred_element_type=F32)


def _split2(x):
    hi = x.astype(BF16)
    lo = (x - hi.astype(F32)).astype(BF16)
    return hi, lo


def _split3(x):
    hi = x.astype(BF16)
    r = x - hi.astype(F32)
    mid = r.astype(BF16)
    lo = (r - mid.astype(F32)).astype(BF16)
    return hi, mid, lo


def _dot3(a, b, dot=_dot):
    ah, al = _split2(a)
    bh, bl = _split2(b)
    return dot(ah, bh) + (dot(ah, bl) + dot(al, bh))


def _gelu_tanh(x):
    c = math.sqrt(2.0 / math.pi)
    half = 0.5 * x
    return half + half * jnp.tanh(x * (c + (0.044715 * c) * (x * x)))


def _gelu_tanh_lowp(x):
    c = -2.0 * math.sqrt(2.0 / math.pi)
    return x / (1.0 + jnp.exp(x * (c + (0.044715 * c) * (x * x))))


def _log_sigmoid(x):
    return -(jnp.maximum(-x, 0.0) + jnp.log1p(jnp.exp(-jnp.abs(x))))


def _mod_row(i, n_ctx_tiles, tiles_per_latent):
    return jnp.where(i < n_ctx_tiles, 0, 1 + (i - n_ctx_tiles) // tiles_per_latent)


def _pack_kernel(x_ref, o_ref, *, transpose):
    x = x_ref[...]
    if transpose:
        x = x.T
    o_ref[...] = pltpu.bitcast(x.astype(BF16), F32)


def _pack_rows(x, layer, ncols=None, transpose=False):
    _, r, c = x.shape
    c = c if ncols is None else ncols
    tr, tc = 512, 1024
    assert r % tr == 0 and c % tc == 0
    if transpose:
        out_shape, out_spec = (c // 2, r), pl.BlockSpec((tc // 2, tr), lambda i, j: (j, i))
    else:
        out_shape, out_spec = (r // 2, c), pl.BlockSpec((tr // 2, tc), lambda i, j: (i, j))
    return pl.pallas_call(
        functools.partial(_pack_kernel, transpose=transpose),
        out_shape=jax.ShapeDtypeStruct(out_shape, F32),
        grid=(r // tr, c // tc),
        in_specs=[pl.BlockSpec((None, tr, tc), lambda i, j: (layer, i, j))],
        out_specs=out_spec,
        compiler_params=_params("arbitrary", "arbitrary"),
        name="pack_t" if transpose else "pack",
    )(x)


def _adaln_kernel(cond_ref, w_ref, b_ref, o_ref):
    c = cond_ref[...]
    s = c * jax.nn.sigmoid(c)
    o_ref[...] = _dot3(s, w_ref[...]) + b_ref[...]


def _adaln(cond8, w_mod, b_mod):
    depth, d, dout = w_mod.shape
    tn = 1024
    return pl.pallas_call(
        _adaln_kernel,
        out_shape=jax.ShapeDtypeStruct((depth, 8, dout), F32),
        grid=(depth, dout // tn),
        in_specs=[
            pl.BlockSpec((8, d), lambda l, j: (0, 0)),
            pl.BlockSpec((None, d, tn), lambda l, j: (l, 0, j)),
            pl.BlockSpec((None, 1, tn), lambda l, j: (l, 0, j)),
        ],
        out_specs=pl.BlockSpec((None, 8, tn), lambda l, j: (l, 0, j)),
        compiler_params=_params("arbitrary", "arbitrary"),
        name="adaln",
    )(cond8, w_mod, b_mod.reshape(depth, 1, dout))


def _stream(h, tm, cols, col_of=lambda j: 0, single_buffer_split=False):
    if not isinstance(h, tuple):
        return [h], [pl.BlockSpec((tm, cols), lambda i, j: (i, col_of(j)))], h.shape[0], None
    hc, hl = h
    nct = hc.shape[0] // tm
    kw = dict(pipeline_mode=pl.Buffered(1)) if single_buffer_split else {}
    specs = [pl.BlockSpec((tm, cols), lambda i, j: (jnp.minimum(i, nct - 1), col_of(j)), **kw),
             pl.BlockSpec((tm, cols), lambda i, j: (jnp.maximum(i - nct, 0), col_of(j)), **kw)]
    return [hc, hl], specs, hc.shape[0] + hl.shape[0], nct


def _load_stream(h_refs, split_tiles):
    if split_tiles is None:
        return h_refs[0][...]
    return jnp.where(pl.program_id(0) < split_tiles, h_refs[0][...], h_refs[1][...])


def _inproj_kernel(*refs, with_gates, emit_xn, split_tiles):
    nh = 1 if split_tiles is None else 2
    h_refs = refs[:nh]
    gain_ref, shift_ref, scale_ref, w_ref = refs[nh:nh + 4]
    rest = list(refs[nh + 4:])
    if with_gates:
        wg2_ref, bg_ref = rest[:2]
        rest = rest[2:]
    z_ref = rest.pop(0)
    if with_gates:
        g_ref, gt_ref = rest[:2]
        rest = rest[2:]
    if emit_xn:
        xn_out_ref = rest.pop(0)
    xn_scr = rest.pop(0)

    @pl.when(pl.program_id(1) == 0)
    def _():
        x = _load_stream(h_refs, split_tiles)
        xn = x * lax.rsqrt(jnp.mean(x * x, axis=-1, keepdims=True) + EPS) * gain_ref[...]
        xn = xn * (1.0 + scale_ref[...]) + shift_ref[...]
        xb = xn.astype(BF16)
        xn_scr[...] = xb
        if emit_xn:
            xn_out_ref[...] = pltpu.bitcast(xn.T.astype(BF16), F32)
        if with_gates:
            ng = bg_ref.shape[1]
            w2 = wg2_ref[...]
            w2_hi = w2.astype(BF16)
            lane = lax.broadcasted_iota(jnp.int32, w2.shape, 1)
            w_cat = jnp.where(lane < ng, w2_hi, (w2 - w2_hi.astype(F32)).astype(BF16))
            xl = (xn - xb.astype(F32)).astype(BF16)
            p = _dot(xb, w_cat)
            g = p[:, :ng] + (p[:, ng:] + _dot(xl, w2_hi)[:, :ng]) + bg_ref[...]
            g_ref[...] = g
            eye = (lax.broadcasted_iota(jnp.int32, (ng, ng), 0)
                   == lax.broadcasted_iota(jnp.int32, (ng, ng), 1)).astype(F32).astype(BF16)
            g3 = _split3(g)
            gt_ref[...] = _dot_nt(eye, g3[0]) + (_dot_nt(eye, g3[1]) + _dot_nt(eye, g3[2]))

    z_ref[...] = _dot(xn_scr[...], pltpu.bitcast(w_ref[...], BF16))


def _inproj(h, gain, mod3, layer, mod_base, w_packed, n_ctx_tiles, tiles_per_latent, w_gates=None, b_gates=None,
            emit_xn=False, tn=1024, name="inproj"):
    d, dout = 2 * w_packed.shape[0], w_packed.shape[1]
    tm = TOKEN_TILE * math.gcd(4, n_ctx_tiles, tiles_per_latent)
    per = tm // TOKEN_TILE
    with_gates = w_gates is not None
    h_arrays, h_specs, n, split_tiles = _stream(h, tm, d, single_buffer_split=True)

    def mod_idx(off):
        return lambda i, j: ((layer * 8 + _mod_row(i * per, n_ctx_tiles, tiles_per_latent)) * N_MOD + mod_base + off,
                             0, 0)

    in_specs = h_specs + [
        pl.BlockSpec((1, d), lambda i, j: (0, 0)),
        pl.BlockSpec((None, 1, d), mod_idx(0)),
        pl.BlockSpec((None, 1, d), mod_idx(1)),
        pl.BlockSpec((d // 2, tn), lambda i, j: (0, j)),
    ]
    args = h_arrays + [gain.reshape(1, d), mod3, mod3, w_packed]
    out_shape = [jax.ShapeDtypeStruct((n, dout), F32)]
    out_specs = [pl.BlockSpec((tm, tn), lambda i, j: (i, j))]
    if with_gates:
        ng = w_gates.shape[1]
        in_specs += [pl.BlockSpec((d, 2 * ng), lambda i, j: (0, 0)), pl.BlockSpec((1, ng), lambda i, j: (0, 0))]
        args += [jnp.concatenate([w_gates, w_gates], axis=1), b_gates.reshape(1, ng)]
        out_shape += [jax.ShapeDtypeStruct((n, ng), F32), jax.ShapeDtypeStruct((ng, n), F32)]
        out_specs += [pl.BlockSpec((tm, ng), lambda i, j: (i, 0)), pl.BlockSpec((ng, tm), lambda i, j: (0, i))]
    if emit_xn:
        out_shape.append(jax.ShapeDtypeStruct((d // 2, n), F32))
        out_specs.append(pl.BlockSpec((d // 2, tm), lambda i, j: (0, i)))
    return pl.pallas_call(
        functools.partial(_inproj_kernel, with_gates=with_gates, emit_xn=emit_xn, split_tiles=split_tiles),
        out_shape=out_shape,
        grid=(n // tm, dout // tn),
        in_specs=in_specs,
        out_specs=out_specs,
        scratch_shapes=[pltpu.VMEM((tm, d), BF16)],
        compiler_params=_params("arbitrary", "arbitrary"),
        name=name,
    )(*args)


def _mlstm_kernel(*refs, has_init, emit_state, nchunks):
    q_ref, k_ref, v_ref, g_ref, gt_ref = refs[:5]
    rest = list(refs[5:])
    if has_init:
        c0_ref, n0_ref, m0_ref = rest[:3]
        rest = rest[3:]
    hs_ref = rest.pop(0)
    if emit_state:
        cout_ref, nout_ref, mout_ref = rest[:3]
        rest = rest[3:]
    c_scr, n_scr, m_scr = rest
    nh, dh, L = MLSTM_HEADS, MLSTM_DH, MLSTM_CHUNK
    d = pl.program_id(0)
    s = pl.program_id(2)

    @pl.when(s == 0)
    def _():
        if has_init:
            c_scr[...] = c0_ref[...]
            n_scr[...] = n0_ref[...]
            m_scr[...] = m0_ref[...]
        else:
            c_scr[...] = jnp.zeros_like(c_scr)
            n_scr[...] = jnp.zeros_like(n_scr)
            m_scr[...] = jnp.zeros_like(m_scr)

    row = lax.broadcasted_iota(jnp.int32, (L, L), 0)
    col = lax.broadcasted_iota(jnp.int32, (L, L), 1)
    sgn = jnp.where(d == 0, 1, -1)
    mask = (row - col) * sgn >= 0
    maskb = jnp.where(mask, 1.0, 0.0).astype(BF16)

    g = g_ref[...]
    gt = gt_ref[...]
    fwd = d == 0
    i_col = jnp.where(fwd, g[:, 0:nh], g[:, nh:2 * nh])
    f_col = _log_sigmoid(jnp.where(fwd, g[:, 2 * nh:3 * nh], g[:, 3 * nh:4 * nh]))
    i_row = jnp.where(fwd, gt[0:nh], gt[nh:2 * nh])
    f_row = _log_sigmoid(jnp.where(fwd, gt[2 * nh:3 * nh], gt[3 * nh:4 * nh]))
    fc = _split3(f_col)
    b_col = _dot(maskb, fc[0]) + (_dot(maskb, fc[1]) + _dot(maskb, fc[2]))
    fr = _split3(f_row)
    b_row = _dot_nt(fr[0], maskb) + (_dot_nt(fr[1], maskb) + _dot_nt(fr[2], maskb))
    btot_col = jnp.sum(f_col, axis=0, keepdims=True)
    m_all = m_scr[...]

    m_new_parts = []
    for h in range(nh):
        sl = slice(h * dh, (h + 1) * dh)
        qh = q_ref[:, sl].astype(BF16)
        kf = k_ref[:, sl] * (dh ** -0.5)
        kh = kf.astype(BF16)
        vh = v_ref[:, sl].astype(BF16)
        b_c = b_col[:, h:h + 1]
        b_r = b_row[h:h + 1, :]
        i_c = i_col[:, h:h + 1]
        i_r = i_row[h:h + 1, :]
        m = m_all[:, h:h + 1]
        btot = btot_col[:, h:h + 1]

        dm = jnp.where(mask, b_c - b_r + i_r, NEG_INF)
        inter = b_c + m
        m_t = jnp.maximum(inter, jnp.max(dm, axis=1, keepdims=True))
        w = jnp.exp(dm - m_t)
        a = jnp.exp(inter - m_t)
        sc = _dot_nt(qh, kh) * w
        cb = c_scr[h].astype(BF16)
        num = a * _dot(qh, cb) + _dot(sc.astype(BF16), vh)
        nb = n_scr[h:h + 1, :].astype(BF16).astype(F32)
        qn = jnp.sum(qh.astype(F32) * nb, axis=1, keepdims=True)
        den = a * qn + jnp.sum(sc, axis=1, keepdims=True)
        hs_ref[:, sl] = num / jnp.maximum(jnp.abs(den), jnp.exp(-m_t))

        g_c = btot - b_c + i_c
        g_r = btot - b_r + i_r
        m_new = jnp.maximum(btot + m, jnp.max(g_c, axis=0, keepdims=True))
        decay = jnp.exp(btot + m - m_new)
        ws_c = jnp.exp(g_c - m_new)
        ws_r = jnp.exp(g_r - m_new)
        kw = (kf * ws_c).astype(BF16)
        c_scr[h] = decay * c_scr[h] + _dot_tn(kw, vh)
        n_scr[h:h + 1, :] = decay * n_scr[h:h + 1, :] + _dot(ws_r.astype(BF16), kh)
        m_new_parts.append(m_new)
    m_scr[...] = jnp.concatenate(m_new_parts, axis=1)

    if emit_state:
        @pl.when(s == nchunks - 1)
        def _():
            cout_ref[...] = c_scr[...]
            nout_ref[...] = n_scr[...]
            mout_ref[...] = m_scr[...]


def _mlstm(z, gates, gates_t, tok_off, nseq, seqlen, j, init=None, emit_state=False, n_even=1):
    nh, dh, L = MLSTM_HEADS, MLSTM_DH, MLSTM_CHUNK
    w = nh * dh
    nchunks = seqlen // L
    off = tok_off // L

    def chunk(d, b, s):
        return off + b * nchunks + jnp.where(d == 0, s, nchunks - 1 - s)

    in_specs = [
        pl.BlockSpec((L, w), lambda d, b, s: (chunk(d, b, s), 0)),
        pl.BlockSpec((L, w), lambda d, b, s: (chunk(d, b, s), 1)),
        pl.BlockSpec((L, w), lambda d, b, s: (chunk(d, b, s), 2)),
        pl.BlockSpec((L, 4 * nh), lambda d, b, s: (chunk(d, b, s), 0)),
        pl.BlockSpec((4 * nh, L), lambda d, b, s: (0, chunk(d, b, s))),
    ]
    args = [z, z, z, gates, gates_t]
    has_init = init is not None
    if has_init:
        c0, n0, m0 = init
        in_specs += [
            pl.BlockSpec((None, None, None, nh, dh, dh), lambda d, b, s: (b, j, d, 0, 0, 0)),
            pl.BlockSpec((None, None, None, nh, dh), lambda d, b, s: (b, j, d, 0, 0)),
            pl.BlockSpec((None, None, None, 1, nh), lambda d, b, s: (b, j, d, 0, 0)),
        ]
        args += [c0, n0, m0.reshape(m0.shape[:3] + (1, nh))]
    out_shape = [jax.ShapeDtypeStruct((2, nseq * seqlen, w), F32)]
    out_specs = [pl.BlockSpec((None, L, w), lambda d, b, s: (d, chunk(d, b, s) - off, 0))]
    if emit_state:
        out_shape += [
            jax.ShapeDtypeStruct((nseq, n_even, 2, nh, dh, dh), F32),
            jax.ShapeDtypeStruct((nseq, n_even, 2, nh, dh), F32),
            jax.ShapeDtypeStruct((nseq, n_even, 2, 1, nh), F32),
        ]
        out_specs += [
            pl.BlockSpec((None, None, None, nh, dh, dh), lambda d, b, s: (b, j, d, 0, 0, 0)),
            pl.BlockSpec((None, None, None, nh, dh), lambda d, b, s: (b, j, d, 0, 0)),
            pl.BlockSpec((None, None, None, 1, nh), lambda d, b, s: (b, j, d, 0, 0)),
        ]
    return pl.pallas_call(
        functools.partial(_mlstm_kernel, has_init=has_init, emit_state=emit_state, nchunks=nchunks),
        out_shape=out_shape,
        grid=(2, nseq, nchunks),
        in_specs=in_specs,
        out_specs=out_specs,
        scratch_shapes=[pltpu.VMEM((nh, dh, dh), F32), pltpu.VMEM((nh, dh), F32), pltpu.VMEM((1, nh), F32)],
        compiler_params=_params("arbitrary", "arbitrary", "arbitrary"),
        name="mlstm_ctx" if emit_state else "mlstm_lat",
    )(*args)


def _even_out_kernel(*refs, n_ctx_tiles, tiles_per_latent, ctx_len, lat_len, split_tiles):
    nhr = 1 if split_tiles is None else 2
    h_refs = refs[:nhr]
    (gate_ref, hsc_ref, hsl_ref, o_ref, p_ref, pprev_ref, pnext_ref, gain_ref, pw_ref, ps_ref, w_ref, out_ref,
     cat_scr) = refs[nhr:]
    i = pl.program_id(0)
    tm = TOKEN_TILE
    nh, dh = MLSTM_HEADS, MLSTM_DH
    wm = nh * dh

    @pl.when(pl.program_id(1) == 0)
    def _():
        is_ctx = i < n_ctx_tiles
        hs = jnp.where(is_ctx, hsc_ref[0] + hsc_ref[1], hsl_ref[0] + hsl_ref[1])
        for h in range(nh):
            sl = slice(h * dh, (h + 1) * dh)
            x = hs[:, sl]
            y = x * lax.rsqrt(jnp.mean(x * x, axis=-1, keepdims=True) + EPS) * gain_ref[:, sl]
            cat_scr[:, sl] = (y * jax.nn.sigmoid(o_ref[:, sl])).astype(BF16)

        tiles_ctx = ctx_len // tm
        pos = jnp.where(is_ctx, i % tiles_ctx, (i - n_ctx_tiles) % tiles_per_latent)
        ntile = jnp.where(is_ctx, tiles_ctx, tiles_per_latent)
        seqlen = jnp.where(is_ctx, ctx_len, lat_len)
        x = p_ref[...]
        prev = jnp.where(pos > 0, pprev_ref[...], 0.0)
        nxt = jnp.where(pos < ntile - 1, pnext_ref[...], 0.0)
        pad = jnp.zeros((LANES - 2 * POOL_HALO, x.shape[1]), F32)
        xcat = jnp.concatenate([prev, x, nxt, pad], axis=0)
        xh, xl = _split2(xcat)
        t = lax.broadcasted_iota(jnp.int32, (tm, tm + LANES), 0)
        sidx = lax.broadcasted_iota(jnp.int32, (tm, tm + LANES), 1) - POOL_HALO
        tpos = pos * tm + lax.broadcasted_iota(jnp.int32, (tm, 1), 0)
        gw = wm // len(POOL_WINDOWS)
        for gi, win in enumerate(POOL_WINDOWS):
            sl = slice(gi * gw, (gi + 1) * gw)
            band = jnp.where((sidx >= t - win // 2) & (sidx < t - win // 2 + win), 1.0, 0.0).astype(BF16)
            lo = jnp.maximum(tpos - win // 2, 0)
            hi = jnp.minimum(tpos - win // 2 + win, seqlen)
            cnt = (hi - lo).astype(F32)
            p = (_dot(band, xh[:, sl]) + _dot(band, xl[:, sl])) / cnt - x[:, sl]
            y = _dot(p.astype(BF16), pw_ref[gi]) * ps_ref[:, sl]
            cat_scr[:, wm + gi * gw:wm + (gi + 1) * gw] = y.astype(BF16)

    out_ref[...] = (_load_stream(h_refs, split_tiles)
                    + gate_ref[...] * _dot(cat_scr[...], pltpu.bitcast(w_ref[...], BF16)))


def _even_out(h, mod3, layer, hs_ctx, hs_lat, z, gain, pool_w_bf16, pool_scale, w_out_bf16, n_ctx_tiles,
              tiles_per_latent, ctx_len, lat_len, tn=2048):
    tm = TOKEN_TILE
    d = 2 * w_out_bf16.shape[0]
    h_arrays, h_specs, n, split_tiles = _stream(h, tm, tn, col_of=lambda j: j)
    wm = MLSTM_HEADS * MLSTM_DH
    nlt = n // tm - n_ctx_tiles
    rows8 = n // POOL_HALO
    per = tm // POOL_HALO

    def gate_idx(i, j):
        return ((layer * 8 + _mod_row(i, n_ctx_tiles, tiles_per_latent)) * N_MOD + 2, 0, j)

    return pl.pallas_call(
        functools.partial(_even_out_kernel, n_ctx_tiles=n_ctx_tiles, tiles_per_latent=tiles_per_latent,
                          ctx_len=ctx_len, lat_len=lat_len, split_tiles=split_tiles),
        out_shape=jax.ShapeDtypeStruct((n, d), F32),
        grid=(n // tm, d // tn),
        in_specs=h_specs + [
            pl.BlockSpec((None, 1, tn), gate_idx),
            pl.BlockSpec((2, tm, wm), lambda i, j: (0, jnp.minimum(i, n_ctx_tiles - 1), 0)),
            pl.BlockSpec((2, tm, wm), lambda i, j: (0, jnp.clip(i - n_ctx_tiles, 0, nlt - 1), 0)),
            pl.BlockSpec((tm, wm), lambda i, j: (i, 3)),
            pl.BlockSpec((tm, wm), lambda i, j: (i, 4)),
            pl.BlockSpec((POOL_HALO, wm), lambda i, j: (jnp.maximum(i * per - 1, 0), 4)),
            pl.BlockSpec((POOL_HALO, wm), lambda i, j: (jnp.minimum((i + 1) * per, rows8 - 1), 4)),
            pl.BlockSpec((1, wm), lambda i, j: (0, 0)),
            pl.BlockSpec(pool_w_bf16.shape, lambda i, j: (0, 0, 0)),
            pl.BlockSpec((1, wm), lambda i, j: (0, 0)),
            pl.BlockSpec((d // 2, tn), lambda i, j: (0, j)),
        ],
        out_specs=pl.BlockSpec((tm, tn), lambda i, j: (i, j)),
        scratch_shapes=[pltpu.VMEM((tm, d), BF16)],
        compiler_params=_params("arbitrary", "arbitrary"),
        name="even_out",
    )(*h_arrays, mod3, hs_ctx, hs_lat, z, z, z, z, gain.reshape(1, wm), pool_w_bf16, pool_scale.reshape(1, wm),
      w_out_bf16)


def _qk_kernel(*refs, rope, emit_cache):
    q_ref, k_ref, v_ref, gain_ref = refs[:4]
    rest = list(refs[4:])
    if rope:
        cos_ref, sin_ref = rest[:2]
        rest = rest[2:]
    qn_ref, kn_ref, vb_ref = rest[:3]
    rest = rest[3:]
    hd = 2 * DA_DH
    lane = lax.broadcasted_iota(jnp.int32, (q_ref.shape[0], hd), 1)
    first = lane < DA_DH

    def norm(x, gain):
        sq = x * x
        s1 = jnp.sum(jnp.where(first, sq, 0.0), axis=-1, keepdims=True)
        s2 = jnp.sum(jnp.where(first, 0.0, sq), axis=-1, keepdims=True)
        ms = jnp.where(first, s1, s2) * (1.0 / DA_DH)
        y = x * lax.rsqrt(ms + EPS) * gain
        if rope:
            quarter = DA_DH // 4
            partner = jnp.where((lane & quarter) == 0, pltpu.roll(y, LANES - quarter, 1), pltpu.roll(y, quarter, 1))
            y = y * cos_ref[...] + partner * sin_ref[...]
        return y

    vb_ref[...] = v_ref[...].astype(BF16)
    for h in range(DA_HEADS):
        cols = slice(h * hd, (h + 1) * hd)
        qn = norm(q_ref[:, cols], gain_ref[0:1, :])
        kn = norm(k_ref[:, cols], gain_ref[1:2, :])
        qn_ref[:, cols] = (qn * (DA_DH ** -0.5)).astype(BF16)
        kn_ref[:, cols] = kn.astype(BF16)
        if emit_cache:
            newk_ref, newv_ref = rest
            newk_ref[h] = kn
            newv_ref[h] = v_ref[:, cols]


def _qk(z, qk_gain2, tok_off, ntok, j, rope_tables=None, cache_shape=None, seqlen=None):
    tm = TOKEN_TILE
    hd = 2 * DA_DH
    w = DA_HEADS * hd
    off = tok_off // tm
    nblk_w = w // hd
    rope = rope_tables is not None
    emit_cache = cache_shape is not None
    in_specs = [
        pl.BlockSpec((tm, w), lambda i: (off + i, 0)),
        pl.BlockSpec((tm, w), lambda i: (off + i, 1)),
        pl.BlockSpec((tm, w), lambda i: (off + i, 2)),
        pl.BlockSpec((2, hd), lambda i: (0, 0)),
    ]
    args = [z, z, z, qk_gain2]
    if rope:
        tps = seqlen // tm
        in_specs += [pl.BlockSpec((tm, hd), lambda i: (i % tps, 0))] * 2
        args += list(rope_tables)
    out_shape = [jax.ShapeDtypeStruct((ntok, w), BF16)] * 3
    out_specs = [pl.BlockSpec((tm, w), lambda i: (i, 0))] * 3
    if emit_cache:
        out_shape += [jax.ShapeDtypeStruct(cache_shape, F32)] * 2
        out_specs += [pl.BlockSpec((None, None, DA_HEADS, tm, hd), lambda i: (i, j, 0, 0, 0))] * 2
    return pl.pallas_call(
        functools.partial(_qk_kernel, rope=rope, emit_cache=emit_cache),
        out_shape=out_shape,
        grid=(ntok // tm,),
        in_specs=in_specs,
        out_specs=out_specs,
        compiler_params=_params("arbitrary"),
        name="qk_ctx" if emit_cache else "qk_lat",
    )(*args)


def _rope_tables(t):
    rows = t // GRID_W
    row = jnp.repeat(jnp.arange(rows), GRID_W).astype(F32)
    col = (jnp.arange(rows * GRID_W) % GRID_W).astype(F32)
    n_freq = DA_DH // 4
    inv = ROPE_BASE ** (-jnp.arange(n_freq, dtype=F32) / n_freq)
    ar, ac = row[:, None] * inv, col[:, None] * inv
    cos = jnp.concatenate([jnp.cos(ar), jnp.cos(ar), jnp.cos(ac), jnp.cos(ac)], axis=-1)
    sin = jnp.concatenate([-jnp.sin(ar), jnp.sin(ar), -jnp.sin(ac), jnp.sin(ac)], axis=-1)
    return jnp.tile(cos, (1, 2)), jnp.tile(sin, (1, 2))


def _attn_kernel(*refs, n_pieces, lam_init):
    q_ref, lp_ref, subln_ref = refs[:3]
    kv = refs[3:3 + 2 * n_pieces]
    o_ref = refs[3 + 2 * n_pieces]
    lp = lp_ref[...]
    lam = (jnp.exp(jnp.sum(lp[0:1] * lp[1:2], axis=-1, keepdims=True))
           - jnp.exp(jnp.sum(lp[2:3] * lp[3:4], axis=-1, keepdims=True)) + lam_init)
    q = q_ref[...]
    lane = lax.broadcasted_iota(jnp.int32, q.shape, 1)
    zero = jnp.zeros_like(q)
    qa = jnp.where(lane < DA_DH, q, zero)
    qb = jnp.where(lane < DA_DH, zero, q)
    ks = [kv[2 * p][...].astype(BF16) for p in range(n_pieces)]
    vs = [kv[2 * p + 1][...].astype(BF16) for p in range(n_pieces)]

    sa = [_dot_nt(qa, k) for k in ks]
    sb = [_dot_nt(qb, k) for k in ks]

    def attend(parts):
        m = functools.reduce(jnp.maximum, [jnp.max(x, axis=-1, keepdims=True) for x in parts])
        es = [jnp.exp(x - m) for x in parts]
        tot = functools.reduce(lambda a, b: a + b, [jnp.sum(e, axis=-1, keepdims=True) for e in es])
        pv = functools.reduce(lambda a, b: a + b, [_dot(e.astype(BF16), v) for e, v in zip(es, vs)])
        return pv / tot

    o = attend(sa) - lam * attend(sb)
    o = o * lax.rsqrt(jnp.mean(o * o, axis=-1, keepdims=True) + EPS) * subln_ref[...]
    o_ref[...] = (o * (1.0 - lam_init)).astype(BF16)


def _attn(qn, kn, vb, lam_params, subln, lam_init, nseq, seqlen, tq, j, caches=None):
    hd = 2 * DA_DH
    nq = seqlen // tq
    in_specs = [
        pl.BlockSpec((tq, hd), lambda b, h, qi: (b * nq + qi, h)),
        pl.BlockSpec((None, 4, DA_DH), lambda b, h, qi: (j, 0, 0)),
        pl.BlockSpec((None, 1, hd), lambda b, h, qi: (j, 0, 0)),
        pl.BlockSpec((seqlen, hd), lambda b, h, qi: (b, h)),
        pl.BlockSpec((seqlen, hd), lambda b, h, qi: (b, h)),
    ]
    args = [qn, lam_params, subln.reshape(subln.shape[0], 1, hd), kn, vb]
    n_pieces = 1
    if caches is not None:
        ck, cv = caches
        past = ck.shape[3]
        in_specs += [pl.BlockSpec((None, None, None, past, hd), lambda b, h, qi: (b, j, h, 0, 0))] * 2
        args += [ck, cv]
        n_pieces = 2
    return pl.pallas_call(
        functools.partial(_attn_kernel, n_pieces=n_pieces, lam_init=lam_init),
        out_shape=jax.ShapeDtypeStruct(qn.shape, BF16),
        grid=(nseq, DA_HEADS, nq),
        in_specs=in_specs,
        out_specs=pl.BlockSpec((tq, hd), lambda b, h, qi: (b * nq + qi, h)),
        compiler_params=_params("arbitrary", "arbitrary", "arbitrary"),
        name="attn_lat" if caches is not None else "attn_ctx",
    )(*args)


def _odd_out_kernel(h_ref, gate_ref, ac_ref, al_ref, gu_ref, gv_ref, ws_ref, bt_ref, w_ref, out_ref, cat_scr, *,
                    n_ctx_tiles):
    i = pl.program_id(0)
    tm = TOKEN_TILE
    wa = DA_HEADS * 2 * DA_DH
    gw = LANES

    @pl.when(pl.program_id(1) == 0)
    def _():
        cat_scr[:, 0:wa] = jnp.where(i < n_ctx_tiles, ac_ref[...], al_ref[...])
        for c in range(tm // GM_CHUNK):
            rows = slice(c * GM_CHUNK, (c + 1) * GM_CHUNK)
            for g in range(GM_GROUPS):
                cols = slice(g * gw, (g + 1) * gw)
                u = _gelu_tanh(gu_ref[rows, cols])
                v = _gelu_tanh(gv_ref[rows, cols])
                vn = v * lax.rsqrt(jnp.mean(v * v, axis=-1, keepdims=True) + EPS)
                mixed = _dot(ws_ref[g], vn.astype(BF16)) + bt_ref[:, g:g + 1]
                cat_scr[rows, wa + g * gw:wa + (g + 1) * gw] = (u * mixed).astype(BF16)

    out_ref[...] = h_ref[...] + gate_ref[...] * _dot(cat_scr[...], pltpu.bitcast(w_ref[...], BF16))


def _odd_out(h, mod3, layer, a_ctx, a_lat, z, gm_ws_bf16, gm_b_t, w_out_bf16, n_ctx_tiles, tiles_per_latent, tn=2048):
    n, d = h.shape
    tm = TOKEN_TILE
    wa = DA_HEADS * 2 * DA_DH
    wg = GM_GROUPS * LANES
    nlt = n // tm - n_ctx_tiles

    def gate_idx(i, j):
        return ((layer * 8 + _mod_row(i, n_ctx_tiles, tiles_per_latent)) * N_MOD + 2, 0, j)

    return pl.pallas_call(
        functools.partial(_odd_out_kernel, n_ctx_tiles=n_ctx_tiles),
        out_shape=jax.ShapeDtypeStruct((n, d), F32),
        grid=(n // tm, d // tn),
        in_specs=[
            pl.BlockSpec((tm, tn), lambda i, j: (i, j)),
            pl.BlockSpec((None, 1, tn), gate_idx),
            pl.BlockSpec((tm, wa), lambda i, j: (jnp.minimum(i, n_ctx_tiles - 1), 0)),
            pl.BlockSpec((tm, wa), lambda i, j: (jnp.clip(i - n_ctx_tiles, 0, nlt - 1), 0)),
            pl.BlockSpec((tm, wg), lambda i, j: (i, 3)),
            pl.BlockSpec((tm, wg), lambda i, j: (i, 4)),
            pl.BlockSpec(gm_ws_bf16.shape, lambda i, j: (0, 0, 0)),
            pl.BlockSpec(gm_b_t.shape, lambda i, j: (0, 0)),
            pl.BlockSpec((d // 2, tn), lambda i, j: (0, j)),
        ],
        out_specs=pl.BlockSpec((tm, tn), lambda i, j: (i, j)),
        scratch_shapes=[pltpu.VMEM((tm, d), BF16)],
        compiler_params=_params("arbitrary", "arbitrary"),
        name="odd_out",
    )(h, mod3, a_ctx, a_lat, z, z, gm_ws_bf16, gm_b_t, w_out_bf16)


def _top_values(x, k):
    vals = []
    cur = x
    for r in range(k):
        m = jnp.max(cur, axis=0, keepdims=True)
        vals.append(m)
        if r + 1 < k:
            cur = jnp.where(cur == m, NEG_INF, cur)
    return jnp.concatenate(vals, axis=0)


def _oddeven_merge(lo, hi, r):
    step = r * 2
    if step < hi - lo:
        yield from _oddeven_merge(lo, hi, step)
        yield from _oddeven_merge(lo + r, hi, step)
        yield from [(i, i + r) for i in range(lo + r, hi - r, step)]
    else:
        yield (lo, lo + r)


def _oddeven_merge_sort(lo, hi):
    if hi - lo >= 1:
        mid = lo + (hi - lo) // 2
        yield from _oddeven_merge_sort(lo, mid)
        yield from _oddeven_merge_sort(mid + 1, hi)
        yield from _oddeven_merge(lo, hi, 1)


def _top_sorted(x):
    n = len(x)
    assert n == PEER_TOPK

    def exchange(v, i, j):
        v[i], v[j] = jnp.maximum(v[i], v[j]), jnp.minimum(v[i], v[j])

    v = list(x)
    for i, j in _oddeven_merge_sort(0, n - 1):
        exchange(v, i, j)
    shift = SUBLANES // 2
    while shift:
        w = [pltpu.roll(t, SUBLANES - shift, 0) for t in v]
        v = [jnp.maximum(v[i], w[n - 1 - i]) for i in range(n)]
        d = n // 2
        while d:
            for i in range(n):
                if not i & d:
                    exchange(v, i, i + d)
            d //= 2
        shift //= 2
    return jnp.concatenate([t[0:1, :] for t in v], axis=0)


def _dup_bf16(x):
    b = pltpu.bitcast(x.astype(BF16).astype(F32), jnp.uint32)
    return pltpu.bitcast(b | (b >> 16), F32)


def _peer_route_kernel(q_ref, sk_ref, rank_ref, e2_ref, cnt_ref, c_ref, s_scr):
    half = PEER_NKEYS
    k = PEER_TOPK
    n_slabs = half // SUBLANES
    for h in range(PEER_HEADS):
        q1 = q_ref[:, 2 * h * half:(2 * h + 1) * half].astype(BF16)
        q2 = q_ref[:, (2 * h + 1) * half:(2 * h + 2) * half].astype(BF16)
        s_scr[0] = _dot_nt(sk_ref[2 * h], q1)
        s_scr[1] = _dot_nt(sk_ref[2 * h + 1], q2)

        def lane_tile(t, carry):
            lanes = pl.ds(pl.multiple_of(t * LANES, LANES), LANES)
            s1 = s_scr[0, :, lanes]
            s2 = s_scr[1, :, lanes]
            a_all = _top_sorted([s1[SUBLANES * i:SUBLANES * (i + 1)] for i in range(n_slabs)])
            b_all = _top_sorted([s2[SUBLANES * i:SUBLANES * (i + 1)] for i in range(n_slabs)])
            b = [b_all[r:r + 1] for r in range(k)]
            rows = [a_all[r:r + 1] + b_all[0:k // (r + 1)] for r in range(k)]
            nrows = sum(k // (r + 1) for r in range(k))
            rows.append(jnp.full((-nrows % SUBLANES, LANES), NEG_INF, F32))
            top = _top_values(jnp.concatenate(rows, axis=0), k)
            tau = top[k - 1:k]
            z = jnp.sum(jnp.exp(top - top[0:1]), axis=0, keepdims=True)
            cnt = jnp.zeros(s1.shape, F32)
            rank2 = jnp.zeros(s2.shape, F32)
            for r in range(k):
                cnt = jnp.where(s1 + b[r] >= tau, float(r + 1), cnt)
                rank2 = jnp.where(b[r] > s2, float(r + 1), rank2)
            rank_ref[h, :, lanes] = pltpu.bitcast(rank2.astype(BF16), F32)
            e2_ref[h, :, lanes] = pltpu.bitcast(jnp.exp(s2 - b[0]).astype(BF16), F32)
            cnt_ref[h, :, lanes] = _dup_bf16(cnt)
            c_ref[h, :, lanes] = _dup_bf16(jnp.exp(s1 - a_all[0:1]) / z)
            return carry

        lax.fori_loop(0, q_ref.shape[0] // LANES, lane_tile, 0)


def _peer_route(qp, subkeys_bf16, tm=512):
    n = qp.shape[0]
    def out(rows):
        return (jax.ShapeDtypeStruct((PEER_HEADS, rows, n), F32),
                pl.BlockSpec((PEER_HEADS, rows, tm), lambda i: (0, 0, i)))

    outs = [out(PEER_NKEYS // 2)] * 2 + [out(PEER_NKEYS)] * 2
    return pl.pallas_call(
        _peer_route_kernel,
        out_shape=[o[0] for o in outs],
        grid=(n // tm,),
        in_specs=[
            pl.BlockSpec((tm, qp.shape[1]), lambda i: (i, 0)),
            pl.BlockSpec(subkeys_bf16.shape, lambda i: (0, 0, 0)),
        ],
        out_specs=[o[1] for o in outs],
        scratch_shapes=[pltpu.VMEM((2, PEER_NKEYS, tm), F32)],
        compiler_params=_params("arbitrary"),
        name="peer_route",
    )(qp, subkeys_bf16)


def _peer_dense_kernel(xnt_ref, h_ref, gate_ref, u_ref, vt_ref, rank_ref, e2_ref, cnt_ref, c_ref, *rest, te, n_tiles,
                       split_tiles):
    n_out = 1 if split_tiles is None else 2
    out_refs = rest[:n_out]
    act0_scr, act1_scr, wa0_scr, wa1_scr, acc_scr = rest[n_out:]
    s = pl.program_id(0)
    n_pairs = pl.num_programs(0) - 2
    nkeys = PEER_NKEYS
    per = te // nkeys
    tm = acc_scr.shape[1]
    half = tm // 2
    assert per % 4 == 0
    pack = 2 * SUBLANES

    @pl.when(s == 0)
    def _():
        act1_scr[...] = jnp.zeros_like(act1_scr)
        wa0_scr[...] = jnp.zeros_like(wa0_scr)

    live = jnp.logical_and(s >= 1, s <= n_pairs)
    t2 = jnp.clip(s - 1, 0, n_pairs - 1) % n_tiles
    pair3 = jnp.clip(s - 2, 0, n_pairs - 1)
    t3 = pair3 % n_tiles
    tok3 = pair3 // n_tiles

    @pl.when(t3 == 0)
    def _():
        acc_scr[...] = jnp.zeros_like(acc_scr)

    def step(act_w, act_r, wa_w, wa_r):
        e_split = 1
        e_rows = te // e_split

        def stage1(c, eq):
            cols = slice(c * half, (c + 1) * half)
            u = pltpu.bitcast(u_ref[eq * e_rows // 2:(eq + 1) * e_rows // 2, :], BF16)
            act_w[eq * e_rows:(eq + 1) * e_rows, cols] = _dot(u, pltpu.bitcast(xnt_ref[:, cols], BF16))

        d_split = 8
        d_rows = acc_scr.shape[0] // d_split

        def stage3(c, dq):
            cols = slice(c * half, (c + 1) * half)
            rows = slice(dq * d_rows, (dq + 1) * d_rows)
            vt = pltpu.bitcast(vt_ref[dq * d_rows // 2:(dq + 1) * d_rows // 2, :], BF16)
            acc_scr[rows, cols] += _dot(vt, wa_r[:, cols])

        mxu_work = []
        for c in range(2):
            for eq in range(e_split):
                mxu_work.append(functools.partial(stage1, c, eq))
                mxu_work += [functools.partial(stage3, c, dq)
                             for dq in range(eq * d_split // e_split, (eq + 1) * d_split // e_split)]
        n_lt = tm // LANES
        n_valu = per * n_lt
        issued = 0
        for ii in range(per):
            r = t2 * per + ii
            cnt_rows = [jnp.where(live, cnt_ref[h, pl.ds(r, 1), :], 0.0) for h in range(PEER_HEADS)]
            c_rows = [c_ref[h, pl.ds(r, 1), :] for h in range(PEER_HEADS)]
            for lt in range(n_lt):
                while issued < len(mxu_work) and (issued * n_valu) // len(mxu_work) <= ii * n_lt + lt:
                    mxu_work[issued]()
                    issued += 1
                lanes = slice(lt * LANES, (lt + 1) * LANES)
                cnts = [pltpu.bitcast(jnp.broadcast_to(x[:, lanes], (SUBLANES, LANES)), BF16) for x in cnt_rows]
                ccs = [pltpu.bitcast(jnp.broadcast_to(x[:, lanes], (SUBLANES, LANES)), BF16) for x in c_rows]
                for sb in range(nkeys // pack):
                    words = slice(sb * SUBLANES, (sb + 1) * SUBLANES)
                    rows = slice(ii * nkeys + sb * pack, ii * nkeys + (sb + 1) * pack)
                    w = None
                    for h in range(PEER_HEADS):
                        e2 = pltpu.bitcast(e2_ref[h, words, lanes], BF16)
                        rank = pltpu.bitcast(rank_ref[h, words, lanes], BF16)
                        wh = jnp.where(rank < cnts[h], e2 * ccs[h], jnp.zeros_like(e2))
                        w = wh if w is None else w + wh
                    wa_w[rows, lanes] = w * _gelu_tanh_lowp(act_r[rows, lanes].astype(BF16))
        assert issued == len(mxu_work)

    @pl.when(s % 2 == 0)
    def _():
        step(act0_scr, act1_scr, wa1_scr, wa0_scr)

    @pl.when(s % 2 == 1)
    def _():
        step(act1_scr, act0_scr, wa0_scr, wa1_scr)

    @pl.when(jnp.logical_and(t3 == n_tiles - 1, s >= 2))
    def _():
        res = h_ref[...] + gate_ref[...] * acc_scr[...].T
        if split_tiles is None:
            out_refs[0][...] = res
        else:
            @pl.when(tok3 < split_tiles)
            def _():
                out_refs[0][...] = res

            @pl.when(tok3 >= split_tiles)
            def _():
                out_refs[1][...] = res


def _peer_dense(xnt, h, mod3, layer, u_packed, vt_packed, rank2, e2, cnt, c, n_ctx_tiles, tiles_per_latent, tm=512,
                te=512, split_out=False):
    n, d = h.shape
    n_tiles = 2 * u_packed.shape[0] // te
    per = tm // TOKEN_TILE
    n_pairs = (n // tm) * n_tiles

    def tok(lag):
        return lambda t: jnp.clip(t - lag, 0, n_pairs - 1) // n_tiles

    def exp(lag):
        return lambda t: jnp.clip(t - lag, 0, n_pairs - 1) % n_tiles

    tok1, tok2, tok3, exp1, exp3 = tok(0), tok(1), tok(2), exp(0), exp(2)
    if split_out:
        split_tiles = n_ctx_tiles // per
        n_ctx = split_tiles * tm
        out_shape = [jax.ShapeDtypeStruct((n_ctx, d), F32), jax.ShapeDtypeStruct((n - n_ctx, d), F32)]
        out_specs = [pl.BlockSpec((tm, d), lambda t: (jnp.minimum(tok3(t), split_tiles - 1), 0)),
                     pl.BlockSpec((tm, d), lambda t: (jnp.maximum(tok3(t) - split_tiles, 0), 0))]
    else:
        split_tiles = None
        out_shape = jax.ShapeDtypeStruct((n, d), F32)
        out_specs = pl.BlockSpec((tm, d), lambda t: (tok3(t), 0))

    def gate_idx(t):
        return ((layer * 8 + _mod_row(tok3(t) * per, n_ctx_tiles, tiles_per_latent)) * N_MOD + 5, 0, 0)

    rspec = pl.BlockSpec((PEER_HEADS, PEER_NKEYS, tm), lambda t: (0, 0, tok2(t)))
    pspec = pl.BlockSpec((PEER_HEADS, PEER_NKEYS // 2, tm), lambda t: (0, 0, tok2(t)))
    return pl.pallas_call(
        functools.partial(_peer_dense_kernel, te=te, n_tiles=n_tiles, split_tiles=split_tiles),
        out_shape=out_shape,
        grid=(n_pairs + 2,),
        in_specs=[
            pl.BlockSpec((d // 2, tm), lambda t: (0, tok1(t))),
            pl.BlockSpec((tm, d), lambda t: (tok3(t), 0), pipeline_mode=pl.Buffered(1)),
            pl.BlockSpec((None, 1, d), gate_idx),
            pl.BlockSpec((te // 2, d), lambda t: (exp1(t), 0)),
            pl.BlockSpec((d // 2, te), lambda t: (0, exp3(t))),
            pspec, pspec, rspec, rspec,
        ],
        out_specs=out_specs,
        scratch_shapes=[pltpu.VMEM((te, tm), F32), pltpu.VMEM((te, tm), F32), pltpu.VMEM((te, tm), BF16),
                        pltpu.VMEM((te, tm), BF16), pltpu.VMEM((d, tm), F32)],
        compiler_params=_params("arbitrary"),
        name="peer_dense",
    )(xnt, h, mod3, u_packed, vt_packed, rank2, e2, cnt, c)


def kernel(x_prompt, x_sample, state_mlstm_C, state_mlstm_n, state_mlstm_m, cache_da_k, cache_da_v, c, c_ctx, norm_mix, norm_ffn, w_mod, b_mod, w_in_even, b_gate_even, mlstm_gain, pool_w, pool_scale, w_out_even, w_in_odd, qk_gain, da_lambda, da_subln, gm_ws, gm_b, w_out_odd, peer_wq, peer_subkeys, peer_u, peer_v):
    nb, s_len, d = x_prompt.shape
    nbd, t_len, _ = x_sample.shape
    depth = w_mod.shape[0]
    tm = TOKEN_TILE
    assert s_len == tm and t_len % tm == 0 and nbd <= 7 and t_len % GRID_W == 0
    n_ctx = nb * s_len
    n_lat = nbd * t_len
    n_ctx_tiles = n_ctx // tm
    tiles_per_latent = t_len // tm
    n_even = (depth + 1) // 2
    n_odd = depth // 2

    h = (x_prompt.reshape(n_ctx, d), x_sample.reshape(n_lat, d))
    cond8 = jnp.concatenate([c_ctx[None], c, jnp.zeros((7 - nbd, d), F32)], axis=0)
    mod3 = _adaln(cond8, w_mod, b_mod).reshape(depth * 8 * N_MOD, 1, d)
    rope = _rope_tables(t_len)

    new_c, new_n, new_m, new_k, new_v = [], [], [], [], []
    wm = MLSTM_HEADS * MLSTM_DH
    for l in range(depth):
        j = l // 2
        if l % 2 == 0:
            n_main = 5 * wm
            z, gates, gates_t = _inproj(h, norm_mix[l], mod3, l, 0, _pack_rows(w_in_even, j, ncols=n_main),
                                        n_ctx_tiles, tiles_per_latent, w_gates=w_in_even[j, :, n_main:],
                                        b_gates=b_gate_even[j],
                                        name="inproj_even")
            hs_ctx, cc, cn, cm = _mlstm(z, gates, gates_t, 0, nb, s_len, 0, emit_state=True)
            (hs_lat,) = _mlstm(z, gates, gates_t, n_ctx, nbd, t_len, j,
                               init=(state_mlstm_C, state_mlstm_n, state_mlstm_m))
            new_c.append(cc)
            new_n.append(cn)
            new_m.append(cm[:, :, :, 0, :])
            h = _even_out(h, mod3, l, hs_ctx, hs_lat, z, mlstm_gain[j], pool_w[j].astype(BF16), pool_scale[j],
                          _pack_rows(w_out_even, j), n_ctx_tiles, tiles_per_latent, s_len, t_len)
        else:
            lam_init = 0.8 - 0.6 * math.exp(-0.3 * l)
            (z,) = _inproj(h, norm_mix[l], mod3, l, 0, _pack_rows(w_in_odd, j), n_ctx_tiles, tiles_per_latent,
                           name="inproj_odd")
            gain2 = jnp.tile(qk_gain[j], (1, 2))
            cache_shape = (nb, 1, DA_HEADS, s_len, 2 * DA_DH)
            qn_c, kn_c, vb_c, nk, nv = _qk(z, gain2, 0, n_ctx, 0, cache_shape=cache_shape)
            qn_l, kn_l, vb_l = _qk(z, gain2, n_ctx, n_lat, 0, rope_tables=rope, seqlen=t_len)
            new_k.append(nk)
            new_v.append(nv)
            a_ctx = _attn(qn_c, kn_c, vb_c, da_lambda, da_subln, lam_init, nb, s_len, s_len, j)
            a_lat = _attn(qn_l, kn_l, vb_l, da_lambda, da_subln, lam_init, nbd, t_len, tm, j,
                          caches=(cache_da_k, cache_da_v))
            h = _odd_out(h, mod3, l, a_ctx, a_lat, z, gm_ws[j].astype(BF16), gm_b[j].T, _pack_rows(w_out_odd, j),
                         n_ctx_tiles, tiles_per_latent)
        qp, xnt = _inproj(h, norm_ffn[l], mod3, l, 3, _pack_rows(peer_wq, l), n_ctx_tiles, tiles_per_latent,
                          emit_xn=True, name="inproj_peer")
        sk = peer_subkeys[l].reshape(2 * PEER_HEADS, PEER_NKEYS, PEER_NKEYS).astype(BF16)
        rank2, e2, cnt, cw = _peer_route(qp, sk)
        h = _peer_dense(xnt, h, mod3, l, _pack_rows(peer_u, l), _pack_rows(peer_v, l, transpose=True), rank2,
                        e2, cnt, cw,
                        n_ctx_tiles, tiles_per_latent, split_out=(l == depth - 1))

    def join(parts):
        return parts[0] if len(parts) == 1 else jnp.concatenate(parts, axis=1)

    return (h[0].reshape(nb, s_len, d), h[1].reshape(nbd, t_len, d), join(new_c), join(new_n), join(new_m),
            join(new_k), join(new_v))
```

```python
import functools
import math

import jax
import jax.numpy as jnp
from jax import lax
from jax.experimental import pallas as pl
from jax.experimental.pallas import tpu as pltpu

F32 = jnp.float32
BF16 = jnp.bfloat16

N_MOD = 6
EPS = 1e-6
TOKEN_TILE = 256
LANES = 128
SUBLANES = 8
VMEM_LIMIT_BYTES = 56 * 1024 * 1024

MLSTM_HEADS = 4
MLSTM_DH = 256
MLSTM_CHUNK = 128
POOL_WINDOWS = (2, 4, 8, 16)
POOL_HALO = 8
DA_HEADS = 8
DA_DH = 64
GRID_W = 64
ROPE_BASE = 10000.0
GM_GROUPS = 8
GM_CHUNK = 128
PEER_HEADS = 8
PEER_NKEYS = 128
PEER_TOPK = 16
NEG_INF = float("-inf")
POS_INF = float("inf")


def _params(*sem):
    return pltpu.CompilerParams(dimension_semantics=sem, vmem_limit_bytes=VMEM_LIMIT_BYTES)


def _dot(a, b):
    return jnp.dot(a, b, preferred_element_type=F32)


def _dot_nt(a, b):
    return lax.dot_general(a, b, (((1,), (1,)), ((), ())), preferred_element_type=F32)


def _dot_tn(a, b):
    return lax.dot_general(a, b, (((0,), (0,)), ((), ())), preferred_element_type=F32)


def _split2(x):
    hi = x.astype(BF16)
    lo = (x - hi.astype(F32)).astype(BF16)
    return hi, lo


def _split3(x):
    hi = x.astype(BF16)
    r = x - hi.astype(F32)
    mid = r.astype(BF16)
    lo = (r - mid.astype(F32)).astype(BF16)
    return hi, mid, lo


def _dot3(a, b, dot=_dot):
    ah, al = _split2(a)
    bh, bl = _split2(b)
    return dot(ah, bh) + (dot(ah, bl) + dot(al, bh))


def _gelu_tanh(x):
    c = math.sqrt(2.0 / math.pi)
    half = 0.5 * x
    return half + half * jnp.tanh(x * (c + (0.044715 * c) * (x * x)))


def _gelu_tanh_lowp(x):
    c = -2.0 * math.sqrt(2.0 / math.pi)
    return x / (1.0 + jnp.exp(x * (c + (0.044715 * c) * (x * x))))


def _log_sigmoid(x):
    return -(jnp.maximum(-x, 0.0) + jnp.log1p(jnp.exp(-jnp.abs(x))))


def _mod_row(i, n_ctx_tiles, tiles_per_latent):
    return jnp.where(i < n_ctx_tiles, 0, 1 + (i - n_ctx_tiles) // tiles_per_latent)


def _pack_kernel(x_ref, o_ref, *, transpose):
    x = x_ref[...]
    if transpose:
        x = x.T
    o_ref[...] = pltpu.bitcast(x.astype(BF16), F32)


PACK_ROWS = 512


def _pack_rows(x, layer, ncols=None, transpose=False):
    _, r, c = x.shape
    c = c if ncols is None else ncols
    tr, tc = PACK_ROWS, 1024
    assert r % tr == 0 and c % tc == 0
    if transpose:
        out_shape, out_spec = (r // tr, c // 2, tr), pl.BlockSpec((None, tc // 2, tr), lambda i, j: (i, j, 0))
    else:
        out_shape, out_spec = (r // 2, c), pl.BlockSpec((tr // 2, tc), lambda i, j: (i, j))
    return pl.pallas_call(
        functools.partial(_pack_kernel, transpose=transpose),
        out_shape=jax.ShapeDtypeStruct(out_shape, F32),
        grid=(r // tr, c // tc),
        in_specs=[pl.BlockSpec((None, tr, tc), lambda i, j: (layer, i, j))],
        out_specs=out_spec,
        compiler_params=_params("arbitrary", "arbitrary"),
        name="pack_t" if transpose else "pack",
    )(x)


def _adaln_kernel(cond_ref, w_ref, b_ref, o_ref):
    c = cond_ref[...]
    s = c * jax.nn.sigmoid(c)
    o_ref[...] = _dot3(s, w_ref[...]) + b_ref[...]


def _adaln(cond8, w_mod, b_mod):
    depth, d, dout = w_mod.shape
    tn = 1024
    return pl.pallas_call(
        _adaln_kernel,
        out_shape=jax.ShapeDtypeStruct((depth, 8, dout), F32),
        grid=(depth, dout // tn),
        in_specs=[
            pl.BlockSpec((8, d), lambda l, j: (0, 0)),
            pl.BlockSpec((None, d, tn), lambda l, j: (l, 0, j)),
            pl.BlockSpec((None, 1, tn), lambda l, j: (l, 0, j)),
        ],
        out_specs=pl.BlockSpec((None, 8, tn), lambda l, j: (l, 0, j)),
        compiler_params=_params("arbitrary", "arbitrary"),
        name="adaln",
    )(cond8, w_mod, b_mod.reshape(depth, 1, dout))


def _stream(h, tm, cols, col_of=lambda j: 0, single_buffer_split=False):
    if not isinstance(h, tuple):
        return [h], [pl.BlockSpec((tm, cols), lambda i, j: (i, col_of(j)))], h.shape[0], None
    hc, hl = h
    nct = hc.shape[0] // tm
    kw = dict(pipeline_mode=pl.Buffered(1)) if single_buffer_split else {}
    specs = [pl.BlockSpec((tm, cols), lambda i, j: (jnp.minimum(i, nct - 1), col_of(j)), **kw),
             pl.BlockSpec((tm, cols), lambda i, j: (jnp.maximum(i - nct, 0), col_of(j)), **kw)]
    return [hc, hl], specs, hc.shape[0] + hl.shape[0], nct


def _load_stream(h_refs, split_tiles):
    if split_tiles is None:
        return h_refs[0][...]
    return jnp.where(pl.program_id(0) < split_tiles, h_refs[0][...], h_refs[1][...])


def _inproj_kernel(*refs, with_gates, emit_xn, split_tiles):
    nh = 1 if split_tiles is None else 2
    h_refs = refs[:nh]
    gain_ref, shift_ref, scale_ref, w_ref = refs[nh:nh + 4]
    rest = list(refs[nh + 4:])
    if with_gates:
        wg2_ref, bg_ref = rest[:2]
        rest = rest[2:]
    z_ref = rest.pop(0)
    if with_gates:
        g_ref, gt_ref = rest[:2]
        rest = rest[2:]
    if emit_xn:
        xn_out_ref = rest.pop(0)
    xn_scr = rest.pop(0)

    @pl.when(pl.program_id(1) == 0)
    def _():
        x = _load_stream(h_refs, split_tiles)
        xn = x * lax.rsqrt(jnp.mean(x * x, axis=-1, keepdims=True) + EPS) * gain_ref[...]
        xn = xn * (1.0 + scale_ref[...]) + shift_ref[...]
        xb = xn.astype(BF16)
        xn_scr[...] = xb
        if emit_xn:
            xn_out_ref[...] = pltpu.bitcast(xn.T.astype(BF16), F32)
        if with_gates:
            ng = bg_ref.shape[1]
            w2 = wg2_ref[...]
            w2_hi = w2.astype(BF16)
            lane = lax.broadcasted_iota(jnp.int32, w2.shape, 1)
            w_cat = jnp.where(lane < ng, w2_hi, (w2 - w2_hi.astype(F32)).astype(BF16))
            xl = (xn - xb.astype(F32)).astype(BF16)
            p = _dot(xb, w_cat)
            g = p[:, :ng] + (p[:, ng:] + _dot(xl, w2_hi)[:, :ng]) + bg_ref[...]
            g_ref[...] = g
            eye = (lax.broadcasted_iota(jnp.int32, (ng, ng), 0)
                   == lax.broadcasted_iota(jnp.int32, (ng, ng), 1)).astype(F32).astype(BF16)
            g3 = _split3(g)
            gt_ref[...] = _dot_nt(eye, g3[0]) + (_dot_nt(eye, g3[1]) + _dot_nt(eye, g3[2]))

    z_ref[...] = _dot(xn_scr[...], pltpu.bitcast(w_ref[...], BF16))


def _inproj(h, gain, mod3, layer, mod_base, w_packed, n_ctx_tiles, tiles_per_latent, w_gates=None, b_gates=None,
            emit_xn=False, tn=1024, name="inproj"):
    d, dout = 2 * w_packed.shape[0], w_packed.shape[1]
    tm = TOKEN_TILE * math.gcd(4, n_ctx_tiles, tiles_per_latent)
    per = tm // TOKEN_TILE
    with_gates = w_gates is not None
    h_arrays, h_specs, n, split_tiles = _stream(h, tm, d, single_buffer_split=True)

    def mod_idx(off):
        return lambda i, j: ((layer * 8 + _mod_row(i * per, n_ctx_tiles, tiles_per_latent)) * N_MOD + mod_base + off,
                             0, 0)

    in_specs = h_specs + [
        pl.BlockSpec((1, d), lambda i, j: (0, 0)),
        pl.BlockSpec((None, 1, d), mod_idx(0)),
        pl.BlockSpec((None, 1, d), mod_idx(1)),
        pl.BlockSpec((d // 2, tn), lambda i, j: (0, j)),
    ]
    args = h_arrays + [gain.reshape(1, d), mod3, mod3, w_packed]
    out_shape = [jax.ShapeDtypeStruct((n, dout), F32)]
    out_specs = [pl.BlockSpec((tm, tn), lambda i, j: (i, j))]
    if with_gates:
        ng = w_gates.shape[1]
        in_specs += [pl.BlockSpec((d, 2 * ng), lambda i, j: (0, 0)), pl.BlockSpec((1, ng), lambda i, j: (0, 0))]
        args += [jnp.concatenate([w_gates, w_gates], axis=1), b_gates.reshape(1, ng)]
        out_shape += [jax.ShapeDtypeStruct((n, ng), F32), jax.ShapeDtypeStruct((ng, n), F32)]
        out_specs += [pl.BlockSpec((tm, ng), lambda i, j: (i, 0)), pl.BlockSpec((ng, tm), lambda i, j: (0, i))]
    if emit_xn:
        out_shape.append(jax.ShapeDtypeStruct((d // 2, n), F32))
        out_specs.append(pl.BlockSpec((d // 2, tm), lambda i, j: (0, i)))
    return pl.pallas_call(
        functools.partial(_inproj_kernel, with_gates=with_gates, emit_xn=emit_xn, split_tiles=split_tiles),
        out_shape=out_shape,
        grid=(n // tm, dout // tn),
        in_specs=in_specs,
        out_specs=out_specs,
        scratch_shapes=[pltpu.VMEM((tm, d), BF16)],
        compiler_params=_params("arbitrary", "arbitrary"),
        name=name,
    )(*args)


def _mlstm_kernel(*refs, has_init, emit_state, nchunks):
    q_ref, k_ref, v_ref, g_ref, gt_ref = refs[:5]
    rest = list(refs[5:])
    if has_init:
        c0_ref, n0_ref, m0_ref = rest[:3]
        rest = rest[3:]
    hs_ref = rest.pop(0)
    if emit_state:
        cout_ref, nout_ref, mout_ref = rest[:3]
        rest = rest[3:]
    c_scr, n_scr, m_scr = rest
    nh, dh, L = MLSTM_HEADS, MLSTM_DH, MLSTM_CHUNK
    d = pl.program_id(0)
    s = pl.program_id(2)

    @pl.when(s == 0)
    def _():
        if has_init:
            c_scr[...] = c0_ref[...]
            n_scr[...] = n0_ref[...]
            m_scr[...] = m0_ref[...]
        else:
            c_scr[...] = jnp.zeros_like(c_scr)
            n_scr[...] = jnp.zeros_like(n_scr)
            m_scr[...] = jnp.zeros_like(m_scr)

    row = lax.broadcasted_iota(jnp.int32, (L, L), 0)
    col = lax.broadcasted_iota(jnp.int32, (L, L), 1)
    sgn = jnp.where(d == 0, 1, -1)
    mask = (row - col) * sgn >= 0
    maskb = jnp.where(mask, 1.0, 0.0).astype(BF16)

    g = g_ref[...]
    gt = gt_ref[...]
    fwd = d == 0
    i_col = jnp.where(fwd, g[:, 0:nh], g[:, nh:2 * nh])
    f_col = _log_sigmoid(jnp.where(fwd, g[:, 2 * nh:3 * nh], g[:, 3 * nh:4 * nh]))
    i_row = jnp.where(fwd, gt[0:nh], gt[nh:2 * nh])
    f_row = _log_sigmoid(jnp.where(fwd, gt[2 * nh:3 * nh], gt[3 * nh:4 * nh]))
    fc = _split3(f_col)
    b_col = _dot(maskb, fc[0]) + (_dot(maskb, fc[1]) + _dot(maskb, fc[2]))
    fr = _split3(f_row)
    b_row = _dot_nt(fr[0], maskb) + (_dot_nt(fr[1], maskb) + _dot_nt(fr[2], maskb))
    btot_col = jnp.sum(f_col, axis=0, keepdims=True)
    m_all = m_scr[...]

    m_new_parts = []
    for h in range(nh):
        sl = slice(h * dh, (h + 1) * dh)
        qh = q_ref[:, sl].astype(BF16)
        kf = k_ref[:, sl] * (dh ** -0.5)
        kh = kf.astype(BF16)
        vh = v_ref[:, sl].astype(BF16)
        b_c = b_col[:, h:h + 1]
        b_r = b_row[h:h + 1, :]
        i_c = i_col[:, h:h + 1]
        i_r = i_row[h:h + 1, :]
        m = m_all[:, h:h + 1]
        btot = btot_col[:, h:h + 1]

        dm = jnp.where(mask, b_c - b_r + i_r, NEG_INF)
        inter = b_c + m
        m_t = jnp.maximum(inter, jnp.max(dm, axis=1, keepdims=True))
        w = jnp.exp(dm - m_t)
        a = jnp.exp(inter - m_t)
        sc = _dot_nt(qh, kh) * w
        cb = c_scr[h].astype(BF16)
        num = a * _dot(qh, cb) + _dot(sc.astype(BF16), vh)
        nb = n_scr[h:h + 1, :].astype(BF16).astype(F32)
        qn = jnp.sum(qh.astype(F32) * nb, axis=1, keepdims=True)
        den = a * qn + jnp.sum(sc, axis=1, keepdims=True)
        hs_ref[:, sl] = num / jnp.maximum(jnp.abs(den), jnp.exp(-m_t))

        g_c = btot - b_c + i_c
        g_r = btot - b_r + i_r
        m_new = jnp.maximum(btot + m, jnp.max(g_c, axis=0, keepdims=True))
        decay = jnp.exp(btot + m - m_new)
        ws_c = jnp.exp(g_c - m_new)
        ws_r = jnp.exp(g_r - m_new)
        kw = (kf * ws_c).astype(BF16)
        c_scr[h] = decay * c_scr[h] + _dot_tn(kw, vh)
        n_scr[h:h + 1, :] = decay * n_scr[h:h + 1, :] + _dot(ws_r.astype(BF16), kh)
        m_new_parts.append(m_new)
    m_scr[...] = jnp.concatenate(m_new_parts, axis=1)

    if emit_state:
        @pl.when(s == nchunks - 1)
        def _():
            cout_ref[...] = c_scr[...]
            nout_ref[...] = n_scr[...]
            mout_ref[...] = m_scr[...]


def _mlstm(z, gates, gates_t, tok_off, nseq, seqlen, j, init=None, emit_state=False, n_even=1):
    nh, dh, L = MLSTM_HEADS, MLSTM_DH, MLSTM_CHUNK
    w = nh * dh
    nchunks = seqlen // L
    off = tok_off // L

    def chunk(d, b, s):
        return off + b * nchunks + jnp.where(d == 0, s, nchunks - 1 - s)

    in_specs = [
        pl.BlockSpec((L, w), lambda d, b, s: (chunk(d, b, s), 0)),
        pl.BlockSpec((L, w), lambda d, b, s: (chunk(d, b, s), 1)),
        pl.BlockSpec((L, w), lambda d, b, s: (chunk(d, b, s), 2)),
        pl.BlockSpec((L, 4 * nh), lambda d, b, s: (chunk(d, b, s), 0)),
        pl.BlockSpec((4 * nh, L), lambda d, b, s: (0, chunk(d, b, s))),
    ]
    args = [z, z, z, gates, gates_t]
    has_init = init is not None
    if has_init:
        c0, n0, m0 = init
        in_specs += [
            pl.BlockSpec((None, None, None, nh, dh, dh), lambda d, b, s: (b, j, d, 0, 0, 0)),
            pl.BlockSpec((None, None, None, nh, dh), lambda d, b, s: (b, j, d, 0, 0)),
            pl.BlockSpec((None, None, None, 1, nh), lambda d, b, s: (b, j, d, 0, 0)),
        ]
        args += [c0, n0, m0.reshape(m0.shape[:3] + (1, nh))]
    out_shape = [jax.ShapeDtypeStruct((2, nseq * seqlen, w), F32)]
    out_specs = [pl.BlockSpec((None, L, w), lambda d, b, s: (d, chunk(d, b, s) - off, 0))]
    if emit_state:
        out_shape += [
            jax.ShapeDtypeStruct((nseq, n_even, 2, nh, dh, dh), F32),
            jax.ShapeDtypeStruct((nseq, n_even, 2, nh, dh), F32),
            jax.ShapeDtypeStruct((nseq, n_even, 2, 1, nh), F32),
        ]
        out_specs += [
            pl.BlockSpec((None, None, None, nh, dh, dh), lambda d, b, s: (b, j, d, 0, 0, 0)),
            pl.BlockSpec((None, None, None, nh, dh), lambda d, b, s: (b, j, d, 0, 0)),
            pl.BlockSpec((None, None, None, 1, nh), lambda d, b, s: (b, j, d, 0, 0)),
        ]
    return pl.pallas_call(
        functools.partial(_mlstm_kernel, has_init=has_init, emit_state=emit_state, nchunks=nchunks),
        out_shape=out_shape,
        grid=(2, nseq, nchunks),
        in_specs=in_specs,
        out_specs=out_specs,
        scratch_shapes=[pltpu.VMEM((nh, dh, dh), F32), pltpu.VMEM((nh, dh), F32), pltpu.VMEM((1, nh), F32)],
        compiler_params=_params("arbitrary", "arbitrary", "arbitrary"),
        name="mlstm_ctx" if emit_state else "mlstm_lat",
    )(*args)


def _even_out_kernel(*refs, n_ctx_tiles, tiles_per_latent, ctx_len, lat_len, split_tiles):
    nhr = 1 if split_tiles is None else 2
    h_refs = refs[:nhr]
    (gate_ref, hsc_ref, hsl_ref, o_ref, p_ref, pprev_ref, pnext_ref, gain_ref, pw_ref, ps_ref, w_ref, out_ref,
     cat_scr) = refs[nhr:]
    i = pl.program_id(0)
    tm = TOKEN_TILE
    nh, dh = MLSTM_HEADS, MLSTM_DH
    wm = nh * dh

    @pl.when(pl.program_id(1) == 0)
    def _():
        is_ctx = i < n_ctx_tiles
        hs = jnp.where(is_ctx, hsc_ref[0] + hsc_ref[1], hsl_ref[0] + hsl_ref[1])
        for h in range(nh):
            sl = slice(h * dh, (h + 1) * dh)
            x = hs[:, sl]
            y = x * lax.rsqrt(jnp.mean(x * x, axis=-1, keepdims=True) + EPS) * gain_ref[:, sl]
            cat_scr[:, sl] = (y * jax.nn.sigmoid(o_ref[:, sl])).astype(BF16)

        tiles_ctx = ctx_len // tm
        pos = jnp.where(is_ctx, i % tiles_ctx, (i - n_ctx_tiles) % tiles_per_latent)
        ntile = jnp.where(is_ctx, tiles_ctx, tiles_per_latent)
        seqlen = jnp.where(is_ctx, ctx_len, lat_len)
        x = p_ref[...]
        prev = jnp.where(pos > 0, pprev_ref[...], 0.0)
        nxt = jnp.where(pos < ntile - 1, pnext_ref[...], 0.0)
        pad = jnp.zeros((LANES - 2 * POOL_HALO, x.shape[1]), F32)
        xcat = jnp.concatenate([prev, x, nxt, pad], axis=0)
        xh, xl = _split2(xcat)
        t = lax.broadcasted_iota(jnp.int32, (tm, tm + LANES), 0)
        sidx = lax.broadcasted_iota(jnp.int32, (tm, tm + LANES), 1) - POOL_HALO
        tpos = pos * tm + lax.broadcasted_iota(jnp.int32, (tm, 1), 0)
        gw = wm // len(POOL_WINDOWS)
        for gi, win in enumerate(POOL_WINDOWS):
            sl = slice(gi * gw, (gi + 1) * gw)
            band = jnp.where((sidx >= t - win // 2) & (sidx < t - win // 2 + win), 1.0, 0.0).astype(BF16)
            lo = jnp.maximum(tpos - win // 2, 0)
            hi = jnp.minimum(tpos - win // 2 + win, seqlen)
            cnt = (hi - lo).astype(F32)
            p = (_dot(band, xh[:, sl]) + _dot(band, xl[:, sl])) / cnt - x[:, sl]
            y = _dot(p.astype(BF16), pw_ref[gi]) * ps_ref[:, sl]
            cat_scr[:, wm + gi * gw:wm + (gi + 1) * gw] = y.astype(BF16)

    out_ref[...] = (_load_stream(h_refs, split_tiles)
                    + gate_ref[...] * _dot(cat_scr[...], pltpu.bitcast(w_ref[...], BF16)))


def _even_out(h, mod3, layer, hs_ctx, hs_lat, z, gain, pool_w_bf16, pool_scale, w_out_bf16, n_ctx_tiles,
              tiles_per_latent, ctx_len, lat_len, tn=2048):
    tm = TOKEN_TILE
    d = 2 * w_out_bf16.shape[0]
    h_arrays, h_specs, n, split_tiles = _stream(h, tm, tn, col_of=lambda j: j)
    wm = MLSTM_HEADS * MLSTM_DH
    nlt = n // tm - n_ctx_tiles
    rows8 = n // POOL_HALO
    per = tm // POOL_HALO

    def gate_idx(i, j):
        return ((layer * 8 + _mod_row(i, n_ctx_tiles, tiles_per_latent)) * N_MOD + 2, 0, j)

    return pl.pallas_call(
        functools.partial(_even_out_kernel, n_ctx_tiles=n_ctx_tiles, tiles_per_latent=tiles_per_latent,
                          ctx_len=ctx_len, lat_len=lat_len, split_tiles=split_tiles),
        out_shape=jax.ShapeDtypeStruct((n, d), F32),
        grid=(n // tm, d // tn),
        in_specs=h_specs + [
            pl.BlockSpec((None, 1, tn), gate_idx),
            pl.BlockSpec((2, tm, wm), lambda i, j: (0, jnp.minimum(i, n_ctx_tiles - 1), 0)),
            pl.BlockSpec((2, tm, wm), lambda i, j: (0, jnp.clip(i - n_ctx_tiles, 0, nlt - 1), 0)),
            pl.BlockSpec((tm, wm), lambda i, j: (i, 3)),
            pl.BlockSpec((tm, wm), lambda i, j: (i, 4)),
            pl.BlockSpec((POOL_HALO, wm), lambda i, j: (jnp.maximum(i * per - 1, 0), 4)),
            pl.BlockSpec((POOL_HALO, wm), lambda i, j: (jnp.minimum((i + 1) * per, rows8 - 1), 4)),
            pl.BlockSpec((1, wm), lambda i, j: (0, 0)),
            pl.BlockSpec(pool_w_bf16.shape, lambda i, j: (0, 0, 0)),
            pl.BlockSpec((1, wm), lambda i, j: (0, 0)),
            pl.BlockSpec((d // 2, tn), lambda i, j: (0, j)),
        ],
        out_specs=pl.BlockSpec((tm, tn), lambda i, j: (i, j)),
        scratch_shapes=[pltpu.VMEM((tm, d), BF16)],
        compiler_params=_params("arbitrary", "arbitrary"),
        name="even_out",
    )(*h_arrays, mod3, hs_ctx, hs_lat, z, z, z, z, gain.reshape(1, wm), pool_w_bf16, pool_scale.reshape(1, wm),
      w_out_bf16)


def _qk_kernel(*refs, rope, emit_cache):
    q_ref, k_ref, v_ref, gain_ref = refs[:4]
    rest = list(refs[4:])
    if rope:
        cos_ref, sin_ref = rest[:2]
        rest = rest[2:]
    qn_ref, kn_ref, vb_ref = rest[:3]
    rest = rest[3:]
    hd = 2 * DA_DH
    lane = lax.broadcasted_iota(jnp.int32, (q_ref.shape[0], hd), 1)
    first = lane < DA_DH

    def norm(x, gain):
        sq = x * x
        s1 = jnp.sum(jnp.where(first, sq, 0.0), axis=-1, keepdims=True)
        s2 = jnp.sum(jnp.where(first, 0.0, sq), axis=-1, keepdims=True)
        ms = jnp.where(first, s1, s2) * (1.0 / DA_DH)
        y = x * lax.rsqrt(ms + EPS) * gain
        if rope:
            quarter = DA_DH // 4
            partner = jnp.where((lane & quarter) == 0, pltpu.roll(y, LANES - quarter, 1), pltpu.roll(y, quarter, 1))
            y = y * cos_ref[...] + partner * sin_ref[...]
        return y

    vb_ref[...] = v_ref[...].astype(BF16)
    for h in range(DA_HEADS):
        cols = slice(h * hd, (h + 1) * hd)
        qn = norm(q_ref[:, cols], gain_ref[0:1, :])
        kn = norm(k_ref[:, cols], gain_ref[1:2, :])
        qn_ref[:, cols] = (qn * (DA_DH ** -0.5)).astype(BF16)
        kn_ref[:, cols] = kn.astype(BF16)
        if emit_cache:
            newk_ref, newv_ref = rest
            newk_ref[h] = kn
            newv_ref[h] = v_ref[:, cols]


def _qk(z, qk_gain2, tok_off, ntok, j, rope_tables=None, cache_shape=None, seqlen=None):
    tm = TOKEN_TILE
    hd = 2 * DA_DH
    w = DA_HEADS * hd
    off = tok_off // tm
    nblk_w = w // hd
    rope = rope_tables is not None
    emit_cache = cache_shape is not None
    in_specs = [
        pl.BlockSpec((tm, w), lambda i: (off + i, 0)),
        pl.BlockSpec((tm, w), lambda i: (off + i, 1)),
        pl.BlockSpec((tm, w), lambda i: (off + i, 2)),
        pl.BlockSpec((2, hd), lambda i: (0, 0)),
    ]
    args = [z, z, z, qk_gain2]
    if rope:
        tps = seqlen // tm
        in_specs += [pl.BlockSpec((tm, hd), lambda i: (i % tps, 0))] * 2
        args += list(rope_tables)
    out_shape = [jax.ShapeDtypeStruct((ntok, w), BF16)] * 3
    out_specs = [pl.BlockSpec((tm, w), lambda i: (i, 0))] * 3
    if emit_cache:
        out_shape += [jax.ShapeDtypeStruct(cache_shape, F32)] * 2
        out_specs += [pl.BlockSpec((None, None, DA_HEADS, tm, hd), lambda i: (i, j, 0, 0, 0))] * 2
    return pl.pallas_call(
        functools.partial(_qk_kernel, rope=rope, emit_cache=emit_cache),
        out_shape=out_shape,
        grid=(ntok // tm,),
        in_specs=in_specs,
        out_specs=out_specs,
        compiler_params=_params("arbitrary"),
        name="qk_ctx" if emit_cache else "qk_lat",
    )(*args)


def _rope_tables(t):
    rows = t // GRID_W
    row = jnp.repeat(jnp.arange(rows), GRID_W).astype(F32)
    col = (jnp.arange(rows * GRID_W) % GRID_W).astype(F32)
    n_freq = DA_DH // 4
    inv = ROPE_BASE ** (-jnp.arange(n_freq, dtype=F32) / n_freq)
    ar, ac = row[:, None] * inv, col[:, None] * inv
    cos = jnp.concatenate([jnp.cos(ar), jnp.cos(ar), jnp.cos(ac), jnp.cos(ac)], axis=-1)
    sin = jnp.concatenate([-jnp.sin(ar), jnp.sin(ar), -jnp.sin(ac), jnp.sin(ac)], axis=-1)
    return jnp.tile(cos, (1, 2)), jnp.tile(sin, (1, 2))


def _attn_kernel(*refs, n_pieces, lam_init):
    q_ref, lp_ref, subln_ref = refs[:3]
    kv = refs[3:3 + 2 * n_pieces]
    o_ref = refs[3 + 2 * n_pieces]
    lp = lp_ref[...]
    lam = (jnp.exp(jnp.sum(lp[0:1] * lp[1:2], axis=-1, keepdims=True))
           - jnp.exp(jnp.sum(lp[2:3] * lp[3:4], axis=-1, keepdims=True)) + lam_init)
    q = q_ref[...]
    lane = lax.broadcasted_iota(jnp.int32, q.shape, 1)
    zero = jnp.zeros_like(q)
    qa = jnp.where(lane < DA_DH, q, zero)
    qb = jnp.where(lane < DA_DH, zero, q)
    ks = [kv[2 * p][...].astype(BF16) for p in range(n_pieces)]
    vs = [kv[2 * p + 1][...].astype(BF16) for p in range(n_pieces)]

    sa = [_dot_nt(qa, k) for k in ks]
    sb = [_dot_nt(qb, k) for k in ks]

    def attend(parts):
        m = functools.reduce(jnp.maximum, [jnp.max(x, axis=-1, keepdims=True) for x in parts])
        es = [jnp.exp(x - m) for x in parts]
        tot = functools.reduce(lambda a, b: a + b, [jnp.sum(e, axis=-1, keepdims=True) for e in es])
        pv = functools.reduce(lambda a, b: a + b, [_dot(e.astype(BF16), v) for e, v in zip(es, vs)])
        return pv / tot

    o = attend(sa) - lam * attend(sb)
    o = o * lax.rsqrt(jnp.mean(o * o, axis=-1, keepdims=True) + EPS) * subln_ref[...]
    o_ref[...] = (o * (1.0 - lam_init)).astype(BF16)


def _attn(qn, kn, vb, lam_params, subln, lam_init, nseq, seqlen, tq, j, caches=None):
    hd = 2 * DA_DH
    nq = seqlen // tq
    in_specs = [
        pl.BlockSpec((tq, hd), lambda b, h, qi: (b * nq + qi, h)),
        pl.BlockSpec((None, 4, DA_DH), lambda b, h, qi: (j, 0, 0)),
        pl.BlockSpec((None, 1, hd), lambda b, h, qi: (j, 0, 0)),
        pl.BlockSpec((seqlen, hd), lambda b, h, qi: (b, h)),
        pl.BlockSpec((seqlen, hd), lambda b, h, qi: (b, h)),
    ]
    args = [qn, lam_params, subln.reshape(subln.shape[0], 1, hd), kn, vb]
    n_pieces = 1
    if caches is not None:
        ck, cv = caches
        past = ck.shape[3]
        in_specs += [pl.BlockSpec((None, None, None, past, hd), lambda b, h, qi: (b, j, h, 0, 0))] * 2
        args += [ck, cv]
        n_pieces = 2
    return pl.pallas_call(
        functools.partial(_attn_kernel, n_pieces=n_pieces, lam_init=lam_init),
        out_shape=jax.ShapeDtypeStruct(qn.shape, BF16),
        grid=(nseq, DA_HEADS, nq),
        in_specs=in_specs,
        out_specs=pl.BlockSpec((tq, hd), lambda b, h, qi: (b * nq + qi, h)),
        compiler_params=_params("arbitrary", "arbitrary", "arbitrary"),
        name="attn_lat" if caches is not None else "attn_ctx",
    )(*args)


def _odd_out_kernel(h_ref, gate_ref, ac_ref, al_ref, gu_ref, gv_ref, ws_ref, bt_ref, w_ref, out_ref, cat_scr, *,
                    n_ctx_tiles):
    i = pl.program_id(0)
    tm = TOKEN_TILE
    wa = DA_HEADS * 2 * DA_DH
    gw = LANES

    @pl.when(pl.program_id(1) == 0)
    def _():
        cat_scr[:, 0:wa] = jnp.where(i < n_ctx_tiles, ac_ref[...], al_ref[...])
        for c in range(tm // GM_CHUNK):
            rows = slice(c * GM_CHUNK, (c + 1) * GM_CHUNK)
            for g in range(GM_GROUPS):
                cols = slice(g * gw, (g + 1) * gw)
                u = _gelu_tanh(gu_ref[rows, cols])
                v = _gelu_tanh(gv_ref[rows, cols])
                vn = v * lax.rsqrt(jnp.mean(v * v, axis=-1, keepdims=True) + EPS)
                mixed = _dot(ws_ref[g], vn.astype(BF16)) + bt_ref[:, g:g + 1]
                cat_scr[rows, wa + g * gw:wa + (g + 1) * gw] = (u * mixed).astype(BF16)

    out_ref[...] = h_ref[...] + gate_ref[...] * _dot(cat_scr[...], pltpu.bitcast(w_ref[...], BF16))


def _odd_out(h, mod3, layer, a_ctx, a_lat, z, gm_ws_bf16, gm_b_t, w_out_bf16, n_ctx_tiles, tiles_per_latent, tn=2048):
    n, d = h.shape
    tm = TOKEN_TILE
    wa = DA_HEADS * 2 * DA_DH
    wg = GM_GROUPS * LANES
    nlt = n // tm - n_ctx_tiles

    def gate_idx(i, j):
        return ((layer * 8 + _mod_row(i, n_ctx_tiles, tiles_per_latent)) * N_MOD + 2, 0, j)

    return pl.pallas_call(
        functools.partial(_odd_out_kernel, n_ctx_tiles=n_ctx_tiles),
        out_shape=jax.ShapeDtypeStruct((n, d), F32),
        grid=(n // tm, d // tn),
        in_specs=[
            pl.BlockSpec((tm, tn), lambda i, j: (i, j)),
            pl.BlockSpec((None, 1, tn), gate_idx),
            pl.BlockSpec((tm, wa), lambda i, j: (jnp.minimum(i, n_ctx_tiles - 1), 0)),
            pl.BlockSpec((tm, wa), lambda i, j: (jnp.clip(i - n_ctx_tiles, 0, nlt - 1), 0)),
            pl.BlockSpec((tm, wg), lambda i, j: (i, 3)),
            pl.BlockSpec((tm, wg), lambda i, j: (i, 4)),
            pl.BlockSpec(gm_ws_bf16.shape, lambda i, j: (0, 0, 0)),
            pl.BlockSpec(gm_b_t.shape, lambda i, j: (0, 0)),
            pl.BlockSpec((d // 2, tn), lambda i, j: (0, j)),
        ],
        out_specs=pl.BlockSpec((tm, tn), lambda i, j: (i, j)),
        scratch_shapes=[pltpu.VMEM((tm, d), BF16)],
        compiler_params=_params("arbitrary", "arbitrary"),
        name="odd_out",
    )(h, mod3, a_ctx, a_lat, z, z, gm_ws_bf16, gm_b_t, w_out_bf16)


def _top_values(x, k):
    vals = []
    cur = x
    for r in range(k):
        m = jnp.max(cur, axis=0, keepdims=True)
        vals.append(m)
        if r + 1 < k:
            cur = jnp.where(cur == m, NEG_INF, cur)
    return jnp.concatenate(vals, axis=0)


def _oddeven_merge(lo, hi, r):
    step = r * 2
    if step < hi - lo:
        yield from _oddeven_merge(lo, hi, step)
        yield from _oddeven_merge(lo + r, hi, step)
        yield from [(i, i + r) for i in range(lo + r, hi - r, step)]
    else:
        yield (lo, lo + r)


def _oddeven_merge_sort(lo, hi):
    if hi - lo >= 1:
        mid = lo + (hi - lo) // 2
        yield from _oddeven_merge_sort(lo, mid)
        yield from _oddeven_merge_sort(mid + 1, hi)
        yield from _oddeven_merge(lo, hi, 1)


def _top_sorted(x):
    n = len(x)
    assert n == PEER_TOPK

    def exchange(v, i, j):
        v[i], v[j] = jnp.maximum(v[i], v[j]), jnp.minimum(v[i], v[j])

    v = list(x)
    for i, j in _oddeven_merge_sort(0, n - 1):
        exchange(v, i, j)
    shift = SUBLANES // 2
    while shift:
        w = [pltpu.roll(t, SUBLANES - shift, 0) for t in v]
        v = [jnp.maximum(v[i], w[n - 1 - i]) for i in range(n)]
        d = n // 2
        while d:
            for i in range(n):
                if not i & d:
                    exchange(v, i, i + d)
            d //= 2
        shift //= 2
    return jnp.concatenate([t[0:1, :] for t in v], axis=0)


def _dup_bf16(x):
    b = pltpu.bitcast(x.astype(BF16).astype(F32), jnp.uint32)
    return pltpu.bitcast(b | (b >> 16), F32)


def _peer_route_kernel(q_ref, sk_ref, rank_ref, e2_ref, cnt_ref, c_ref, s_scr):
    half = PEER_NKEYS
    k = PEER_TOPK
    n_slabs = half // SUBLANES
    for h in range(PEER_HEADS):
        q1 = q_ref[:, 2 * h * half:(2 * h + 1) * half].astype(BF16)
        q2 = q_ref[:, (2 * h + 1) * half:(2 * h + 2) * half].astype(BF16)
        s_scr[0] = _dot_nt(sk_ref[2 * h], q1)
        s_scr[1] = _dot_nt(sk_ref[2 * h + 1], q2)

        def lane_tile(t, carry):
            lanes = pl.ds(pl.multiple_of(t * LANES, LANES), LANES)
            s1 = s_scr[0, :, lanes]
            s2 = s_scr[1, :, lanes]
            a_all = _top_sorted([s1[SUBLANES * i:SUBLANES * (i + 1)] for i in range(n_slabs)])
            b_all = _top_sorted([s2[SUBLANES * i:SUBLANES * (i + 1)] for i in range(n_slabs)])
            b = [b_all[r:r + 1] for r in range(k)]
            rows = [a_all[r:r + 1] + b_all[0:k // (r + 1)] for r in range(k)]
            nrows = sum(k // (r + 1) for r in range(k))
            rows.append(jnp.full((-nrows % SUBLANES, LANES), NEG_INF, F32))
            top = _top_values(jnp.concatenate(rows, axis=0), k)
            tau = top[k - 1:k]
            z = jnp.sum(jnp.exp(top - top[0:1]), axis=0, keepdims=True)
            cnt = jnp.zeros(s1.shape, F32)
            rank2 = jnp.zeros(s2.shape, F32)
            for r in range(k):
                cnt = jnp.where(s1 + b[r] >= tau, float(r + 1), cnt)
                rank2 = jnp.where(b[r] > s2, float(r + 1), rank2)
            rank_ref[h, :, lanes] = pltpu.bitcast(rank2.astype(BF16), F32)
            e2_ref[h, :, lanes] = pltpu.bitcast(jnp.exp(s2 - b[0]).astype(BF16), F32)
            cnt_ref[h, :, lanes] = _dup_bf16(cnt)
            c_ref[h, :, lanes] = _dup_bf16(jnp.exp(s1 - a_all[0:1]) / z)
            return carry

        lax.fori_loop(0, q_ref.shape[0] // LANES, lane_tile, 0)


def _peer_route(qp, subkeys_bf16, tm=512):
    n = qp.shape[0]
    def out(rows):
        return (jax.ShapeDtypeStruct((PEER_HEADS, rows, n), F32),
                pl.BlockSpec((PEER_HEADS, rows, tm), lambda i: (0, 0, i)))

    outs = [out(PEER_NKEYS // 2)] * 2 + [out(PEER_NKEYS)] * 2
    return pl.pallas_call(
        _peer_route_kernel,
        out_shape=[o[0] for o in outs],
        grid=(n // tm,),
        in_specs=[
            pl.BlockSpec((tm, qp.shape[1]), lambda i: (i, 0)),
            pl.BlockSpec(subkeys_bf16.shape, lambda i: (0, 0, 0)),
        ],
        out_specs=[o[1] for o in outs],
        scratch_shapes=[pltpu.VMEM((2, PEER_NKEYS, tm), F32)],
        compiler_params=_params("arbitrary"),
        name="peer_route",
    )(qp, subkeys_bf16)


def _peer_dense_kernel(xnt_ref, h_ref, gate_ref, u_ref, vt_ref, rank_ref, e2_ref, cnt_ref, c_ref, *rest, te, n_tiles,
                       split_tiles):
    n_out = 1 if split_tiles is None else 2
    out_refs = rest[:n_out]
    act0_scr, act1_scr, wa0_scr, wa1_scr, acc_scr = rest[n_out:]
    s = pl.program_id(0)
    n_pairs = pl.num_programs(0) - 2
    nkeys = PEER_NKEYS
    per = te // nkeys
    tm = acc_scr.shape[1]
    half = tm // 2
    assert per % 4 == 0
    pack = 2 * SUBLANES

    @pl.when(s == 0)
    def _():
        act1_scr[...] = jnp.zeros_like(act1_scr)
        wa0_scr[...] = jnp.zeros_like(wa0_scr)

    live = jnp.logical_and(s >= 1, s <= n_pairs)
    t2 = jnp.clip(s - 1, 0, n_pairs - 1) % n_tiles
    pair3 = jnp.clip(s - 2, 0, n_pairs - 1)
    t3 = pair3 % n_tiles
    tok3 = pair3 // n_tiles

    @pl.when(t3 == 0)
    def _():
        acc_scr[...] = jnp.zeros_like(acc_scr)

    def step(act_w, act_r, wa_w, wa_r):
        e_split = 1
        e_rows = te // e_split

        def stage1(c, eq):
            cols = slice(c * half, (c + 1) * half)
            u = pltpu.bitcast(u_ref[eq * e_rows // 2:(eq + 1) * e_rows // 2, :], BF16)
            act_w[eq * e_rows:(eq + 1) * e_rows, cols] = _dot(u, pltpu.bitcast(xnt_ref[:, cols], BF16))

        d_split = 8
        d_rows = acc_scr.shape[0] // d_split

        def stage3(c, dq):
            cols = slice(c * half, (c + 1) * half)
            rows = slice(dq * d_rows, (dq + 1) * d_rows)
            vt = pltpu.bitcast(vt_ref[dq * d_rows // 2:(dq + 1) * d_rows // 2, :], BF16)
            acc_scr[rows, cols] += _dot(vt, wa_r[:, cols])

        mxu_work = []
        for c in range(2):
            for eq in range(e_split):
                mxu_work.append(functools.partial(stage1, c, eq))
                mxu_work += [functools.partial(stage3, c, dq)
                             for dq in range(eq * d_split // e_split, (eq + 1) * d_split // e_split)]
        n_lt = tm // LANES
        n_valu = per * n_lt
        issued = 0
        for ii in range(per):
            r = t2 * per + ii
            cnt_rows = [jnp.where(live, cnt_ref[h, pl.ds(r, 1), :], 0.0) for h in range(PEER_HEADS)]
            c_rows = [c_ref[h, pl.ds(r, 1), :] for h in range(PEER_HEADS)]
            for lt in range(n_lt):
                while issued < len(mxu_work) and (issued * n_valu) // len(mxu_work) <= ii * n_lt + lt:
                    mxu_work[issued]()
                    issued += 1
                lanes = slice(lt * LANES, (lt + 1) * LANES)
                cnts = [pltpu.bitcast(jnp.broadcast_to(x[:, lanes], (SUBLANES, LANES)), BF16) for x in cnt_rows]
                ccs = [pltpu.bitcast(jnp.broadcast_to(x[:, lanes], (SUBLANES, LANES)), BF16) for x in c_rows]
                for sb in range(nkeys // pack):
                    words = slice(sb * SUBLANES, (sb + 1) * SUBLANES)
                    rows = slice(ii * nkeys + sb * pack, ii * nkeys + (sb + 1) * pack)
                    w = None
                    for h in range(PEER_HEADS):
                        e2 = pltpu.bitcast(e2_ref[h, words, lanes], BF16)
                        rank = pltpu.bitcast(rank_ref[h, words, lanes], BF16)
                        wh = jnp.where(rank < cnts[h], e2 * ccs[h], jnp.zeros_like(e2))
                        w = wh if w is None else w + wh
                    wa_w[rows, lanes] = w * _gelu_tanh_lowp(act_r[rows, lanes].astype(BF16))
        assert issued == len(mxu_work)

    @pl.when(s % 2 == 0)
    def _():
        step(act0_scr, act1_scr, wa1_scr, wa0_scr)

    @pl.when(s % 2 == 1)
    def _():
        step(act1_scr, act0_scr, wa0_scr, wa1_scr)

    @pl.when(jnp.logical_and(t3 == n_tiles - 1, s >= 2))
    def _():
        res = h_ref[...] + gate_ref[...] * acc_scr[...].T
        if split_tiles is None:
            out_refs[0][...] = res
        else:
            @pl.when(tok3 < split_tiles)
            def _():
                out_refs[0][...] = res

            @pl.when(tok3 >= split_tiles)
            def _():
                out_refs[1][...] = res


def _peer_dense(xnt, h, mod3, layer, u_packed, vt_packed, rank2, e2, cnt, c, n_ctx_tiles, tiles_per_latent, tm=512,
                te=PACK_ROWS, split_out=False):
    n, d = h.shape
    n_tiles = 2 * u_packed.shape[0] // te
    per = tm // TOKEN_TILE
    n_pairs = (n // tm) * n_tiles

    def tok(lag):
        return lambda t: jnp.clip(t - lag, 0, n_pairs - 1) // n_tiles

    def exp(lag):
        return lambda t: jnp.clip(t - lag, 0, n_pairs - 1) % n_tiles

    tok1, tok2, tok3, exp1, exp3 = tok(0), tok(1), tok(2), exp(0), exp(2)
    if split_out:
        split_tiles = n_ctx_tiles // per
        n_ctx = split_tiles * tm
        out_shape = [jax.ShapeDtypeStruct((n_ctx, d), F32), jax.ShapeDtypeStruct((n - n_ctx, d), F32)]
        out_specs = [pl.BlockSpec((tm, d), lambda t: (jnp.minimum(tok3(t), split_tiles - 1), 0)),
                     pl.BlockSpec((tm, d), lambda t: (jnp.maximum(tok3(t) - split_tiles, 0), 0))]
    else:
        split_tiles = None
        out_shape = jax.ShapeDtypeStruct((n, d), F32)
        out_specs = pl.BlockSpec((tm, d), lambda t: (tok3(t), 0))

    def gate_idx(t):
        return ((layer * 8 + _mod_row(tok3(t) * per, n_ctx_tiles, tiles_per_latent)) * N_MOD + 5, 0, 0)

    rspec = pl.BlockSpec((PEER_HEADS, PEER_NKEYS, tm), lambda t: (0, 0, tok2(t)))
    pspec = pl.BlockSpec((PEER_HEADS, PEER_NKEYS // 2, tm), lambda t: (0, 0, tok2(t)))
    return pl.pallas_call(
        functools.partial(_peer_dense_kernel, te=te, n_tiles=n_tiles, split_tiles=split_tiles),
        out_shape=out_shape,
        grid=(n_pairs + 2,),
        in_specs=[
            pl.BlockSpec((d // 2, tm), lambda t: (0, tok1(t))),
            pl.BlockSpec((tm, d), lambda t: (tok3(t), 0), pipeline_mode=pl.Buffered(1)),
            pl.BlockSpec((None, 1, d), gate_idx),
            pl.BlockSpec((te // 2, d), lambda t: (exp1(t), 0)),
            pl.BlockSpec((None, d // 2, te), lambda t: (exp3(t), 0, 0)),
            pspec, pspec, rspec, rspec,
        ],
        out_specs=out_specs,
        scratch_shapes=[pltpu.VMEM((te, tm), F32), pltpu.VMEM((te, tm), F32), pltpu.VMEM((te, tm), BF16),
                        pltpu.VMEM((te, tm), BF16), pltpu.VMEM((d, tm), F32)],
        compiler_params=_params("arbitrary"),
        name="peer_dense",
    )(xnt, h, mod3, u_packed, vt_packed, rank2, e2, cnt, c)


def kernel(x_prompt, x_sample, state_mlstm_C, state_mlstm_n, state_mlstm_m, cache_da_k, cache_da_v, c, c_ctx, norm_mix, norm_ffn, w_mod, b_mod, w_in_even, b_gate_even, mlstm_gain, pool_w, pool_scale, w_out_even, w_in_odd, qk_gain, da_lambda, da_subln, gm_ws, gm_b, w_out_odd, peer_wq, peer_subkeys, peer_u, peer_v):
    nb, s_len, d = x_prompt.shape
    nbd, t_len, _ = x_sample.shape
    depth = w_mod.shape[0]
    tm = TOKEN_TILE
    assert s_len == tm and t_len % tm == 0 and nbd <= 7 and t_len % GRID_W == 0
    n_ctx = nb * s_len
    n_lat = nbd * t_len
    n_ctx_tiles = n_ctx // tm
    tiles_per_latent = t_len // tm
    n_even = (depth + 1) // 2
    n_odd = depth // 2

    h = (x_prompt.reshape(n_ctx, d), x_sample.reshape(n_lat, d))
    cond8 = jnp.concatenate([c_ctx[None], c, jnp.zeros((7 - nbd, d), F32)], axis=0)
    mod3 = _adaln(cond8, w_mod, b_mod).reshape(depth * 8 * N_MOD, 1, d)
    rope = _rope_tables(t_len)

    new_c, new_n, new_m, new_k, new_v = [], [], [], [], []
    wm = MLSTM_HEADS * MLSTM_DH
    for l in range(depth):
        j = l // 2
        if l % 2 == 0:
            n_main = 5 * wm
            z, gates, gates_t = _inproj(h, norm_mix[l], mod3, l, 0, _pack_rows(w_in_even, j, ncols=n_main),
                                        n_ctx_tiles, tiles_per_latent, w_gates=w_in_even[j, :, n_main:],
                                        b_gates=b_gate_even[j],
                                        name="inproj_even")
            hs_ctx, cc, cn, cm = _mlstm(z, gates, gates_t, 0, nb, s_len, 0, emit_state=True)
            (hs_lat,) = _mlstm(z, gates, gates_t, n_ctx, nbd, t_len, j,
                               init=(state_mlstm_C, state_mlstm_n, state_mlstm_m))
            new_c.append(cc)
            new_n.append(cn)
            new_m.append(cm[:, :, :, 0, :])
            h = _even_out(h, mod3, l, hs_ctx, hs_lat, z, mlstm_gain[j], pool_w[j].astype(BF16), pool_scale[j],
                          _pack_rows(w_out_even, j), n_ctx_tiles, tiles_per_latent, s_len, t_len)
        else:
            lam_init = 0.8 - 0.6 * math.exp(-0.3 * l)
            (z,) = _inproj(h, norm_mix[l], mod3, l, 0, _pack_rows(w_in_odd, j), n_ctx_tiles, tiles_per_latent,
                           name="inproj_odd")
            gain2 = jnp.tile(qk_gain[j], (1, 2))
            cache_shape = (nb, 1, DA_HEADS, s_len, 2 * DA_DH)
            qn_c, kn_c, vb_c, nk, nv = _qk(z, gain2, 0, n_ctx, 0, cache_shape=cache_shape)
            qn_l, kn_l, vb_l = _qk(z, gain2, n_ctx, n_lat, 0, rope_tables=rope, seqlen=t_len)
            new_k.append(nk)
            new_v.append(nv)
            a_ctx = _attn(qn_c, kn_c, vb_c, da_lambda, da_subln, lam_init, nb, s_len, s_len, j)
            a_lat = _attn(qn_l, kn_l, vb_l, da_lambda, da_subln, lam_init, nbd, t_len, tm, j,
                          caches=(cache_da_k, cache_da_v))
            h = _odd_out(h, mod3, l, a_ctx, a_lat, z, gm_ws[j].astype(BF16), gm_b[j].T, _pack_rows(w_out_odd, j),
                         n_ctx_tiles, tiles_per_latent)
        qp, xnt = _inproj(h, norm_ffn[l], mod3, l, 3, _pack_rows(peer_wq, l), n_ctx_tiles, tiles_per_latent,
                          emit_xn=True, name="inproj_peer")
        sk = peer_subkeys[l].reshape(2 * PEER_HEADS, PEER_NKEYS, PEER_NKEYS).astype(BF16)
        rank2, e2, cnt, cw = _peer_route(qp, sk)
        h = _peer_dense(xnt, h, mod3, l, _pack_rows(peer_u, l), _pack_rows(peer_v, l, transpose=True), rank2,
                        e2, cnt, cw,
                        n_ctx_tiles, tiles_per_latent, split_out=(l == depth - 1))

    def join(parts):
        return parts[0] if len(parts) == 1 else jnp.concatenate(parts, axis=1)

    return (h[0].reshape(nb, s_len, d), h[1].reshape(nbd, t_len, d), join(new_c), join(new_n), join(new_m),
            join(new_k), join(new_v))
```

```python
import functools
import math

import jax
import jax.numpy as jnp
from jax import lax
from jax.experimental import pallas as pl
from jax.experimental.pallas import tpu as pltpu

F32 = jnp.float32
BF16 = jnp.bfloat16

N_MOD = 6
EPS = 1e-6
TOKEN_TILE = 256
LANES = 128
SUBLANES = 8
VMEM_LIMIT_BYTES = 56 * 1024 * 1024

MLSTM_HEADS = 4
MLSTM_DH = 256
MLSTM_CHUNK = 128
POOL_WINDOWS = (2, 4, 8, 16)
POOL_HALO = 8
DA_HEADS = 8
DA_DH = 64
GRID_W = 64
ROPE_BASE = 10000.0
GM_GROUPS = 8
GM_CHUNK = 128
PEER_HEADS = 8
PEER_NKEYS = 128
PEER_TOPK = 16
PEER_V_ROW_PIECES = 8
NEG_INF = float("-inf")


def _params(*sem):
    return pltpu.CompilerParams(dimension_semantics=sem, vmem_limit_bytes=VMEM_LIMIT_BYTES)


def _dot(a, b):
    return jnp.dot(a, b, preferred_element_type=F32)


def _dot_nt(a, b):
    return lax.dot_general(a, b, (((1,), (1,)), ((), ())), preferred_element_type=F32)


def _dot_tn(a, b):
    return lax.dot_general(a, b, (((0,), (0,)), ((), ())), preferred_element_type=F32)


def _split2(x):
    hi = x.astype(BF16)
    lo = (x - hi.astype(F32)).astype(BF16)
    return hi, lo


def _split3(x):
    hi = x.astype(BF16)
    r = x - hi.astype(F32)
    mid = r.astype(BF16)
    lo = (r - mid.astype(F32)).astype(BF16)
    return hi, mid, lo


def _dot3(a, b, dot=_dot):
    ah, al = _split2(a)
    bh, bl = _split2(b)
    return dot(ah, bh) + (dot(ah, bl) + dot(al, bh))


def _gelu_tanh(x):
    c = math.sqrt(2.0 / math.pi)
    half = 0.5 * x
    return half + half * jnp.tanh(x * (c + (0.044715 * c) * (x * x)))


def _gelu_tanh_lowp(x):
    c = -2.0 * math.sqrt(2.0 / math.pi)
    return x / (1.0 + jnp.exp(x * (c + (0.044715 * c) * (x * x))))


def _log_sigmoid(x):
    return -(jnp.maximum(-x, 0.0) + jnp.log1p(jnp.exp(-jnp.abs(x))))


def _mod_row(i, n_ctx_tiles, tiles_per_latent):
    return jnp.where(i < n_ctx_tiles, 0, 1 + (i - n_ctx_tiles) // tiles_per_latent)


def _pack_kernel(x_ref, o_ref, *, transpose):
    x = x_ref[...]
    if transpose:
        x = x.T
    o_ref[...] = pltpu.bitcast(x.astype(BF16), F32)


PACK_ROWS = 512


def _pack_rows(x, layer, ncols=None, transpose=False):
    _, r, c = x.shape
    c = c if ncols is None else ncols
    tr, tc = PACK_ROWS, 1024
    assert r % tr == 0 and c % tc == 0
    if transpose:
        out_shape, out_spec = (r // tr, c // 2, tr), pl.BlockSpec((None, tc // 2, tr), lambda i, j: (i, j, 0))
    else:
        out_shape, out_spec = (r // 2, c), pl.BlockSpec((tr // 2, tc), lambda i, j: (i, j))
    return pl.pallas_call(
        functools.partial(_pack_kernel, transpose=transpose),
        out_shape=jax.ShapeDtypeStruct(out_shape, F32),
        grid=(r // tr, c // tc),
        in_specs=[pl.BlockSpec((None, tr, tc), lambda i, j: (layer, i, j))],
        out_specs=out_spec,
        compiler_params=_params("arbitrary", "arbitrary"),
        name="pack_t" if transpose else "pack",
    )(x)


def _adaln_kernel(cond_ref, w_ref, b_ref, o_ref):
    c = cond_ref[...]
    s = c * jax.nn.sigmoid(c)
    o_ref[...] = _dot3(s, w_ref[...]) + b_ref[...]


def _adaln(cond8, w_mod, b_mod):
    depth, d, dout = w_mod.shape
    tn = 1024
    return pl.pallas_call(
        _adaln_kernel,
        out_shape=jax.ShapeDtypeStruct((depth, 8, dout), F32),
        grid=(depth, dout // tn),
        in_specs=[
            pl.BlockSpec((8, d), lambda l, j: (0, 0)),
            pl.BlockSpec((None, d, tn), lambda l, j: (l, 0, j)),
            pl.BlockSpec((None, 1, tn), lambda l, j: (l, 0, j)),
        ],
        out_specs=pl.BlockSpec((None, 8, tn), lambda l, j: (l, 0, j)),
        compiler_params=_params("arbitrary", "arbitrary"),
        name="adaln",
    )(cond8, w_mod, b_mod.reshape(depth, 1, dout))


def _stream(h, tm, cols, col_of=lambda j: 0, single_buffer_split=False):
    if not isinstance(h, tuple):
        return [h], [pl.BlockSpec((tm, cols), lambda i, j: (i, col_of(j)))], h.shape[0], None
    hc, hl = h
    nct = hc.shape[0] // tm
    kw = dict(pipeline_mode=pl.Buffered(1)) if single_buffer_split else {}
    specs = [pl.BlockSpec((tm, cols), lambda i, j: (jnp.minimum(i, nct - 1), col_of(j)), **kw),
             pl.BlockSpec((tm, cols), lambda i, j: (jnp.maximum(i - nct, 0), col_of(j)), **kw)]
    return [hc, hl], specs, hc.shape[0] + hl.shape[0], nct


def _load_stream(h_refs, split_tiles):
    if split_tiles is None:
        return h_refs[0][...]
    return jnp.where(pl.program_id(0) < split_tiles, h_refs[0][...], h_refs[1][...])


def _inproj_kernel(*refs, with_gates, emit_xn, split_tiles):
    nh = 1 if split_tiles is None else 2
    h_refs = refs[:nh]
    gain_ref, shift_ref, scale_ref, w_ref = refs[nh:nh + 4]
    rest = list(refs[nh + 4:])
    if with_gates:
        wg2_ref, bg_ref = rest[:2]
        rest = rest[2:]
    z_ref = rest.pop(0)
    if with_gates:
        g_ref, gt_ref = rest[:2]
        rest = rest[2:]
    if emit_xn:
        xn_out_ref = rest.pop(0)
    xn_scr = rest.pop(0)

    @pl.when(pl.program_id(1) == 0)
    def _():
        x = _load_stream(h_refs, split_tiles)
        xn = x * lax.rsqrt(jnp.mean(x * x, axis=-1, keepdims=True) + EPS) * gain_ref[...]
        xn = xn * (1.0 + scale_ref[...]) + shift_ref[...]
        xb = xn.astype(BF16)
        xn_scr[...] = xb
        if emit_xn:
            xn_out_ref[...] = pltpu.bitcast(xn.T.astype(BF16), F32)
        if with_gates:
            ng = bg_ref.shape[1]
            w2 = wg2_ref[...]
            w2_hi = w2.astype(BF16)
            lane = lax.broadcasted_iota(jnp.int32, w2.shape, 1)
            w_cat = jnp.where(lane < ng, w2_hi, (w2 - w2_hi.astype(F32)).astype(BF16))
            xl = (xn - xb.astype(F32)).astype(BF16)
            p = _dot(xb, w_cat)
            g = p[:, :ng] + (p[:, ng:] + _dot(xl, w2_hi)[:, :ng]) + bg_ref[...]
            g_ref[...] = g
            eye = (lax.broadcasted_iota(jnp.int32, (ng, ng), 0)
                   == lax.broadcasted_iota(jnp.int32, (ng, ng), 1)).astype(F32).astype(BF16)
            g3 = _split3(g)
            gt_ref[...] = _dot_nt(eye, g3[0]) + (_dot_nt(eye, g3[1]) + _dot_nt(eye, g3[2]))

    z_ref[...] = _dot(xn_scr[...], pltpu.bitcast(w_ref[...], BF16))


def _inproj(h, gain, mod3, layer, mod_base, w_packed, n_ctx_tiles, tiles_per_latent, w_gates=None, b_gates=None,
            emit_xn=False, tn=1024, name="inproj"):
    d, dout = 2 * w_packed.shape[0], w_packed.shape[1]
    tm = TOKEN_TILE * math.gcd(4, n_ctx_tiles, tiles_per_latent)
    per = tm // TOKEN_TILE
    with_gates = w_gates is not None
    h_arrays, h_specs, n, split_tiles = _stream(h, tm, d, single_buffer_split=True)

    def mod_idx(off):
        return lambda i, j: ((layer * 8 + _mod_row(i * per, n_ctx_tiles, tiles_per_latent)) * N_MOD + mod_base + off,
                             0, 0)

    in_specs = h_specs + [
        pl.BlockSpec((1, d), lambda i, j: (0, 0)),
        pl.BlockSpec((None, 1, d), mod_idx(0)),
        pl.BlockSpec((None, 1, d), mod_idx(1)),
        pl.BlockSpec((d // 2, tn), lambda i, j: (0, j)),
    ]
    args = h_arrays + [gain.reshape(1, d), mod3, mod3, w_packed]
    out_shape = [jax.ShapeDtypeStruct((n, dout), F32)]
    out_specs = [pl.BlockSpec((tm, tn), lambda i, j: (i, j))]
    if with_gates:
        ng = w_gates.shape[1]
        in_specs += [pl.BlockSpec((d, 2 * ng), lambda i, j: (0, 0)), pl.BlockSpec((1, ng), lambda i, j: (0, 0))]
        args += [jnp.concatenate([w_gates, w_gates], axis=1), b_gates.reshape(1, ng)]
        out_shape += [jax.ShapeDtypeStruct((n, ng), F32), jax.ShapeDtypeStruct((ng, n), F32)]
        out_specs += [pl.BlockSpec((tm, ng), lambda i, j: (i, 0)), pl.BlockSpec((ng, tm), lambda i, j: (0, i))]
    if emit_xn:
        out_shape.append(jax.ShapeDtypeStruct((d // 2, n), F32))
        out_specs.append(pl.BlockSpec((d // 2, tm), lambda i, j: (0, i)))
    return pl.pallas_call(
        functools.partial(_inproj_kernel, with_gates=with_gates, emit_xn=emit_xn, split_tiles=split_tiles),
        out_shape=out_shape,
        grid=(n // tm, dout // tn),
        in_specs=in_specs,
        out_specs=out_specs,
        scratch_shapes=[pltpu.VMEM((tm, d), BF16)],
        compiler_params=_params("arbitrary", "arbitrary"),
        name=name,
    )(*args)


def _mlstm_kernel(*refs, has_init, emit_state, nchunks):
    q_ref, k_ref, v_ref, g_ref, gt_ref = refs[:5]
    rest = list(refs[5:])
    if has_init:
        c0_ref, n0_ref, m0_ref = rest[:3]
        rest = rest[3:]
    hs_ref = rest.pop(0)
    if emit_state:
        cout_ref, nout_ref, mout_ref = rest[:3]
        rest = rest[3:]
    c_scr, n_scr, m_scr = rest
    nh, dh, L = MLSTM_HEADS, MLSTM_DH, MLSTM_CHUNK
    d = pl.program_id(0)
    s = pl.program_id(2)

    @pl.when(s == 0)
    def _():
        if has_init:
            c_scr[...] = c0_ref[...]
            n_scr[...] = n0_ref[...]
            m_scr[...] = m0_ref[...]
        else:
            c_scr[...] = jnp.zeros_like(c_scr)
            n_scr[...] = jnp.zeros_like(n_scr)
            m_scr[...] = jnp.zeros_like(m_scr)

    row = lax.broadcasted_iota(jnp.int32, (L, L), 0)
    col = lax.broadcasted_iota(jnp.int32, (L, L), 1)
    sgn = jnp.where(d == 0, 1, -1)
    mask = (row - col) * sgn >= 0
    maskb = jnp.where(mask, 1.0, 0.0).astype(BF16)

    g = g_ref[...]
    gt = gt_ref[...]
    fwd = d == 0
    i_col = jnp.where(fwd, g[:, 0:nh], g[:, nh:2 * nh])
    f_col = _log_sigmoid(jnp.where(fwd, g[:, 2 * nh:3 * nh], g[:, 3 * nh:4 * nh]))
    i_row = jnp.where(fwd, gt[0:nh], gt[nh:2 * nh])
    f_row = _log_sigmoid(jnp.where(fwd, gt[2 * nh:3 * nh], gt[3 * nh:4 * nh]))
    fc = _split3(f_col)
    b_col = _dot(maskb, fc[0]) + (_dot(maskb, fc[1]) + _dot(maskb, fc[2]))
    fr = _split3(f_row)
    b_row = _dot_nt(fr[0], maskb) + (_dot_nt(fr[1], maskb) + _dot_nt(fr[2], maskb))
    btot_col = jnp.sum(f_col, axis=0, keepdims=True)
    m_all = m_scr[...]

    m_new_parts = []
    for h in range(nh):
        sl = slice(h * dh, (h + 1) * dh)
        qh = q_ref[:, sl].astype(BF16)
        kf = k_ref[:, sl] * (dh ** -0.5)
        kh = kf.astype(BF16)
        vh = v_ref[:, sl].astype(BF16)
        b_c = b_col[:, h:h + 1]
        b_r = b_row[h:h + 1, :]
        i_c = i_col[:, h:h + 1]
        i_r = i_row[h:h + 1, :]
        m = m_all[:, h:h + 1]
        btot = btot_col[:, h:h + 1]

        dm = jnp.where(mask, b_c - b_r + i_r, NEG_INF)
        inter = b_c + m
        m_t = jnp.maximum(inter, jnp.max(dm, axis=1, keepdims=True))
        w = jnp.exp(dm - m_t)
        a = jnp.exp(inter - m_t)
        sc = _dot_nt(qh, kh) * w
        cb = c_scr[h].astype(BF16)
        num = a * _dot(qh, cb) + _dot(sc.astype(BF16), vh)
        nb = n_scr[h:h + 1, :].astype(BF16).astype(F32)
        qn = jnp.sum(qh.astype(F32) * nb, axis=1, keepdims=True)
        den = a * qn + jnp.sum(sc, axis=1, keepdims=True)
        hs_ref[:, sl] = num / jnp.maximum(jnp.abs(den), jnp.exp(-m_t))

        g_c = btot - b_c + i_c
        g_r = btot - b_r + i_r
        m_new = jnp.maximum(btot + m, jnp.max(g_c, axis=0, keepdims=True))
        decay = jnp.exp(btot + m - m_new)
        ws_c = jnp.exp(g_c - m_new)
        ws_r = jnp.exp(g_r - m_new)
        kw = (kf * ws_c).astype(BF16)
        c_scr[h] = decay * c_scr[h] + _dot_tn(kw, vh)
        n_scr[h:h + 1, :] = decay * n_scr[h:h + 1, :] + _dot(ws_r.astype(BF16), kh)
        m_new_parts.append(m_new)
    m_scr[...] = jnp.concatenate(m_new_parts, axis=1)

    if emit_state:
        @pl.when(s == nchunks - 1)
        def _():
            cout_ref[...] = c_scr[...]
            nout_ref[...] = n_scr[...]
            mout_ref[...] = m_scr[...]


def _mlstm(z, gates, gates_t, tok_off, nseq, seqlen, j, init=None, emit_state=False, n_even=1):
    nh, dh, L = MLSTM_HEADS, MLSTM_DH, MLSTM_CHUNK
    w = nh * dh
    nchunks = seqlen // L
    off = tok_off // L

    def chunk(d, b, s):
        return off + b * nchunks + jnp.where(d == 0, s, nchunks - 1 - s)

    in_specs = [
        pl.BlockSpec((L, w), lambda d, b, s: (chunk(d, b, s), 0)),
        pl.BlockSpec((L, w), lambda d, b, s: (chunk(d, b, s), 1)),
        pl.BlockSpec((L, w), lambda d, b, s: (chunk(d, b, s), 2)),
        pl.BlockSpec((L, 4 * nh), lambda d, b, s: (chunk(d, b, s), 0)),
        pl.BlockSpec((4 * nh, L), lambda d, b, s: (0, chunk(d, b, s))),
    ]
    args = [z, z, z, gates, gates_t]
    has_init = init is not None
    if has_init:
        c0, n0, m0 = init
        in_specs += [
            pl.BlockSpec((None, None, None, nh, dh, dh), lambda d, b, s: (b, j, d, 0, 0, 0)),
            pl.BlockSpec((None, None, None, nh, dh), lambda d, b, s: (b, j, d, 0, 0)),
            pl.BlockSpec((None, None, None, 1, nh), lambda d, b, s: (b, j, d, 0, 0)),
        ]
        args += [c0, n0, m0.reshape(m0.shape[:3] + (1, nh))]
    out_shape = [jax.ShapeDtypeStruct((2, nseq * seqlen, w), F32)]
    out_specs = [pl.BlockSpec((None, L, w), lambda d, b, s: (d, chunk(d, b, s) - off, 0))]
    if emit_state:
        out_shape += [
            jax.ShapeDtypeStruct((nseq, n_even, 2, nh, dh, dh), F32),
            jax.ShapeDtypeStruct((nseq, n_even, 2, nh, dh), F32),
            jax.ShapeDtypeStruct((nseq, n_even, 2, 1, nh), F32),
        ]
        out_specs += [
            pl.BlockSpec((None, None, None, nh, dh, dh), lambda d, b, s: (b, j, d, 0, 0, 0)),
            pl.BlockSpec((None, None, None, nh, dh), lambda d, b, s: (b, j, d, 0, 0)),
            pl.BlockSpec((None, None, None, 1, nh), lambda d, b, s: (b, j, d, 0, 0)),
        ]
    return pl.pallas_call(
        functools.partial(_mlstm_kernel, has_init=has_init, emit_state=emit_state, nchunks=nchunks),
        out_shape=out_shape,
        grid=(2, nseq, nchunks),
        in_specs=in_specs,
        out_specs=out_specs,
        scratch_shapes=[pltpu.VMEM((nh, dh, dh), F32), pltpu.VMEM((nh, dh), F32), pltpu.VMEM((1, nh), F32)],
        compiler_params=_params("arbitrary", "arbitrary", "arbitrary"),
        name="mlstm_ctx" if emit_state else "mlstm_lat",
    )(*args)


def _even_out_kernel(*refs, n_ctx_tiles, tiles_per_latent, ctx_len, lat_len, split_tiles):
    nhr = 1 if split_tiles is None else 2
    h_refs = refs[:nhr]
    (gate_ref, hsc_ref, hsl_ref, o_ref, p_ref, pprev_ref, pnext_ref, gain_ref, pw_ref, ps_ref, w_ref, out_ref,
     cat_scr) = refs[nhr:]
    i = pl.program_id(0)
    tm = TOKEN_TILE
    nh, dh = MLSTM_HEADS, MLSTM_DH
    wm = nh * dh

    @pl.when(pl.program_id(1) == 0)
    def _():
        is_ctx = i < n_ctx_tiles
        hs = jnp.where(is_ctx, hsc_ref[0] + hsc_ref[1], hsl_ref[0] + hsl_ref[1])
        for h in range(nh):
            sl = slice(h * dh, (h + 1) * dh)
            x = hs[:, sl]
            y = x * lax.rsqrt(jnp.mean(x * x, axis=-1, keepdims=True) + EPS) * gain_ref[:, sl]
            cat_scr[:, sl] = (y * jax.nn.sigmoid(o_ref[:, sl])).astype(BF16)

        tiles_ctx = ctx_len // tm
        pos = jnp.where(is_ctx, i % tiles_ctx, (i - n_ctx_tiles) % tiles_per_latent)
        ntile = jnp.where(is_ctx, tiles_ctx, tiles_per_latent)
        seqlen = jnp.where(is_ctx, ctx_len, lat_len)
        x = p_ref[...]
        prev = jnp.where(pos > 0, pprev_ref[...], 0.0)
        nxt = jnp.where(pos < ntile - 1, pnext_ref[...], 0.0)
        pad = jnp.zeros((LANES - 2 * POOL_HALO, x.shape[1]), F32)
        xcat = jnp.concatenate([prev, x, nxt, pad], axis=0)
        xh, xl = _split2(xcat)
        t = lax.broadcasted_iota(jnp.int32, (tm, tm + LANES), 0)
        sidx = lax.broadcasted_iota(jnp.int32, (tm, tm + LANES), 1) - POOL_HALO
        tpos = pos * tm + lax.broadcasted_iota(jnp.int32, (tm, 1), 0)
        gw = wm // len(POOL_WINDOWS)
        for gi, win in enumerate(POOL_WINDOWS):
            sl = slice(gi * gw, (gi + 1) * gw)
            band = jnp.where((sidx >= t - win // 2) & (sidx < t - win // 2 + win), 1.0, 0.0).astype(BF16)
            lo = jnp.maximum(tpos - win // 2, 0)
            hi = jnp.minimum(tpos - win // 2 + win, seqlen)
            cnt = (hi - lo).astype(F32)
            p = (_dot(band, xh[:, sl]) + _dot(band, xl[:, sl])) / cnt - x[:, sl]
            y = _dot(p.astype(BF16), pw_ref[gi]) * ps_ref[:, sl]
            cat_scr[:, wm + gi * gw:wm + (gi + 1) * gw] = y.astype(BF16)

    out_ref[...] = (_load_stream(h_refs, split_tiles)
                    + gate_ref[...] * _dot(cat_scr[...], pltpu.bitcast(w_ref[...], BF16)))


def _even_out(h, mod3, layer, hs_ctx, hs_lat, z, gain, pool_w_bf16, pool_scale, w_out_bf16, n_ctx_tiles,
              tiles_per_latent, ctx_len, lat_len, tn=2048):
    tm = TOKEN_TILE
    d = 2 * w_out_bf16.shape[0]
    h_arrays, h_specs, n, split_tiles = _stream(h, tm, tn, col_of=lambda j: j)
    wm = MLSTM_HEADS * MLSTM_DH
    nlt = n // tm - n_ctx_tiles
    rows8 = n // POOL_HALO
    per = tm // POOL_HALO

    def gate_idx(i, j):
        return ((layer * 8 + _mod_row(i, n_ctx_tiles, tiles_per_latent)) * N_MOD + 2, 0, j)

    return pl.pallas_call(
        functools.partial(_even_out_kernel, n_ctx_tiles=n_ctx_tiles, tiles_per_latent=tiles_per_latent,
                          ctx_len=ctx_len, lat_len=lat_len, split_tiles=split_tiles),
        out_shape=jax.ShapeDtypeStruct((n, d), F32),
        grid=(n // tm, d // tn),
        in_specs=h_specs + [
            pl.BlockSpec((None, 1, tn), gate_idx),
            pl.BlockSpec((2, tm, wm), lambda i, j: (0, jnp.minimum(i, n_ctx_tiles - 1), 0)),
            pl.BlockSpec((2, tm, wm), lambda i, j: (0, jnp.clip(i - n_ctx_tiles, 0, nlt - 1), 0)),
            pl.BlockSpec((tm, wm), lambda i, j: (i, 3)),
            pl.BlockSpec((tm, wm), lambda i, j: (i, 4)),
            pl.BlockSpec((POOL_HALO, wm), lambda i, j: (jnp.maximum(i * per - 1, 0), 4)),
            pl.BlockSpec((POOL_HALO, wm), lambda i, j: (jnp.minimum((i + 1) * per, rows8 - 1), 4)),
            pl.BlockSpec((1, wm), lambda i, j: (0, 0)),
            pl.BlockSpec(pool_w_bf16.shape, lambda i, j: (0, 0, 0)),
            pl.BlockSpec((1, wm), lambda i, j: (0, 0)),
            pl.BlockSpec((d // 2, tn), lambda i, j: (0, j)),
        ],
        out_specs=pl.BlockSpec((tm, tn), lambda i, j: (i, j)),
        scratch_shapes=[pltpu.VMEM((tm, d), BF16)],
        compiler_params=_params("arbitrary", "arbitrary"),
        name="even_out",
    )(*h_arrays, mod3, hs_ctx, hs_lat, z, z, z, z, gain.reshape(1, wm), pool_w_bf16, pool_scale.reshape(1, wm),
      w_out_bf16)


def _qk_kernel(*refs, rope, emit_cache):
    q_ref, k_ref, v_ref, gain_ref = refs[:4]
    rest = list(refs[4:])
    if rope:
        cos_ref, sin_ref = rest[:2]
        rest = rest[2:]
    qn_ref, kn_ref, vb_ref = rest[:3]
    rest = rest[3:]
    hd = 2 * DA_DH
    lane = lax.broadcasted_iota(jnp.int32, (q_ref.shape[0], hd), 1)
    first = lane < DA_DH

    def norm(x, gain):
        sq = x * x
        s1 = jnp.sum(jnp.where(first, sq, 0.0), axis=-1, keepdims=True)
        s2 = jnp.sum(jnp.where(first, 0.0, sq), axis=-1, keepdims=True)
        ms = jnp.where(first, s1, s2) * (1.0 / DA_DH)
        y = x * lax.rsqrt(ms + EPS) * gain
        if rope:
            quarter = DA_DH // 4
            partner = jnp.where((lane & quarter) == 0, pltpu.roll(y, LANES - quarter, 1), pltpu.roll(y, quarter, 1))
            y = y * cos_ref[...] + partner * sin_ref[...]
        return y

    vb_ref[...] = v_ref[...].astype(BF16)
    for h in range(DA_HEADS):
        cols = slice(h * hd, (h + 1) * hd)
        qn = norm(q_ref[:, cols], gain_ref[0:1, :])
        kn = norm(k_ref[:, cols], gain_ref[1:2, :])
        qn_ref[:, cols] = (qn * (DA_DH ** -0.5)).astype(BF16)
        kn_ref[:, cols] = kn.astype(BF16)
        if emit_cache:
            newk_ref, newv_ref = rest
            newk_ref[h] = kn
            newv_ref[h] = v_ref[:, cols]


def _qk(z, qk_gain2, tok_off, ntok, j, rope_tables=None, cache_shape=None, seqlen=None):
    tm = TOKEN_TILE
    hd = 2 * DA_DH
    w = DA_HEADS * hd
    off = tok_off // tm
    rope = rope_tables is not None
    emit_cache = cache_shape is not None
    in_specs = [
        pl.BlockSpec((tm, w), lambda i: (off + i, 0)),
        pl.BlockSpec((tm, w), lambda i: (off + i, 1)),
        pl.BlockSpec((tm, w), lambda i: (off + i, 2)),
        pl.BlockSpec((2, hd), lambda i: (0, 0)),
    ]
    args = [z, z, z, qk_gain2]
    if rope:
        tps = seqlen // tm
        in_specs += [pl.BlockSpec((tm, hd), lambda i: (i % tps, 0))] * 2
        args += list(rope_tables)
    out_shape = [jax.ShapeDtypeStruct((ntok, w), BF16)] * 3
    out_specs = [pl.BlockSpec((tm, w), lambda i: (i, 0))] * 3
    if emit_cache:
        out_shape += [jax.ShapeDtypeStruct(cache_shape, F32)] * 2
        out_specs += [pl.BlockSpec((None, None, DA_HEADS, tm, hd), lambda i: (i, j, 0, 0, 0))] * 2
    return pl.pallas_call(
        functools.partial(_qk_kernel, rope=rope, emit_cache=emit_cache),
        out_shape=out_shape,
        grid=(ntok // tm,),
        in_specs=in_specs,
        out_specs=out_specs,
        compiler_params=_params("arbitrary"),
        name="qk_ctx" if emit_cache else "qk_lat",
    )(*args)


def _rope_tables(t):
    rows = t // GRID_W
    row = jnp.repeat(jnp.arange(rows), GRID_W).astype(F32)
    col = (jnp.arange(rows * GRID_W) % GRID_W).astype(F32)
    n_freq = DA_DH // 4
    inv = ROPE_BASE ** (-jnp.arange(n_freq, dtype=F32) / n_freq)
    ar, ac = row[:, None] * inv, col[:, None] * inv
    cos = jnp.concatenate([jnp.cos(ar), jnp.cos(ar), jnp.cos(ac), jnp.cos(ac)], axis=-1)
    sin = jnp.concatenate([-jnp.sin(ar), jnp.sin(ar), -jnp.sin(ac), jnp.sin(ac)], axis=-1)
    return jnp.tile(cos, (1, 2)), jnp.tile(sin, (1, 2))


def _attn_kernel(*refs, n_pieces, lam_init):
    q_ref, lp_ref, subln_ref = refs[:3]
    kv = refs[3:3 + 2 * n_pieces]
    o_ref = refs[3 + 2 * n_pieces]
    lp = lp_ref[...]
    lam = (jnp.exp(jnp.sum(lp[0:1] * lp[1:2], axis=-1, keepdims=True))
           - jnp.exp(jnp.sum(lp[2:3] * lp[3:4], axis=-1, keepdims=True)) + lam_init)
    q = q_ref[...]
    lane = lax.broadcasted_iota(jnp.int32, q.shape, 1)
    zero = jnp.zeros_like(q)
    qa = jnp.where(lane < DA_DH, q, zero)
    qb = jnp.where(lane < DA_DH, zero, q)
    ks = [kv[2 * p][...].astype(BF16) for p in range(n_pieces)]
    vs = [kv[2 * p + 1][...].astype(BF16) for p in range(n_pieces)]

    sa = [_dot_nt(qa, k) for k in ks]
    sb = [_dot_nt(qb, k) for k in ks]

    def attend(parts):
        m = functools.reduce(jnp.maximum, [jnp.max(x, axis=-1, keepdims=True) for x in parts])
        es = [jnp.exp(x - m) for x in parts]
        tot = functools.reduce(lambda a, b: a + b, [jnp.sum(e, axis=-1, keepdims=True) for e in es])
        pv = functools.reduce(lambda a, b: a + b, [_dot(e.astype(BF16), v) for e, v in zip(es, vs)])
        return pv / tot

    o = attend(sa) - lam * attend(sb)
    o = o * lax.rsqrt(jnp.mean(o * o, axis=-1, keepdims=True) + EPS) * subln_ref[...]
    o_ref[...] = (o * (1.0 - lam_init)).astype(BF16)


def _attn(qn, kn, vb, lam_params, subln, lam_init, nseq, seqlen, tq, j, caches=None):
    hd = 2 * DA_DH
    nq = seqlen // tq
    in_specs = [
        pl.BlockSpec((tq, hd), lambda b, h, qi: (b * nq + qi, h)),
        pl.BlockSpec((None, 4, DA_DH), lambda b, h, qi: (j, 0, 0)),
        pl.BlockSpec((None, 1, hd), lambda b, h, qi: (j, 0, 0)),
        pl.BlockSpec((seqlen, hd), lambda b, h, qi: (b, h)),
        pl.BlockSpec((seqlen, hd), lambda b, h, qi: (b, h)),
    ]
    args = [qn, lam_params, subln.reshape(subln.shape[0], 1, hd), kn, vb]
    n_pieces = 1
    if caches is not None:
        ck, cv = caches
        past = ck.shape[3]
        in_specs += [pl.BlockSpec((None, None, None, past, hd), lambda b, h, qi: (b, j, h, 0, 0))] * 2
        args += [ck, cv]
        n_pieces = 2
    return pl.pallas_call(
        functools.partial(_attn_kernel, n_pieces=n_pieces, lam_init=lam_init),
        out_shape=jax.ShapeDtypeStruct(qn.shape, BF16),
        grid=(nseq, DA_HEADS, nq),
        in_specs=in_specs,
        out_specs=pl.BlockSpec((tq, hd), lambda b, h, qi: (b * nq + qi, h)),
        compiler_params=_params("arbitrary", "arbitrary", "arbitrary"),
        name="attn_lat" if caches is not None else "attn_ctx",
    )(*args)


def _odd_out_kernel(h_ref, gate_ref, ac_ref, al_ref, gu_ref, gv_ref, ws_ref, bt_ref, w_ref, out_ref, cat_scr, *,
                    n_ctx_tiles):
    i = pl.program_id(0)
    tm = TOKEN_TILE
    wa = DA_HEADS * 2 * DA_DH
    gw = LANES

    @pl.when(pl.program_id(1) == 0)
    def _():
        cat_scr[:, 0:wa] = jnp.where(i < n_ctx_tiles, ac_ref[...], al_ref[...])
        for c in range(tm // GM_CHUNK):
            rows = slice(c * GM_CHUNK, (c + 1) * GM_CHUNK)
            for g in range(GM_GROUPS):
                cols = slice(g * gw, (g + 1) * gw)
                u = _gelu_tanh(gu_ref[rows, cols])
                v = _gelu_tanh(gv_ref[rows, cols])
                vn = v * lax.rsqrt(jnp.mean(v * v, axis=-1, keepdims=True) + EPS)
                mixed = _dot(ws_ref[g], vn.astype(BF16)) + bt_ref[:, g:g + 1]
                cat_scr[rows, wa + g * gw:wa + (g + 1) * gw] = (u * mixed).astype(BF16)

    out_ref[...] = h_ref[...] + gate_ref[...] * _dot(cat_scr[...], pltpu.bitcast(w_ref[...], BF16))


def _odd_out(h, mod3, layer, a_ctx, a_lat, z, gm_ws_bf16, gm_b_t, w_out_bf16, n_ctx_tiles, tiles_per_latent, tn=2048):
    n, d = h.shape
    tm = TOKEN_TILE
    wa = DA_HEADS * 2 * DA_DH
    wg = GM_GROUPS * LANES
    nlt = n // tm - n_ctx_tiles

    def gate_idx(i, j):
        return ((layer * 8 + _mod_row(i, n_ctx_tiles, tiles_per_latent)) * N_MOD + 2, 0, j)

    return pl.pallas_call(
        functools.partial(_odd_out_kernel, n_ctx_tiles=n_ctx_tiles),
        out_shape=jax.ShapeDtypeStruct((n, d), F32),
        grid=(n // tm, d // tn),
        in_specs=[
            pl.BlockSpec((tm, tn), lambda i, j: (i, j)),
            pl.BlockSpec((None, 1, tn), gate_idx),
            pl.BlockSpec((tm, wa), lambda i, j: (jnp.minimum(i, n_ctx_tiles - 1), 0)),
            pl.BlockSpec((tm, wa), lambda i, j: (jnp.clip(i - n_ctx_tiles, 0, nlt - 1), 0)),
            pl.BlockSpec((tm, wg), lambda i, j: (i, 3)),
            pl.BlockSpec((tm, wg), lambda i, j: (i, 4)),
            pl.BlockSpec(gm_ws_bf16.shape, lambda i, j: (0, 0, 0)),
            pl.BlockSpec(gm_b_t.shape, lambda i, j: (0, 0)),
            pl.BlockSpec((d // 2, tn), lambda i, j: (0, j)),
        ],
        out_specs=pl.BlockSpec((tm, tn), lambda i, j: (i, j)),
        scratch_shapes=[pltpu.VMEM((tm, d), BF16)],
        compiler_params=_params("arbitrary", "arbitrary"),
        name="odd_out",
    )(h, mod3, a_ctx, a_lat, z, z, gm_ws_bf16, gm_b_t, w_out_bf16)


def _top_values(x, k):
    vals = []
    cur = x
    for r in range(k):
        m = jnp.max(cur, axis=0, keepdims=True)
        vals.append(m)
        if r + 1 < k:
            cur = jnp.where(cur == m, NEG_INF, cur)
    return jnp.concatenate(vals, axis=0)


def _oddeven_merge(lo, hi, r):
    step = r * 2
    if step < hi - lo:
        yield from _oddeven_merge(lo, hi, step)
        yield from _oddeven_merge(lo + r, hi, step)
        yield from [(i, i + r) for i in range(lo + r, hi - r, step)]
    else:
        yield (lo, lo + r)


def _oddeven_merge_sort(lo, hi):
    if hi - lo >= 1:
        mid = lo + (hi - lo) // 2
        yield from _oddeven_merge_sort(lo, mid)
        yield from _oddeven_merge_sort(mid + 1, hi)
        yield from _oddeven_merge(lo, hi, 1)


def _top_sorted(x):
    n = len(x)
    assert n == PEER_TOPK

    def exchange(v, i, j):
        v[i], v[j] = jnp.maximum(v[i], v[j]), jnp.minimum(v[i], v[j])

    v = list(x)
    for i, j in _oddeven_merge_sort(0, n - 1):
        exchange(v, i, j)
    shift = SUBLANES // 2
    while shift:
        w = [pltpu.roll(t, SUBLANES - shift, 0) for t in v]
        v = [jnp.maximum(v[i], w[n - 1 - i]) for i in range(n)]
        d = n // 2
        while d:
            for i in range(n):
                if not i & d:
                    exchange(v, i, i + d)
            d //= 2
        shift //= 2
    return jnp.concatenate([t[0:1, :] for t in v], axis=0)


def _dup_bf16(x):
    b = pltpu.bitcast(x.astype(BF16).astype(F32), jnp.uint32)
    return pltpu.bitcast(b | (b >> 16), F32)


def _peer_route_kernel(q_ref, sk_ref, rank_ref, e2_ref, cnt_ref, c_ref, s_scr):
    half = PEER_NKEYS
    k = PEER_TOPK
    n_slabs = half // SUBLANES
    for h in range(PEER_HEADS):
        q1 = q_ref[:, 2 * h * half:(2 * h + 1) * half].astype(BF16)
        q2 = q_ref[:, (2 * h + 1) * half:(2 * h + 2) * half].astype(BF16)
        s_scr[0] = _dot_nt(sk_ref[2 * h], q1)
        s_scr[1] = _dot_nt(sk_ref[2 * h + 1], q2)

        def lane_tile(t, carry):
            lanes = pl.ds(pl.multiple_of(t * LANES, LANES), LANES)
            s1 = s_scr[0, :, lanes]
            s2 = s_scr[1, :, lanes]
            a_all = _top_sorted([s1[SUBLANES * i:SUBLANES * (i + 1)] for i in range(n_slabs)])
            b_all = _top_sorted([s2[SUBLANES * i:SUBLANES * (i + 1)] for i in range(n_slabs)])
            b = [b_all[r:r + 1] for r in range(k)]
            rows = [a_all[r:r + 1] + b_all[0:k // (r + 1)] for r in range(k)]
            nrows = sum(k // (r + 1) for r in range(k))
            rows.append(jnp.full((-nrows % SUBLANES, LANES), NEG_INF, F32))
            top = _top_values(jnp.concatenate(rows, axis=0), k)
            tau = top[k - 1:k]
            z = jnp.sum(jnp.exp(top - top[0:1]), axis=0, keepdims=True)
            cnt = jnp.zeros(s1.shape, F32)
            rank2 = jnp.zeros(s2.shape, F32)
            for r in range(k):
                cnt = jnp.where(s1 + b[r] >= tau, float(r + 1), cnt)
                rank2 = jnp.where(b[r] > s2, float(r + 1), rank2)
            rank_ref[h, :, lanes] = pltpu.bitcast(rank2.astype(BF16), F32)
            e2_ref[h, :, lanes] = pltpu.bitcast(jnp.exp(s2 - b[0]).astype(BF16), F32)
            cnt_ref[h, :, lanes] = _dup_bf16(cnt)
            c_ref[h, :, lanes] = _dup_bf16(jnp.exp(s1 - a_all[0:1]) / z)
            return carry

        lax.fori_loop(0, q_ref.shape[0] // LANES, lane_tile, 0)


def _peer_route(qp, subkeys_bf16, tm=512):
    n = qp.shape[0]
    def out(rows):
        return (jax.ShapeDtypeStruct((PEER_HEADS, rows, n), F32),
                pl.BlockSpec((PEER_HEADS, rows, tm), lambda i: (0, 0, i)))

    outs = [out(PEER_NKEYS // 2)] * 2 + [out(PEER_NKEYS)] * 2
    return pl.pallas_call(
        _peer_route_kernel,
        out_shape=[o[0] for o in outs],
        grid=(n // tm,),
        in_specs=[
            pl.BlockSpec((tm, qp.shape[1]), lambda i: (i, 0)),
            pl.BlockSpec(subkeys_bf16.shape, lambda i: (0, 0, 0)),
        ],
        out_specs=[o[1] for o in outs],
        scratch_shapes=[pltpu.VMEM((2, PEER_NKEYS, tm), F32)],
        compiler_params=_params("arbitrary"),
        name="peer_route",
    )(qp, subkeys_bf16)


def _peer_dense_kernel(xnt_ref, h_ref, gate_ref, u_ref, vt_ref, rank_ref, e2_ref, cnt_ref, c_ref, *rest, te, n_tiles,
                       split_tiles):
    n_out = 1 if split_tiles is None else 2
    out_refs = rest[:n_out]
    act0_scr, act1_scr, wa0_scr, wa1_scr, acc_scr = rest[n_out:]
    s = pl.program_id(0)
    n_pairs = pl.num_programs(0) - 2
    nkeys = PEER_NKEYS
    per = te // nkeys
    tm = acc_scr.shape[1]
    half = tm // 2
    assert per % 4 == 0
    pack = 2 * SUBLANES

    @pl.when(s == 0)
    def _():
        act1_scr[...] = jnp.zeros_like(act1_scr)
        wa0_scr[...] = jnp.zeros_like(wa0_scr)

    live = jnp.logical_and(s >= 1, s <= n_pairs)
    t2 = jnp.clip(s - 1, 0, n_pairs - 1) % n_tiles
    pair3 = jnp.clip(s - 2, 0, n_pairs - 1)
    t3 = pair3 % n_tiles
    tok3 = pair3 // n_tiles

    @pl.when(t3 == 0)
    def _():
        acc_scr[...] = jnp.zeros_like(acc_scr)

    def step(act_w, act_r, wa_w, wa_r):
        def stage1(c):
            cols = slice(c * half, (c + 1) * half)
            u = pltpu.bitcast(u_ref[0:te // 2, :], BF16)
            act_w[0:te, cols] = _dot(u, pltpu.bitcast(xnt_ref[:, cols], BF16))

        d_rows = acc_scr.shape[0] // PEER_V_ROW_PIECES

        def stage3(c, dq):
            cols = slice(c * half, (c + 1) * half)
            rows = slice(dq * d_rows, (dq + 1) * d_rows)
            vt = pltpu.bitcast(vt_ref[dq * d_rows // 2:(dq + 1) * d_rows // 2, :], BF16)
            acc_scr[rows, cols] += _dot(vt, wa_r[:, cols])

        mxu_work = []
        for c in range(2):
            mxu_work.append(functools.partial(stage1, c))
            mxu_work += [functools.partial(stage3, c, dq) for dq in range(PEER_V_ROW_PIECES)]
        n_lt = tm // LANES
        n_valu = per * n_lt
        issued = 0
        for ii in range(per):
            r = t2 * per + ii
            cnt_rows = [jnp.where(live, cnt_ref[h, pl.ds(r, 1), :], 0.0) for h in range(PEER_HEADS)]
            c_rows = [c_ref[h, pl.ds(r, 1), :] for h in range(PEER_HEADS)]
            for lt in range(n_lt):
                while issued < len(mxu_work) and (issued * n_valu) // len(mxu_work) <= ii * n_lt + lt:
                    mxu_work[issued]()
                    issued += 1
                lanes = slice(lt * LANES, (lt + 1) * LANES)
                cnts = [pltpu.bitcast(jnp.broadcast_to(x[:, lanes], (SUBLANES, LANES)), BF16) for x in cnt_rows]
                ccs = [pltpu.bitcast(jnp.broadcast_to(x[:, lanes], (SUBLANES, LANES)), BF16) for x in c_rows]
                for sb in range(nkeys // pack):
                    words = slice(sb * SUBLANES, (sb + 1) * SUBLANES)
                    rows = slice(ii * nkeys + sb * pack, ii * nkeys + (sb + 1) * pack)
                    w = None
                    for h in range(PEER_HEADS):
                        e2 = pltpu.bitcast(e2_ref[h, words, lanes], BF16)
                        rank = pltpu.bitcast(rank_ref[h, words, lanes], BF16)
                        wh = jnp.where(rank < cnts[h], e2 * ccs[h], jnp.zeros_like(e2))
                        w = wh if w is None else w + wh
                    wa_w[rows, lanes] = w * _gelu_tanh_lowp(act_r[rows, lanes].astype(BF16))
        assert issued == len(mxu_work)

    @pl.when(s % 2 == 0)
    def _():
        step(act0_scr, act1_scr, wa1_scr, wa0_scr)

    @pl.when(s % 2 == 1)
    def _():
        step(act1_scr, act0_scr, wa0_scr, wa1_scr)

    @pl.when(jnp.logical_and(t3 == n_tiles - 1, s >= 2))
    def _():
        res = h_ref[...] + gate_ref[...] * acc_scr[...].T
        if split_tiles is None:
            out_refs[0][...] = res
        else:
            @pl.when(tok3 < split_tiles)
            def _():
                out_refs[0][...] = res

            @pl.when(tok3 >= split_tiles)
            def _():
                out_refs[1][...] = res


def _peer_dense(xnt, h, mod3, layer, u_packed, vt_packed, rank2, e2, cnt, c, n_ctx_tiles, tiles_per_latent, tm=512,
                te=PACK_ROWS, split_out=False):
    n, d = h.shape
    n_tiles = 2 * u_packed.shape[0] // te
    per = tm // TOKEN_TILE
    n_pairs = (n // tm) * n_tiles

    def tok(lag):
        return lambda t: jnp.clip(t - lag, 0, n_pairs - 1) // n_tiles

    def exp(lag):
        return lambda t: jnp.clip(t - lag, 0, n_pairs - 1) % n_tiles

    tok1, tok2, tok3, exp1, exp3 = tok(0), tok(1), tok(2), exp(0), exp(2)
    if split_out:
        split_tiles = n_ctx_tiles // per
        n_ctx = split_tiles * tm
        out_shape = [jax.ShapeDtypeStruct((n_ctx, d), F32), jax.ShapeDtypeStruct((n - n_ctx, d), F32)]
        out_specs = [pl.BlockSpec((tm, d), lambda t: (jnp.minimum(tok3(t), split_tiles - 1), 0)),
                     pl.BlockSpec((tm, d), lambda t: (jnp.maximum(tok3(t) - split_tiles, 0), 0))]
    else:
        split_tiles = None
        out_shape = jax.ShapeDtypeStruct((n, d), F32)
        out_specs = pl.BlockSpec((tm, d), lambda t: (tok3(t), 0))

    def gate_idx(t):
        return ((layer * 8 + _mod_row(tok3(t) * per, n_ctx_tiles, tiles_per_latent)) * N_MOD + 5, 0, 0)

    rspec = pl.BlockSpec((PEER_HEADS, PEER_NKEYS, tm), lambda t: (0, 0, tok2(t)))
    pspec = pl.BlockSpec((PEER_HEADS, PEER_NKEYS // 2, tm), lambda t: (0, 0, tok2(t)))
    return pl.pallas_call(
        functools.partial(_peer_dense_kernel, te=te, n_tiles=n_tiles, split_tiles=split_tiles),
        out_shape=out_shape,
        grid=(n_pairs + 2,),
        in_specs=[
            pl.BlockSpec((d // 2, tm), lambda t: (0, tok1(t))),
            pl.BlockSpec((tm, d), lambda t: (tok3(t), 0), pipeline_mode=pl.Buffered(1)),
            pl.BlockSpec((None, 1, d), gate_idx),
            pl.BlockSpec((te // 2, d), lambda t: (exp1(t), 0)),
            pl.BlockSpec((None, d // 2, te), lambda t: (exp3(t), 0, 0)),
            pspec, pspec, rspec, rspec,
        ],
        out_specs=out_specs,
        scratch_shapes=[pltpu.VMEM((te, tm), F32), pltpu.VMEM((te, tm), F32), pltpu.VMEM((te, tm), BF16),
                        pltpu.VMEM((te, tm), BF16), pltpu.VMEM((d, tm), F32)],
        compiler_params=_params("arbitrary"),
        name="peer_dense",
    )(xnt, h, mod3, u_packed, vt_packed, rank2, e2, cnt, c)


def kernel(x_prompt, x_sample, state_mlstm_C, state_mlstm_n, state_mlstm_m, cache_da_k, cache_da_v, c, c_ctx, norm_mix, norm_ffn, w_mod, b_mod, w_in_even, b_gate_even, mlstm_gain, pool_w, pool_scale, w_out_even, w_in_odd, qk_gain, da_lambda, da_subln, gm_ws, gm_b, w_out_odd, peer_wq, peer_subkeys, peer_u, peer_v):
    nb, s_len, d = x_prompt.shape
    nbd, t_len, _ = x_sample.shape
    depth = w_mod.shape[0]
    tm = TOKEN_TILE
    assert s_len == tm and t_len % tm == 0 and nbd <= 7 and t_len % GRID_W == 0
    n_ctx = nb * s_len
    n_lat = nbd * t_len
    n_ctx_tiles = n_ctx // tm
    tiles_per_latent = t_len // tm
    n_even = (depth + 1) // 2
    n_odd = depth // 2

    h = (x_prompt.reshape(n_ctx, d), x_sample.reshape(n_lat, d))
    cond8 = jnp.concatenate([c_ctx[None], c, jnp.zeros((7 - nbd, d), F32)], axis=0)
    mod3 = _adaln(cond8, w_mod, b_mod).reshape(depth * 8 * N_MOD, 1, d)
    rope = _rope_tables(t_len)

    new_c, new_n, new_m, new_k, new_v = [], [], [], [], []
    wm = MLSTM_HEADS * MLSTM_DH
    for l in range(depth):
        j = l // 2
        if l % 2 == 0:
            n_main = 5 * wm
            z, gates, gates_t = _inproj(h, norm_mix[l], mod3, l, 0, _pack_rows(w_in_even, j, ncols=n_main),
                                        n_ctx_tiles, tiles_per_latent, w_gates=w_in_even[j, :, n_main:],
                                        b_gates=b_gate_even[j],
                                        name="inproj_even")
            hs_ctx, cc, cn, cm = _mlstm(z, gates, gates_t, 0, nb, s_len, 0, emit_state=True)
            (hs_lat,) = _mlstm(z, gates, gates_t, n_ctx, nbd, t_len, j,
                               init=(state_mlstm_C, state_mlstm_n, state_mlstm_m))
            new_c.append(cc)
            new_n.append(cn)
            new_m.append(cm[:, :, :, 0, :])
            h = _even_out(h, mod3, l, hs_ctx, hs_lat, z, mlstm_gain[j], pool_w[j].astype(BF16), pool_scale[j],
                          _pack_rows(w_out_even, j), n_ctx_tiles, tiles_per_latent, s_len, t_len)
        else:
            lam_init = 0.8 - 0.6 * math.exp(-0.3 * l)
            (z,) = _inproj(h, norm_mix[l], mod3, l, 0, _pack_rows(w_in_odd, j), n_ctx_tiles, tiles_per_latent,
                           name="inproj_odd")
            gain2 = jnp.tile(qk_gain[j], (1, 2))
            cache_shape = (nb, 1, DA_HEADS, s_len, 2 * DA_DH)
            qn_c, kn_c, vb_c, nk, nv = _qk(z, gain2, 0, n_ctx, 0, cache_shape=cache_shape)
            qn_l, kn_l, vb_l = _qk(z, gain2, n_ctx, n_lat, 0, rope_tables=rope, seqlen=t_len)
            new_k.append(nk)
            new_v.append(nv)
            a_ctx = _attn(qn_c, kn_c, vb_c, da_lambda, da_subln, lam_init, nb, s_len, s_len, j)
            a_lat = _attn(qn_l, kn_l, vb_l, da_lambda, da_subln, lam_init, nbd, t_len, tm, j,
                          caches=(cache_da_k, cache_da_v))
            h = _odd_out(h, mod3, l, a_ctx, a_lat, z, gm_ws[j].astype(BF16), gm_b[j].T, _pack_rows(w_out_odd, j),
                         n_ctx_tiles, tiles_per_latent)
        qp, xnt = _inproj(h, norm_ffn[l], mod3, l, 3, _pack_rows(peer_wq, l), n_ctx_tiles, tiles_per_latent,
                          emit_xn=True, name="inproj_peer")
        sk = peer_subkeys[l].reshape(2 * PEER_HEADS, PEER_NKEYS, PEER_NKEYS).astype(BF16)
        rank2, e2, cnt, cw = _peer_route(qp, sk)
        h = _peer_dense(xnt, h, mod3, l, _pack_rows(peer_u, l), _pack_rows(peer_v, l, transpose=True), rank2,
                        e2, cnt, cw,
                        n_ctx_tiles, tiles_per_latent, split_out=(l == depth - 1))

    def join(parts):
        return parts[0] if len(parts) == 1 else jnp.concatenate(parts, axis=1)

    return (h[0].reshape(nb, s_len, d), h[1].reshape(nbd, t_len, d), join(new_c), join(new_n), join(new_m),
            join(new_k), join(new_v))
```

```python
import functools
import math

import jax
import jax.numpy as jnp
from jax import lax
from jax.experimental import pallas as pl
from jax.experimental.pallas import tpu as pltpu

F32 = jnp.float32
BF16 = jnp.bfloat16

N_MOD = 6
EPS = 1e-6
TOKEN_TILE = 256
LANES = 128
SUBLANES = 8
VMEM_LIMIT_BYTES = 56 * 1024 * 1024

MLSTM_HEADS = 4
MLSTM_DH = 256
MLSTM_CHUNK = 128
POOL_WINDOWS = (2, 4, 8, 16)
POOL_HALO = 8
DA_HEADS = 8
DA_DH = 64
GRID_W = 64
ROPE_BASE = 10000.0
GM_GROUPS = 8
GM_CHUNK = 128
PEER_HEADS = 8
PEER_NKEYS = 128
PEER_TOPK = 16
PEER_V_ROW_PIECES = 8
NEG_INF = float("-inf")


def _params(*sem):
    return pltpu.CompilerParams(dimension_semantics=sem, vmem_limit_bytes=VMEM_LIMIT_BYTES)


def _dot(a, b):
    return jnp.dot(a, b, preferred_element_type=F32)


def _dot_nt(a, b):
    return lax.dot_general(a, b, (((1,), (1,)), ((), ())), preferred_element_type=F32)


def _dot_tn(a, b):
    return lax.dot_general(a, b, (((0,), (0,)), ((), ())), preferred_element_type=F32)


def _split2(x):
    hi = x.astype(BF16)
    lo = (x - hi.astype(F32)).astype(BF16)
    return hi, lo


def _split3(x):
    hi = x.astype(BF16)
    r = x - hi.astype(F32)
    mid = r.astype(BF16)
    lo = (r - mid.astype(F32)).astype(BF16)
    return hi, mid, lo


def _dot3(a, b, dot=_dot):
    ah, al = _split2(a)
    bh, bl = _split2(b)
    return dot(ah, bh) + (dot(ah, bl) + dot(al, bh))


def _gelu_tanh(x):
    c = math.sqrt(2.0 / math.pi)
    half = 0.5 * x
    return half + half * jnp.tanh(x * (c + (0.044715 * c) * (x * x)))


def _gelu_tanh_lowp(x):
    c = -2.0 * math.sqrt(2.0 / math.pi)
    return x / (1.0 + jnp.exp(x * (c + (0.044715 * c) * (x * x))))


def _log_sigmoid(x):
    return -(jnp.maximum(-x, 0.0) + jnp.log1p(jnp.exp(-jnp.abs(x))))


def _mod_row(i, n_ctx_tiles, tiles_per_latent):
    return jnp.where(i < n_ctx_tiles, 0, 1 + (i - n_ctx_tiles) // tiles_per_latent)


def _pack_kernel(x_ref, o_ref, *, transpose):
    x = x_ref[...]
    if transpose:
        x = x.T
    o_ref[...] = pltpu.bitcast(x.astype(BF16), F32)


PACK_ROWS = 512


def _pack_rows(x, layer, ncols=None, transpose=False):
    _, r, c = x.shape
    c = c if ncols is None else ncols
    tr, tc = PACK_ROWS, 1024
    assert r % tr == 0 and c % tc == 0
    if transpose:
        out_shape, out_spec = (r // tr, c // 2, tr), pl.BlockSpec((None, tc // 2, tr), lambda i, j: (i, j, 0))
    else:
        out_shape, out_spec = (r // 2, c), pl.BlockSpec((tr // 2, tc), lambda i, j: (i, j))
    return pl.pallas_call(
        functools.partial(_pack_kernel, transpose=transpose),
        out_shape=jax.ShapeDtypeStruct(out_shape, F32),
        grid=(r // tr, c // tc),
        in_specs=[pl.BlockSpec((None, tr, tc), lambda i, j: (layer, i, j))],
        out_specs=out_spec,
        compiler_params=_params("arbitrary", "arbitrary"),
        name="pack_t" if transpose else "pack",
    )(x)


def _adaln_kernel(cond_ref, w_ref, b_ref, o_ref):
    c = cond_ref[...]
    s = c * jax.nn.sigmoid(c)
    o_ref[...] = _dot3(s, w_ref[...]) + b_ref[...]


def _adaln(cond8, w_mod, b_mod):
    depth, d, dout = w_mod.shape
    tn = 1024
    return pl.pallas_call(
        _adaln_kernel,
        out_shape=jax.ShapeDtypeStruct((depth, 8, dout), F32),
        grid=(depth, dout // tn),
        in_specs=[
            pl.BlockSpec((8, d), lambda l, j: (0, 0)),
            pl.BlockSpec((None, d, tn), lambda l, j: (l, 0, j)),
            pl.BlockSpec((None, 1, tn), lambda l, j: (l, 0, j)),
        ],
        out_specs=pl.BlockSpec((None, 8, tn), lambda l, j: (l, 0, j)),
        compiler_params=_params("arbitrary", "arbitrary"),
        name="adaln",
    )(cond8, w_mod, b_mod.reshape(depth, 1, dout))


def _stream(h, tm, cols, col_of=lambda j: 0, single_buffer_split=False):
    if not isinstance(h, tuple):
        return [h], [pl.BlockSpec((tm, cols), lambda i, j: (i, col_of(j)))], h.shape[0], None
    hc, hl = h
    nct = hc.shape[0] // tm
    kw = dict(pipeline_mode=pl.Buffered(1)) if single_buffer_split else {}
    specs = [pl.BlockSpec((tm, cols), lambda i, j: (jnp.minimum(i, nct - 1), col_of(j)), **kw),
             pl.BlockSpec((tm, cols), lambda i, j: (jnp.maximum(i - nct, 0), col_of(j)), **kw)]
    return [hc, hl], specs, hc.shape[0] + hl.shape[0], nct


def _load_stream(h_refs, split_tiles):
    if split_tiles is None:
        return h_refs[0][...]
    return jnp.where(pl.program_id(0) < split_tiles, h_refs[0][...], h_refs[1][...])


def _inproj_kernel(*refs, with_gates, emit_xn, split_tiles):
    nh = 1 if split_tiles is None else 2
    h_refs = refs[:nh]
    gain_ref, shift_ref, scale_ref, w_ref = refs[nh:nh + 4]
    rest = list(refs[nh + 4:])
    if with_gates:
        wg2_ref, bg_ref = rest[:2]
        rest = rest[2:]
    z_ref = rest.pop(0)
    if with_gates:
        g_ref, gt_ref = rest[:2]
        rest = rest[2:]
    if emit_xn:
        xn_out_ref = rest.pop(0)
    xn_scr = rest.pop(0)

    @pl.when(pl.program_id(1) == 0)
    def _():
        x = _load_stream(h_refs, split_tiles)
        xn = x * lax.rsqrt(jnp.mean(x * x, axis=-1, keepdims=True) + EPS) * gain_ref[...]
        xn = xn * (1.0 + scale_ref[...]) + shift_ref[...]
        xb = xn.astype(BF16)
        xn_scr[...] = xb
        if emit_xn:
            xn_out_ref[...] = pltpu.bitcast(xn.T.astype(BF16), F32)
        if with_gates:
            ng = bg_ref.shape[1]
            w2 = wg2_ref[...]
            w2_hi = w2.astype(BF16)
            lane = lax.broadcasted_iota(jnp.int32, w2.shape, 1)
            w_cat = jnp.where(lane < ng, w2_hi, (w2 - w2_hi.astype(F32)).astype(BF16))
            xl = (xn - xb.astype(F32)).astype(BF16)
            p = _dot(xb, w_cat)
            g = p[:, :ng] + (p[:, ng:] + _dot(xl, w2_hi)[:, :ng]) + bg_ref[...]
            g_ref[...] = g
            eye = (lax.broadcasted_iota(jnp.int32, (ng, ng), 0)
                   == lax.broadcasted_iota(jnp.int32, (ng, ng), 1)).astype(F32).astype(BF16)
            g3 = _split3(g)
            gt_ref[...] = _dot_nt(eye, g3[0]) + (_dot_nt(eye, g3[1]) + _dot_nt(eye, g3[2]))

    z_ref[...] = _dot(xn_scr[...], pltpu.bitcast(w_ref[...], BF16))


def _inproj(h, gain, mod3, layer, mod_base, w_packed, n_ctx_tiles, tiles_per_latent, w_gates=None, b_gates=None,
            emit_xn=False, tn=1024, name="inproj"):
    d, dout = 2 * w_packed.shape[0], w_packed.shape[1]
    tm = TOKEN_TILE * math.gcd(4, n_ctx_tiles, tiles_per_latent)
    per = tm // TOKEN_TILE
    with_gates = w_gates is not None
    h_arrays, h_specs, n, split_tiles = _stream(h, tm, d, single_buffer_split=True)

    def mod_idx(off):
        return lambda i, j: ((layer * 8 + _mod_row(i * per, n_ctx_tiles, tiles_per_latent)) * N_MOD + mod_base + off,
                             0, 0)

    in_specs = h_specs + [
        pl.BlockSpec((1, d), lambda i, j: (0, 0)),
        pl.BlockSpec((None, 1, d), mod_idx(0)),
        pl.BlockSpec((None, 1, d), mod_idx(1)),
        pl.BlockSpec((d // 2, tn), lambda i, j: (0, j)),
    ]
    args = h_arrays + [gain.reshape(1, d), mod3, mod3, w_packed]
    out_shape = [jax.ShapeDtypeStruct((n, dout), F32)]
    out_specs = [pl.BlockSpec((tm, tn), lambda i, j: (i, j))]
    if with_gates:
        ng = w_gates.shape[1]
        in_specs += [pl.BlockSpec((d, 2 * ng), lambda i, j: (0, 0)), pl.BlockSpec((1, ng), lambda i, j: (0, 0))]
        args += [jnp.concatenate([w_gates, w_gates], axis=1), b_gates.reshape(1, ng)]
        out_shape += [jax.ShapeDtypeStruct((n, ng), F32), jax.ShapeDtypeStruct((ng, n), F32)]
        out_specs += [pl.BlockSpec((tm, ng), lambda i, j: (i, 0)), pl.BlockSpec((ng, tm), lambda i, j: (0, i))]
    if emit_xn:
        out_shape.append(jax.ShapeDtypeStruct((d // 2, n), F32))
        out_specs.append(pl.BlockSpec((d // 2, tm), lambda i, j: (0, i)))
    return pl.pallas_call(
        functools.partial(_inproj_kernel, with_gates=with_gates, emit_xn=emit_xn, split_tiles=split_tiles),
        out_shape=out_shape,
        grid=(n // tm, dout // tn),
        in_specs=in_specs,
        out_specs=out_specs,
        scratch_shapes=[pltpu.VMEM((tm, d), BF16)],
        compiler_params=_params("arbitrary", "arbitrary"),
        name=name,
    )(*args)


def _mlstm_kernel(*refs, has_init, emit_state, nchunks):
    q_ref, k_ref, v_ref, g_ref, gt_ref = refs[:5]
    rest = list(refs[5:])
    if has_init:
        c0_ref, n0_ref, m0_ref = rest[:3]
        rest = rest[3:]
    hs_ref = rest.pop(0)
    if emit_state:
        cout_ref, nout_ref, mout_ref = rest[:3]
        rest = rest[3:]
    c_scr, n_scr, m_scr = rest
    nh, dh, L = MLSTM_HEADS, MLSTM_DH, MLSTM_CHUNK
    d = pl.program_id(0)
    s = pl.program_id(2)

    @pl.when(s == 0)
    def _():
        if has_init:
            c_scr[...] = c0_ref[...]
            n_scr[...] = n0_ref[...]
            m_scr[...] = m0_ref[...]
        else:
            c_scr[...] = jnp.zeros_like(c_scr)
            n_scr[...] = jnp.zeros_like(n_scr)
            m_scr[...] = jnp.zeros_like(m_scr)

    row = lax.broadcasted_iota(jnp.int32, (L, L), 0)
    col = lax.broadcasted_iota(jnp.int32, (L, L), 1)
    sgn = jnp.where(d == 0, 1, -1)
    mask = (row - col) * sgn >= 0
    maskb = jnp.where(mask, 1.0, 0.0).astype(BF16)

    g = g_ref[...]
    gt = gt_ref[...]
    fwd = d == 0
    i_col = jnp.where(fwd, g[:, 0:nh], g[:, nh:2 * nh])
    f_col = _log_sigmoid(jnp.where(fwd, g[:, 2 * nh:3 * nh], g[:, 3 * nh:4 * nh]))
    i_row = jnp.where(fwd, gt[0:nh], gt[nh:2 * nh])
    f_row = _log_sigmoid(jnp.where(fwd, gt[2 * nh:3 * nh], gt[3 * nh:4 * nh]))
    fc = _split3(f_col)
    b_col = _dot(maskb, fc[0]) + (_dot(maskb, fc[1]) + _dot(maskb, fc[2]))
    fr = _split3(f_row)
    b_row = _dot_nt(fr[0], maskb) + (_dot_nt(fr[1], maskb) + _dot_nt(fr[2], maskb))
    btot_col = jnp.sum(f_col, axis=0, keepdims=True)
    m_all = m_scr[...]

    heads = []
    for h in range(nh):
        sl = slice(h * dh, (h + 1) * dh)
        qh = q_ref[:, sl].astype(BF16)
        kf = k_ref[:, sl] * (dh ** -0.5)
        kh = kf.astype(BF16)
        vh = v_ref[:, sl].astype(BF16)
        heads.append(dict(sl=sl, qh=qh, kf=kf, kh=kh, vh=vh, qk=_dot_nt(qh, kh),
                          qc=_dot(qh, c_scr[h].astype(BF16))))
    for h, t in enumerate(heads):
        b_c = b_col[:, h:h + 1]
        b_r = b_row[h:h + 1, :]
        i_c = i_col[:, h:h + 1]
        i_r = i_row[h:h + 1, :]
        m = m_all[:, h:h + 1]
        btot = btot_col[:, h:h + 1]
        dm = jnp.where(mask, b_c - b_r + i_r, NEG_INF)
        inter = b_c + m
        m_t = jnp.maximum(inter, jnp.max(dm, axis=1, keepdims=True))
        g_c = btot - b_c + i_c
        g_r = btot - b_r + i_r
        m_new = jnp.maximum(btot + m, jnp.max(g_c, axis=0, keepdims=True))
        t.update(m_t=m_t, w=jnp.exp(dm - m_t), a=jnp.exp(inter - m_t), m_new=m_new,
                 decay=jnp.exp(btot + m - m_new), ws_c=jnp.exp(g_c - m_new), ws_r=jnp.exp(g_r - m_new))
    for h, t in enumerate(heads):
        sc = t["qk"] * t["w"]
        num = t["a"] * t["qc"] + _dot(sc.astype(BF16), t["vh"])
        nb = n_scr[h:h + 1, :].astype(BF16).astype(F32)
        qn = jnp.sum(t["qh"].astype(F32) * nb, axis=1, keepdims=True)
        den = t["a"] * qn + jnp.sum(sc, axis=1, keepdims=True)
        hs_ref[:, t["sl"]] = num / jnp.maximum(jnp.abs(den), jnp.exp(-t["m_t"]))
        kw = (t["kf"] * t["ws_c"]).astype(BF16)
        c_scr[h] = t["decay"] * c_scr[h] + _dot_tn(kw, t["vh"])
        n_scr[h:h + 1, :] = t["decay"] * n_scr[h:h + 1, :] + _dot(t["ws_r"].astype(BF16), t["kh"])
    m_scr[...] = jnp.concatenate([t["m_new"] for t in heads], axis=1)

    if emit_state:
        @pl.when(s == nchunks - 1)
        def _():
            cout_ref[...] = c_scr[...]
            nout_ref[...] = n_scr[...]
            mout_ref[...] = m_scr[...]


def _mlstm(z, gates, gates_t, tok_off, nseq, seqlen, j, init=None, emit_state=False, n_even=1):
    nh, dh, L = MLSTM_HEADS, MLSTM_DH, MLSTM_CHUNK
    w = nh * dh
    nchunks = seqlen // L
    off = tok_off // L

    def chunk(d, b, s):
        return off + b * nchunks + jnp.where(d == 0, s, nchunks - 1 - s)

    in_specs = [
        pl.BlockSpec((L, w), lambda d, b, s: (chunk(d, b, s), 0)),
        pl.BlockSpec((L, w), lambda d, b, s: (chunk(d, b, s), 1)),
        pl.BlockSpec((L, w), lambda d, b, s: (chunk(d, b, s), 2)),
        pl.BlockSpec((L, 4 * nh), lambda d, b, s: (chunk(d, b, s), 0)),
        pl.BlockSpec((4 * nh, L), lambda d, b, s: (0, chunk(d, b, s))),
    ]
    args = [z, z, z, gates, gates_t]
    has_init = init is not None
    if has_init:
        c0, n0, m0 = init
        in_specs += [
            pl.BlockSpec((None, None, None, nh, dh, dh), lambda d, b, s: (b, j, d, 0, 0, 0)),
            pl.BlockSpec((None, None, None, nh, dh), lambda d, b, s: (b, j, d, 0, 0)),
            pl.BlockSpec((None, None, None, 1, nh), lambda d, b, s: (b, j, d, 0, 0)),
        ]
        args += [c0, n0, m0.reshape(m0.shape[:3] + (1, nh))]
    out_shape = [jax.ShapeDtypeStruct((2, nseq * seqlen, w), F32)]
    out_specs = [pl.BlockSpec((None, L, w), lambda d, b, s: (d, chunk(d, b, s) - off, 0))]
    if emit_state:
        out_shape += [
            jax.ShapeDtypeStruct((nseq, n_even, 2, nh, dh, dh), F32),
            jax.ShapeDtypeStruct((nseq, n_even, 2, nh, dh), F32),
            jax.ShapeDtypeStruct((nseq, n_even, 2, 1, nh), F32),
        ]
        out_specs += [
            pl.BlockSpec((None, None, None, nh, dh, dh), lambda d, b, s: (b, j, d, 0, 0, 0)),
            pl.BlockSpec((None, None, None, nh, dh), lambda d, b, s: (b, j, d, 0, 0)),
            pl.BlockSpec((None, None, None, 1, nh), lambda d, b, s: (b, j, d, 0, 0)),
        ]
    return pl.pallas_call(
        functools.partial(_mlstm_kernel, has_init=has_init, emit_state=emit_state, nchunks=nchunks),
        out_shape=out_shape,
        grid=(2, nseq, nchunks),
        in_specs=in_specs,
        out_specs=out_specs,
        scratch_shapes=[pltpu.VMEM((nh, dh, dh), F32), pltpu.VMEM((nh, dh), F32), pltpu.VMEM((1, nh), F32)],
        compiler_params=_params("arbitrary", "arbitrary", "arbitrary"),
        name="mlstm_ctx" if emit_state else "mlstm_lat",
    )(*args)


def _even_out_kernel(*refs, n_ctx_tiles, tiles_per_latent, ctx_len, lat_len, split_tiles):
    nhr = 1 if split_tiles is None else 2
    h_refs = refs[:nhr]
    (gate_ref, hsc_ref, hsl_ref, o_ref, p_ref, pprev_ref, pnext_ref, gain_ref, pw_ref, ps_ref, w_ref, out_ref,
     cat_scr) = refs[nhr:]
    i = pl.program_id(0)
    tm = TOKEN_TILE
    nh, dh = MLSTM_HEADS, MLSTM_DH
    wm = nh * dh

    @pl.when(pl.program_id(1) == 0)
    def _():
        is_ctx = i < n_ctx_tiles
        hs = jnp.where(is_ctx, hsc_ref[0] + hsc_ref[1], hsl_ref[0] + hsl_ref[1])
        for h in range(nh):
            sl = slice(h * dh, (h + 1) * dh)
            x = hs[:, sl]
            y = x * lax.rsqrt(jnp.mean(x * x, axis=-1, keepdims=True) + EPS) * gain_ref[:, sl]
            cat_scr[:, sl] = (y * jax.nn.sigmoid(o_ref[:, sl])).astype(BF16)

        tiles_ctx = ctx_len // tm
        pos = jnp.where(is_ctx, i % tiles_ctx, (i - n_ctx_tiles) % tiles_per_latent)
        ntile = jnp.where(is_ctx, tiles_ctx, tiles_per_latent)
        seqlen = jnp.where(is_ctx, ctx_len, lat_len)
        x = p_ref[...]
        prev = jnp.where(pos > 0, pprev_ref[...], 0.0)
        nxt = jnp.where(pos < ntile - 1, pnext_ref[...], 0.0)
        pad = jnp.zeros((LANES - 2 * POOL_HALO, x.shape[1]), F32)
        xcat = jnp.concatenate([prev, x, nxt, pad], axis=0)
        xh, xl = _split2(xcat)
        t = lax.broadcasted_iota(jnp.int32, (tm, tm + LANES), 0)
        sidx = lax.broadcasted_iota(jnp.int32, (tm, tm + LANES), 1) - POOL_HALO
        tpos = pos * tm + lax.broadcasted_iota(jnp.int32, (tm, 1), 0)
        gw = wm // len(POOL_WINDOWS)
        for gi, win in enumerate(POOL_WINDOWS):
            sl = slice(gi * gw, (gi + 1) * gw)
            band = jnp.where((sidx >= t - win // 2) & (sidx < t - win // 2 + win), 1.0, 0.0).astype(BF16)
            lo = jnp.maximum(tpos - win // 2, 0)
            hi = jnp.minimum(tpos - win // 2 + win, seqlen)
            cnt = (hi - lo).astype(F32)
            p = (_dot(band, xh[:, sl]) + _dot(band, xl[:, sl])) / cnt - x[:, sl]
            y = _dot(p.astype(BF16), pw_ref[gi]) * ps_ref[:, sl]
            cat_scr[:, wm + gi * gw:wm + (gi + 1) * gw] = y.astype(BF16)

    out_ref[...] = (_load_stream(h_refs, split_tiles)
                    + gate_ref[...] * _dot(cat_scr[...], pltpu.bitcast(w_ref[...], BF16)))


def _even_out(h, mod3, layer, hs_ctx, hs_lat, z, gain, pool_w_bf16, pool_scale, w_out_bf16, n_ctx_tiles,
              tiles_per_latent, ctx_len, lat_len, tn=2048):
    tm = TOKEN_TILE
    d = 2 * w_out_bf16.shape[0]
    h_arrays, h_specs, n, split_tiles = _stream(h, tm, tn, col_of=lambda j: j)
    wm = MLSTM_HEADS * MLSTM_DH
    nlt = n // tm - n_ctx_tiles
    rows8 = n // POOL_HALO
    per = tm // POOL_HALO

    def gate_idx(i, j):
        return ((layer * 8 + _mod_row(i, n_ctx_tiles, tiles_per_latent)) * N_MOD + 2, 0, j)

    return pl.pallas_call(
        functools.partial(_even_out_kernel, n_ctx_tiles=n_ctx_tiles, tiles_per_latent=tiles_per_latent,
                          ctx_len=ctx_len, lat_len=lat_len, split_tiles=split_tiles),
        out_shape=jax.ShapeDtypeStruct((n, d), F32),
        grid=(n // tm, d // tn),
        in_specs=h_specs + [
            pl.BlockSpec((None, 1, tn), gate_idx),
            pl.BlockSpec((2, tm, wm), lambda i, j: (0, jnp.minimum(i, n_ctx_tiles - 1), 0)),
            pl.BlockSpec((2, tm, wm), lambda i, j: (0, jnp.clip(i - n_ctx_tiles, 0, nlt - 1), 0)),
            pl.BlockSpec((tm, wm), lambda i, j: (i, 3)),
            pl.BlockSpec((tm, wm), lambda i, j: (i, 4)),
            pl.BlockSpec((POOL_HALO, wm), lambda i, j: (jnp.maximum(i * per - 1, 0), 4)),
            pl.BlockSpec((POOL_HALO, wm), lambda i, j: (jnp.minimum((i + 1) * per, rows8 - 1), 4)),
            pl.BlockSpec((1, wm), lambda i, j: (0, 0)),
            pl.BlockSpec(pool_w_bf16.shape, lambda i, j: (0, 0, 0)),
            pl.BlockSpec((1, wm), lambda i, j: (0, 0)),
            pl.BlockSpec((d // 2, tn), lambda i, j: (0, j)),
        ],
        out_specs=pl.BlockSpec((tm, tn), lambda i, j: (i, j)),
        scratch_shapes=[pltpu.VMEM((tm, d), BF16)],
        compiler_params=_params("arbitrary", "arbitrary"),
        name="even_out",
    )(*h_arrays, mod3, hs_ctx, hs_lat, z, z, z, z, gain.reshape(1, wm), pool_w_bf16, pool_scale.reshape(1, wm),
      w_out_bf16)


def _qk_kernel(*refs, rope, emit_cache):
    q_ref, k_ref, v_ref, gain_ref = refs[:4]
    rest = list(refs[4:])
    if rope:
        cos_ref, sin_ref = rest[:2]
        rest = rest[2:]
    qn_ref, kn_ref, vb_ref = rest[:3]
    rest = rest[3:]
    hd = 2 * DA_DH
    lane = lax.broadcasted_iota(jnp.int32, (q_ref.shape[0], hd), 1)
    first = lane < DA_DH

    def norm(x, gain):
        sq = x * x
        s1 = jnp.sum(jnp.where(first, sq, 0.0), axis=-1, keepdims=True)
        s2 = jnp.sum(jnp.where(first, 0.0, sq), axis=-1, keepdims=True)
        ms = jnp.where(first, s1, s2) * (1.0 / DA_DH)
        y = x * lax.rsqrt(ms + EPS) * gain
        if rope:
            quarter = DA_DH // 4
            partner = jnp.where((lane & quarter) == 0, pltpu.roll(y, LANES - quarter, 1), pltpu.roll(y, quarter, 1))
            y = y * cos_ref[...] + partner * sin_ref[...]
        return y

    vb_ref[...] = v_ref[...].astype(BF16)
    for h in range(DA_HEADS):
        cols = slice(h * hd, (h + 1) * hd)
        qn = norm(q_ref[:, cols], gain_ref[0:1, :])
        kn = norm(k_ref[:, cols], gain_ref[1:2, :])
        qn_ref[:, cols] = (qn * (DA_DH ** -0.5)).astype(BF16)
        kn_ref[:, cols] = kn.astype(BF16)
        if emit_cache:
            newk_ref, newv_ref = rest
            newk_ref[h] = kn
            newv_ref[h] = v_ref[:, cols]


def _qk(z, qk_gain2, tok_off, ntok, j, rope_tables=None, cache_shape=None, seqlen=None):
    tm = TOKEN_TILE
    hd = 2 * DA_DH
    w = DA_HEADS * hd
    off = tok_off // tm
    rope = rope_tables is not None
    emit_cache = cache_shape is not None
    in_specs = [
        pl.BlockSpec((tm, w), lambda i: (off + i, 0)),
        pl.BlockSpec((tm, w), lambda i: (off + i, 1)),
        pl.BlockSpec((tm, w), lambda i: (off + i, 2)),
        pl.BlockSpec((2, hd), lambda i: (0, 0)),
    ]
    args = [z, z, z, qk_gain2]
    if rope:
        tps = seqlen // tm
        in_specs += [pl.BlockSpec((tm, hd), lambda i: (i % tps, 0))] * 2
        args += list(rope_tables)
    out_shape = [jax.ShapeDtypeStruct((ntok, w), BF16)] * 3
    out_specs = [pl.BlockSpec((tm, w), lambda i: (i, 0))] * 3
    if emit_cache:
        out_shape += [jax.ShapeDtypeStruct(cache_shape, F32)] * 2
        out_specs += [pl.BlockSpec((None, None, DA_HEADS, tm, hd), lambda i: (i, j, 0, 0, 0))] * 2
    return pl.pallas_call(
        functools.partial(_qk_kernel, rope=rope, emit_cache=emit_cache),
        out_shape=out_shape,
        grid=(ntok // tm,),
        in_specs=in_specs,
        out_specs=out_specs,
        compiler_params=_params("arbitrary"),
        name="qk_ctx" if emit_cache else "qk_lat",
    )(*args)


def _rope_tables(t):
    rows = t // GRID_W
    row = jnp.repeat(jnp.arange(rows), GRID_W).astype(F32)
    col = (jnp.arange(rows * GRID_W) % GRID_W).astype(F32)
    n_freq = DA_DH // 4
    inv = ROPE_BASE ** (-jnp.arange(n_freq, dtype=F32) / n_freq)
    ar, ac = row[:, None] * inv, col[:, None] * inv
    cos = jnp.concatenate([jnp.cos(ar), jnp.cos(ar), jnp.cos(ac), jnp.cos(ac)], axis=-1)
    sin = jnp.concatenate([-jnp.sin(ar), jnp.sin(ar), -jnp.sin(ac), jnp.sin(ac)], axis=-1)
    return jnp.tile(cos, (1, 2)), jnp.tile(sin, (1, 2))


def _attn_kernel(*refs, n_pieces, lam_init):
    q_ref, lp_ref, subln_ref = refs[:3]
    kv = refs[3:3 + 2 * n_pieces]
    o_ref = refs[3 + 2 * n_pieces]
    lp = lp_ref[...]
    lam = (jnp.exp(jnp.sum(lp[0:1] * lp[1:2], axis=-1, keepdims=True))
           - jnp.exp(jnp.sum(lp[2:3] * lp[3:4], axis=-1, keepdims=True)) + lam_init)
    q = q_ref[...]
    lane = lax.broadcasted_iota(jnp.int32, q.shape, 1)
    zero = jnp.zeros_like(q)
    qa = jnp.where(lane < DA_DH, q, zero)
    qb = jnp.where(lane < DA_DH, zero, q)
    ks = [kv[2 * p][...].astype(BF16) for p in range(n_pieces)]
    vs = [kv[2 * p + 1][...].astype(BF16) for p in range(n_pieces)]

    sa = [_dot_nt(qa, k) for k in ks]
    sb = [_dot_nt(qb, k) for k in ks]

    def attend(parts):
        m = functools.reduce(jnp.maximum, [jnp.max(x, axis=-1, keepdims=True) for x in parts])
        es = [jnp.exp(x - m) for x in parts]
        tot = functools.reduce(lambda a, b: a + b, [jnp.sum(e, axis=-1, keepdims=True) for e in es])
        pv = functools.reduce(lambda a, b: a + b, [_dot(e.astype(BF16), v) for e, v in zip(es, vs)])
        return pv / tot

    o = attend(sa) - lam * attend(sb)
    o = o * lax.rsqrt(jnp.mean(o * o, axis=-1, keepdims=True) + EPS) * subln_ref[...]
    o_ref[...] = (o * (1.0 - lam_init)).astype(BF16)


def _attn(qn, kn, vb, lam_params, subln, lam_init, nseq, seqlen, tq, j, caches=None):
    hd = 2 * DA_DH
    nq = seqlen // tq
    in_specs = [
        pl.BlockSpec((tq, hd), lambda b, h, qi: (b * nq + qi, h)),
        pl.BlockSpec((None, 4, DA_DH), lambda b, h, qi: (j, 0, 0)),
        pl.BlockSpec((None, 1, hd), lambda b, h, qi: (j, 0, 0)),
        pl.BlockSpec((seqlen, hd), lambda b, h, qi: (b, h)),
        pl.BlockSpec((seqlen, hd), lambda b, h, qi: (b, h)),
    ]
    args = [qn, lam_params, subln.reshape(subln.shape[0], 1, hd), kn, vb]
    n_pieces = 1
    if caches is not None:
        ck, cv = caches
        past = ck.shape[3]
        in_specs += [pl.BlockSpec((None, None, None, past, hd), lambda b, h, qi: (b, j, h, 0, 0))] * 2
        args += [ck, cv]
        n_pieces = 2
    return pl.pallas_call(
        functools.partial(_attn_kernel, n_pieces=n_pieces, lam_init=lam_init),
        out_shape=jax.ShapeDtypeStruct(qn.shape, BF16),
        grid=(nseq, DA_HEADS, nq),
        in_specs=in_specs,
        out_specs=pl.BlockSpec((tq, hd), lambda b, h, qi: (b * nq + qi, h)),
        compiler_params=_params("arbitrary", "arbitrary", "arbitrary"),
        name="attn_lat" if caches is not None else "attn_ctx",
    )(*args)


def _odd_out_kernel(h_ref, gate_ref, ac_ref, al_ref, gu_ref, gv_ref, ws_ref, bt_ref, w_ref, out_ref, cat_scr, *,
                    n_ctx_tiles):
    i = pl.program_id(0)
    tm = TOKEN_TILE
    wa = DA_HEADS * 2 * DA_DH
    gw = LANES

    @pl.when(pl.program_id(1) == 0)
    def _():
        cat_scr[:, 0:wa] = jnp.where(i < n_ctx_tiles, ac_ref[...], al_ref[...])
        for c in range(tm // GM_CHUNK):
            rows = slice(c * GM_CHUNK, (c + 1) * GM_CHUNK)
            for g in range(GM_GROUPS):
                cols = slice(g * gw, (g + 1) * gw)
                u = _gelu_tanh(gu_ref[rows, cols])
                v = _gelu_tanh(gv_ref[rows, cols])
                vn = v * lax.rsqrt(jnp.mean(v * v, axis=-1, keepdims=True) + EPS)
                mixed = _dot(ws_ref[g], vn.astype(BF16)) + bt_ref[:, g:g + 1]
                cat_scr[rows, wa + g * gw:wa + (g + 1) * gw] = (u * mixed).astype(BF16)

    out_ref[...] = h_ref[...] + gate_ref[...] * _dot(cat_scr[...], pltpu.bitcast(w_ref[...], BF16))


def _odd_out(h, mod3, layer, a_ctx, a_lat, z, gm_ws_bf16, gm_b_t, w_out_bf16, n_ctx_tiles, tiles_per_latent, tn=2048):
    n, d = h.shape
    tm = TOKEN_TILE
    wa = DA_HEADS * 2 * DA_DH
    wg = GM_GROUPS * LANES
    nlt = n // tm - n_ctx_tiles

    def gate_idx(i, j):
        return ((layer * 8 + _mod_row(i, n_ctx_tiles, tiles_per_latent)) * N_MOD + 2, 0, j)

    return pl.pallas_call(
        functools.partial(_odd_out_kernel, n_ctx_tiles=n_ctx_tiles),
        out_shape=jax.ShapeDtypeStruct((n, d), F32),
        grid=(n // tm, d // tn),
        in_specs=[
            pl.BlockSpec((tm, tn), lambda i, j: (i, j)),
            pl.BlockSpec((None, 1, tn), gate_idx),
            pl.BlockSpec((tm, wa), lambda i, j: (jnp.minimum(i, n_ctx_tiles - 1), 0)),
            pl.BlockSpec((tm, wa), lambda i, j: (jnp.clip(i - n_ctx_tiles, 0, nlt - 1), 0)),
            pl.BlockSpec((tm, wg), lambda i, j: (i, 3)),
            pl.BlockSpec((tm, wg), lambda i, j: (i, 4)),
            pl.BlockSpec(gm_ws_bf16.shape, lambda i, j: (0, 0, 0)),
            pl.BlockSpec(gm_b_t.shape, lambda i, j: (0, 0)),
            pl.BlockSpec((d // 2, tn), lambda i, j: (0, j)),
        ],
        out_specs=pl.BlockSpec((tm, tn), lambda i, j: (i, j)),
        scratch_shapes=[pltpu.VMEM((tm, d), BF16)],
        compiler_params=_params("arbitrary", "arbitrary"),
        name="odd_out",
    )(h, mod3, a_ctx, a_lat, z, z, gm_ws_bf16, gm_b_t, w_out_bf16)


def _top_values(x, k):
    vals = []
    cur = x
    for r in range(k):
        m = jnp.max(cur, axis=0, keepdims=True)
        vals.append(m)
        if r + 1 < k:
            cur = jnp.where(cur == m, NEG_INF, cur)
    return jnp.concatenate(vals, axis=0)


def _oddeven_merge(lo, hi, r):
    step = r * 2
    if step < hi - lo:
        yield from _oddeven_merge(lo, hi, step)
        yield from _oddeven_merge(lo + r, hi, step)
        yield from [(i, i + r) for i in range(lo + r, hi - r, step)]
    else:
        yield (lo, lo + r)


def _oddeven_merge_sort(lo, hi):
    if hi - lo >= 1:
        mid = lo + (hi - lo) // 2
        yield from _oddeven_merge_sort(lo, mid)
        yield from _oddeven_merge_sort(mid + 1, hi)
        yield from _oddeven_merge(lo, hi, 1)


def _top_sorted(x):
    n = len(x)
    assert n == PEER_TOPK

    def exchange(v, i, j):
        v[i], v[j] = jnp.maximum(v[i], v[j]), jnp.minimum(v[i], v[j])

    v = list(x)
    for i, j in _oddeven_merge_sort(0, n - 1):
        exchange(v, i, j)
    shift = SUBLANES // 2
    while shift:
        w = [pltpu.roll(t, SUBLANES - shift, 0) for t in v]
        v = [jnp.maximum(v[i], w[n - 1 - i]) for i in range(n)]
        d = n // 2
        while d:
            for i in range(n):
                if not i & d:
                    exchange(v, i, i + d)
            d //= 2
        shift //= 2
    return jnp.concatenate([t[0:1, :] for t in v], axis=0)


def _dup_bf16(x):
    b = pltpu.bitcast(x.astype(BF16).astype(F32), jnp.uint32)
    return pltpu.bitcast(b | (b >> 16), F32)


def _peer_route_kernel(q_ref, sk_ref, rank_ref, e2_ref, cnt_ref, c_ref, s_scr):
    half = PEER_NKEYS
    k = PEER_TOPK
    n_slabs = half // SUBLANES
    for h in range(PEER_HEADS):
        q1 = q_ref[:, 2 * h * half:(2 * h + 1) * half].astype(BF16)
        q2 = q_ref[:, (2 * h + 1) * half:(2 * h + 2) * half].astype(BF16)
        s_scr[0] = _dot_nt(sk_ref[2 * h], q1)
        s_scr[1] = _dot_nt(sk_ref[2 * h + 1], q2)

        def lane_tile(t, carry):
            lanes = pl.ds(pl.multiple_of(t * LANES, LANES), LANES)
            s1 = s_scr[0, :, lanes]
            s2 = s_scr[1, :, lanes]
            a_all = _top_sorted([s1[SUBLANES * i:SUBLANES * (i + 1)] for i in range(n_slabs)])
            b_all = _top_sorted([s2[SUBLANES * i:SUBLANES * (i + 1)] for i in range(n_slabs)])
            b = [b_all[r:r + 1] for r in range(k)]
            rows = [a_all[r:r + 1] + b_all[0:k // (r + 1)] for r in range(k)]
            nrows = sum(k // (r + 1) for r in range(k))
            rows.append(jnp.full((-nrows % SUBLANES, LANES), NEG_INF, F32))
            top = _top_values(jnp.concatenate(rows, axis=0), k)
            tau = top[k - 1:k]
            z = jnp.sum(jnp.exp(top - top[0:1]), axis=0, keepdims=True)
            cnt = jnp.zeros(s1.shape, F32)
            rank2 = jnp.zeros(s2.shape, F32)
            for r in range(k):
                cnt = jnp.where(s1 + b[r] >= tau, float(r + 1), cnt)
                rank2 = jnp.where(b[r] > s2, float(r + 1), rank2)
            rank_ref[h, :, lanes] = pltpu.bitcast(rank2.astype(BF16), F32)
            e2_ref[h, :, lanes] = pltpu.bitcast(jnp.exp(s2 - b[0]).astype(BF16), F32)
            cnt_ref[h, :, lanes] = _dup_bf16(cnt)
            c_ref[h, :, lanes] = _dup_bf16(jnp.exp(s1 - a_all[0:1]) / z)
            return carry

        lax.fori_loop(0, q_ref.shape[0] // LANES, lane_tile, 0, unroll=2)


def _peer_route(qp, subkeys_bf16, tm=512):
    n = qp.shape[0]
    def out(rows):
        return (jax.ShapeDtypeStruct((PEER_HEADS, rows, n), F32),
                pl.BlockSpec((PEER_HEADS, rows, tm), lambda i: (0, 0, i)))

    outs = [out(PEER_NKEYS // 2)] * 2 + [out(PEER_NKEYS)] * 2
    return pl.pallas_call(
        _peer_route_kernel,
        out_shape=[o[0] for o in outs],
        grid=(n // tm,),
        in_specs=[
            pl.BlockSpec((tm, qp.shape[1]), lambda i: (i, 0)),
            pl.BlockSpec(subkeys_bf16.shape, lambda i: (0, 0, 0)),
        ],
        out_specs=[o[1] for o in outs],
        scratch_shapes=[pltpu.VMEM((2, PEER_NKEYS, tm), F32)],
        compiler_params=_params("arbitrary"),
        name="peer_route",
    )(qp, subkeys_bf16)


def _peer_dense_kernel(xnt_ref, h_ref, gate_ref, u_ref, vt_ref, rank_ref, e2_ref, cnt_ref, c_ref, *rest, te, n_tiles,
                       split_tiles):
    n_out = 1 if split_tiles is None else 2
    out_refs = rest[:n_out]
    act0_scr, act1_scr, wa0_scr, wa1_scr, acc_scr = rest[n_out:]
    s = pl.program_id(0)
    n_pairs = pl.num_programs(0) - 2
    nkeys = PEER_NKEYS
    per = te // nkeys
    tm = acc_scr.shape[1]
    half = tm // 2
    assert per % 4 == 0
    pack = 2 * SUBLANES

    @pl.when(s == 0)
    def _():
        act1_scr[...] = jnp.zeros_like(act1_scr)
        wa0_scr[...] = jnp.zeros_like(wa0_scr)

    live = jnp.logical_and(s >= 1, s <= n_pairs)
    t2 = jnp.clip(s - 1, 0, n_pairs - 1) % n_tiles
    pair3 = jnp.clip(s - 2, 0, n_pairs - 1)
    t3 = pair3 % n_tiles
    tok3 = pair3 // n_tiles

    @pl.when(t3 == 0)
    def _():
        acc_scr[...] = jnp.zeros_like(acc_scr)

    def step(act_w, act_r, wa_w, wa_r):
        def stage1(c):
            cols = slice(c * half, (c + 1) * half)
            u = pltpu.bitcast(u_ref[0:te // 2, :], BF16)
            act_w[0:te, cols] = _dot(u, pltpu.bitcast(xnt_ref[:, cols], BF16))

        d_rows = acc_scr.shape[0] // PEER_V_ROW_PIECES

        def stage3(c, dq):
            cols = slice(c * half, (c + 1) * half)
            rows = slice(dq * d_rows, (dq + 1) * d_rows)
            vt = pltpu.bitcast(vt_ref[dq * d_rows // 2:(dq + 1) * d_rows // 2, :], BF16)
            acc_scr[rows, cols] += _dot(vt, wa_r[:, cols])

        mxu_work = []
        for c in range(2):
            mxu_work.append(functools.partial(stage1, c))
            mxu_work += [functools.partial(stage3, c, dq) for dq in range(PEER_V_ROW_PIECES)]
        n_lt = tm // LANES
        n_valu = per * n_lt
        issued = 0
        for ii in range(per):
            r = t2 * per + ii
            cnt_rows = [jnp.where(live, cnt_ref[h, pl.ds(r, 1), :], 0.0) for h in range(PEER_HEADS)]
            c_rows = [c_ref[h, pl.ds(r, 1), :] for h in range(PEER_HEADS)]
            for lt in range(n_lt):
                while issued < len(mxu_work) and (issued * n_valu) // len(mxu_work) <= ii * n_lt + lt:
                    mxu_work[issued]()
                    issued += 1
                lanes = slice(lt * LANES, (lt + 1) * LANES)
                cnts = [pltpu.bitcast(jnp.broadcast_to(x[:, lanes], (SUBLANES, LANES)), BF16) for x in cnt_rows]
                ccs = [pltpu.bitcast(jnp.broadcast_to(x[:, lanes], (SUBLANES, LANES)), BF16) for x in c_rows]
                for sb in range(nkeys // pack):
                    words = slice(sb * SUBLANES, (sb + 1) * SUBLANES)
                    rows = slice(ii * nkeys + sb * pack, ii * nkeys + (sb + 1) * pack)
                    w = None
                    for h in range(PEER_HEADS):
                        e2 = pltpu.bitcast(e2_ref[h, words, lanes], BF16)
                        rank = pltpu.bitcast(rank_ref[h, words, lanes], BF16)
                        wh = jnp.where(rank < cnts[h], e2 * ccs[h], jnp.zeros_like(e2))
                        w = wh if w is None else w + wh
                    wa_w[rows, lanes] = w * _gelu_tanh_lowp(act_r[rows, lanes].astype(BF16))
        assert issued == len(mxu_work)

    @pl.when(s % 2 == 0)
    def _():
        step(act0_scr, act1_scr, wa1_scr, wa0_scr)

    @pl.when(s % 2 == 1)
    def _():
        step(act1_scr, act0_scr, wa0_scr, wa1_scr)

    @pl.when(jnp.logical_and(t3 == n_tiles - 1, s >= 2))
    def _():
        res = h_ref[...] + gate_ref[...] * acc_scr[...].T
        if split_tiles is None:
            out_refs[0][...] = res
        else:
            @pl.when(tok3 < split_tiles)
            def _():
                out_refs[0][...] = res

            @pl.when(tok3 >= split_tiles)
            def _():
                out_refs[1][...] = res


def _peer_dense(xnt, h, mod3, layer, u_packed, vt_packed, rank2, e2, cnt, c, n_ctx_tiles, tiles_per_latent, tm=512,
                te=PACK_ROWS, split_out=False):
    n, d = h.shape
    n_tiles = 2 * u_packed.shape[0] // te
    per = tm // TOKEN_TILE
    n_pairs = (n // tm) * n_tiles

    def tok(lag):
        return lambda t: jnp.clip(t - lag, 0, n_pairs - 1) // n_tiles

    def exp(lag):
        return lambda t: jnp.clip(t - lag, 0, n_pairs - 1) % n_tiles

    tok1, tok2, tok3, exp1, exp3 = tok(0), tok(1), tok(2), exp(0), exp(2)
    if split_out:
        split_tiles = n_ctx_tiles // per
        n_ctx = split_tiles * tm
        out_shape = [jax.ShapeDtypeStruct((n_ctx, d), F32), jax.ShapeDtypeStruct((n - n_ctx, d), F32)]
        out_specs = [pl.BlockSpec((tm, d), lambda t: (jnp.minimum(tok3(t), split_tiles - 1), 0)),
                     pl.BlockSpec((tm, d), lambda t: (jnp.maximum(tok3(t) - split_tiles, 0), 0))]
    else:
        split_tiles = None
        out_shape = jax.ShapeDtypeStruct((n, d), F32)
        out_specs = pl.BlockSpec((tm, d), lambda t: (tok3(t), 0))

    def gate_idx(t):
        return ((layer * 8 + _mod_row(tok3(t) * per, n_ctx_tiles, tiles_per_latent)) * N_MOD + 5, 0, 0)

    rspec = pl.BlockSpec((PEER_HEADS, PEER_NKEYS, tm), lambda t: (0, 0, tok2(t)))
    pspec = pl.BlockSpec((PEER_HEADS, PEER_NKEYS // 2, tm), lambda t: (0, 0, tok2(t)))
    return pl.pallas_call(
        functools.partial(_peer_dense_kernel, te=te, n_tiles=n_tiles, split_tiles=split_tiles),
        out_shape=out_shape,
        grid=(n_pairs + 2,),
        in_specs=[
            pl.BlockSpec((d // 2, tm), lambda t: (0, tok1(t))),
            pl.BlockSpec((tm, d), lambda t: (tok3(t), 0), pipeline_mode=pl.Buffered(1)),
            pl.BlockSpec((None, 1, d), gate_idx),
            pl.BlockSpec((te // 2, d), lambda t: (exp1(t), 0)),
            pl.BlockSpec((None, d // 2, te), lambda t: (exp3(t), 0, 0)),
            pspec, pspec, rspec, rspec,
        ],
        out_specs=out_specs,
        scratch_shapes=[pltpu.VMEM((te, tm), F32), pltpu.VMEM((te, tm), F32), pltpu.VMEM((te, tm), BF16),
                        pltpu.VMEM((te, tm), BF16), pltpu.VMEM((d, tm), F32)],
        compiler_params=_params("arbitrary"),
        name="peer_dense",
    )(xnt, h, mod3, u_packed, vt_packed, rank2, e2, cnt, c)


def kernel(x_prompt, x_sample, state_mlstm_C, state_mlstm_n, state_mlstm_m, cache_da_k, cache_da_v, c, c_ctx, norm_mix, norm_ffn, w_mod, b_mod, w_in_even, b_gate_even, mlstm_gain, pool_w, pool_scale, w_out_even, w_in_odd, qk_gain, da_lambda, da_subln, gm_ws, gm_b, w_out_odd, peer_wq, peer_subkeys, peer_u, peer_v):
    nb, s_len, d = x_prompt.shape
    nbd, t_len, _ = x_sample.shape
    depth = w_mod.shape[0]
    tm = TOKEN_TILE
    assert s_len == tm and t_len % tm == 0 and nbd <= 7 and t_len % GRID_W == 0
    n_ctx = nb * s_len
    n_lat = nbd * t_len
    n_ctx_tiles = n_ctx // tm
    tiles_per_latent = t_len // tm
    n_even = (depth + 1) // 2
    n_odd = depth // 2

    h = (x_prompt.reshape(n_ctx, d), x_sample.reshape(n_lat, d))
    cond8 = jnp.concatenate([c_ctx[None], c, jnp.zeros((7 - nbd, d), F32)], axis=0)
    mod3 = _adaln(cond8, w_mod, b_mod).reshape(depth * 8 * N_MOD, 1, d)
    rope = _rope_tables(t_len)

    new_c, new_n, new_m, new_k, new_v = [], [], [], [], []
    wm = MLSTM_HEADS * MLSTM_DH
    for l in range(depth):
        j = l // 2
        if l % 2 == 0:
            n_main = 5 * wm
            z, gates, gates_t = _inproj(h, norm_mix[l], mod3, l, 0, _pack_rows(w_in_even, j, ncols=n_main),
                                        n_ctx_tiles, tiles_per_latent, w_gates=w_in_even[j, :, n_main:],
                                        b_gates=b_gate_even[j],
                                        name="inproj_even")
            hs_ctx, cc, cn, cm = _mlstm(z, gates, gates_t, 0, nb, s_len, 0, emit_state=True)
            (hs_lat,) = _mlstm(z, gates, gates_t, n_ctx, nbd, t_len, j,
                               init=(state_mlstm_C, state_mlstm_n, state_mlstm_m))
            new_c.append(cc)
            new_n.append(cn)
            new_m.append(cm[:, :, :, 0, :])
            h = _even_out(h, mod3, l, hs_ctx, hs_lat, z, mlstm_gain[j], pool_w[j].astype(BF16), pool_scale[j],
                          _pack_rows(w_out_even, j), n_ctx_tiles, tiles_per_latent, s_len, t_len)
        else:
            lam_init = 0.8 - 0.6 * math.exp(-0.3 * l)
            (z,) = _inproj(h, norm_mix[l], mod3, l, 0, _pack_rows(w_in_odd, j), n_ctx_tiles, tiles_per_latent,
                           name="inproj_odd")
            gain2 = jnp.tile(qk_gain[j], (1, 2))
            cache_shape = (nb, 1, DA_HEADS, s_len, 2 * DA_DH)
            qn_c, kn_c, vb_c, nk, nv = _qk(z, gain2, 0, n_ctx, 0, cache_shape=cache_shape)
            qn_l, kn_l, vb_l = _qk(z, gain2, n_ctx, n_lat, 0, rope_tables=rope, seqlen=t_len)
            new_k.append(nk)
            new_v.append(nv)
            a_ctx = _attn(qn_c, kn_c, vb_c, da_lambda, da_subln, lam_init, nb, s_len, s_len, j)
            a_lat = _attn(qn_l, kn_l, vb_l, da_lambda, da_subln, lam_init, nbd, t_len, tm, j,
                          caches=(cache_da_k, cache_da_v))
            h = _odd_out(h, mod3, l, a_ctx, a_lat, z, gm_ws[j].astype(BF16), gm_b[j].T, _pack_rows(w_out_odd, j),
                         n_ctx_tiles, tiles_per_latent)
        qp, xnt = _inproj(h, norm_ffn[l], mod3, l, 3, _pack_rows(peer_wq, l), n_ctx_tiles, tiles_per_latent,
                          emit_xn=True, name="inproj_peer")
        sk = peer_subkeys[l].reshape(2 * PEER_HEADS, PEER_NKEYS, PEER_NKEYS).astype(BF16)
        rank2, e2, cnt, cw = _peer_route(qp, sk)
        h = _peer_dense(xnt, h, mod3, l, _pack_rows(peer_u, l), _pack_rows(peer_v, l, transpose=True), rank2,
                        e2, cnt, cw,
                        n_ctx_tiles, tiles_per_latent, split_out=(l == depth - 1))

    def join(parts):
        return parts[0] if len(parts) == 1 else jnp.concatenate(parts, axis=1)

    return (h[0].reshape(nb, s_len, d), h[1].reshape(nbd, t_len, d), join(new_c), join(new_n), join(new_m),
            join(new_k), join(new_v))
```

```python
import functools
import math

import jax
import jax.numpy as jnp
from jax import lax
from jax.experimental import pallas as pl
from jax.experimental.pallas import tpu as pltpu

F32 = jnp.float32
BF16 = jnp.bfloat16

N_MOD = 6
EPS = 1e-6
TOKEN_TILE = 256
LANES = 128
SUBLANES = 8
VMEM_LIMIT_BYTES = 56 * 1024 * 1024

MLSTM_HEADS = 4
MLSTM_DH = 256
MLSTM_CHUNK = 128
POOL_WINDOWS = (2, 4, 8, 16)
POOL_HALO = 8
DA_HEADS = 8
DA_DH = 64
GRID_W = 64
ROPE_BASE = 10000.0
GM_GROUPS = 8
GM_CHUNK = 128
PEER_HEADS = 8
PEER_NKEYS = 128
PEER_TOPK = 16
PEER_V_ROW_PIECES = 8
NEG_INF = float("-inf")


def _params(*sem):
    return pltpu.CompilerParams(dimension_semantics=sem, vmem_limit_bytes=VMEM_LIMIT_BYTES)


def _dot(a, b):
    return jnp.dot(a, b, preferred_element_type=F32)


def _dot_nt(a, b):
    return lax.dot_general(a, b, (((1,), (1,)), ((), ())), preferred_element_type=F32)


def _dot_tn(a, b):
    return lax.dot_general(a, b, (((0,), (0,)), ((), ())), preferred_element_type=F32)


def _split2(x):
    hi = x.astype(BF16)
    lo = (x - hi.astype(F32)).astype(BF16)
    return hi, lo


def _split3(x):
    hi = x.astype(BF16)
    r = x - hi.astype(F32)
    mid = r.astype(BF16)
    lo = (r - mid.astype(F32)).astype(BF16)
    return hi, mid, lo


def _dot3(a, b, dot=_dot):
    ah, al = _split2(a)
    bh, bl = _split2(b)
    return dot(ah, bh) + (dot(ah, bl) + dot(al, bh))


def _gelu_tanh(x):
    c = math.sqrt(2.0 / math.pi)
    half = 0.5 * x
    return half + half * jnp.tanh(x * (c + (0.044715 * c) * (x * x)))


def _gelu_tanh_lowp(x):
    c = -2.0 * math.sqrt(2.0 / math.pi)
    return x / (1.0 + jnp.exp(x * (c + (0.044715 * c) * (x * x))))


def _log_sigmoid(x):
    return -(jnp.maximum(-x, 0.0) + jnp.log1p(jnp.exp(-jnp.abs(x))))


def _mod_row(i, n_ctx_tiles, tiles_per_latent):
    return jnp.where(i < n_ctx_tiles, 0, 1 + (i - n_ctx_tiles) // tiles_per_latent)


def _pack_kernel(x_ref, o_ref, *, transpose):
    x = x_ref[...]
    if transpose:
        x = x.T
    o_ref[...] = pltpu.bitcast(x.astype(BF16), F32)


PACK_ROWS = 512


def _pack_rows(x, layer, ncols=None, transpose=False):
    _, r, c = x.shape
    c = c if ncols is None else ncols
    tr, tc = PACK_ROWS, 1024
    assert r % tr == 0 and c % tc == 0
    if transpose:
        out_shape, out_spec = (r // tr, c // 2, tr), pl.BlockSpec((None, tc // 2, tr), lambda i, j: (i, j, 0))
    else:
        out_shape, out_spec = (r // 2, c), pl.BlockSpec((tr // 2, tc), lambda i, j: (i, j))
    return pl.pallas_call(
        functools.partial(_pack_kernel, transpose=transpose),
        out_shape=jax.ShapeDtypeStruct(out_shape, F32),
        grid=(r // tr, c // tc),
        in_specs=[pl.BlockSpec((None, tr, tc), lambda i, j: (layer, i, j))],
        out_specs=out_spec,
        compiler_params=_params("arbitrary", "arbitrary"),
        name="pack_t" if transpose else "pack",
    )(x)


def _adaln_kernel(cond_ref, w_ref, b_ref, o_ref):
    c = cond_ref[...]
    s = c * jax.nn.sigmoid(c)
    o_ref[...] = _dot3(s, w_ref[...]) + b_ref[...]


def _adaln(cond8, w_mod, b_mod):
    depth, d, dout = w_mod.shape
    tn = 1024
    return pl.pallas_call(
        _adaln_kernel,
        out_shape=jax.ShapeDtypeStruct((depth, 8, dout), F32),
        grid=(depth, dout // tn),
        in_specs=[
            pl.BlockSpec((8, d), lambda l, j: (0, 0)),
            pl.BlockSpec((None, d, tn), lambda l, j: (l, 0, j)),
            pl.BlockSpec((None, 1, tn), lambda l, j: (l, 0, j)),
        ],
        out_specs=pl.BlockSpec((None, 8, tn), lambda l, j: (l, 0, j)),
        compiler_params=_params("arbitrary", "arbitrary"),
        name="adaln",
    )(cond8, w_mod, b_mod.reshape(depth, 1, dout))


def _stream(h, tm, cols, col_of=lambda j: 0, single_buffer_split=False):
    if not isinstance(h, tuple):
        return [h], [pl.BlockSpec((tm, cols), lambda i, j: (i, col_of(j)))], h.shape[0], None
    hc, hl = h
    nct = hc.shape[0] // tm
    kw = dict(pipeline_mode=pl.Buffered(1)) if single_buffer_split else {}
    specs = [pl.BlockSpec((tm, cols), lambda i, j: (jnp.minimum(i, nct - 1), col_of(j)), **kw),
             pl.BlockSpec((tm, cols), lambda i, j: (jnp.maximum(i - nct, 0), col_of(j)), **kw)]
    return [hc, hl], specs, hc.shape[0] + hl.shape[0], nct


def _load_stream(h_refs, split_tiles):
    if split_tiles is None:
        return h_refs[0][...]
    return jnp.where(pl.program_id(0) < split_tiles, h_refs[0][...], h_refs[1][...])


def _inproj_kernel(*refs, with_gates, emit_xn, split_tiles, pack_out):
    nh = 1 if split_tiles is None else 2
    h_refs = refs[:nh]
    gain_ref, shift_ref, scale_ref, w_ref = refs[nh:nh + 4]
    rest = list(refs[nh + 4:])
    if with_gates:
        wg2_ref, bg_ref = rest[:2]
        rest = rest[2:]
    z_ref = rest.pop(0)
    if with_gates:
        g_ref, gt_ref = rest[:2]
        rest = rest[2:]
    if emit_xn:
        xn_out_ref = rest.pop(0)
    xn_scr = rest.pop(0)

    @pl.when(pl.program_id(1) == 0)
    def _():
        x = _load_stream(h_refs, split_tiles)
        xn = x * lax.rsqrt(jnp.mean(x * x, axis=-1, keepdims=True) + EPS) * gain_ref[...]
        xn = xn * (1.0 + scale_ref[...]) + shift_ref[...]
        xb = xn.astype(BF16)
        xn_scr[...] = xb
        if emit_xn:
            xn_out_ref[...] = pltpu.bitcast(xn.T.astype(BF16), F32)
        if with_gates:
            ng = bg_ref.shape[1]
            w2 = wg2_ref[...]
            w2_hi = w2.astype(BF16)
            lane = lax.broadcasted_iota(jnp.int32, w2.shape, 1)
            w_cat = jnp.where(lane < ng, w2_hi, (w2 - w2_hi.astype(F32)).astype(BF16))
            xl = (xn - xb.astype(F32)).astype(BF16)
            p = _dot(xb, w_cat)
            g = p[:, :ng] + (p[:, ng:] + _dot(xl, w2_hi)[:, :ng]) + bg_ref[...]
            g_ref[...] = g
            eye = (lax.broadcasted_iota(jnp.int32, (ng, ng), 0)
                   == lax.broadcasted_iota(jnp.int32, (ng, ng), 1)).astype(F32).astype(BF16)
            g3 = _split3(g)
            gt_ref[...] = _dot_nt(eye, g3[0]) + (_dot_nt(eye, g3[1]) + _dot_nt(eye, g3[2]))

    z = _dot(xn_scr[...], pltpu.bitcast(w_ref[...], BF16))
    z_ref[...] = pltpu.bitcast(z.astype(BF16), F32) if pack_out else z


def _inproj(h, gain, mod3, layer, mod_base, w_packed, n_ctx_tiles, tiles_per_latent, w_gates=None, b_gates=None,
            emit_xn=False, pack_out=False, tn=1024, name="inproj"):
    d, dout = 2 * w_packed.shape[0], w_packed.shape[1]
    tm = TOKEN_TILE * math.gcd(4, n_ctx_tiles, tiles_per_latent)
    per = tm // TOKEN_TILE
    with_gates = w_gates is not None
    h_arrays, h_specs, n, split_tiles = _stream(h, tm, d, single_buffer_split=True)

    def mod_idx(off):
        return lambda i, j: ((layer * 8 + _mod_row(i * per, n_ctx_tiles, tiles_per_latent)) * N_MOD + mod_base + off,
                             0, 0)

    in_specs = h_specs + [
        pl.BlockSpec((1, d), lambda i, j: (0, 0)),
        pl.BlockSpec((None, 1, d), mod_idx(0)),
        pl.BlockSpec((None, 1, d), mod_idx(1)),
        pl.BlockSpec((d // 2, tn), lambda i, j: (0, j)),
    ]
    args = h_arrays + [gain.reshape(1, d), mod3, mod3, w_packed]
    zr = 2 if pack_out else 1
    out_shape = [jax.ShapeDtypeStruct((n // zr, dout), F32)]
    out_specs = [pl.BlockSpec((tm // zr, tn), lambda i, j: (i, j))]
    if with_gates:
        ng = w_gates.shape[1]
        in_specs += [pl.BlockSpec((d, 2 * ng), lambda i, j: (0, 0)), pl.BlockSpec((1, ng), lambda i, j: (0, 0))]
        args += [jnp.concatenate([w_gates, w_gates], axis=1), b_gates.reshape(1, ng)]
        out_shape += [jax.ShapeDtypeStruct((n, ng), F32), jax.ShapeDtypeStruct((ng, n), F32)]
        out_specs += [pl.BlockSpec((tm, ng), lambda i, j: (i, 0)), pl.BlockSpec((ng, tm), lambda i, j: (0, i))]
    if emit_xn:
        out_shape.append(jax.ShapeDtypeStruct((d // 2, n), F32))
        out_specs.append(pl.BlockSpec((d // 2, tm), lambda i, j: (0, i)))
    return pl.pallas_call(
        functools.partial(_inproj_kernel, with_gates=with_gates, emit_xn=emit_xn, split_tiles=split_tiles,
                          pack_out=pack_out),
        out_shape=out_shape,
        grid=(n // tm, dout // tn),
        in_specs=in_specs,
        out_specs=out_specs,
        scratch_shapes=[pltpu.VMEM((tm, d), BF16)],
        compiler_params=_params("arbitrary", "arbitrary"),
        name=name,
    )(*args)


def _mlstm_kernel(*refs, has_init, emit_state, nchunks):
    q_ref, k_ref, v_ref, g_ref, gt_ref = refs[:5]
    rest = list(refs[5:])
    if has_init:
        c0_ref, n0_ref, m0_ref = rest[:3]
        rest = rest[3:]
    hs_ref = rest.pop(0)
    if emit_state:
        cout_ref, nout_ref, mout_ref = rest[:3]
        rest = rest[3:]
    c_scr, n_scr, m_scr = rest
    nh, dh, L = MLSTM_HEADS, MLSTM_DH, MLSTM_CHUNK
    d = pl.program_id(0)
    s = pl.program_id(2)

    @pl.when(s == 0)
    def _():
        if has_init:
            c_scr[...] = c0_ref[...]
            n_scr[...] = n0_ref[...]
            m_scr[...] = m0_ref[...]
        else:
            c_scr[...] = jnp.zeros_like(c_scr)
            n_scr[...] = jnp.zeros_like(n_scr)
            m_scr[...] = jnp.zeros_like(m_scr)

    row = lax.broadcasted_iota(jnp.int32, (L, L), 0)
    col = lax.broadcasted_iota(jnp.int32, (L, L), 1)
    sgn = jnp.where(d == 0, 1, -1)
    mask = (row - col) * sgn >= 0
    maskb = jnp.where(mask, 1.0, 0.0).astype(BF16)

    g = g_ref[...]
    gt = gt_ref[...]
    fwd = d == 0
    i_col = jnp.where(fwd, g[:, 0:nh], g[:, nh:2 * nh])
    f_col = _log_sigmoid(jnp.where(fwd, g[:, 2 * nh:3 * nh], g[:, 3 * nh:4 * nh]))
    i_row = jnp.where(fwd, gt[0:nh], gt[nh:2 * nh])
    f_row = _log_sigmoid(jnp.where(fwd, gt[2 * nh:3 * nh], gt[3 * nh:4 * nh]))
    fc = _split3(f_col)
    b_col = _dot(maskb, fc[0]) + (_dot(maskb, fc[1]) + _dot(maskb, fc[2]))
    fr = _split3(f_row)
    b_row = _dot_nt(fr[0], maskb) + (_dot_nt(fr[1], maskb) + _dot_nt(fr[2], maskb))
    btot_col = jnp.sum(f_col, axis=0, keepdims=True)
    m_all = m_scr[...]

    heads = []
    for h in range(nh):
        sl = slice(h * dh, (h + 1) * dh)
        qh = q_ref[:, sl].astype(BF16)
        kf = k_ref[:, sl] * (dh ** -0.5)
        kh = kf.astype(BF16)
        vh = v_ref[:, sl].astype(BF16)
        heads.append(dict(sl=sl, qh=qh, kf=kf, kh=kh, vh=vh, qk=_dot_nt(qh, kh),
                          qc=_dot(qh, c_scr[h].astype(BF16))))
    for h, t in enumerate(heads):
        b_c = b_col[:, h:h + 1]
        b_r = b_row[h:h + 1, :]
        i_c = i_col[:, h:h + 1]
        i_r = i_row[h:h + 1, :]
        m = m_all[:, h:h + 1]
        btot = btot_col[:, h:h + 1]
        dm = jnp.where(mask, b_c - b_r + i_r, NEG_INF)
        inter = b_c + m
        m_t = jnp.maximum(inter, jnp.max(dm, axis=1, keepdims=True))
        g_c = btot - b_c + i_c
        g_r = btot - b_r + i_r
        m_new = jnp.maximum(btot + m, jnp.max(g_c, axis=0, keepdims=True))
        t.update(m_t=m_t, w=jnp.exp(dm - m_t), a=jnp.exp(inter - m_t), m_new=m_new,
                 decay=jnp.exp(btot + m - m_new), ws_c=jnp.exp(g_c - m_new), ws_r=jnp.exp(g_r - m_new))
    for h, t in enumerate(heads):
        sc = t["qk"] * t["w"]
        num = t["a"] * t["qc"] + _dot(sc.astype(BF16), t["vh"])
        nb = n_scr[h:h + 1, :].astype(BF16).astype(F32)
        qn = jnp.sum(t["qh"].astype(F32) * nb, axis=1, keepdims=True)
        den = t["a"] * qn + jnp.sum(sc, axis=1, keepdims=True)
        hs_ref[:, t["sl"]] = num / jnp.maximum(jnp.abs(den), jnp.exp(-t["m_t"]))
        kw = (t["kf"] * t["ws_c"]).astype(BF16)
        c_scr[h] = t["decay"] * c_scr[h] + _dot_tn(kw, t["vh"])
        n_scr[h:h + 1, :] = t["decay"] * n_scr[h:h + 1, :] + _dot(t["ws_r"].astype(BF16), t["kh"])
    m_scr[...] = jnp.concatenate([t["m_new"] for t in heads], axis=1)

    if emit_state:
        @pl.when(s == nchunks - 1)
        def _():
            cout_ref[...] = c_scr[...]
            nout_ref[...] = n_scr[...]
            mout_ref[...] = m_scr[...]


def _mlstm(z, gates, gates_t, tok_off, nseq, seqlen, j, init=None, emit_state=False, n_even=1):
    nh, dh, L = MLSTM_HEADS, MLSTM_DH, MLSTM_CHUNK
    w = nh * dh
    nchunks = seqlen // L
    off = tok_off // L

    def chunk(d, b, s):
        return off + b * nchunks + jnp.where(d == 0, s, nchunks - 1 - s)

    in_specs = [
        pl.BlockSpec((L, w), lambda d, b, s: (chunk(d, b, s), 0)),
        pl.BlockSpec((L, w), lambda d, b, s: (chunk(d, b, s), 1)),
        pl.BlockSpec((L, w), lambda d, b, s: (chunk(d, b, s), 2)),
        pl.BlockSpec((L, 4 * nh), lambda d, b, s: (chunk(d, b, s), 0)),
        pl.BlockSpec((4 * nh, L), lambda d, b, s: (0, chunk(d, b, s))),
    ]
    args = [z, z, z, gates, gates_t]
    has_init = init is not None
    if has_init:
        c0, n0, m0 = init
        in_specs += [
            pl.BlockSpec((None, None, None, nh, dh, dh), lambda d, b, s: (b, j, d, 0, 0, 0)),
            pl.BlockSpec((None, None, None, nh, dh), lambda d, b, s: (b, j, d, 0, 0)),
            pl.BlockSpec((None, None, None, 1, nh), lambda d, b, s: (b, j, d, 0, 0)),
        ]
        args += [c0, n0, m0.reshape(m0.shape[:3] + (1, nh))]
    out_shape = [jax.ShapeDtypeStruct((2, nseq * seqlen, w), F32)]
    out_specs = [pl.BlockSpec((None, L, w), lambda d, b, s: (d, chunk(d, b, s) - off, 0))]
    if emit_state:
        out_shape += [
            jax.ShapeDtypeStruct((nseq, n_even, 2, nh, dh, dh), F32),
            jax.ShapeDtypeStruct((nseq, n_even, 2, nh, dh), F32),
            jax.ShapeDtypeStruct((nseq, n_even, 2, 1, nh), F32),
        ]
        out_specs += [
            pl.BlockSpec((None, None, None, nh, dh, dh), lambda d, b, s: (b, j, d, 0, 0, 0)),
            pl.BlockSpec((None, None, None, nh, dh), lambda d, b, s: (b, j, d, 0, 0)),
            pl.BlockSpec((None, None, None, 1, nh), lambda d, b, s: (b, j, d, 0, 0)),
        ]
    return pl.pallas_call(
        functools.partial(_mlstm_kernel, has_init=has_init, emit_state=emit_state, nchunks=nchunks),
        out_shape=out_shape,
        grid=(2, nseq, nchunks),
        in_specs=in_specs,
        out_specs=out_specs,
        scratch_shapes=[pltpu.VMEM((nh, dh, dh), F32), pltpu.VMEM((nh, dh), F32), pltpu.VMEM((1, nh), F32)],
        compiler_params=_params("arbitrary", "arbitrary", "arbitrary"),
        name="mlstm_ctx" if emit_state else "mlstm_lat",
    )(*args)


def _even_out_kernel(*refs, n_ctx_tiles, tiles_per_latent, ctx_len, lat_len, split_tiles):
    nhr = 1 if split_tiles is None else 2
    h_refs = refs[:nhr]
    (gate_ref, hsc_ref, hsl_ref, o_ref, p_ref, pprev_ref, pnext_ref, gain_ref, pw_ref, ps_ref, w_ref, out_ref,
     cat_scr) = refs[nhr:]
    i = pl.program_id(0)
    tm = TOKEN_TILE
    nh, dh = MLSTM_HEADS, MLSTM_DH
    wm = nh * dh

    @pl.when(pl.program_id(1) == 0)
    def _():
        is_ctx = i < n_ctx_tiles
        hs = jnp.where(is_ctx, hsc_ref[0] + hsc_ref[1], hsl_ref[0] + hsl_ref[1])
        for h in range(nh):
            sl = slice(h * dh, (h + 1) * dh)
            x = hs[:, sl]
            y = x * lax.rsqrt(jnp.mean(x * x, axis=-1, keepdims=True) + EPS) * gain_ref[:, sl]
            cat_scr[:, sl] = (y * jax.nn.sigmoid(o_ref[:, sl])).astype(BF16)

        tiles_ctx = ctx_len // tm
        pos = jnp.where(is_ctx, i % tiles_ctx, (i - n_ctx_tiles) % tiles_per_latent)
        ntile = jnp.where(is_ctx, tiles_ctx, tiles_per_latent)
        seqlen = jnp.where(is_ctx, ctx_len, lat_len)
        x = p_ref[...]
        prev = jnp.where(pos > 0, pprev_ref[...], 0.0)
        nxt = jnp.where(pos < ntile - 1, pnext_ref[...], 0.0)
        pad = jnp.zeros((LANES - 2 * POOL_HALO, x.shape[1]), F32)
        xcat = jnp.concatenate([prev, x, nxt, pad], axis=0)
        xh, xl = _split2(xcat)
        t = lax.broadcasted_iota(jnp.int32, (tm, tm + LANES), 0)
        sidx = lax.broadcasted_iota(jnp.int32, (tm, tm + LANES), 1) - POOL_HALO
        tpos = pos * tm + lax.broadcasted_iota(jnp.int32, (tm, 1), 0)
        gw = wm // len(POOL_WINDOWS)
        for gi, win in enumerate(POOL_WINDOWS):
            sl = slice(gi * gw, (gi + 1) * gw)
            band = jnp.where((sidx >= t - win // 2) & (sidx < t - win // 2 + win), 1.0, 0.0).astype(BF16)
            lo = jnp.maximum(tpos - win // 2, 0)
            hi = jnp.minimum(tpos - win // 2 + win, seqlen)
            cnt = (hi - lo).astype(F32)
            p = (_dot(band, xh[:, sl]) + _dot(band, xl[:, sl])) / cnt - x[:, sl]
            y = _dot(p.astype(BF16), pw_ref[gi]) * ps_ref[:, sl]
            cat_scr[:, wm + gi * gw:wm + (gi + 1) * gw] = y.astype(BF16)

    out_ref[...] = (_load_stream(h_refs, split_tiles)
                    + gate_ref[...] * _dot(cat_scr[...], pltpu.bitcast(w_ref[...], BF16)))


def _even_out(h, mod3, layer, hs_ctx, hs_lat, z, gain, pool_w_bf16, pool_scale, w_out_bf16, n_ctx_tiles,
              tiles_per_latent, ctx_len, lat_len, tn=2048):
    tm = TOKEN_TILE
    d = 2 * w_out_bf16.shape[0]
    h_arrays, h_specs, n, split_tiles = _stream(h, tm, tn, col_of=lambda j: j)
    wm = MLSTM_HEADS * MLSTM_DH
    nlt = n // tm - n_ctx_tiles
    rows8 = n // POOL_HALO
    per = tm // POOL_HALO

    def gate_idx(i, j):
        return ((layer * 8 + _mod_row(i, n_ctx_tiles, tiles_per_latent)) * N_MOD + 2, 0, j)

    return pl.pallas_call(
        functools.partial(_even_out_kernel, n_ctx_tiles=n_ctx_tiles, tiles_per_latent=tiles_per_latent,
                          ctx_len=ctx_len, lat_len=lat_len, split_tiles=split_tiles),
        out_shape=jax.ShapeDtypeStruct((n, d), F32),
        grid=(n // tm, d // tn),
        in_specs=h_specs + [
            pl.BlockSpec((None, 1, tn), gate_idx),
            pl.BlockSpec((2, tm, wm), lambda i, j: (0, jnp.minimum(i, n_ctx_tiles - 1), 0)),
            pl.BlockSpec((2, tm, wm), lambda i, j: (0, jnp.clip(i - n_ctx_tiles, 0, nlt - 1), 0)),
            pl.BlockSpec((tm, wm), lambda i, j: (i, 3)),
            pl.BlockSpec((tm, wm), lambda i, j: (i, 4)),
            pl.BlockSpec((POOL_HALO, wm), lambda i, j: (jnp.maximum(i * per - 1, 0), 4)),
            pl.BlockSpec((POOL_HALO, wm), lambda i, j: (jnp.minimum((i + 1) * per, rows8 - 1), 4)),
            pl.BlockSpec((1, wm), lambda i, j: (0, 0)),
            pl.BlockSpec(pool_w_bf16.shape, lambda i, j: (0, 0, 0)),
            pl.BlockSpec((1, wm), lambda i, j: (0, 0)),
            pl.BlockSpec((d // 2, tn), lambda i, j: (0, j)),
        ],
        out_specs=pl.BlockSpec((tm, tn), lambda i, j: (i, j)),
        scratch_shapes=[pltpu.VMEM((tm, d), BF16)],
        compiler_params=_params("arbitrary", "arbitrary"),
        name="even_out",
    )(*h_arrays, mod3, hs_ctx, hs_lat, z, z, z, z, gain.reshape(1, wm), pool_w_bf16, pool_scale.reshape(1, wm),
      w_out_bf16)


def _qk_kernel(*refs, rope, emit_cache):
    q_ref, k_ref, v_ref, gain_ref = refs[:4]
    rest = list(refs[4:])
    if rope:
        cos_ref, sin_ref = rest[:2]
        rest = rest[2:]
    qn_ref, kn_ref, vb_ref = rest[:3]
    rest = rest[3:]
    hd = 2 * DA_DH
    lane = lax.broadcasted_iota(jnp.int32, (q_ref.shape[0], hd), 1)
    first = lane < DA_DH

    def norm(x, gain):
        sq = x * x
        s1 = jnp.sum(jnp.where(first, sq, 0.0), axis=-1, keepdims=True)
        s2 = jnp.sum(jnp.where(first, 0.0, sq), axis=-1, keepdims=True)
        ms = jnp.where(first, s1, s2) * (1.0 / DA_DH)
        y = x * lax.rsqrt(ms + EPS) * gain
        if rope:
            quarter = DA_DH // 4
            partner = jnp.where((lane & quarter) == 0, pltpu.roll(y, LANES - quarter, 1), pltpu.roll(y, quarter, 1))
            y = y * cos_ref[...] + partner * sin_ref[...]
        return y

    vb_ref[...] = v_ref[...].astype(BF16)
    for h in range(DA_HEADS):
        cols = slice(h * hd, (h + 1) * hd)
        qn = norm(q_ref[:, cols], gain_ref[0:1, :])
        kn = norm(k_ref[:, cols], gain_ref[1:2, :])
        qn_ref[:, cols] = (qn * (DA_DH ** -0.5)).astype(BF16)
        kn_ref[:, cols] = kn.astype(BF16)
        if emit_cache:
            newk_ref, newv_ref = rest
            newk_ref[h] = kn
            newv_ref[h] = v_ref[:, cols]


def _qk(z, qk_gain2, tok_off, ntok, j, rope_tables=None, cache_shape=None, seqlen=None):
    tm = TOKEN_TILE
    hd = 2 * DA_DH
    w = DA_HEADS * hd
    off = tok_off // tm
    rope = rope_tables is not None
    emit_cache = cache_shape is not None
    in_specs = [
        pl.BlockSpec((tm, w), lambda i: (off + i, 0)),
        pl.BlockSpec((tm, w), lambda i: (off + i, 1)),
        pl.BlockSpec((tm, w), lambda i: (off + i, 2)),
        pl.BlockSpec((2, hd), lambda i: (0, 0)),
    ]
    args = [z, z, z, qk_gain2]
    if rope:
        tps = seqlen // tm
        in_specs += [pl.BlockSpec((tm, hd), lambda i: (i % tps, 0))] * 2
        args += list(rope_tables)
    out_shape = [jax.ShapeDtypeStruct((ntok, w), BF16)] * 3
    out_specs = [pl.BlockSpec((tm, w), lambda i: (i, 0))] * 3
    if emit_cache:
        out_shape += [jax.ShapeDtypeStruct(cache_shape, F32)] * 2
        out_specs += [pl.BlockSpec((None, None, DA_HEADS, tm, hd), lambda i: (i, j, 0, 0, 0))] * 2
    return pl.pallas_call(
        functools.partial(_qk_kernel, rope=rope, emit_cache=emit_cache),
        out_shape=out_shape,
        grid=(ntok // tm,),
        in_specs=in_specs,
        out_specs=out_specs,
        compiler_params=_params("arbitrary"),
        name="qk_ctx" if emit_cache else "qk_lat",
    )(*args)


def _rope_tables(t):
    rows = t // GRID_W
    row = jnp.repeat(jnp.arange(rows), GRID_W).astype(F32)
    col = (jnp.arange(rows * GRID_W) % GRID_W).astype(F32)
    n_freq = DA_DH // 4
    inv = ROPE_BASE ** (-jnp.arange(n_freq, dtype=F32) / n_freq)
    ar, ac = row[:, None] * inv, col[:, None] * inv
    cos = jnp.concatenate([jnp.cos(ar), jnp.cos(ar), jnp.cos(ac), jnp.cos(ac)], axis=-1)
    sin = jnp.concatenate([-jnp.sin(ar), jnp.sin(ar), -jnp.sin(ac), jnp.sin(ac)], axis=-1)
    return jnp.tile(cos, (1, 2)), jnp.tile(sin, (1, 2))


def _attn_kernel(*refs, n_pieces, lam_init):
    q_ref, lp_ref, subln_ref = refs[:3]
    kv = refs[3:3 + 2 * n_pieces]
    o_ref = refs[3 + 2 * n_pieces]
    lp = lp_ref[...]
    lam = (jnp.exp(jnp.sum(lp[0:1] * lp[1:2], axis=-1, keepdims=True))
           - jnp.exp(jnp.sum(lp[2:3] * lp[3:4], axis=-1, keepdims=True)) + lam_init)
    q = q_ref[...]
    lane = lax.broadcasted_iota(jnp.int32, q.shape, 1)
    zero = jnp.zeros_like(q)
    qa = jnp.where(lane < DA_DH, q, zero)
    qb = jnp.where(lane < DA_DH, zero, q)
    ks = [kv[2 * p][...].astype(BF16) for p in range(n_pieces)]
    vs = [kv[2 * p + 1][...].astype(BF16) for p in range(n_pieces)]

    sa = [_dot_nt(qa, k) for k in ks]
    sb = [_dot_nt(qb, k) for k in ks]

    def attend(parts):
        m = functools.reduce(jnp.maximum, [jnp.max(x, axis=-1, keepdims=True) for x in parts])
        es = [jnp.exp(x - m) for x in parts]
        tot = functools.reduce(lambda a, b: a + b, [jnp.sum(e, axis=-1, keepdims=True) for e in es])
        pv = functools.reduce(lambda a, b: a + b, [_dot(e.astype(BF16), v) for e, v in zip(es, vs)])
        return pv / tot

    o = attend(sa) - lam * attend(sb)
    o = o * lax.rsqrt(jnp.mean(o * o, axis=-1, keepdims=True) + EPS) * subln_ref[...]
    o_ref[...] = (o * (1.0 - lam_init)).astype(BF16)


def _attn(qn, kn, vb, lam_params, subln, lam_init, nseq, seqlen, tq, j, caches=None):
    hd = 2 * DA_DH
    nq = seqlen // tq
    in_specs = [
        pl.BlockSpec((tq, hd), lambda b, h, qi: (b * nq + qi, h)),
        pl.BlockSpec((None, 4, DA_DH), lambda b, h, qi: (j, 0, 0)),
        pl.BlockSpec((None, 1, hd), lambda b, h, qi: (j, 0, 0)),
        pl.BlockSpec((seqlen, hd), lambda b, h, qi: (b, h)),
        pl.BlockSpec((seqlen, hd), lambda b, h, qi: (b, h)),
    ]
    args = [qn, lam_params, subln.reshape(subln.shape[0], 1, hd), kn, vb]
    n_pieces = 1
    if caches is not None:
        ck, cv = caches
        past = ck.shape[3]
        in_specs += [pl.BlockSpec((None, None, None, past, hd), lambda b, h, qi: (b, j, h, 0, 0))] * 2
        args += [ck, cv]
        n_pieces = 2
    return pl.pallas_call(
        functools.partial(_attn_kernel, n_pieces=n_pieces, lam_init=lam_init),
        out_shape=jax.ShapeDtypeStruct(qn.shape, BF16),
        grid=(nseq, DA_HEADS, nq),
        in_specs=in_specs,
        out_specs=pl.BlockSpec((tq, hd), lambda b, h, qi: (b * nq + qi, h)),
        compiler_params=_params("arbitrary", "arbitrary", "arbitrary"),
        name="attn_lat" if caches is not None else "attn_ctx",
    )(*args)


def _odd_out_kernel(h_ref, gate_ref, ac_ref, al_ref, gu_ref, gv_ref, ws_ref, bt_ref, w_ref, out_ref, cat_scr, *,
                    n_ctx_tiles):
    i = pl.program_id(0)
    tm = TOKEN_TILE
    wa = DA_HEADS * 2 * DA_DH
    gw = LANES

    @pl.when(pl.program_id(1) == 0)
    def _():
        cat_scr[:, 0:wa] = jnp.where(i < n_ctx_tiles, ac_ref[...], al_ref[...])
        for c in range(tm // GM_CHUNK):
            rows = slice(c * GM_CHUNK, (c + 1) * GM_CHUNK)
            for g in range(GM_GROUPS):
                cols = slice(g * gw, (g + 1) * gw)
                u = _gelu_tanh(gu_ref[rows, cols])
                v = _gelu_tanh(gv_ref[rows, cols])
                vn = v * lax.rsqrt(jnp.mean(v * v, axis=-1, keepdims=True) + EPS)
                mixed = _dot(ws_ref[g], vn.astype(BF16)) + bt_ref[:, g:g + 1]
                cat_scr[rows, wa + g * gw:wa + (g + 1) * gw] = (u * mixed).astype(BF16)

    out_ref[...] = h_ref[...] + gate_ref[...] * _dot(cat_scr[...], pltpu.bitcast(w_ref[...], BF16))


def _odd_out(h, mod3, layer, a_ctx, a_lat, z, gm_ws_bf16, gm_b_t, w_out_bf16, n_ctx_tiles, tiles_per_latent, tn=2048):
    n, d = h.shape
    tm = TOKEN_TILE
    wa = DA_HEADS * 2 * DA_DH
    wg = GM_GROUPS * LANES
    nlt = n // tm - n_ctx_tiles

    def gate_idx(i, j):
        return ((layer * 8 + _mod_row(i, n_ctx_tiles, tiles_per_latent)) * N_MOD + 2, 0, j)

    return pl.pallas_call(
        functools.partial(_odd_out_kernel, n_ctx_tiles=n_ctx_tiles),
        out_shape=jax.ShapeDtypeStruct((n, d), F32),
        grid=(n // tm, d // tn),
        in_specs=[
            pl.BlockSpec((tm, tn), lambda i, j: (i, j)),
            pl.BlockSpec((None, 1, tn), gate_idx),
            pl.BlockSpec((tm, wa), lambda i, j: (jnp.minimum(i, n_ctx_tiles - 1), 0)),
            pl.BlockSpec((tm, wa), lambda i, j: (jnp.clip(i - n_ctx_tiles, 0, nlt - 1), 0)),
            pl.BlockSpec((tm, wg), lambda i, j: (i, 3)),
            pl.BlockSpec((tm, wg), lambda i, j: (i, 4)),
            pl.BlockSpec(gm_ws_bf16.shape, lambda i, j: (0, 0, 0)),
            pl.BlockSpec(gm_b_t.shape, lambda i, j: (0, 0)),
            pl.BlockSpec((d // 2, tn), lambda i, j: (0, j)),
        ],
        out_specs=pl.BlockSpec((tm, tn), lambda i, j: (i, j)),
        scratch_shapes=[pltpu.VMEM((tm, d), BF16)],
        compiler_params=_params("arbitrary", "arbitrary"),
        name="odd_out",
    )(h, mod3, a_ctx, a_lat, z, z, gm_ws_bf16, gm_b_t, w_out_bf16)


def _top_values(x, k):
    vals = []
    cur = x
    for r in range(k):
        m = jnp.max(cur, axis=0, keepdims=True)
        vals.append(m)
        if r + 1 < k:
            cur = jnp.where(cur == m, NEG_INF, cur)
    return jnp.concatenate(vals, axis=0)


def _oddeven_merge(lo, hi, r):
    step = r * 2
    if step < hi - lo:
        yield from _oddeven_merge(lo, hi, step)
        yield from _oddeven_merge(lo + r, hi, step)
        yield from [(i, i + r) for i in range(lo + r, hi - r, step)]
    else:
        yield (lo, lo + r)


def _oddeven_merge_sort(lo, hi):
    if hi - lo >= 1:
        mid = lo + (hi - lo) // 2
        yield from _oddeven_merge_sort(lo, mid)
        yield from _oddeven_merge_sort(mid + 1, hi)
        yield from _oddeven_merge(lo, hi, 1)


def _top_sorted(x):
    n = len(x)
    assert n == PEER_TOPK

    def exchange(v, i, j):
        v[i], v[j] = jnp.maximum(v[i], v[j]), jnp.minimum(v[i], v[j])

    v = list(x)
    for i, j in _oddeven_merge_sort(0, n - 1):
        exchange(v, i, j)
    shift = SUBLANES // 2
    while shift:
        w = [pltpu.roll(t, SUBLANES - shift, 0) for t in v]
        v = [jnp.maximum(v[i], w[n - 1 - i]) for i in range(n)]
        d = n // 2
        while d:
            for i in range(n):
                if not i & d:
                    exchange(v, i, i + d)
            d //= 2
        shift //= 2
    return jnp.concatenate([t[0:1, :] for t in v], axis=0)


def _dup_bf16(x):
    b = pltpu.bitcast(x.astype(BF16).astype(F32), jnp.uint32)
    return pltpu.bitcast(b | (b >> 16), F32)


def _peer_route_kernel(q_ref, sk_ref, rank_ref, e2_ref, cnt_ref, c_ref, s_scr):
    half = PEER_NKEYS
    k = PEER_TOPK
    n_slabs = half // SUBLANES
    for h in range(PEER_HEADS):
        q1 = pltpu.bitcast(q_ref[:, 2 * h * half:(2 * h + 1) * half], BF16)
        q2 = pltpu.bitcast(q_ref[:, (2 * h + 1) * half:(2 * h + 2) * half], BF16)
        s_scr[0] = _dot_nt(sk_ref[2 * h], q1)
        s_scr[1] = _dot_nt(sk_ref[2 * h + 1], q2)

        def lane_tile(t, carry):
            lanes = pl.ds(pl.multiple_of(t * LANES, LANES), LANES)
            s1 = s_scr[0, :, lanes]
            s2 = s_scr[1, :, lanes]
            a_all = _top_sorted([s1[SUBLANES * i:SUBLANES * (i + 1)] for i in range(n_slabs)])
            b_all = _top_sorted([s2[SUBLANES * i:SUBLANES * (i + 1)] for i in range(n_slabs)])
            b = [b_all[r:r + 1] for r in range(k)]
            rows = [a_all[r:r + 1] + b_all[0:k // (r + 1)] for r in range(k)]
            nrows = sum(k // (r + 1) for r in range(k))
            rows.append(jnp.full((-nrows % SUBLANES, LANES), NEG_INF, F32))
            top = _top_values(jnp.concatenate(rows, axis=0), k)
            tau = top[k - 1:k]
            z = jnp.sum(jnp.exp(top - top[0:1]), axis=0, keepdims=True)
            cnt = jnp.zeros(s1.shape, F32)
            rank2 = jnp.zeros(s2.shape, F32)
            for r in range(k):
                cnt = jnp.where(s1 + b[r] >= tau, float(r + 1), cnt)
                rank2 = jnp.where(b[r] > s2, float(r + 1), rank2)
            rank_ref[h, :, lanes] = pltpu.bitcast(rank2.astype(BF16), F32)
            e2_ref[h, :, lanes] = pltpu.bitcast(jnp.exp(s2 - b[0]).astype(BF16), F32)
            cnt_ref[h, :, lanes] = _dup_bf16(cnt)
            c_ref[h, :, lanes] = _dup_bf16(jnp.exp(s1 - a_all[0:1]) / z)
            return carry

        lax.fori_loop(0, s_scr.shape[2] // LANES, lane_tile, 0, unroll=2)


def _peer_route(qp, subkeys_bf16, tm=512):
    n = 2 * qp.shape[0]
    def out(rows):
        return (jax.ShapeDtypeStruct((PEER_HEADS, rows, n), F32),
                pl.BlockSpec((PEER_HEADS, rows, tm), lambda i: (0, 0, i)))

    outs = [out(PEER_NKEYS // 2)] * 2 + [out(PEER_NKEYS)] * 2
    return pl.pallas_call(
        _peer_route_kernel,
        out_shape=[o[0] for o in outs],
        grid=(n // tm,),
        in_specs=[
            pl.BlockSpec((tm // 2, qp.shape[1]), lambda i: (i, 0)),
            pl.BlockSpec(subkeys_bf16.shape, lambda i: (0, 0, 0)),
        ],
        out_specs=[o[1] for o in outs],
        scratch_shapes=[pltpu.VMEM((2, PEER_NKEYS, tm), F32)],
        compiler_params=_params("arbitrary"),
        name="peer_route",
    )(qp, subkeys_bf16)


def _peer_dense_kernel(xnt_ref, h_ref, gate_ref, u_ref, vt_ref, rank_ref, e2_ref, cnt_ref, c_ref, *rest, te, n_tiles,
                       split_tiles):
    n_out = 1 if split_tiles is None else 2
    out_refs = rest[:n_out]
    act0_scr, act1_scr, wa0_scr, wa1_scr, acc_scr = rest[n_out:]
    s = pl.program_id(0)
    n_pairs = pl.num_programs(0) - 2
    nkeys = PEER_NKEYS
    per = te // nkeys
    tm = acc_scr.shape[1]
    half = tm // 2
    assert per % 4 == 0
    pack = 2 * SUBLANES

    @pl.when(s == 0)
    def _():
        act1_scr[...] = jnp.zeros_like(act1_scr)
        wa0_scr[...] = jnp.zeros_like(wa0_scr)

    live = jnp.logical_and(s >= 1, s <= n_pairs)
    t2 = jnp.clip(s - 1, 0, n_pairs - 1) % n_tiles
    pair3 = jnp.clip(s - 2, 0, n_pairs - 1)
    t3 = pair3 % n_tiles
    tok3 = pair3 // n_tiles

    @pl.when(t3 == 0)
    def _():
        acc_scr[...] = jnp.zeros_like(acc_scr)

    def step(act_w, act_r, wa_w, wa_r):
        def stage1(c):
            cols = slice(c * half, (c + 1) * half)
            u = pltpu.bitcast(u_ref[0:te // 2, :], BF16)
            act_w[0:te, cols] = _dot(u, pltpu.bitcast(xnt_ref[:, cols], BF16))

        d_rows = acc_scr.shape[0] // PEER_V_ROW_PIECES

        def stage3(c, dq):
            cols = slice(c * half, (c + 1) * half)
            rows = slice(dq * d_rows, (dq + 1) * d_rows)
            vt = pltpu.bitcast(vt_ref[dq * d_rows // 2:(dq + 1) * d_rows // 2, :], BF16)
            acc_scr[rows, cols] += _dot(vt, wa_r[:, cols])

        mxu_work = []
        for c in range(2):
            mxu_work.append(functools.partial(stage1, c))
            mxu_work += [functools.partial(stage3, c, dq) for dq in range(PEER_V_ROW_PIECES)]
        n_lt = tm // LANES
        n_valu = per * n_lt
        issued = 0
        for ii in range(per):
            r = t2 * per + ii
            cnt_rows = [jnp.where(live, cnt_ref[h, pl.ds(r, 1), :], 0.0) for h in range(PEER_HEADS)]
            c_rows = [c_ref[h, pl.ds(r, 1), :] for h in range(PEER_HEADS)]
            for lt in range(n_lt):
                while issued < len(mxu_work) and (issued * n_valu) // len(mxu_work) <= ii * n_lt + lt:
                    mxu_work[issued]()
                    issued += 1
                lanes = slice(lt * LANES, (lt + 1) * LANES)
                cnts = [pltpu.bitcast(jnp.broadcast_to(x[:, lanes], (SUBLANES, LANES)), BF16) for x in cnt_rows]
                ccs = [pltpu.bitcast(jnp.broadcast_to(x[:, lanes], (SUBLANES, LANES)), BF16) for x in c_rows]
                for sb in range(nkeys // pack):
                    words = slice(sb * SUBLANES, (sb + 1) * SUBLANES)
                    rows = slice(ii * nkeys + sb * pack, ii * nkeys + (sb + 1) * pack)
                    w = None
                    for h in range(PEER_HEADS):
                        e2 = pltpu.bitcast(e2_ref[h, words, lanes], BF16)
                        rank = pltpu.bitcast(rank_ref[h, words, lanes], BF16)
                        wh = jnp.where(rank < cnts[h], e2 * ccs[h], jnp.zeros_like(e2))
                        w = wh if w is None else w + wh
                    wa_w[rows, lanes] = w * _gelu_tanh_lowp(act_r[rows, lanes].astype(BF16))
        assert issued == len(mxu_work)

    @pl.when(s % 2 == 0)
    def _():
        step(act0_scr, act1_scr, wa1_scr, wa0_scr)

    @pl.when(s % 2 == 1)
    def _():
        step(act1_scr, act0_scr, wa0_scr, wa1_scr)

    @pl.when(jnp.logical_and(t3 == n_tiles - 1, s >= 2))
    def _():
        res = h_ref[...] + gate_ref[...] * acc_scr[...].T
        if split_tiles is None:
            out_refs[0][...] = res
        else:
            @pl.when(tok3 < split_tiles)
            def _():
                out_refs[0][...] = res

            @pl.when(tok3 >= split_tiles)
            def _():
                out_refs[1][...] = res


def _peer_dense(xnt, h, mod3, layer, u_packed, vt_packed, rank2, e2, cnt, c, n_ctx_tiles, tiles_per_latent, tm=512,
                te=PACK_ROWS, split_out=False):
    n, d = h.shape
    n_tiles = 2 * u_packed.shape[0] // te
    per = tm // TOKEN_TILE
    n_pairs = (n // tm) * n_tiles

    def tok(lag):
        return lambda t: jnp.clip(t - lag, 0, n_pairs - 1) // n_tiles

    def exp(lag):
        return lambda t: jnp.clip(t - lag, 0, n_pairs - 1) % n_tiles

    tok1, tok2, tok3, exp1, exp3 = tok(0), tok(1), tok(2), exp(0), exp(2)
    if split_out:
        split_tiles = n_ctx_tiles // per
        n_ctx = split_tiles * tm
        out_shape = [jax.ShapeDtypeStruct((n_ctx, d), F32), jax.ShapeDtypeStruct((n - n_ctx, d), F32)]
        out_specs = [pl.BlockSpec((tm, d), lambda t: (jnp.minimum(tok3(t), split_tiles - 1), 0)),
                     pl.BlockSpec((tm, d), lambda t: (jnp.maximum(tok3(t) - split_tiles, 0), 0))]
    else:
        split_tiles = None
        out_shape = jax.ShapeDtypeStruct((n, d), F32)
        out_specs = pl.BlockSpec((tm, d), lambda t: (tok3(t), 0))

    def gate_idx(t):
        return ((layer * 8 + _mod_row(tok3(t) * per, n_ctx_tiles, tiles_per_latent)) * N_MOD + 5, 0, 0)

    rspec = pl.BlockSpec((PEER_HEADS, PEER_NKEYS, tm), lambda t: (0, 0, tok2(t)))
    pspec = pl.BlockSpec((PEER_HEADS, PEER_NKEYS // 2, tm), lambda t: (0, 0, tok2(t)))
    return pl.pallas_call(
        functools.partial(_peer_dense_kernel, te=te, n_tiles=n_tiles, split_tiles=split_tiles),
        out_shape=out_shape,
        grid=(n_pairs + 2,),
        in_specs=[
            pl.BlockSpec((d // 2, tm), lambda t: (0, tok1(t))),
            pl.BlockSpec((tm, d), lambda t: (tok3(t), 0), pipeline_mode=pl.Buffered(1)),
            pl.BlockSpec((None, 1, d), gate_idx),
            pl.BlockSpec((te // 2, d), lambda t: (exp1(t), 0)),
            pl.BlockSpec((None, d // 2, te), lambda t: (exp3(t), 0, 0)),
            pspec, pspec, rspec, rspec,
        ],
        out_specs=out_specs,
        scratch_shapes=[pltpu.VMEM((te, tm), F32), pltpu.VMEM((te, tm), F32), pltpu.VMEM((te, tm), BF16),
                        pltpu.VMEM((te, tm), BF16), pltpu.VMEM((d, tm), F32)],
        compiler_params=_params("arbitrary"),
        name="peer_dense",
    )(xnt, h, mod3, u_packed, vt_packed, rank2, e2, cnt, c)


def kernel(x_prompt, x_sample, state_mlstm_C, state_mlstm_n, state_mlstm_m, cache_da_k, cache_da_v, c, c_ctx, norm_mix, norm_ffn, w_mod, b_mod, w_in_even, b_gate_even, mlstm_gain, pool_w, pool_scale, w_out_even, w_in_odd, qk_gain, da_lambda, da_subln, gm_ws, gm_b, w_out_odd, peer_wq, peer_subkeys, peer_u, peer_v):
    nb, s_len, d = x_prompt.shape
    nbd, t_len, _ = x_sample.shape
    depth = w_mod.shape[0]
    tm = TOKEN_TILE
    assert s_len == tm and t_len % tm == 0 and nbd <= 7 and t_len % GRID_W == 0
    n_ctx = nb * s_len
    n_lat = nbd * t_len
    n_ctx_tiles = n_ctx // tm
    tiles_per_latent = t_len // tm
    n_even = (depth + 1) // 2
    n_odd = depth // 2

    h = (x_prompt.reshape(n_ctx, d), x_sample.reshape(n_lat, d))
    cond8 = jnp.concatenate([c_ctx[None], c, jnp.zeros((7 - nbd, d), F32)], axis=0)
    mod3 = _adaln(cond8, w_mod, b_mod).reshape(depth * 8 * N_MOD, 1, d)
    rope = _rope_tables(t_len)

    new_c, new_n, new_m, new_k, new_v = [], [], [], [], []
    wm = MLSTM_HEADS * MLSTM_DH
    for l in range(depth):
        j = l // 2
        if l % 2 == 0:
            n_main = 5 * wm
            z, gates, gates_t = _inproj(h, norm_mix[l], mod3, l, 0, _pack_rows(w_in_even, j, ncols=n_main),
                                        n_ctx_tiles, tiles_per_latent, w_gates=w_in_even[j, :, n_main:],
                                        b_gates=b_gate_even[j],
                                        name="inproj_even")
            hs_ctx, cc, cn, cm = _mlstm(z, gates, gates_t, 0, nb, s_len, 0, emit_state=True)
            (hs_lat,) = _mlstm(z, gates, gates_t, n_ctx, nbd, t_len, j,
                               init=(state_mlstm_C, state_mlstm_n, state_mlstm_m))
            new_c.append(cc)
            new_n.append(cn)
            new_m.append(cm[:, :, :, 0, :])
            h = _even_out(h, mod3, l, hs_ctx, hs_lat, z, mlstm_gain[j], pool_w[j].astype(BF16), pool_scale[j],
                          _pack_rows(w_out_even, j), n_ctx_tiles, tiles_per_latent, s_len, t_len)
        else:
            lam_init = 0.8 - 0.6 * math.exp(-0.3 * l)
            (z,) = _inproj(h, norm_mix[l], mod3, l, 0, _pack_rows(w_in_odd, j), n_ctx_tiles, tiles_per_latent,
                           name="inproj_odd")
            gain2 = jnp.tile(qk_gain[j], (1, 2))
            cache_shape = (nb, 1, DA_HEADS, s_len, 2 * DA_DH)
            qn_c, kn_c, vb_c, nk, nv = _qk(z, gain2, 0, n_ctx, 0, cache_shape=cache_shape)
            qn_l, kn_l, vb_l = _qk(z, gain2, n_ctx, n_lat, 0, rope_tables=rope, seqlen=t_len)
            new_k.append(nk)
            new_v.append(nv)
            a_ctx = _attn(qn_c, kn_c, vb_c, da_lambda, da_subln, lam_init, nb, s_len, s_len, j)
            a_lat = _attn(qn_l, kn_l, vb_l, da_lambda, da_subln, lam_init, nbd, t_len, tm, j,
                          caches=(cache_da_k, cache_da_v))
            h = _odd_out(h, mod3, l, a_ctx, a_lat, z, gm_ws[j].astype(BF16), gm_b[j].T, _pack_rows(w_out_odd, j),
                         n_ctx_tiles, tiles_per_latent)
        qp, xnt = _inproj(h, norm_ffn[l], mod3, l, 3, _pack_rows(peer_wq, l), n_ctx_tiles, tiles_per_latent,
                          emit_xn=True, pack_out=True, name="inproj_peer")
        sk = peer_subkeys[l].reshape(2 * PEER_HEADS, PEER_NKEYS, PEER_NKEYS).astype(BF16)
        rank2, e2, cnt, cw = _peer_route(qp, sk)
        h = _peer_dense(xnt, h, mod3, l, _pack_rows(peer_u, l), _pack_rows(peer_v, l, transpose=True), rank2,
                        e2, cnt, cw,
                        n_ctx_tiles, tiles_per_latent, split_out=(l == depth - 1))

    def join(parts):
        return parts[0] if len(parts) == 1 else jnp.concatenate(parts, axis=1)

    return (h[0].reshape(nb, s_len, d), h[1].reshape(nbd, t_len, d), join(new_c), join(new_n), join(new_m),
            join(new_k), join(new_v))
```

```python
import functools
import math

import jax
import jax.numpy as jnp
from jax import lax
from jax.experimental import pallas as pl
from jax.experimental.pallas import tpu as pltpu

F32 = jnp.float32
BF16 = jnp.bfloat16

N_MOD = 6
EPS = 1e-6
TOKEN_TILE = 256
LANES = 128
SUBLANES = 8
VMEM_LIMIT_BYTES = 56 * 1024 * 1024

MLSTM_HEADS = 4
MLSTM_DH = 256
MLSTM_CHUNK = 128
POOL_WINDOWS = (2, 4, 8, 16)
POOL_HALO = 8
DA_HEADS = 8
DA_DH = 64
GRID_W = 64
ROPE_BASE = 10000.0
GM_GROUPS = 8
GM_CHUNK = 128
PEER_HEADS = 8
PEER_NKEYS = 128
PEER_TOPK = 16
PEER_V_ROW_PIECES = 8
NEG_INF = float("-inf")


def _params(*sem):
    return pltpu.CompilerParams(dimension_semantics=sem, vmem_limit_bytes=VMEM_LIMIT_BYTES)


def _dot(a, b):
    return jnp.dot(a, b, preferred_element_type=F32)


def _dot_nt(a, b):
    return lax.dot_general(a, b, (((1,), (1,)), ((), ())), preferred_element_type=F32)


def _dot_tn(a, b):
    return lax.dot_general(a, b, (((0,), (0,)), ((), ())), preferred_element_type=F32)


def _split2(x):
    hi = x.astype(BF16)
    lo = (x - hi.astype(F32)).astype(BF16)
    return hi, lo


def _split3(x):
    hi = x.astype(BF16)
    r = x - hi.astype(F32)
    mid = r.astype(BF16)
    lo = (r - mid.astype(F32)).astype(BF16)
    return hi, mid, lo


def _dot3(a, b, dot=_dot):
    ah, al = _split2(a)
    bh, bl = _split2(b)
    return dot(ah, bh) + (dot(ah, bl) + dot(al, bh))


def _gelu_tanh(x):
    c = math.sqrt(2.0 / math.pi)
    half = 0.5 * x
    return half + half * jnp.tanh(x * (c + (0.044715 * c) * (x * x)))


def _gelu_tanh_lowp(x):
    c = -2.0 * math.sqrt(2.0 / math.pi)
    return x / (1.0 + jnp.exp(x * (c + (0.044715 * c) * (x * x))))


def _log_sigmoid(x):
    return -(jnp.maximum(-x, 0.0) + jnp.log1p(jnp.exp(-jnp.abs(x))))


def _mod_row(i, n_ctx_tiles, tiles_per_latent):
    return jnp.where(i < n_ctx_tiles, 0, 1 + (i - n_ctx_tiles) // tiles_per_latent)


def _pack_kernel(x_ref, o_ref, *, transpose):
    x = x_ref[...]
    if transpose:
        x = x.T
    o_ref[...] = pltpu.bitcast(x.astype(BF16), F32)


PACK_ROWS = 512


def _pack_rows(x, layer, ncols=None, transpose=False):
    _, r, c = x.shape
    c = c if ncols is None else ncols
    tr, tc = PACK_ROWS, (2048 if c % 2048 == 0 else 1024)
    assert r % tr == 0 and c % tc == 0
    if transpose:
        out_shape, out_spec = (r // tr, c // 2, tr), pl.BlockSpec((None, tc // 2, tr), lambda i, j: (i, j, 0))
    else:
        out_shape, out_spec = (r // 2, c), pl.BlockSpec((tr // 2, tc), lambda i, j: (i, j))
    return pl.pallas_call(
        functools.partial(_pack_kernel, transpose=transpose),
        out_shape=jax.ShapeDtypeStruct(out_shape, F32),
        grid=(r // tr, c // tc),
        in_specs=[pl.BlockSpec((None, tr, tc), lambda i, j: (layer, i, j))],
        out_specs=out_spec,
        compiler_params=_params("arbitrary", "arbitrary"),
        name="pack_t" if transpose else "pack",
    )(x)


def _adaln_kernel(cond_ref, w_ref, b_ref, o_ref):
    c = cond_ref[...]
    s = c * jax.nn.sigmoid(c)
    o_ref[...] = _dot3(s, w_ref[...]) + b_ref[...]


def _adaln(cond8, w_mod, b_mod):
    depth, d, dout = w_mod.shape
    tn = 1024
    return pl.pallas_call(
        _adaln_kernel,
        out_shape=jax.ShapeDtypeStruct((depth, 8, dout), F32),
        grid=(depth, dout // tn),
        in_specs=[
            pl.BlockSpec((8, d), lambda l, j: (0, 0)),
            pl.BlockSpec((None, d, tn), lambda l, j: (l, 0, j)),
            pl.BlockSpec((None, 1, tn), lambda l, j: (l, 0, j)),
        ],
        out_specs=pl.BlockSpec((None, 8, tn), lambda l, j: (l, 0, j)),
        compiler_params=_params("arbitrary", "arbitrary"),
        name="adaln",
    )(cond8, w_mod, b_mod.reshape(depth, 1, dout))


def _stream(h, tm, cols, col_of=lambda j: 0, single_buffer_split=False):
    if not isinstance(h, tuple):
        return [h], [pl.BlockSpec((tm, cols), lambda i, j: (i, col_of(j)))], h.shape[0], None
    hc, hl = h
    nct = hc.shape[0] // tm
    kw = dict(pipeline_mode=pl.Buffered(1)) if single_buffer_split else {}
    specs = [pl.BlockSpec((tm, cols), lambda i, j: (jnp.minimum(i, nct - 1), col_of(j)), **kw),
             pl.BlockSpec((tm, cols), lambda i, j: (jnp.maximum(i - nct, 0), col_of(j)), **kw)]
    return [hc, hl], specs, hc.shape[0] + hl.shape[0], nct


def _load_stream(h_refs, split_tiles):
    if split_tiles is None:
        return h_refs[0][...]
    return jnp.where(pl.program_id(0) < split_tiles, h_refs[0][...], h_refs[1][...])


def _inproj_kernel(*refs, with_gates, emit_xn, split_tiles, pack_out):
    nh = 1 if split_tiles is None else 2
    h_refs = refs[:nh]
    gain_ref, shift_ref, scale_ref, w_ref = refs[nh:nh + 4]
    rest = list(refs[nh + 4:])
    if with_gates:
        wg2_ref, bg_ref = rest[:2]
        rest = rest[2:]
    z_ref = rest.pop(0)
    if with_gates:
        g_ref, gt_ref = rest[:2]
        rest = rest[2:]
    if emit_xn:
        xn_out_ref = rest.pop(0)
    xn_scr = rest.pop(0)

    @pl.when(pl.program_id(1) == 0)
    def _():
        x = _load_stream(h_refs, split_tiles)
        xn = x * lax.rsqrt(jnp.mean(x * x, axis=-1, keepdims=True) + EPS) * gain_ref[...]
        xn = xn * (1.0 + scale_ref[...]) + shift_ref[...]
        xb = xn.astype(BF16)
        xn_scr[...] = xb
        if emit_xn:
            xn_out_ref[...] = pltpu.bitcast(xn.T.astype(BF16), F32)
        if with_gates:
            ng = bg_ref.shape[1]
            w2 = wg2_ref[...]
            w2_hi = w2.astype(BF16)
            lane = lax.broadcasted_iota(jnp.int32, w2.shape, 1)
            w_cat = jnp.where(lane < ng, w2_hi, (w2 - w2_hi.astype(F32)).astype(BF16))
            xl = (xn - xb.astype(F32)).astype(BF16)
            p = _dot(xb, w_cat)
            g = p[:, :ng] + (p[:, ng:] + _dot(xl, w2_hi)[:, :ng]) + bg_ref[...]
            g_ref[...] = g
            eye = (lax.broadcasted_iota(jnp.int32, (ng, ng), 0)
                   == lax.broadcasted_iota(jnp.int32, (ng, ng), 1)).astype(F32).astype(BF16)
            g3 = _split3(g)
            gt_ref[...] = _dot_nt(eye, g3[0]) + (_dot_nt(eye, g3[1]) + _dot_nt(eye, g3[2]))

    z = _dot(xn_scr[...], pltpu.bitcast(w_ref[...], BF16))
    z_ref[...] = pltpu.bitcast(z.astype(BF16), F32) if pack_out else z


def _inproj(h, gain, mod3, layer, mod_base, w_packed, n_ctx_tiles, tiles_per_latent, w_gates=None, b_gates=None,
            emit_xn=False, pack_out=False, tn=1024, name="inproj"):
    d, dout = 2 * w_packed.shape[0], w_packed.shape[1]
    tm = TOKEN_TILE * math.gcd(4, n_ctx_tiles, tiles_per_latent)
    per = tm // TOKEN_TILE
    with_gates = w_gates is not None
    h_arrays, h_specs, n, split_tiles = _stream(h, tm, d, single_buffer_split=True)

    def mod_idx(off):
        return lambda i, j: ((layer * 8 + _mod_row(i * per, n_ctx_tiles, tiles_per_latent)) * N_MOD + mod_base + off,
                             0, 0)

    in_specs = h_specs + [
        pl.BlockSpec((1, d), lambda i, j: (0, 0)),
        pl.BlockSpec((None, 1, d), mod_idx(0)),
        pl.BlockSpec((None, 1, d), mod_idx(1)),
        pl.BlockSpec((d // 2, tn), lambda i, j: (0, j)),
    ]
    args = h_arrays + [gain.reshape(1, d), mod3, mod3, w_packed]
    zr = 2 if pack_out else 1
    out_shape = [jax.ShapeDtypeStruct((n // zr, dout), F32)]
    out_specs = [pl.BlockSpec((tm // zr, tn), lambda i, j: (i, j))]
    if with_gates:
        ng = w_gates.shape[1]
        in_specs += [pl.BlockSpec((d, 2 * ng), lambda i, j: (0, 0)), pl.BlockSpec((1, ng), lambda i, j: (0, 0))]
        args += [jnp.concatenate([w_gates, w_gates], axis=1), b_gates.reshape(1, ng)]
        out_shape += [jax.ShapeDtypeStruct((n, ng), F32), jax.ShapeDtypeStruct((ng, n), F32)]
        out_specs += [pl.BlockSpec((tm, ng), lambda i, j: (i, 0)), pl.BlockSpec((ng, tm), lambda i, j: (0, i))]
    if emit_xn:
        out_shape.append(jax.ShapeDtypeStruct((d // 2, n), F32))
        out_specs.append(pl.BlockSpec((d // 2, tm), lambda i, j: (0, i)))
    return pl.pallas_call(
        functools.partial(_inproj_kernel, with_gates=with_gates, emit_xn=emit_xn, split_tiles=split_tiles,
                          pack_out=pack_out),
        out_shape=out_shape,
        grid=(n // tm, dout // tn),
        in_specs=in_specs,
        out_specs=out_specs,
        scratch_shapes=[pltpu.VMEM((tm, d), BF16)],
        compiler_params=_params("arbitrary", "arbitrary"),
        name=name,
    )(*args)


def _mlstm_kernel(*refs, has_init, emit_state, nchunks):
    q_ref, k_ref, v_ref, g_ref, gt_ref = refs[:5]
    rest = list(refs[5:])
    if has_init:
        c0_ref, n0_ref, m0_ref = rest[:3]
        rest = rest[3:]
    hs_ref = rest.pop(0)
    if emit_state:
        cout_ref, nout_ref, mout_ref = rest[:3]
        rest = rest[3:]
    c_scr, n_scr, m_scr = rest
    nh, dh, L = MLSTM_HEADS, MLSTM_DH, MLSTM_CHUNK
    d = pl.program_id(0)
    s = pl.program_id(2)

    @pl.when(s == 0)
    def _():
        if has_init:
            c_scr[...] = c0_ref[...]
            n_scr[...] = n0_ref[...]
            m_scr[...] = m0_ref[...]
        else:
            c_scr[...] = jnp.zeros_like(c_scr)
            n_scr[...] = jnp.zeros_like(n_scr)
            m_scr[...] = jnp.zeros_like(m_scr)

    row = lax.broadcasted_iota(jnp.int32, (L, L), 0)
    col = lax.broadcasted_iota(jnp.int32, (L, L), 1)
    sgn = jnp.where(d == 0, 1, -1)
    mask = (row - col) * sgn >= 0
    maskb = jnp.where(mask, 1.0, 0.0).astype(BF16)

    g = g_ref[...]
    gt = gt_ref[...]
    fwd = d == 0
    i_col = jnp.where(fwd, g[:, 0:nh], g[:, nh:2 * nh])
    f_col = _log_sigmoid(jnp.where(fwd, g[:, 2 * nh:3 * nh], g[:, 3 * nh:4 * nh]))
    i_row = jnp.where(fwd, gt[0:nh], gt[nh:2 * nh])
    f_row = _log_sigmoid(jnp.where(fwd, gt[2 * nh:3 * nh], gt[3 * nh:4 * nh]))
    fc = _split3(f_col)
    b_col = _dot(maskb, fc[0]) + (_dot(maskb, fc[1]) + _dot(maskb, fc[2]))
    fr = _split3(f_row)
    b_row = _dot_nt(fr[0], maskb) + (_dot_nt(fr[1], maskb) + _dot_nt(fr[2], maskb))
    btot_col = jnp.sum(f_col, axis=0, keepdims=True)
    m_all = m_scr[...]

    heads = []
    for h in range(nh):
        sl = slice(h * dh, (h + 1) * dh)
        qh = q_ref[:, sl].astype(BF16)
        kf = k_ref[:, sl] * (dh ** -0.5)
        kh = kf.astype(BF16)
        vh = v_ref[:, sl].astype(BF16)
        heads.append(dict(sl=sl, qh=qh, kf=kf, kh=kh, vh=vh, qk=_dot_nt(qh, kh),
                          qc=_dot(qh, c_scr[h].astype(BF16))))
    for h, t in enumerate(heads):
        b_c = b_col[:, h:h + 1]
        b_r = b_row[h:h + 1, :]
        i_c = i_col[:, h:h + 1]
        i_r = i_row[h:h + 1, :]
        m = m_all[:, h:h + 1]
        btot = btot_col[:, h:h + 1]
        dm = jnp.where(mask, b_c - b_r + i_r, NEG_INF)
        inter = b_c + m
        m_t = jnp.maximum(inter, jnp.max(dm, axis=1, keepdims=True))
        g_c = btot - b_c + i_c
        g_r = btot - b_r + i_r
        m_new = jnp.maximum(btot + m, jnp.max(g_c, axis=0, keepdims=True))
        t.update(m_t=m_t, w=jnp.exp(dm - m_t), a=jnp.exp(inter - m_t), m_new=m_new,
                 decay=jnp.exp(btot + m - m_new), ws_c=jnp.exp(g_c - m_new), ws_r=jnp.exp(g_r - m_new))
    for h, t in enumerate(heads):
        sc = t["qk"] * t["w"]
        num = t["a"] * t["qc"] + _dot(sc.astype(BF16), t["vh"])
        nb = n_scr[h:h + 1, :].astype(BF16).astype(F32)
        qn = jnp.sum(t["qh"].astype(F32) * nb, axis=1, keepdims=True)
        den = t["a"] * qn + jnp.sum(sc, axis=1, keepdims=True)
        hs_ref[:, t["sl"]] = num / jnp.maximum(jnp.abs(den), jnp.exp(-t["m_t"]))
        kw = (t["kf"] * t["ws_c"]).astype(BF16)
        c_scr[h] = t["decay"] * c_scr[h] + _dot_tn(kw, t["vh"])
        n_scr[h:h + 1, :] = t["decay"] * n_scr[h:h + 1, :] + _dot(t["ws_r"].astype(BF16), t["kh"])
    m_scr[...] = jnp.concatenate([t["m_new"] for t in heads], axis=1)

    if emit_state:
        @pl.when(s == nchunks - 1)
        def _():
            cout_ref[...] = c_scr[...]
            nout_ref[...] = n_scr[...]
            mout_ref[...] = m_scr[...]


def _mlstm(z, gates, gates_t, tok_off, nseq, seqlen, j, init=None, emit_state=False, n_even=1):
    nh, dh, L = MLSTM_HEADS, MLSTM_DH, MLSTM_CHUNK
    w = nh * dh
    nchunks = seqlen // L
    off = tok_off // L

    def chunk(d, b, s):
        return off + b * nchunks + jnp.where(d == 0, s, nchunks - 1 - s)

    in_specs = [
        pl.BlockSpec((L, w), lambda d, b, s: (chunk(d, b, s), 0)),
        pl.BlockSpec((L, w), lambda d, b, s: (chunk(d, b, s), 1)),
        pl.BlockSpec((L, w), lambda d, b, s: (chunk(d, b, s), 2)),
        pl.BlockSpec((L, 4 * nh), lambda d, b, s: (chunk(d, b, s), 0)),
        pl.BlockSpec((4 * nh, L), lambda d, b, s: (0, chunk(d, b, s))),
    ]
    args = [z, z, z, gates, gates_t]
    has_init = init is not None
    if has_init:
        c0, n0, m0 = init
        in_specs += [
            pl.BlockSpec((None, None, None, nh, dh, dh), lambda d, b, s: (b, j, d, 0, 0, 0)),
            pl.BlockSpec((None, None, None, nh, dh), lambda d, b, s: (b, j, d, 0, 0)),
            pl.BlockSpec((None, None, None, 1, nh), lambda d, b, s: (b, j, d, 0, 0)),
        ]
        args += [c0, n0, m0.reshape(m0.shape[:3] + (1, nh))]
    out_shape = [jax.ShapeDtypeStruct((2, nseq * seqlen, w), F32)]
    out_specs = [pl.BlockSpec((None, L, w), lambda d, b, s: (d, chunk(d, b, s) - off, 0))]
    if emit_state:
        out_shape += [
            jax.ShapeDtypeStruct((nseq, n_even, 2, nh, dh, dh), F32),
            jax.ShapeDtypeStruct((nseq, n_even, 2, nh, dh), F32),
            jax.ShapeDtypeStruct((nseq, n_even, 2, 1, nh), F32),
        ]
        out_specs += [
            pl.BlockSpec((None, None, None, nh, dh, dh), lambda d, b, s: (b, j, d, 0, 0, 0)),
            pl.BlockSpec((None, None, None, nh, dh), lambda d, b, s: (b, j, d, 0, 0)),
            pl.BlockSpec((None, None, None, 1, nh), lambda d, b, s: (b, j, d, 0, 0)),
        ]
    return pl.pallas_call(
        functools.partial(_mlstm_kernel, has_init=has_init, emit_state=emit_state, nchunks=nchunks),
        out_shape=out_shape,
        grid=(2, nseq, nchunks),
        in_specs=in_specs,
        out_specs=out_specs,
        scratch_shapes=[pltpu.VMEM((nh, dh, dh), F32), pltpu.VMEM((nh, dh), F32), pltpu.VMEM((1, nh), F32)],
        compiler_params=_params("arbitrary", "arbitrary", "arbitrary"),
        name="mlstm_ctx" if emit_state else "mlstm_lat",
    )(*args)


def _even_out_kernel(*refs, n_ctx_tiles, tiles_per_latent, ctx_len, lat_len, split_tiles):
    nhr = 1 if split_tiles is None else 2
    h_refs = refs[:nhr]
    (gate_ref, hsc_ref, hsl_ref, o_ref, p_ref, pprev_ref, pnext_ref, gain_ref, pw_ref, ps_ref, w_ref, out_ref,
     cat_scr) = refs[nhr:]
    i = pl.program_id(0)
    tm = TOKEN_TILE
    nh, dh = MLSTM_HEADS, MLSTM_DH
    wm = nh * dh

    @pl.when(pl.program_id(1) == 0)
    def _():
        is_ctx = i < n_ctx_tiles
        hs = jnp.where(is_ctx, hsc_ref[0] + hsc_ref[1], hsl_ref[0] + hsl_ref[1])
        for h in range(nh):
            sl = slice(h * dh, (h + 1) * dh)
            x = hs[:, sl]
            y = x * lax.rsqrt(jnp.mean(x * x, axis=-1, keepdims=True) + EPS) * gain_ref[:, sl]
            cat_scr[:, sl] = (y * jax.nn.sigmoid(o_ref[:, sl])).astype(BF16)

        tiles_ctx = ctx_len // tm
        pos = jnp.where(is_ctx, i % tiles_ctx, (i - n_ctx_tiles) % tiles_per_latent)
        ntile = jnp.where(is_ctx, tiles_ctx, tiles_per_latent)
        seqlen = jnp.where(is_ctx, ctx_len, lat_len)
        x = p_ref[...]
        prev = jnp.where(pos > 0, pprev_ref[...], 0.0)
        nxt = jnp.where(pos < ntile - 1, pnext_ref[...], 0.0)
        pad = jnp.zeros((LANES - 2 * POOL_HALO, x.shape[1]), F32)
        xcat = jnp.concatenate([prev, x, nxt, pad], axis=0)
        xh, xl = _split2(xcat)
        t = lax.broadcasted_iota(jnp.int32, (tm, tm + LANES), 0)
        sidx = lax.broadcasted_iota(jnp.int32, (tm, tm + LANES), 1) - POOL_HALO
        tpos = pos * tm + lax.broadcasted_iota(jnp.int32, (tm, 1), 0)
        gw = wm // len(POOL_WINDOWS)
        for gi, win in enumerate(POOL_WINDOWS):
            sl = slice(gi * gw, (gi + 1) * gw)
            band = jnp.where((sidx >= t - win // 2) & (sidx < t - win // 2 + win), 1.0, 0.0).astype(BF16)
            lo = jnp.maximum(tpos - win // 2, 0)
            hi = jnp.minimum(tpos - win // 2 + win, seqlen)
            cnt = (hi - lo).astype(F32)
            p = (_dot(band, xh[:, sl]) + _dot(band, xl[:, sl])) / cnt - x[:, sl]
            y = _dot(p.astype(BF16), pw_ref[gi]) * ps_ref[:, sl]
            cat_scr[:, wm + gi * gw:wm + (gi + 1) * gw] = y.astype(BF16)

    out_ref[...] = (_load_stream(h_refs, split_tiles)
                    + gate_ref[...] * _dot(cat_scr[...], pltpu.bitcast(w_ref[...], BF16)))


def _even_out(h, mod3, layer, hs_ctx, hs_lat, z, gain, pool_w_bf16, pool_scale, w_out_bf16, n_ctx_tiles,
              tiles_per_latent, ctx_len, lat_len, tn=2048):
    tm = TOKEN_TILE
    d = 2 * w_out_bf16.shape[0]
    h_arrays, h_specs, n, split_tiles = _stream(h, tm, tn, col_of=lambda j: j)
    wm = MLSTM_HEADS * MLSTM_DH
    nlt = n // tm - n_ctx_tiles
    rows8 = n // POOL_HALO
    per = tm // POOL_HALO

    def gate_idx(i, j):
        return ((layer * 8 + _mod_row(i, n_ctx_tiles, tiles_per_latent)) * N_MOD + 2, 0, j)

    return pl.pallas_call(
        functools.partial(_even_out_kernel, n_ctx_tiles=n_ctx_tiles, tiles_per_latent=tiles_per_latent,
                          ctx_len=ctx_len, lat_len=lat_len, split_tiles=split_tiles),
        out_shape=jax.ShapeDtypeStruct((n, d), F32),
        grid=(n // tm, d // tn),
        in_specs=h_specs + [
            pl.BlockSpec((None, 1, tn), gate_idx),
            pl.BlockSpec((2, tm, wm), lambda i, j: (0, jnp.minimum(i, n_ctx_tiles - 1), 0)),
            pl.BlockSpec((2, tm, wm), lambda i, j: (0, jnp.clip(i - n_ctx_tiles, 0, nlt - 1), 0)),
            pl.BlockSpec((tm, wm), lambda i, j: (i, 3)),
            pl.BlockSpec((tm, wm), lambda i, j: (i, 4)),
            pl.BlockSpec((POOL_HALO, wm), lambda i, j: (jnp.maximum(i * per - 1, 0), 4)),
            pl.BlockSpec((POOL_HALO, wm), lambda i, j: (jnp.minimum((i + 1) * per, rows8 - 1), 4)),
            pl.BlockSpec((1, wm), lambda i, j: (0, 0)),
            pl.BlockSpec(pool_w_bf16.shape, lambda i, j: (0, 0, 0)),
            pl.BlockSpec((1, wm), lambda i, j: (0, 0)),
            pl.BlockSpec((d // 2, tn), lambda i, j: (0, j)),
        ],
        out_specs=pl.BlockSpec((tm, tn), lambda i, j: (i, j)),
        scratch_shapes=[pltpu.VMEM((tm, d), BF16)],
        compiler_params=_params("arbitrary", "arbitrary"),
        name="even_out",
    )(*h_arrays, mod3, hs_ctx, hs_lat, z, z, z, z, gain.reshape(1, wm), pool_w_bf16, pool_scale.reshape(1, wm),
      w_out_bf16)


def _qk_kernel(*refs, rope, emit_cache):
    q_ref, k_ref, v_ref, gain_ref = refs[:4]
    rest = list(refs[4:])
    if rope:
        cos_ref, sin_ref = rest[:2]
        rest = rest[2:]
    qn_ref, kn_ref, vb_ref = rest[:3]
    rest = rest[3:]
    hd = 2 * DA_DH
    lane = lax.broadcasted_iota(jnp.int32, (q_ref.shape[0], hd), 1)
    first = lane < DA_DH

    def norm(x, gain):
        sq = x * x
        s1 = jnp.sum(jnp.where(first, sq, 0.0), axis=-1, keepdims=True)
        s2 = jnp.sum(jnp.where(first, 0.0, sq), axis=-1, keepdims=True)
        ms = jnp.where(first, s1, s2) * (1.0 / DA_DH)
        y = x * lax.rsqrt(ms + EPS) * gain
        if rope:
            quarter = DA_DH // 4
            partner = jnp.where((lane & quarter) == 0, pltpu.roll(y, LANES - quarter, 1), pltpu.roll(y, quarter, 1))
            y = y * cos_ref[...] + partner * sin_ref[...]
        return y

    vb_ref[...] = v_ref[...].astype(BF16)
    for h in range(DA_HEADS):
        cols = slice(h * hd, (h + 1) * hd)
        qn = norm(q_ref[:, cols], gain_ref[0:1, :])
        kn = norm(k_ref[:, cols], gain_ref[1:2, :])
        qn_ref[:, cols] = (qn * (DA_DH ** -0.5)).astype(BF16)
        kn_ref[:, cols] = kn.astype(BF16)
        if emit_cache:
            newk_ref, newv_ref = rest
            newk_ref[h] = kn
            newv_ref[h] = v_ref[:, cols]


def _qk(z, qk_gain2, tok_off, ntok, j, rope_tables=None, cache_shape=None, seqlen=None):
    tm = TOKEN_TILE
    hd = 2 * DA_DH
    w = DA_HEADS * hd
    off = tok_off // tm
    rope = rope_tables is not None
    emit_cache = cache_shape is not None
    in_specs = [
        pl.BlockSpec((tm, w), lambda i: (off + i, 0)),
        pl.BlockSpec((tm, w), lambda i: (off + i, 1)),
        pl.BlockSpec((tm, w), lambda i: (off + i, 2)),
        pl.BlockSpec((2, hd), lambda i: (0, 0)),
    ]
    args = [z, z, z, qk_gain2]
    if rope:
        tps = seqlen // tm
        in_specs += [pl.BlockSpec((tm, hd), lambda i: (i % tps, 0))] * 2
        args += list(rope_tables)
    out_shape = [jax.ShapeDtypeStruct((ntok, w), BF16)] * 3
    out_specs = [pl.BlockSpec((tm, w), lambda i: (i, 0))] * 3
    if emit_cache:
        out_shape += [jax.ShapeDtypeStruct(cache_shape, F32)] * 2
        out_specs += [pl.BlockSpec((None, None, DA_HEADS, tm, hd), lambda i: (i, j, 0, 0, 0))] * 2
    return pl.pallas_call(
        functools.partial(_qk_kernel, rope=rope, emit_cache=emit_cache),
        out_shape=out_shape,
        grid=(ntok // tm,),
        in_specs=in_specs,
        out_specs=out_specs,
        compiler_params=_params("arbitrary"),
        name="qk_ctx" if emit_cache else "qk_lat",
    )(*args)


def _rope_tables(t):
    rows = t // GRID_W
    row = jnp.repeat(jnp.arange(rows), GRID_W).astype(F32)
    col = (jnp.arange(rows * GRID_W) % GRID_W).astype(F32)
    n_freq = DA_DH // 4
    inv = ROPE_BASE ** (-jnp.arange(n_freq, dtype=F32) / n_freq)
    ar, ac = row[:, None] * inv, col[:, None] * inv
    cos = jnp.concatenate([jnp.cos(ar), jnp.cos(ar), jnp.cos(ac), jnp.cos(ac)], axis=-1)
    sin = jnp.concatenate([-jnp.sin(ar), jnp.sin(ar), -jnp.sin(ac), jnp.sin(ac)], axis=-1)
    return jnp.tile(cos, (1, 2)), jnp.tile(sin, (1, 2))


def _attn_kernel(*refs, n_pieces, lam_init):
    q_ref, lp_ref, subln_ref = refs[:3]
    kv = refs[3:3 + 2 * n_pieces]
    o_ref = refs[3 + 2 * n_pieces]
    lp = lp_ref[...]
    lam = (jnp.exp(jnp.sum(lp[0:1] * lp[1:2], axis=-1, keepdims=True))
           - jnp.exp(jnp.sum(lp[2:3] * lp[3:4], axis=-1, keepdims=True)) + lam_init)
    q = q_ref[...]
    lane = lax.broadcasted_iota(jnp.int32, q.shape, 1)
    zero = jnp.zeros_like(q)
    qa = jnp.where(lane < DA_DH, q, zero)
    qb = jnp.where(lane < DA_DH, zero, q)
    ks = [kv[2 * p][...].astype(BF16) for p in range(n_pieces)]
    vs = [kv[2 * p + 1][...].astype(BF16) for p in range(n_pieces)]

    sa = [_dot_nt(qa, k) for k in ks]
    sb = [_dot_nt(qb, k) for k in ks]

    def attend(parts):
        m = functools.reduce(jnp.maximum, [jnp.max(x, axis=-1, keepdims=True) for x in parts])
        es = [jnp.exp(x - m) for x in parts]
        tot = functools.reduce(lambda a, b: a + b, [jnp.sum(e, axis=-1, keepdims=True) for e in es])
        pv = functools.reduce(lambda a, b: a + b, [_dot(e.astype(BF16), v) for e, v in zip(es, vs)])
        return pv / tot

    o = attend(sa) - lam * attend(sb)
    o = o * lax.rsqrt(jnp.mean(o * o, axis=-1, keepdims=True) + EPS) * subln_ref[...]
    o_ref[...] = (o * (1.0 - lam_init)).astype(BF16)


def _attn(qn, kn, vb, lam_params, subln, lam_init, nseq, seqlen, tq, j, caches=None):
    hd = 2 * DA_DH
    nq = seqlen // tq
    in_specs = [
        pl.BlockSpec((tq, hd), lambda b, h, qi: (b * nq + qi, h)),
        pl.BlockSpec((None, 4, DA_DH), lambda b, h, qi: (j, 0, 0)),
        pl.BlockSpec((None, 1, hd), lambda b, h, qi: (j, 0, 0)),
        pl.BlockSpec((seqlen, hd), lambda b, h, qi: (b, h)),
        pl.BlockSpec((seqlen, hd), lambda b, h, qi: (b, h)),
    ]
    args = [qn, lam_params, subln.reshape(subln.shape[0], 1, hd), kn, vb]
    n_pieces = 1
    if caches is not None:
        ck, cv = caches
        past = ck.shape[3]
        in_specs += [pl.BlockSpec((None, None, None, past, hd), lambda b, h, qi: (b, j, h, 0, 0))] * 2
        args += [ck, cv]
        n_pieces = 2
    return pl.pallas_call(
        functools.partial(_attn_kernel, n_pieces=n_pieces, lam_init=lam_init),
        out_shape=jax.ShapeDtypeStruct(qn.shape, BF16),
        grid=(nseq, DA_HEADS, nq),
        in_specs=in_specs,
        out_specs=pl.BlockSpec((tq, hd), lambda b, h, qi: (b * nq + qi, h)),
        compiler_params=_params("arbitrary", "arbitrary", "arbitrary"),
        name="attn_lat" if caches is not None else "attn_ctx",
    )(*args)


def _odd_out_kernel(h_ref, gate_ref, ac_ref, al_ref, gu_ref, gv_ref, ws_ref, bt_ref, w_ref, out_ref, cat_scr, *,
                    n_ctx_tiles):
    i = pl.program_id(0)
    tm = TOKEN_TILE
    wa = DA_HEADS * 2 * DA_DH
    gw = LANES

    @pl.when(pl.program_id(1) == 0)
    def _():
        cat_scr[:, 0:wa] = jnp.where(i < n_ctx_tiles, ac_ref[...], al_ref[...])
        for c in range(tm // GM_CHUNK):
            rows = slice(c * GM_CHUNK, (c + 1) * GM_CHUNK)
            for g in range(GM_GROUPS):
                cols = slice(g * gw, (g + 1) * gw)
                u = _gelu_tanh(gu_ref[rows, cols])
                v = _gelu_tanh(gv_ref[rows, cols])
                vn = v * lax.rsqrt(jnp.mean(v * v, axis=-1, keepdims=True) + EPS)
                mixed = _dot(ws_ref[g], vn.astype(BF16)) + bt_ref[:, g:g + 1]
                cat_scr[rows, wa + g * gw:wa + (g + 1) * gw] = (u * mixed).astype(BF16)

    out_ref[...] = h_ref[...] + gate_ref[...] * _dot(cat_scr[...], pltpu.bitcast(w_ref[...], BF16))


def _odd_out(h, mod3, layer, a_ctx, a_lat, z, gm_ws_bf16, gm_b_t, w_out_bf16, n_ctx_tiles, tiles_per_latent, tn=2048):
    n, d = h.shape
    tm = TOKEN_TILE
    wa = DA_HEADS * 2 * DA_DH
    wg = GM_GROUPS * LANES
    nlt = n // tm - n_ctx_tiles

    def gate_idx(i, j):
        return ((layer * 8 + _mod_row(i, n_ctx_tiles, tiles_per_latent)) * N_MOD + 2, 0, j)

    return pl.pallas_call(
        functools.partial(_odd_out_kernel, n_ctx_tiles=n_ctx_tiles),
        out_shape=jax.ShapeDtypeStruct((n, d), F32),
        grid=(n // tm, d // tn),
        in_specs=[
            pl.BlockSpec((tm, tn), lambda i, j: (i, j)),
            pl.BlockSpec((None, 1, tn), gate_idx),
            pl.BlockSpec((tm, wa), lambda i, j: (jnp.minimum(i, n_ctx_tiles - 1), 0)),
            pl.BlockSpec((tm, wa), lambda i, j: (jnp.clip(i - n_ctx_tiles, 0, nlt - 1), 0)),
            pl.BlockSpec((tm, wg), lambda i, j: (i, 3)),
            pl.BlockSpec((tm, wg), lambda i, j: (i, 4)),
            pl.BlockSpec(gm_ws_bf16.shape, lambda i, j: (0, 0, 0)),
            pl.BlockSpec(gm_b_t.shape, lambda i, j: (0, 0)),
            pl.BlockSpec((d // 2, tn), lambda i, j: (0, j)),
        ],
        out_specs=pl.BlockSpec((tm, tn), lambda i, j: (i, j)),
        scratch_shapes=[pltpu.VMEM((tm, d), BF16)],
        compiler_params=_params("arbitrary", "arbitrary"),
        name="odd_out",
    )(h, mod3, a_ctx, a_lat, z, z, gm_ws_bf16, gm_b_t, w_out_bf16)


def _top_values(x, k):
    vals = []
    cur = x
    for r in range(k):
        m = jnp.max(cur, axis=0, keepdims=True)
        vals.append(m)
        if r + 1 < k:
            cur = jnp.where(cur == m, NEG_INF, cur)
    return jnp.concatenate(vals, axis=0)


def _oddeven_merge(lo, hi, r):
    step = r * 2
    if step < hi - lo:
        yield from _oddeven_merge(lo, hi, step)
        yield from _oddeven_merge(lo + r, hi, step)
        yield from [(i, i + r) for i in range(lo + r, hi - r, step)]
    else:
        yield (lo, lo + r)


def _oddeven_merge_sort(lo, hi):
    if hi - lo >= 1:
        mid = lo + (hi - lo) // 2
        yield from _oddeven_merge_sort(lo, mid)
        yield from _oddeven_merge_sort(mid + 1, hi)
        yield from _oddeven_merge(lo, hi, 1)


def _top_sorted(x):
    n = len(x)
    assert n == PEER_TOPK

    def exchange(v, i, j):
        v[i], v[j] = jnp.maximum(v[i], v[j]), jnp.minimum(v[i], v[j])

    v = list(x)
    for i, j in _oddeven_merge_sort(0, n - 1):
        exchange(v, i, j)
    shift = SUBLANES // 2
    while shift:
        w = [pltpu.roll(t, SUBLANES - shift, 0) for t in v]
        v = [jnp.maximum(v[i], w[n - 1 - i]) for i in range(n)]
        d = n // 2
        while d:
            for i in range(n):
                if not i & d:
                    exchange(v, i, i + d)
            d //= 2
        shift //= 2
    return jnp.concatenate([t[0:1, :] for t in v], axis=0)


def _dup_bf16(x):
    b = pltpu.bitcast(x.astype(BF16).astype(F32), jnp.uint32)
    return pltpu.bitcast(b | (b >> 16), F32)


def _peer_route_kernel(q_ref, sk_ref, rank_ref, e2_ref, cnt_ref, c_ref, s_scr):
    half = PEER_NKEYS
    k = PEER_TOPK
    n_slabs = half // SUBLANES
    for h in range(PEER_HEADS):
        q1 = pltpu.bitcast(q_ref[:, 2 * h * half:(2 * h + 1) * half], BF16)
        q2 = pltpu.bitcast(q_ref[:, (2 * h + 1) * half:(2 * h + 2) * half], BF16)
        s_scr[0] = _dot_nt(sk_ref[2 * h], q1)
        s_scr[1] = _dot_nt(sk_ref[2 * h + 1], q2)

        def lane_tile(t, carry):
            lanes = pl.ds(pl.multiple_of(t * LANES, LANES), LANES)
            s1 = s_scr[0, :, lanes]
            s2 = s_scr[1, :, lanes]
            a_all = _top_sorted([s1[SUBLANES * i:SUBLANES * (i + 1)] for i in range(n_slabs)])
            b_all = _top_sorted([s2[SUBLANES * i:SUBLANES * (i + 1)] for i in range(n_slabs)])
            b = [b_all[r:r + 1] for r in range(k)]
            rows = [a_all[r:r + 1] + b_all[0:k // (r + 1)] for r in range(k)]
            nrows = sum(k // (r + 1) for r in range(k))
            rows.append(jnp.full((-nrows % SUBLANES, LANES), NEG_INF, F32))
            top = _top_values(jnp.concatenate(rows, axis=0), k)
            tau = top[k - 1:k]
            z = jnp.sum(jnp.exp(top - top[0:1]), axis=0, keepdims=True)
            cnt = jnp.zeros(s1.shape, F32)
            rank2 = jnp.zeros(s2.shape, F32)
            for r in range(k):
                cnt = jnp.where(s1 + b[r] >= tau, float(r + 1), cnt)
                rank2 = jnp.where(b[r] > s2, float(r + 1), rank2)
            rank_ref[h, :, lanes] = pltpu.bitcast(rank2.astype(BF16), F32)
            e2_ref[h, :, lanes] = pltpu.bitcast(jnp.exp(s2 - b[0]).astype(BF16), F32)
            cnt_ref[h, :, lanes] = _dup_bf16(cnt)
            c_ref[h, :, lanes] = _dup_bf16(jnp.exp(s1 - a_all[0:1]) / z)
            return carry

        lax.fori_loop(0, s_scr.shape[2] // LANES, lane_tile, 0, unroll=2)


def _peer_route(qp, subkeys_bf16, tm=512):
    n = 2 * qp.shape[0]
    def out(rows):
        return (jax.ShapeDtypeStruct((PEER_HEADS, rows, n), F32),
                pl.BlockSpec((PEER_HEADS, rows, tm), lambda i: (0, 0, i)))

    outs = [out(PEER_NKEYS // 2)] * 2 + [out(PEER_NKEYS)] * 2
    return pl.pallas_call(
        _peer_route_kernel,
        out_shape=[o[0] for o in outs],
        grid=(n // tm,),
        in_specs=[
            pl.BlockSpec((tm // 2, qp.shape[1]), lambda i: (i, 0)),
            pl.BlockSpec(subkeys_bf16.shape, lambda i: (0, 0, 0)),
        ],
        out_specs=[o[1] for o in outs],
        scratch_shapes=[pltpu.VMEM((2, PEER_NKEYS, tm), F32)],
        compiler_params=_params("arbitrary"),
        name="peer_route",
    )(qp, subkeys_bf16)


def _peer_dense_kernel(xnt_ref, h_ref, gate_ref, u_ref, vt_ref, rank_ref, e2_ref, cnt_ref, c_ref, *rest, te, n_tiles,
                       split_tiles):
    n_out = 1 if split_tiles is None else 2
    out_refs = rest[:n_out]
    act0_scr, act1_scr, wa0_scr, wa1_scr, acc_scr = rest[n_out:]
    s = pl.program_id(0)
    n_pairs = pl.num_programs(0) - 2
    nkeys = PEER_NKEYS
    per = te // nkeys
    tm = acc_scr.shape[1]
    half = tm // 2
    assert per % 4 == 0
    pack = 2 * SUBLANES

    @pl.when(s == 0)
    def _():
        act1_scr[...] = jnp.zeros_like(act1_scr)
        wa0_scr[...] = jnp.zeros_like(wa0_scr)

    live = jnp.logical_and(s >= 1, s <= n_pairs)
    t2 = jnp.clip(s - 1, 0, n_pairs - 1) % n_tiles
    pair3 = jnp.clip(s - 2, 0, n_pairs - 1)
    t3 = pair3 % n_tiles
    tok3 = pair3 // n_tiles

    @pl.when(t3 == 0)
    def _():
        acc_scr[...] = jnp.zeros_like(acc_scr)

    def step(act_w, act_r, wa_w, wa_r):
        def stage1(c):
            cols = slice(c * half, (c + 1) * half)
            u = pltpu.bitcast(u_ref[0:te // 2, :], BF16)
            act_w[0:te, cols] = _dot(u, pltpu.bitcast(xnt_ref[:, cols], BF16))

        d_rows = acc_scr.shape[0] // PEER_V_ROW_PIECES

        def stage3(c, dq):
            cols = slice(c * half, (c + 1) * half)
            rows = slice(dq * d_rows, (dq + 1) * d_rows)
            vt = pltpu.bitcast(vt_ref[dq * d_rows // 2:(dq + 1) * d_rows // 2, :], BF16)
            acc_scr[rows, cols] += _dot(vt, wa_r[:, cols])

        mxu_work = []
        for c in range(2):
            mxu_work.append(functools.partial(stage1, c))
            mxu_work += [functools.partial(stage3, c, dq) for dq in range(PEER_V_ROW_PIECES)]
        n_lt = tm // LANES
        n_valu = per * n_lt
        issued = 0
        for ii in range(per):
            r = t2 * per + ii
            cnt_rows = [jnp.where(live, cnt_ref[h, pl.ds(r, 1), :], 0.0) for h in range(PEER_HEADS)]
            c_rows = [c_ref[h, pl.ds(r, 1), :] for h in range(PEER_HEADS)]
            for lt in range(n_lt):
                while issued < len(mxu_work) and (issued * n_valu) // len(mxu_work) <= ii * n_lt + lt:
                    mxu_work[issued]()
                    issued += 1
                lanes = slice(lt * LANES, (lt + 1) * LANES)
                cnts = [pltpu.bitcast(jnp.broadcast_to(x[:, lanes], (SUBLANES, LANES)), BF16) for x in cnt_rows]
                ccs = [pltpu.bitcast(jnp.broadcast_to(x[:, lanes], (SUBLANES, LANES)), BF16) for x in c_rows]
                for sb in range(nkeys // pack):
                    words = slice(sb * SUBLANES, (sb + 1) * SUBLANES)
                    rows = slice(ii * nkeys + sb * pack, ii * nkeys + (sb + 1) * pack)
                    w = None
                    for h in range(PEER_HEADS):
                        e2 = pltpu.bitcast(e2_ref[h, words, lanes], BF16)
                        rank = pltpu.bitcast(rank_ref[h, words, lanes], BF16)
                        wh = jnp.where(rank < cnts[h], e2 * ccs[h], jnp.zeros_like(e2))
                        w = wh if w is None else w + wh
                    wa_w[rows, lanes] = w * _gelu_tanh_lowp(act_r[rows, lanes].astype(BF16))
        assert issued == len(mxu_work)

    @pl.when(s % 2 == 0)
    def _():
        step(act0_scr, act1_scr, wa1_scr, wa0_scr)

    @pl.when(s % 2 == 1)
    def _():
        step(act1_scr, act0_scr, wa0_scr, wa1_scr)

    @pl.when(jnp.logical_and(t3 == n_tiles - 1, s >= 2))
    def _():
        res = h_ref[...] + gate_ref[...] * acc_scr[...].T
        if split_tiles is None:
            out_refs[0][...] = res
        else:
            @pl.when(tok3 < split_tiles)
            def _():
                out_refs[0][...] = res

            @pl.when(tok3 >= split_tiles)
            def _():
                out_refs[1][...] = res


def _peer_dense(xnt, h, mod3, layer, u_packed, vt_packed, rank2, e2, cnt, c, n_ctx_tiles, tiles_per_latent, tm=512,
                te=PACK_ROWS, split_out=False):
    n, d = h.shape
    n_tiles = 2 * u_packed.shape[0] // te
    per = tm // TOKEN_TILE
    n_pairs = (n // tm) * n_tiles

    def tok(lag):
        return lambda t: jnp.clip(t - lag, 0, n_pairs - 1) // n_tiles

    def exp(lag):
        return lambda t: jnp.clip(t - lag, 0, n_pairs - 1) % n_tiles

    tok1, tok2, tok3, exp1, exp3 = tok(0), tok(1), tok(2), exp(0), exp(2)
    if split_out:
        split_tiles = n_ctx_tiles // per
        n_ctx = split_tiles * tm
        out_shape = [jax.ShapeDtypeStruct((n_ctx, d), F32), jax.ShapeDtypeStruct((n - n_ctx, d), F32)]
        out_specs = [pl.BlockSpec((tm, d), lambda t: (jnp.minimum(tok3(t), split_tiles - 1), 0)),
                     pl.BlockSpec((tm, d), lambda t: (jnp.maximum(tok3(t) - split_tiles, 0), 0))]
    else:
        split_tiles = None
        out_shape = jax.ShapeDtypeStruct((n, d), F32)
        out_specs = pl.BlockSpec((tm, d), lambda t: (tok3(t), 0))

    def gate_idx(t):
        return ((layer * 8 + _mod_row(tok3(t) * per, n_ctx_tiles, tiles_per_latent)) * N_MOD + 5, 0, 0)

    rspec = pl.BlockSpec((PEER_HEADS, PEER_NKEYS, tm), lambda t: (0, 0, tok2(t)))
    pspec = pl.BlockSpec((PEER_HEADS, PEER_NKEYS // 2, tm), lambda t: (0, 0, tok2(t)))
    return pl.pallas_call(
        functools.partial(_peer_dense_kernel, te=te, n_tiles=n_tiles, split_tiles=split_tiles),
        out_shape=out_shape,
        grid=(n_pairs + 2,),
        in_specs=[
            pl.BlockSpec((d // 2, tm), lambda t: (0, tok1(t))),
            pl.BlockSpec((tm, d), lambda t: (tok3(t), 0), pipeline_mode=pl.Buffered(1)),
            pl.BlockSpec((None, 1, d), gate_idx),
            pl.BlockSpec((te // 2, d), lambda t: (exp1(t), 0)),
            pl.BlockSpec((None, d // 2, te), lambda t: (exp3(t), 0, 0)),
            pspec, pspec, rspec, rspec,
        ],
        out_specs=out_specs,
        scratch_shapes=[pltpu.VMEM((te, tm), F32), pltpu.VMEM((te, tm), F32), pltpu.VMEM((te, tm), BF16),
                        pltpu.VMEM((te, tm), BF16), pltpu.VMEM((d, tm), F32)],
        compiler_params=_params("arbitrary"),
        name="peer_dense",
    )(xnt, h, mod3, u_packed, vt_packed, rank2, e2, cnt, c)


def kernel(x_prompt, x_sample, state_mlstm_C, state_mlstm_n, state_mlstm_m, cache_da_k, cache_da_v, c, c_ctx, norm_mix, norm_ffn, w_mod, b_mod, w_in_even, b_gate_even, mlstm_gain, pool_w, pool_scale, w_out_even, w_in_odd, qk_gain, da_lambda, da_subln, gm_ws, gm_b, w_out_odd, peer_wq, peer_subkeys, peer_u, peer_v):
    nb, s_len, d = x_prompt.shape
    nbd, t_len, _ = x_sample.shape
    depth = w_mod.shape[0]
    tm = TOKEN_TILE
    assert s_len == tm and t_len % tm == 0 and nbd <= 7 and t_len % GRID_W == 0
    n_ctx = nb * s_len
    n_lat = nbd * t_len
    n_ctx_tiles = n_ctx // tm
    tiles_per_latent = t_len // tm
    n_even = (depth + 1) // 2
    n_odd = depth // 2

    h = (x_prompt.reshape(n_ctx, d), x_sample.reshape(n_lat, d))
    cond8 = jnp.concatenate([c_ctx[None], c, jnp.zeros((7 - nbd, d), F32)], axis=0)
    mod3 = _adaln(cond8, w_mod, b_mod).reshape(depth * 8 * N_MOD, 1, d)
    rope = _rope_tables(t_len)

    new_c, new_n, new_m, new_k, new_v = [], [], [], [], []
    wm = MLSTM_HEADS * MLSTM_DH
    for l in range(depth):
        j = l // 2
        if l % 2 == 0:
            n_main = 5 * wm
            z, gates, gates_t = _inproj(h, norm_mix[l], mod3, l, 0, _pack_rows(w_in_even, j, ncols=n_main),
                                        n_ctx_tiles, tiles_per_latent, w_gates=w_in_even[j, :, n_main:],
                                        b_gates=b_gate_even[j],
                                        name="inproj_even")
            hs_ctx, cc, cn, cm = _mlstm(z, gates, gates_t, 0, nb, s_len, 0, emit_state=True)
            (hs_lat,) = _mlstm(z, gates, gates_t, n_ctx, nbd, t_len, j,
                               init=(state_mlstm_C, state_mlstm_n, state_mlstm_m))
            new_c.append(cc)
            new_n.append(cn)
            new_m.append(cm[:, :, :, 0, :])
            h = _even_out(h, mod3, l, hs_ctx, hs_lat, z, mlstm_gain[j], pool_w[j].astype(BF16), pool_scale[j],
                          _pack_rows(w_out_even, j), n_ctx_tiles, tiles_per_latent, s_len, t_len)
        else:
            lam_init = 0.8 - 0.6 * math.exp(-0.3 * l)
            (z,) = _inproj(h, norm_mix[l], mod3, l, 0, _pack_rows(w_in_odd, j), n_ctx_tiles, tiles_per_latent,
                           name="inproj_odd")
            gain2 = jnp.tile(qk_gain[j], (1, 2))
            cache_shape = (nb, 1, DA_HEADS, s_len, 2 * DA_DH)
            qn_c, kn_c, vb_c, nk, nv = _qk(z, gain2, 0, n_ctx, 0, cache_shape=cache_shape)
            qn_l, kn_l, vb_l = _qk(z, gain2, n_ctx, n_lat, 0, rope_tables=rope, seqlen=t_len)
            new_k.append(nk)
            new_v.append(nv)
            a_ctx = _attn(qn_c, kn_c, vb_c, da_lambda, da_subln, lam_init, nb, s_len, s_len, j)
            a_lat = _attn(qn_l, kn_l, vb_l, da_lambda, da_subln, lam_init, nbd, t_len, tm, j,
                          caches=(cache_da_k, cache_da_v))
            h = _odd_out(h, mod3, l, a_ctx, a_lat, z, gm_ws[j].astype(BF16), gm_b[j].T, _pack_rows(w_out_odd, j),
                         n_ctx_tiles, tiles_per_latent)
        qp, xnt = _inproj(h, norm_ffn[l], mod3, l, 3, _pack_rows(peer_wq, l), n_ctx_tiles, tiles_per_latent,
                          emit_xn=True, pack_out=True, name="inproj_peer")
        sk = peer_subkeys[l].reshape(2 * PEER_HEADS, PEER_NKEYS, PEER_NKEYS).astype(BF16)
        rank2, e2, cnt, cw = _peer_route(qp, sk)
        h = _peer_dense(xnt, h, mod3, l, _pack_rows(peer_u, l), _pack_rows(peer_v, l, transpose=True), rank2,
                        e2, cnt, cw,
                        n_ctx_tiles, tiles_per_latent, split_out=(l == depth - 1))

    def join(parts):
        return parts[0] if len(parts) == 1 else jnp.concatenate(parts, axis=1)

    return (h[0].reshape(nb, s_len, d), h[1].reshape(nbd, t_len, d), join(new_c), join(new_n), join(new_m),
            join(new_k), join(new_v))
```
